```python
import jax
import jax.numpy as jnp
from jax import lax
import numpy as np

D_MODEL = 2048
BATCH = 1
SEQ = 16384
DEPTH = 2
DEC_BATCH = 32
DEC_SEQ = 32
PAST_LEN = 1024

CHUNK = 64
D_HGRN = D_MODEL // 2
HGRN_HEADS = 8
HGRN_DK = D_HGRN // HGRN_HEADS
HGRN_DV = D_HGRN // HGRN_HEADS
D_S5 = D_MODEL // 2
S5_GROUP = 16
S5_GROUPS = D_S5 // S5_GROUP
S5_STATE = 64
D_FF = -(-8 * D_MODEL // (3 * 256)) * 256
IN_COLS = 4 * D_HGRN + D_S5 + 2 * D_MODEL
SPLITS = (D_HGRN, 2 * D_HGRN, 3 * D_HGRN, 4 * D_HGRN, 4 * D_HGRN + D_S5, 4 * D_HGRN + D_S5 + D_MODEL)
EPS = 1e-6
DT_MIN = 1e-3
DT_MAX = 1e-1

kernel_name = 'hgrn2_s5_gated_hybrid_step'


def rmsnorm(x, w):
    xf = x.astype(jnp.float32)
    xf = xf * lax.rsqrt(jnp.mean(xf * xf, axis=-1, keepdims=True) + EPS)
    return (xf * w.astype(jnp.float32)).astype(x.dtype)


def hgrn_lower_bounds(lb_logits):
    p = jax.nn.softmax(lb_logits.astype(jnp.float32), axis=0)
    cs = jnp.cumsum(p, axis=0)
    return cs - cs[:1]


def hgrn2_chunk(q, k, v, logf, s0):
    c = q.shape[1]
    b = jnp.cumsum(logf, axis=1)
    o_inter = jnp.einsum('bthk,bhkv->bthv', q * jnp.exp(b), s0)
    causal = jnp.tril(jnp.ones((c, c), dtype=bool))
    diff = b[:, :, None] - b[:, None, :]
    decay = jnp.exp(jnp.where(causal[None, :, :, None, None], diff, -jnp.inf))
    scores = jnp.einsum('bthk,bshk,btshk->btsh', q, k, decay)
    o = o_inter + jnp.einsum('btsh,bshv->bthv', scores, v)
    b_last = b[:, -1]
    s_new = jnp.exp(b_last)[..., None] * s0 + jnp.einsum('bshk,bshv->bhkv', k * jnp.exp(b_last[:, None] - b), v)
    return o, s_new


def hgrn2_recurrence(q, k, v, logf, s0):
    bsz, t = q.shape[0], q.shape[1]
    if t <= CHUNK:
        return hgrn2_chunk(q, k, v, logf, s0)
    n = t // CHUNK

    def to_blocks(a):
        return jnp.moveaxis(a.reshape((bsz, n, CHUNK) + a.shape[2:]), 1, 0)

    def step(s, xs):
        qc, kc, vc, fc = xs
        o, s = hgrn2_chunk(qc, kc, vc, fc, s)
        return s, o

    s_fin, o = lax.scan(step, s0, (to_blocks(q), to_blocks(k), to_blocks(v), to_blocks(logf)))
    o = jnp.moveaxis(o, 0, 1).reshape((bsz, t) + o.shape[3:])
    return o, s_fin


def hgrn2_branch(q_lin, f_lin, i_lin, g_lin, lb, gain, s0):
    f32 = jnp.float32
    bsz, t, _ = q_lin.shape
    shp_k = (bsz, t, HGRN_HEADS, HGRN_DK)
    shp_v = (bsz, t, HGRN_HEADS, HGRN_DV)
    z = f_lin.astype(f32)
    logf = jnp.logaddexp(jnp.log(lb), jnp.log1p(-lb) + jax.nn.log_sigmoid(z))
    k = (1.0 - lb) * jax.nn.sigmoid(-z)
    q = q_lin.astype(f32).reshape(shp_k)
    v = jax.nn.silu(i_lin.astype(f32)).reshape(shp_v)
    o, s_new = hgrn2_recurrence(q, k.reshape(shp_k), v, logf.reshape(shp_k), s0.astype(f32))
    o = o * lax.rsqrt(jnp.mean(o * o, axis=-1, keepdims=True) + EPS)
    o = o.reshape(bsz, t, D_HGRN) * gain.astype(f32) * jax.nn.silu(g_lin.astype(f32))
    return o.astype(q_lin.dtype), s_new


def _linear_combine(left, right):
    a_l, b_l = left
    a_r, b_r = right
    return (a_r * a_l, a_r * b_l + b_r)


def s5_branch(u, s_re0, s_im0, a_log_neg_re, a_im, log_dt, b_re, b_im, c_re, c_im, d_skip, w_glu, b_glu):
    f32 = jnp.float32
    bsz, t, _ = u.shape
    uf = u.astype(f32)
    ug = uf.reshape(bsz, t, S5_GROUPS, S5_GROUP)
    lam = lax.complex(-jnp.exp(a_log_neg_re.astype(f32)), a_im.astype(f32))
    dt = jnp.exp(log_dt.astype(f32))[:, None]
    lam_bar = jnp.exp(lam * dt)
    zoh = (lam_bar - 1.0) / lam
    b_bar = zoh[..., None] * lax.complex(b_re.astype(f32), b_im.astype(f32))
    bu = lax.complex(jnp.einsum('gnp,btgp->btgn', b_bar.real, ug),
                     jnp.einsum('gnp,btgp->btgn', b_bar.imag, ug))
    s0 = lax.complex(s_re0.astype(f32), s_im0.astype(f32))
    bu = bu.at[:, 0].add(lam_bar * s0)
    a = jnp.broadcast_to(lam_bar, bu.shape)
    _, xs = lax.associative_scan(_linear_combine, (a, bu), axis=1)
    y = (jnp.einsum('gpn,btgn->btgp', c_re.astype(f32), xs.real)
         - jnp.einsum('gpn,btgn->btgp', c_im.astype(f32), xs.imag))
    y = y.reshape(bsz, t, D_S5) + d_skip.astype(f32) * uf
    y = jax.nn.gelu(y)
    y = y * jax.nn.sigmoid(y @ w_glu.astype(f32) + b_glu.astype(f32))
    s_last = xs[:, -1]
    return y.astype(u.dtype), jnp.real(s_last), jnp.imag(s_last)


def layer(x, s_h, s_re, s_im, lb, norm1, w_in, hgrn_norm, w_bh, a_log_neg_re, a_im, log_dt,
          b_re, b_im, c_re, c_im, d_skip, w_glu, b_glu, w_bs, w_out, norm2, w_gate_up, w_down):
    h = rmsnorm(x, norm1)
    q, f, i, g, u, gh, gs = jnp.split(h @ w_in, SPLITS, axis=-1)
    o_h, s_h_new = hgrn2_branch(q, f, i, g, lb, hgrn_norm, s_h)
    o_s, s_re_new, s_im_new = s5_branch(u, s_re, s_im, a_log_neg_re, a_im, log_dt, b_re, b_im,
                                        c_re, c_im, d_skip, w_glu, b_glu)
    mix = jax.nn.sigmoid(gh) * (o_h @ w_bh) + jax.nn.sigmoid(gs) * (o_s @ w_bs)
    x = x + mix @ w_out
    ga, up = jnp.split(rmsnorm(x, norm2) @ w_gate_up, 2, axis=-1)
    x = x + (jax.nn.silu(ga) * up) @ w_down
    return x, s_h_new, s_re_new, s_im_new


def setup_inputs(seed: int = 0) -> dict:
    key = jax.random.key(seed)
    ks = jax.random.split(key, 26)
    f32 = jnp.float32

    def nrm(k, shape, scale):
        return jax.random.normal(k, shape, f32) * scale

    L, D, G, N, P = DEPTH, D_MODEL, S5_GROUPS, S5_STATE, S5_GROUP
    n_idx = jnp.arange(N, dtype=f32)
    return {
        'x_prompt': nrm(ks[0], (BATCH, SEQ, D), 1.0),
        'x_sample': nrm(ks[1], (DEC_BATCH, DEC_SEQ, D), 1.0),
        'state_hgrn': nrm(ks[2], (L, DEC_BATCH, HGRN_HEADS, HGRN_DK, HGRN_DV), 0.5),
        'state_s5_re': nrm(ks[3], (L, DEC_BATCH, G, N), 0.1),
        'state_s5_im': nrm(ks[4], (L, DEC_BATCH, G, N), 0.1),
        'lb_logits': nrm(ks[5], (L, D_HGRN), 0.5),
        'norm1': 1.0 + nrm(ks[6], (L, D), 0.02),
        'w_in': nrm(ks[7], (L, D, IN_COLS), D ** -0.5),
        'hgrn_norm': 1.0 + nrm(ks[8], (L, D_HGRN), 0.02),
        'w_bh': nrm(ks[9], (L, D_HGRN, D), D_HGRN ** -0.5),
        's5_a_log_neg_re': np.float32(np.log(0.5)) + nrm(ks[10], (L, G, N), 0.02),
        's5_a_im': np.float32(np.pi) * n_idx + nrm(ks[11], (L, G, N), 0.02),
        's5_log_dt': jax.random.uniform(ks[12], (L, G), f32, float(np.log(DT_MIN)), float(np.log(DT_MAX))),
        's5_b_re': nrm(ks[13], (L, G, N, P), (2 * P) ** -0.5),
        's5_b_im': nrm(ks[14], (L, G, N, P), (2 * P) ** -0.5),
        's5_c_re': nrm(ks[15], (L, G, P, N), 0.5),
        's5_c_im': nrm(ks[16], (L, G, P, N), 0.5),
        's5_d': nrm(ks[17], (L, D_S5), 1.0),
        'w_glu': nrm(ks[18], (L, D_S5, D_S5), D_S5 ** -0.5),
        'b_glu': nrm(ks[19], (L, D_S5), 0.02),
        'w_bs': nrm(ks[20], (L, D_S5, D), D_S5 ** -0.5),
        'w_out': nrm(ks[21], (L, D, D), D ** -0.5),
        'norm2': 1.0 + nrm(ks[22], (L, D), 0.02),
        'w_gate_up': nrm(ks[23], (L, D, 2 * D_FF), D ** -0.5),
        'w_down': nrm(ks[24], (L, D_FF, D), D_FF ** -0.5),
        'final_norm': 1.0 + nrm(ks[25], (D,), 0.02),
    }


def reference(x_prompt, x_sample, state_hgrn, state_s5_re, state_s5_im, lb_logits, norm1, w_in,
              hgrn_norm, w_bh, s5_a_log_neg_re, s5_a_im, s5_log_dt, s5_b_re, s5_b_im, s5_c_re, s5_c_im,
              s5_d, w_glu, b_glu, w_bs, w_out, norm2, w_gate_up, w_down, final_norm):
    f32 = jnp.float32
    lbs = hgrn_lower_bounds(lb_logits)

    def trunk(x, s_h, s_re, s_im):
        new_h, new_re, new_im = [], [], []
        for l in range(DEPTH):
            x, sh_l, sre_l, sim_l = layer(
                x, s_h[l], s_re[l], s_im[l], lbs[l], norm1[l], w_in[l], hgrn_norm[l], w_bh[l],
                s5_a_log_neg_re[l], s5_a_im[l], s5_log_dt[l], s5_b_re[l], s5_b_im[l], s5_c_re[l],
                s5_c_im[l], s5_d[l], w_glu[l], b_glu[l], w_bs[l], w_out[l], norm2[l], w_gate_up[l], w_down[l])
            new_h.append(sh_l)
            new_re.append(sre_l)
            new_im.append(sim_l)
        return rmsnorm(x, final_norm), jnp.stack(new_h), jnp.stack(new_re), jnp.stack(new_im)

    bp = x_prompt.shape[0]
    zero_h = jnp.zeros((DEPTH, bp, HGRN_HEADS, HGRN_DK, HGRN_DV), f32)
    zero_s = jnp.zeros((DEPTH, bp, S5_GROUPS, S5_STATE), f32)
    y_prompt, hgrn_p, s5_re_p, s5_im_p = trunk(x_prompt, zero_h, zero_s, zero_s)
    y_sample, hgrn_s, s5_re_s, s5_im_s = trunk(x_sample, state_hgrn, state_s5_re, state_s5_im)
    return (y_prompt, y_sample, hgrn_p, s5_re_p, s5_im_p, hgrn_s, s5_re_s, s5_im_s)
```

```python
import functools
import math

import jax
import jax.numpy as jnp
import numpy as np
from jax import lax
from jax.experimental import pallas as pl
from jax.experimental.pallas import tpu as pltpu

F32 = jnp.float32
BF16 = jnp.bfloat16

EPS = 1e-6
HEAD_DIM = 128
S5_GROUP = 16
S5_STATE = 64
S5_CHUNK = 16
HGRN_STACK_ROWS = 256
VMEM_LIMIT = 56 * 1024 * 1024


def _cparams(sem):
    return pltpu.CompilerParams(dimension_semantics=sem, vmem_limit_bytes=VMEM_LIMIT)


def _resident(shape, index_map):
    return pl.BlockSpec(shape, index_map, pipeline_mode=pl.Buffered(1))


def _inproj_kernel(x_ref, nw_ref, w_ref, gate_ref, main_ref, h_scr, *, n_gate_tiles):
    j = pl.program_id(1)

    @pl.when(j == 0)
    def _():
        x = x_ref[...]
        ms = jnp.mean(x * x, axis=-1, keepdims=True)
        h_scr[...] = (x * lax.rsqrt(ms + EPS) * nw_ref[...]).astype(BF16)

    acc = jnp.dot(h_scr[...], w_ref[...], preferred_element_type=F32)

    @pl.when(j < n_gate_tiles)
    def _():
        gate_ref[...] = jax.nn.sigmoid(acc).astype(BF16)

    @pl.when(j >= n_gate_tiles)
    def _():
        main_ref[...] = acc


def _inproj(x, nw, w, n_gate_cols, tm, tn):
    t, d = x.shape
    n = w.shape[1]
    n_gate_tiles = n_gate_cols // tn
    n_tiles = n // tn
    return pl.pallas_call(
        functools.partial(_inproj_kernel, n_gate_tiles=n_gate_tiles),
        grid=(t // tm, n_tiles),
        in_specs=[
            pl.BlockSpec((tm, d), lambda i, j: (i, 0)),
            pl.BlockSpec((1, d), lambda i, j: (0, 0)),
            pl.BlockSpec((d, tn), lambda i, j: (0, j)),
        ],
        out_specs=[
            pl.BlockSpec((tm, tn), lambda i, j: (i, jnp.minimum(j, n_gate_tiles - 1))),
            pl.BlockSpec((tm, tn), lambda i, j: (i, jnp.maximum(j - n_gate_tiles, 0))),
        ],
        out_shape=[
            jax.ShapeDtypeStruct((t, n_gate_cols), BF16),
            jax.ShapeDtypeStruct((t, n - n_gate_cols), F32),
        ],
        scratch_shapes=[pltpu.VMEM((tm, d), BF16)],
        compiler_params=_cparams(("parallel", "arbitrary")),
        name="inproj",
    )(x, nw.reshape(1, d), w)


def _hgrn_level_masks(c):
    n_lev = int(math.log2(c)) + 1
    gsz = max(1, min(n_lev, HGRN_STACK_ROWS // c))
    n_groups = -(-n_lev // gsz)
    r = gsz * c
    t = np.arange(c)[:, None]
    s = np.arange(c)[None, :]
    masks = np.zeros((n_groups, r, r), np.float32)
    for lev in range(n_lev):
        if lev == 0:
            m = (t == s)
        else:
            h = 1 << (lev - 1)
            m = ((t & h) != 0) & ((s & h) == 0) & ((t // (2 * h)) == (s // (2 * h)))
        g, i = divmod(lev, gsz)
        masks[g, i * c:(i + 1) * c, i * c:(i + 1) * c] = m
    return masks, n_lev, gsz, n_groups


def _hgrn_kernel(q_ref, f_ref, i_ref, g_ref, lb_ref, gain_ref, s0_ref, mask_ref,
                 o_ref, s_ref, st_scr, *, c, nh, n_lev, gsz, n_groups):
    ci = pl.program_id(2)
    nc = pl.num_programs(2)

    @pl.when(ci == 0)
    def _():
        for h in range(nh):
            st_scr[h] = s0_ref[0, h].T

    row = lax.broadcasted_iota(jnp.int32, (c, HEAD_DIM), 0)
    nt = (((1,), (1,)), ((), ()))
    tn = (((0,), (0,)), ((), ()))

    for h in range(nh):
        sl = slice(h * HEAD_DIM, (h + 1) * HEAD_DIM)
        z = f_ref[:, sl]
        lb = lb_ref[:, sl]
        log_sig = jnp.minimum(z, 0.0) - jnp.log1p(jnp.exp(-jnp.abs(z)))
        a0 = jnp.log(lb)
        a1 = jnp.log1p(-lb) + log_sig
        logf = jnp.maximum(a0, a1) + jnp.log1p(jnp.exp(-jnp.abs(a0 - a1)))
        k = (1.0 - lb) * jax.nn.sigmoid(-z)
        q = q_ref[:, sl]
        v = jax.nn.silu(i_ref[:, sl])
        v_bf = v.astype(BF16)

        p = logf
        tb = logf
        a_lev = [q.astype(BF16)]
        b_lev = [k.astype(BF16)]
        for lev in range(1, n_lev):
            hs = 1 << (lev - 1)
            right = (row & hs) != 0
            e = jnp.exp(jnp.where(right, p, tb - p))
            a_lev.append((q * e).astype(BF16))
            b_lev.append((k * e).astype(BF16))
            dn = pltpu.roll(tb, hs, 0)
            up = pltpu.roll(tb, c - hs, 0)
            p = p + jnp.where(right, dn, 0.0)
            tb = tb + jnp.where(right, dn, up)

        o = None
        for g in range(n_groups):
            lo, hi = g * gsz, min((g + 1) * gsz, n_lev)
            n_in = hi - lo
            a_g = jnp.concatenate(a_lev[lo:hi], axis=0) if n_in > 1 else a_lev[lo]
            b_g = jnp.concatenate(b_lev[lo:hi], axis=0) if n_in > 1 else b_lev[lo]
            sc = lax.dot_general(a_g, b_g, nt, preferred_element_type=F32)
            r = n_in * c
            sc = sc * mask_ref[g][:r, :r]
            fold = sc[0:c]
            for i in range(1, n_in):
                fold = fold + sc[i * c:(i + 1) * c]
            v_g = jnp.concatenate([v_bf] * n_in, axis=0) if n_in > 1 else v_bf
            part = jnp.dot(fold.astype(BF16), v_g, preferred_element_type=F32)
            o = part if o is None else o + part

        st = st_scr[h]
        q_in = (q * jnp.exp(p)).astype(BF16)
        o = o + lax.dot_general(q_in, st.astype(BF16), nt, preferred_element_type=F32)
        k_out = (k * jnp.exp(tb - p)).astype(BF16)
        upd = lax.dot_general(v_bf, k_out, tn, preferred_element_type=F32)
        st_scr[h] = st * jnp.exp(tb[0:1, :]) + upd

        ms = jnp.mean(o * o, axis=-1, keepdims=True)
        o = o * lax.rsqrt(ms + EPS) * gain_ref[:, sl] * jax.nn.silu(g_ref[:, sl])
        o_ref[:, sl] = o.astype(o_ref.dtype)

    @pl.when(ci == nc - 1)
    def _():
        for h in range(nh):
            s_ref[0, h] = st_scr[h].T


def _hgrn(main, lb, gain, s0, *, row_off, n_seq, seq_len, c, nh):
    dh = lb.shape[-1]
    n_heads = dh // HEAD_DIM
    w = nh * HEAD_DIM
    n_hg = n_heads // nh
    n_chunks = seq_len // c
    rb0 = row_off // c
    masks, n_lev, gsz, n_groups = _hgrn_level_masks(c)
    r = masks.shape[-1]

    def col_spec(group):
        return pl.BlockSpec(
            (c, w), lambda b, hg, ci: (rb0 + b * n_chunks + ci, group * n_hg + hg))

    vec_spec = pl.BlockSpec((1, w), lambda b, hg, ci: (0, hg))
    st_spec = pl.BlockSpec((1, nh, HEAD_DIM, HEAD_DIM), lambda b, hg, ci: (b, hg, 0, 0))
    o, s_new = pl.pallas_call(
        functools.partial(_hgrn_kernel, c=c, nh=nh, n_lev=n_lev, gsz=gsz, n_groups=n_groups),
        grid=(n_seq, n_hg, n_chunks),
        in_specs=[col_spec(0), col_spec(1), col_spec(2), col_spec(3), vec_spec, vec_spec,
                  st_spec, _resident((n_groups, r, r), lambda b, hg, ci: (0, 0, 0))],
        out_specs=[
            pl.BlockSpec((c, w), lambda b, hg, ci: (b * n_chunks + ci, hg)),
            st_spec,
        ],
        out_shape=[
            jax.ShapeDtypeStruct((n_seq * seq_len, dh), BF16),
            jax.ShapeDtypeStruct((n_seq, n_heads, HEAD_DIM, HEAD_DIM), F32),
        ],
        scratch_shapes=[pltpu.VMEM((nh, HEAD_DIM, HEAD_DIM), F32)],
        compiler_params=_cparams(("parallel", "parallel", "arbitrary")),
        name="hgrn_c%d" % c,
    )(main, main, main, main, lb.reshape(1, dh), gain.reshape(1, dh), s0, jnp.asarray(masks))
    return o, s_new


def _cmul(x, la, lb):
    return x * la + pltpu.roll(x, S5_STATE, 1) * lb


def _s5_kernel(u_ref, toep_ref, bst_ref, cst_ref, lam_ref, x0_ref,
               y_ref, sp_ref, ss_ref, *, n_p, n_b, n_cs):
    u = u_ref[0]
    y = jnp.dot(u, toep_ref[0], preferred_element_type=F32)
    z = jnp.dot(u, bst_ref[0], preferred_element_type=F32)

    x = z[0:n_p]
    rowi = lax.broadcasted_iota(jnp.int32, (n_p, 2 * S5_STATE), 0)
    for lev in range(int(math.log2(n_p))):
        sh = 1 << lev
        xs = jnp.where(rowi >= sh, pltpu.roll(x, sh, 0), 0.0)
        x = x + _cmul(xs, lam_ref[0, lev, 0:1, :], lam_ref[0, lev, 1:2, :])
    prev = [jnp.where(rowi >= 1, pltpu.roll(x, 1, 0), 0.0)]
    sp_ref[0] = x[n_p - 1:n_p]

    la0 = lam_ref[0, 0, 0:1, :]
    lb0 = lam_ref[0, 0, 1:2, :]
    xs = x0_ref[0]
    for ci in range(n_cs):
        prev.append(xs)
        xs = _cmul(xs, la0, lb0) + z[n_p + ci * n_b:n_p + (ci + 1) * n_b]
    ss_ref[0] = xs

    x_prev = jnp.concatenate(prev, axis=0).astype(BF16)
    y_ref[0] = y + jnp.dot(x_prev, cst_ref[0], preferred_element_type=F32)


def _s5(u_g, toep, bst, cst, lam, x0, *, n_p, n_b, n_cs):
    g, n_rows, cw = u_g.shape
    n_lev = lam.shape[1]
    sw = 2 * S5_STATE

    def per_group(shape):
        nd = len(shape)
        return pl.BlockSpec((1,) + shape, lambda i: (i,) + (0,) * nd)

    return pl.pallas_call(
        functools.partial(_s5_kernel, n_p=n_p, n_b=n_b, n_cs=n_cs),
        grid=(g,),
        in_specs=[per_group((n_rows, cw)), per_group((cw, cw)), per_group((cw, sw)),
                  per_group((sw, cw)), per_group((n_lev, 2, sw)), per_group((n_b, sw))],
        out_specs=[per_group((n_rows, cw)), per_group((1, sw)), per_group((n_b, sw))],
        out_shape=[
            jax.ShapeDtypeStruct((g, n_rows, cw), F32),
            jax.ShapeDtypeStruct((g, 1, sw), F32),
            jax.ShapeDtypeStruct((g, n_b, sw), F32),
        ],
        compiler_params=_cparams(("parallel",)),
        name="s5",
    )(u_g, toep, bst, cst, lam, x0)


def _s5_tables(a_log_neg_re, a_im, log_dt, b_re, b_im, c_re, c_im, n_scan_lev):
    hp = lax.Precision.HIGHEST
    lam = lax.complex(-jnp.exp(a_log_neg_re.astype(F32)), a_im.astype(F32))
    dt = jnp.exp(log_dt.astype(F32))[..., None]
    lam_dt = lam * dt
    lam_bar = jnp.exp(lam_dt)
    zoh = (lam_bar - 1.0) / lam
    b_bar = zoh[..., None] * lax.complex(b_re.astype(F32), b_im.astype(F32))
    jj = jnp.arange(S5_CHUNK + 1, dtype=F32)
    pw = jnp.exp(lam_dt[:, :, None, :] * jj[None, None, :, None])
    cc = lax.complex(c_re.astype(F32), c_im.astype(F32))
    cp = cc[:, :, None, :, :] * pw[:, :, :, None, :]
    kj = (jnp.einsum('lgjpn,lgnq->lgjpq', cp.real[:, :, :S5_CHUNK], b_bar.real, precision=hp)
          - jnp.einsum('lgjpn,lgnq->lgjpq', cp.imag[:, :, :S5_CHUNK], b_bar.imag, precision=hp))
    s_idx = np.arange(S5_CHUNK)[:, None]
    t_idx = np.arange(S5_CHUNK)[None, :]
    lag = np.clip(t_idx - s_idx, 0, S5_CHUNK - 1)
    valid = jnp.asarray((t_idx >= s_idx)[None, None, :, :, None, None])
    toep = jnp.where(valid, kj[:, :, lag], 0.0)
    l, g = toep.shape[:2]
    cw = S5_CHUNK * S5_GROUP
    toep = toep.transpose(0, 1, 2, 5, 3, 4).reshape(l, g, cw, cw)
    pb = pw[:, :, S5_CHUNK - 1::-1][:, :, :S5_CHUNK, :, None] * b_bar[:, :, None]
    pb = pb.transpose(0, 1, 2, 4, 3)
    bst = jnp.concatenate([pb.real, pb.imag], axis=-1).reshape(l, g, cw, 2 * S5_STATE)
    cn = cp[:, :, 1:].transpose(0, 1, 4, 2, 3)
    cst = jnp.concatenate([cn.real, -cn.imag], axis=2).reshape(l, g, 2 * S5_STATE, cw)
    kk = (S5_CHUNK * (2.0 ** jnp.arange(n_scan_lev, dtype=F32)))
    lp = jnp.exp(lam_dt[:, :, None, :] * kk[None, None, :, None])
    la = jnp.concatenate([lp.real, lp.real], axis=-1)
    lb = jnp.concatenate([-lp.imag, lp.imag], axis=-1)
    lam_tab = jnp.stack([la, lb], axis=3)
    return toep.astype(BF16), bst.astype(BF16), cst.astype(BF16), lam_tab


def _mix_kernel(gh_ref, gs_ref, oh_ref, y_ref, u_ref, d_ref, bglu_ref,
                wglu_ref, wbh_ref, wbs_ref, out_ref):
    ys = jax.nn.gelu(y_ref[...] + d_ref[...] * u_ref[...])
    gate = jnp.dot(ys.astype(BF16), wglu_ref[...], preferred_element_type=F32) + bglu_ref[...]
    glu = (ys * jax.nn.sigmoid(gate)).astype(BF16)
    t_h = jnp.dot(oh_ref[...], wbh_ref[...], preferred_element_type=F32)
    t_s = jnp.dot(glu, wbs_ref[...], preferred_element_type=F32)
    mix = gh_ref[...].astype(F32) * t_h + gs_ref[...].astype(F32) * t_s
    out_ref[...] = mix.astype(BF16)


def _mix(gates, o_h, y_s, main, u_col_block, d_skip, b_glu, w_glu, w_bh, w_bs, tm):
    t, ds = y_s.shape
    d = w_bh.shape[1]
    const = lambda i: (0, 0)
    return pl.pallas_call(
        _mix_kernel,
        grid=(t // tm,),
        in_specs=[
            pl.BlockSpec((tm, d), lambda i: (i, 0)),
            pl.BlockSpec((tm, d), lambda i: (i, 1)),
            pl.BlockSpec((tm, ds), lambda i: (i, 0)),
            pl.BlockSpec((tm, ds), lambda i: (i, 0)),
            pl.BlockSpec((tm, ds), lambda i: (i, u_col_block)),
            pl.BlockSpec((1, ds), const),
            pl.BlockSpec((1, ds), const),
            _resident((ds, ds), const),
            _resident((ds, d), const),
            _resident((ds, d), const),
        ],
        out_specs=pl.BlockSpec((tm, d), lambda i: (i, 0)),
        out_shape=jax.ShapeDtypeStruct((t, d), BF16),
        compiler_params=_cparams(("parallel",)),
        name="mix",
    )(gates, gates, o_h, y_s, main, d_skip.reshape(1, ds), b_glu.reshape(1, ds), w_glu, w_bh, w_bs)


def _rms(x, w):
    ms = jnp.mean(x * x, axis=-1, keepdims=True)
    return x * lax.rsqrt(ms + EPS) * w


def _ffn_kernel(x_ref, mix_ref, wout_ref, n2_ref, wg_ref, wu_ref, wd_ref, fn_ref,
                out_ref, h_scr, *, final_norm):
    j = pl.program_id(1)
    nj = pl.num_programs(1)

    @pl.when(j == 0)
    def _():
        xn = x_ref[...] + jnp.dot(mix_ref[...], wout_ref[...], preferred_element_type=F32)
        out_ref[...] = xn
        h_scr[...] = _rms(xn, n2_ref[...]).astype(BF16)

    h = h_scr[...]
    ga = jnp.dot(h, wg_ref[...], preferred_element_type=F32)
    up = jnp.dot(h, wu_ref[...], preferred_element_type=F32)
    act = (jax.nn.silu(ga) * up).astype(BF16)
    out_ref[...] += jnp.dot(act, wd_ref[...], preferred_element_type=F32)

    if final_norm:
        @pl.when(j == nj - 1)
        def _():
            out_ref[...] = _rms(out_ref[...], fn_ref[...])


def _ffn(x, mix, w_out, norm2, w_gate_up, w_down, fnorm, final_norm, tm, tf):
    t, d = x.shape
    dff = w_down.shape[0]
    nj = dff // tf
    return pl.pallas_call(
        functools.partial(_ffn_kernel, final_norm=final_norm),
        grid=(t // tm, nj),
        in_specs=[
            pl.BlockSpec((tm, d), lambda i, j: (i, 0)),
            pl.BlockSpec((tm, d), lambda i, j: (i, 0)),
            _resident((d, d), lambda i, j: (0, 0)),
            pl.BlockSpec((1, d), lambda i, j: (0, 0)),
            pl.BlockSpec((d, tf), lambda i, j: (0, j)),
            pl.BlockSpec((d, tf), lambda i, j: (0, nj + j)),
            pl.BlockSpec((tf, d), lambda i, j: (j, 0)),
            pl.BlockSpec((1, d), lambda i, j: (0, 0)),
        ],
        out_specs=pl.BlockSpec((tm, d), lambda i, j: (i, 0)),
        out_shape=jax.ShapeDtypeStruct((t, d), F32),
        scratch_shapes=[pltpu.VMEM((tm, d), BF16)],
        compiler_params=_cparams(("parallel", "arbitrary")),
        name="ffn",
    )(x, mix, w_out, norm2.reshape(1, d), w_gate_up, w_gate_up, w_down, fnorm.reshape(1, d))


def _pick(n, pref):
    t = pref
    while n % t:
        t //= 2
    return t


def _lower_bounds(lb_logits):
    p = jax.nn.softmax(lb_logits.astype(F32), axis=0)
    cs = jnp.cumsum(p, axis=0)
    return cs - cs[:1]


def kernel(x_prompt, x_sample, state_hgrn, state_s5_re, state_s5_im, lb_logits, norm1, w_in, hgrn_norm, w_bh, s5_a_log_neg_re, s5_a_im, s5_log_dt, s5_b_re, s5_b_im, s5_c_re, s5_c_im, s5_d, w_glu, b_glu, w_bs, w_out, norm2, w_gate_up, w_down, final_norm):
    depth = w_in.shape[0]
    bp, tp, d = x_prompt.shape
    bs, ts, _ = x_sample.shape
    assert bp == 1
    dh = lb_logits.shape[1]
    ds = s5_d.shape[1]
    n_groups = ds // S5_GROUP
    n_heads = dh // HEAD_DIM
    n_tok_p = bp * tp
    n_tok_s = bs * ts
    n_tok = n_tok_p + n_tok_s

    n_main = 4 * dh + ds
    w_in_b = jnp.concatenate([w_in[:, :, n_main:], w_in[:, :, :n_main]], axis=-1).astype(BF16)
    w_bh_b = w_bh.astype(BF16)
    w_glu_b = w_glu.astype(BF16)
    w_bs_b = w_bs.astype(BF16)
    w_out_b = w_out.astype(BF16)
    w_gu_b = w_gate_up.astype(BF16)
    w_dn_b = w_down.astype(BF16)
    lbs = _lower_bounds(lb_logits)

    n_p = tp // S5_CHUNK
    n_cs = ts // S5_CHUNK
    toep, bst, cst, lam_tab = _s5_tables(s5_a_log_neg_re, s5_a_im, s5_log_dt, s5_b_re, s5_b_im,
                                          s5_c_re, s5_c_im, int(math.log2(n_p)))

    tm_in = _pick(n_tok, 1024)
    tn_in = 1024
    tm_mix = _pick(n_tok, 512)
    tm_ffn = _pick(n_tok, 512)
    tf = _pick(w_down.shape[1], 512)
    c_p = _pick(tp, 128)
    c_s = _pick(ts, 128)

    x = jnp.concatenate([x_prompt.reshape(n_tok_p, d), x_sample.reshape(n_tok_s, d)], axis=0)
    zero_h = jnp.zeros((bp, n_heads, HEAD_DIM, HEAD_DIM), F32)

    new_h_p, new_re_p, new_im_p, new_h_s, new_re_s, new_im_s = [], [], [], [], [], []
    for l in range(depth):
        gates, main = _inproj(x, norm1[l], w_in_b[l], 2 * d, tm_in, tn_in)

        o_p, sh_p = _hgrn(main, lbs[l], hgrn_norm[l], zero_h,
                          row_off=0, n_seq=bp, seq_len=tp, c=c_p, nh=2)
        o_s, sh_s = _hgrn(main, lbs[l], hgrn_norm[l], state_hgrn[l],
                          row_off=n_tok_p, n_seq=bs, seq_len=ts, c=c_s, nh=n_heads)
        o_h = jnp.concatenate([o_p, o_s], axis=0)

        u = main[:, 4 * dh:].astype(BF16)
        u_p = u[:n_tok_p].reshape(n_p, S5_CHUNK, n_groups, S5_GROUP)
        u_p = u_p.transpose(2, 0, 1, 3).reshape(n_groups, n_p, S5_CHUNK * S5_GROUP)
        u_s = u[n_tok_p:].reshape(bs, n_cs, S5_CHUNK, n_groups, S5_GROUP)
        u_s = u_s.transpose(3, 1, 0, 2, 4).reshape(n_groups, n_cs * bs, S5_CHUNK * S5_GROUP)
        u_g = jnp.concatenate([u_p, u_s], axis=1)
        x0 = jnp.concatenate([state_s5_re[l], state_s5_im[l]], axis=-1).transpose(1, 0, 2)
        y_g, st_p, st_s = _s5(u_g, toep[l], bst[l], cst[l], lam_tab[l], x0,
                              n_p=n_p, n_b=bs, n_cs=n_cs)
        y_p = y_g[:, :n_p].reshape(n_groups, n_p, S5_CHUNK, S5_GROUP)
        y_p = y_p.transpose(1, 2, 0, 3).reshape(n_tok_p, ds)
        y_s = y_g[:, n_p:].reshape(n_groups, n_cs, bs, S5_CHUNK, S5_GROUP)
        y_s = y_s.transpose(2, 1, 3, 0, 4).reshape(n_tok_s, ds)
        y = jnp.concatenate([y_p, y_s], axis=0)

        mix = _mix(gates, o_h, y, main, (4 * dh) // ds, s5_d[l], b_glu[l],
                   w_glu_b[l], w_bh_b[l], w_bs_b[l], tm_mix)
        x = _ffn(x, mix, w_out_b[l], norm2[l], w_gu_b[l], w_dn_b[l], final_norm,
                 l == depth - 1, tm_ffn, tf)

        new_h_p.append(sh_p)
        new_h_s.append(sh_s)
        new_re_p.append(st_p[:, :, :S5_STATE].transpose(1, 0, 2))
        new_im_p.append(st_p[:, :, S5_STATE:].transpose(1, 0, 2))
        new_re_s.append(st_s[:, :, :S5_STATE].transpose(1, 0, 2))
        new_im_s.append(st_s[:, :, S5_STATE:].transpose(1, 0, 2))

    y_prompt = x[:n_tok_p].reshape(bp, tp, d)
    y_sample = x[n_tok_p:].reshape(bs, ts, d)
    return (y_prompt, y_sample, jnp.stack(new_h_p), jnp.stack(new_re_p), jnp.stack(new_im_p),
            jnp.stack(new_h_s), jnp.stack(new_re_s), jnp.stack(new_im_s))
```

```python
import functools
import math

import jax
import jax.numpy as jnp
import numpy as np
from jax import lax
from jax.experimental import pallas as pl
from jax.experimental.pallas import tpu as pltpu

F32 = jnp.float32
BF16 = jnp.bfloat16

EPS = 1e-6
HEAD_DIM = 128
S5_GROUP = 16
S5_STATE = 64
S5_CHUNK = 16
S5_SLAB = 8
HGRN_STACK_ROWS = 256
VMEM_LIMIT = 56 * 1024 * 1024
S5_VMEM_LIMIT = 60 * 1024 * 1024


def _cparams(sem):
    return pltpu.CompilerParams(dimension_semantics=sem, vmem_limit_bytes=VMEM_LIMIT)


def _resident(shape, index_map):
    return pl.BlockSpec(shape, index_map, pipeline_mode=pl.Buffered(1))


def _inproj_kernel(x_ref, nw_ref, w_ref, gate_ref, main_ref, h_scr, *, n_gate_tiles):
    j = pl.program_id(1)

    @pl.when(j == 0)
    def _():
        x = x_ref[...]
        ms = jnp.mean(x * x, axis=-1, keepdims=True)
        h_scr[...] = (x * lax.rsqrt(ms + EPS) * nw_ref[...]).astype(BF16)

    acc = jnp.dot(h_scr[...], w_ref[...], preferred_element_type=F32)

    @pl.when(j < n_gate_tiles)
    def _():
        gate_ref[...] = jax.nn.sigmoid(acc).astype(BF16)

    @pl.when(j >= n_gate_tiles)
    def _():
        main_ref[...] = acc


def _inproj(x, nw, w, n_gate_cols, tm, tn):
    t, d = x.shape
    n = w.shape[1]
    n_gate_tiles = n_gate_cols // tn
    n_tiles = n // tn
    return pl.pallas_call(
        functools.partial(_inproj_kernel, n_gate_tiles=n_gate_tiles),
        grid=(t // tm, n_tiles),
        in_specs=[
            pl.BlockSpec((tm, d), lambda i, j: (i, 0)),
            pl.BlockSpec((1, d), lambda i, j: (0, 0)),
            pl.BlockSpec((d, tn), lambda i, j: (0, j)),
        ],
        out_specs=[
            pl.BlockSpec((tm, tn), lambda i, j: (i, jnp.minimum(j, n_gate_tiles - 1))),
            pl.BlockSpec((tm, tn), lambda i, j: (i, jnp.maximum(j - n_gate_tiles, 0))),
        ],
        out_shape=[
            jax.ShapeDtypeStruct((t, n_gate_cols), BF16),
            jax.ShapeDtypeStruct((t, n - n_gate_cols), F32),
        ],
        scratch_shapes=[pltpu.VMEM((tm, d), BF16)],
        compiler_params=_cparams(("parallel", "arbitrary")),
        name="inproj",
    )(x, nw.reshape(1, d), w)


def _hgrn_level_masks(c):
    n_lev = int(math.log2(c)) + 1
    gsz = max(1, min(n_lev, HGRN_STACK_ROWS // c))
    n_groups = -(-n_lev // gsz)
    r = gsz * c
    t = np.arange(c)[:, None]
    s = np.arange(c)[None, :]
    masks = np.zeros((n_groups, r, r), np.float32)
    for lev in range(n_lev):
        if lev == 0:
            m = (t == s)
        else:
            h = 1 << (lev - 1)
            m = ((t & h) != 0) & ((s & h) == 0) & ((t // (2 * h)) == (s // (2 * h)))
        g, i = divmod(lev, gsz)
        masks[g, i * c:(i + 1) * c, i * c:(i + 1) * c] = m
    return masks, n_lev, gsz, n_groups


def _hgrn_kernel(q_ref, f_ref, i_ref, g_ref, lb_ref, gain_ref, s0_ref, mask_ref, o_all_ref,
                 o_ref, s_ref, st_scr, *, c, nh, n_lev, gsz, n_groups):
    ci = pl.program_id(2)
    nc = pl.num_programs(2)

    @pl.when(ci == 0)
    def _():
        for h in range(nh):
            st_scr[h] = s0_ref[0, h].T

    row = lax.broadcasted_iota(jnp.int32, (c, HEAD_DIM), 0)
    nt = (((1,), (1,)), ((), ()))
    tn = (((0,), (0,)), ((), ()))

    for h in range(nh):
        sl = slice(h * HEAD_DIM, (h + 1) * HEAD_DIM)
        z = f_ref[:, sl]
        lb = lb_ref[:, sl]
        log_sig = jnp.minimum(z, 0.0) - jnp.log1p(jnp.exp(-jnp.abs(z)))
        a0 = jnp.log(lb)
        a1 = jnp.log1p(-lb) + log_sig
        logf = jnp.maximum(a0, a1) + jnp.log1p(jnp.exp(-jnp.abs(a0 - a1)))
        k = (1.0 - lb) * jax.nn.sigmoid(-z)
        q = q_ref[:, sl]
        v = jax.nn.silu(i_ref[:, sl])
        v_bf = v.astype(BF16)

        p = logf
        tb = logf
        a_lev = [q.astype(BF16)]
        b_lev = [k.astype(BF16)]
        for lev in range(1, n_lev):
            hs = 1 << (lev - 1)
            right = (row & hs) != 0
            e = jnp.exp(jnp.where(right, p, tb - p))
            a_lev.append((q * e).astype(BF16))
            b_lev.append((k * e).astype(BF16))
            dn = pltpu.roll(tb, hs, 0)
            up = pltpu.roll(tb, c - hs, 0)
            p = p + jnp.where(right, dn, 0.0)
            tb = tb + jnp.where(right, dn, up)

        o = None
        for g in range(n_groups):
            lo, hi = g * gsz, min((g + 1) * gsz, n_lev)
            n_in = hi - lo
            a_g = jnp.concatenate(a_lev[lo:hi], axis=0) if n_in > 1 else a_lev[lo]
            b_g = jnp.concatenate(b_lev[lo:hi], axis=0) if n_in > 1 else b_lev[lo]
            sc = lax.dot_general(a_g, b_g, nt, preferred_element_type=F32)
            r = n_in * c
            sc = sc * mask_ref[g][:r, :r]
            fold = sc[0:c]
            for i in range(1, n_in):
                fold = fold + sc[i * c:(i + 1) * c]
            v_g = jnp.concatenate([v_bf] * n_in, axis=0) if n_in > 1 else v_bf
            part = jnp.dot(fold.astype(BF16), v_g, preferred_element_type=F32)
            o = part if o is None else o + part

        st = st_scr[h]
        q_in = (q * jnp.exp(p)).astype(BF16)
        o = o + lax.dot_general(q_in, st.astype(BF16), nt, preferred_element_type=F32)
        k_out = (k * jnp.exp(tb - p)).astype(BF16)
        upd = lax.dot_general(v_bf, k_out, tn, preferred_element_type=F32)
        st_scr[h] = st * jnp.exp(tb[0:1, :]) + upd

        ms = jnp.mean(o * o, axis=-1, keepdims=True)
        o = o * lax.rsqrt(ms + EPS) * gain_ref[:, sl] * jax.nn.silu(g_ref[:, sl])
        o_ref[:, sl] = o.astype(o_ref.dtype)

    @pl.when(ci == nc - 1)
    def _():
        for h in range(nh):
            s_ref[0, h] = st_scr[h].T


def _hgrn(main, lb, gain, s0, o_all, *, row_off, n_seq, seq_len, c, nh):
    dh = lb.shape[-1]
    n_heads = dh // HEAD_DIM
    w = nh * HEAD_DIM
    n_hg = n_heads // nh
    n_chunks = seq_len // c
    rb0 = row_off // c
    masks, n_lev, gsz, n_groups = _hgrn_level_masks(c)
    r = masks.shape[-1]

    def col_spec(group):
        return pl.BlockSpec(
            (c, w), lambda b, hg, ci: (rb0 + b * n_chunks + ci, group * n_hg + hg))

    vec_spec = pl.BlockSpec((1, w), lambda b, hg, ci: (0, hg))
    st_spec = pl.BlockSpec((1, nh, HEAD_DIM, HEAD_DIM), lambda b, hg, ci: (b, hg, 0, 0))
    o, s_new = pl.pallas_call(
        functools.partial(_hgrn_kernel, c=c, nh=nh, n_lev=n_lev, gsz=gsz, n_groups=n_groups),
        grid=(n_seq, n_hg, n_chunks),
        in_specs=[col_spec(0), col_spec(1), col_spec(2), col_spec(3), vec_spec, vec_spec,
                  st_spec, _resident((n_groups, r, r), lambda b, hg, ci: (0, 0, 0)),
                  pl.BlockSpec(memory_space=pl.ANY)],
        out_specs=[
            pl.BlockSpec((c, w), lambda b, hg, ci: (rb0 + b * n_chunks + ci, hg)),
            st_spec,
        ],
        out_shape=[
            jax.ShapeDtypeStruct(o_all.shape, o_all.dtype),
            jax.ShapeDtypeStruct((n_seq, n_heads, HEAD_DIM, HEAD_DIM), F32),
        ],
        input_output_aliases={8: 0},
        scratch_shapes=[pltpu.VMEM((nh, HEAD_DIM, HEAD_DIM), F32)],
        compiler_params=_cparams(("parallel", "parallel", "arbitrary")),
        name="hgrn_c%d" % c,
    )(main, main, main, main, lb.reshape(1, dh), gain.reshape(1, dh), s0, jnp.asarray(masks), o_all)
    return o, s_new


def _cmul(x, la, lb):
    return x * la + pltpu.roll(x, S5_STATE, 1) * lb


def _s5_kernel(u_ref, m_ref, b_ref, c_ref, lam_ref, x0_ref, y_ref, sp_ref, ss_ref,
               acat_scr, z_scr, xprev_scr, *, n_p, n_b, n_cs, rb):
    n_rows = n_p + n_b * n_cs
    sw = 2 * S5_STATE
    cw = S5_SLAB * S5_GROUP

    for r0 in range(0, n_rows, rb):
        a = jnp.concatenate(
            [u_ref[pl.ds(S5_CHUNK * r0 + s, rb, stride=S5_CHUNK), :].astype(BF16)
             for s in range(S5_CHUNK)], axis=1)
        acat_scr[r0:r0 + rb, :] = a
        z = jnp.dot(a, b_ref[0], preferred_element_type=F32)
        for g in range(S5_SLAB):
            z_scr[g, r0:r0 + rb, :] = z[:, g * sw:(g + 1) * sw]

    rowi = lax.broadcasted_iota(jnp.int32, (n_p, sw), 0)
    for g in range(S5_SLAB):
        ls = slice(g * sw, (g + 1) * sw)
        x = z_scr[g, 0:n_p, :]
        for lev in range(int(math.log2(n_p))):
            sh = 1 << lev
            xs = jnp.where(rowi >= sh, pltpu.roll(x, sh, 0), 0.0)
            x = x + _cmul(xs, lam_ref[0, lev, 0:1, ls], lam_ref[0, lev, 1:2, ls])
        xprev_scr[g, 0:n_p, :] = jnp.where(rowi >= 1, pltpu.roll(x, 1, 0), 0.0)
        sp_ref[0, :, ls] = x[n_p - 1:n_p]

        la0 = lam_ref[0, 0, 0:1, ls]
        lb0 = lam_ref[0, 0, 1:2, ls]
        xs = x0_ref[0, :, ls]
        for ci in range(n_cs):
            rows = pl.ds(n_p + ci, n_b, stride=n_cs)
            xprev_scr[g, rows, :] = xs
            xs = _cmul(xs, la0, lb0) + z_scr[g, rows, :]
        ss_ref[0, :, ls] = xs

    for r0 in range(0, n_rows, rb):
        x_prev = jnp.concatenate(
            [xprev_scr[g, r0:r0 + rb, :].astype(BF16) for g in range(S5_SLAB)], axis=1)
        y = (jnp.dot(acat_scr[r0:r0 + rb, :], m_ref[0], preferred_element_type=F32)
             + jnp.dot(x_prev, c_ref[0], preferred_element_type=F32))
        for t in range(S5_CHUNK):
            y_ref[pl.ds(S5_CHUNK * r0 + t, rb, stride=S5_CHUNK), :] = y[:, t * cw:(t + 1) * cw]


def _s5(main, u_col0, m_big, b_big, c_big, lam, x0, *, n_p, n_b, n_cs):
    n_slabs = m_big.shape[0]
    n_tok = main.shape[0]
    n_rows = n_p + n_b * n_cs
    assert n_rows * S5_CHUNK == n_tok
    n_lev = lam.shape[1]
    cw = S5_SLAB * S5_GROUP
    kw = S5_CHUNK * cw
    sw = S5_SLAB * 2 * S5_STATE
    rb = max(r for r in range(16, 273, 16) if n_rows % r == 0)
    cb0 = u_col0 // cw

    def per_slab(shape):
        nd = len(shape)
        return _resident((1,) + shape, lambda v: (v,) + (0,) * nd)

    return pl.pallas_call(
        functools.partial(_s5_kernel, n_p=n_p, n_b=n_b, n_cs=n_cs, rb=rb),
        grid=(n_slabs,),
        in_specs=[_resident((n_tok, cw), lambda v: (0, cb0 + v)),
                  per_slab((kw, kw)), per_slab((kw, sw)), per_slab((sw, kw)),
                  per_slab((n_lev, 2, sw)), per_slab((n_b, sw))],
        out_specs=[_resident((n_tok, cw), lambda v: (0, v)),
                   pl.BlockSpec((1, 1, sw), lambda v: (v, 0, 0)),
                   pl.BlockSpec((1, n_b, sw), lambda v: (v, 0, 0))],
        out_shape=[
            jax.ShapeDtypeStruct((n_tok, n_slabs * cw), F32),
            jax.ShapeDtypeStruct((n_slabs, 1, sw), F32),
            jax.ShapeDtypeStruct((n_slabs, n_b, sw), F32),
        ],
        scratch_shapes=[pltpu.VMEM((n_rows, kw), BF16),
                        pltpu.VMEM((S5_SLAB, n_rows, 2 * S5_STATE), F32),
                        pltpu.VMEM((S5_SLAB, n_rows, 2 * S5_STATE), F32)],
        compiler_params=pltpu.CompilerParams(dimension_semantics=("parallel",),
                                             vmem_limit_bytes=S5_VMEM_LIMIT),
        name="s5",
    )(main, m_big, b_big, c_big, lam, x0)


def _s5_tables(a_log_neg_re, a_im, log_dt, b_re, b_im, c_re, c_im, n_scan_lev):
    hp = lax.Precision.HIGHEST
    lam_re = -jnp.exp(a_log_neg_re.astype(F32))
    lam_im = a_im.astype(F32)
    dt = jnp.exp(log_dt.astype(F32))[..., None]

    def powers(jj):
        e = jj[None, None, :, None]
        mag = jnp.exp((lam_re * dt)[:, :, None, :] * e)
        ang = (lam_im * dt)[:, :, None, :] * e
        return mag * jnp.cos(ang), mag * jnp.sin(ang)

    pw_re, pw_im = powers(jnp.arange(S5_CHUNK + 1, dtype=F32))
    num_re, num_im = pw_re[:, :, 1] - 1.0, pw_im[:, :, 1]
    den = lam_re * lam_re + lam_im * lam_im
    zoh_re = ((num_re * lam_re + num_im * lam_im) / den)[..., None]
    zoh_im = ((num_im * lam_re - num_re * lam_im) / den)[..., None]
    b_re, b_im = b_re.astype(F32), b_im.astype(F32)
    bb_re = zoh_re * b_re - zoh_im * b_im
    bb_im = zoh_re * b_im + zoh_im * b_re
    c_re, c_im = c_re.astype(F32)[:, :, None], c_im.astype(F32)[:, :, None]
    pr, pi = pw_re[:, :, :, None, :], pw_im[:, :, :, None, :]
    cp_re = c_re * pr - c_im * pi
    cp_im = c_re * pi + c_im * pr
    kj = (jnp.einsum('lgjpn,lgnq->lgjpq', cp_re[:, :, :S5_CHUNK], bb_re, precision=hp)
          - jnp.einsum('lgjpn,lgnq->lgjpq', cp_im[:, :, :S5_CHUNK], bb_im, precision=hp))
    s_idx = np.arange(S5_CHUNK)[:, None]
    t_idx = np.arange(S5_CHUNK)[None, :]
    lag = np.clip(t_idx - s_idx, 0, S5_CHUNK - 1)
    valid = jnp.asarray((t_idx >= s_idx)[None, None, :, :, None, None])
    toep = jnp.where(valid, kj[:, :, lag], 0.0)
    toep = toep.transpose(0, 1, 2, 5, 3, 4).astype(BF16)
    qr, qi = powers(jnp.asarray(np.arange(S5_CHUNK - 1, -1, -1), F32))
    qr, qi = qr[..., None], qi[..., None]
    pb_re = (qr * bb_re[:, :, None] - qi * bb_im[:, :, None]).transpose(0, 1, 2, 4, 3)
    pb_im = (qr * bb_im[:, :, None] + qi * bb_re[:, :, None]).transpose(0, 1, 2, 4, 3)
    bst = jnp.concatenate([pb_re, pb_im], axis=-1).astype(BF16)
    cn_re = cp_re[:, :, 1:].transpose(0, 1, 4, 2, 3)
    cn_im = cp_im[:, :, 1:].transpose(0, 1, 4, 2, 3)
    cst = jnp.concatenate([cn_re, -cn_im], axis=2).astype(BF16)
    lp_re, lp_im = powers(S5_CHUNK * (2.0 ** jnp.arange(n_scan_lev, dtype=F32)))
    la = jnp.concatenate([lp_re, lp_re], axis=-1)
    lb = jnp.concatenate([-lp_im, lp_im], axis=-1)
    lam_tab = jnp.stack([la, lb], axis=3)

    l, g = toep.shape[:2]
    v, sg = g // S5_SLAB, S5_SLAB
    eye = jnp.eye(sg, dtype=BF16)
    kw = S5_CHUNK * sg * S5_GROUP
    sw = sg * 2 * S5_STATE
    tv = toep.reshape(l, v, sg, S5_CHUNK, S5_GROUP, S5_CHUNK, 1, S5_GROUP)
    m_big = (tv * eye[None, None, :, None, None, None, :, None])
    m_big = m_big.transpose(0, 1, 3, 2, 4, 5, 6, 7).reshape(l, v, kw, kw)
    bv = bst.reshape(l, v, sg, S5_CHUNK, S5_GROUP, 1, 2 * S5_STATE)
    b_big = (bv * eye[None, None, :, None, None, :, None])
    b_big = b_big.transpose(0, 1, 3, 2, 4, 5, 6).reshape(l, v, kw, sw)
    cv = cst.reshape(l, v, sg, 2 * S5_STATE, S5_CHUNK, 1, S5_GROUP)
    c_big = (cv * eye[None, None, :, None, None, :, None]).reshape(l, v, sw, kw)
    lam_big = lam_tab.reshape(l, v, sg, n_scan_lev, 2, 2 * S5_STATE)
    lam_big = lam_big.transpose(0, 1, 3, 4, 2, 5).reshape(l, v, n_scan_lev, 2, sw)
    return m_big, b_big, c_big, lam_big


def _mix_kernel(gh_ref, gs_ref, oh_ref, y_ref, u_ref, d_ref, bglu_ref,
                wglu_ref, wbh_ref, wbs_ref, out_ref):
    ys = jax.nn.gelu(y_ref[...] + d_ref[...] * u_ref[...])
    gate = jnp.dot(ys.astype(BF16), wglu_ref[...], preferred_element_type=F32) + bglu_ref[...]
    glu = (ys * jax.nn.sigmoid(gate)).astype(BF16)
    t_h = jnp.dot(oh_ref[...], wbh_ref[...], preferred_element_type=F32)
    t_s = jnp.dot(glu, wbs_ref[...], preferred_element_type=F32)
    mix = gh_ref[...].astype(F32) * t_h + gs_ref[...].astype(F32) * t_s
    out_ref[...] = mix.astype(BF16)


def _mix(gates, o_h, y_s, main, u_col_block, d_skip, b_glu, w_glu, w_bh, w_bs, tm):
    t, ds = y_s.shape
    d = w_bh.shape[1]
    const = lambda i: (0, 0)
    return pl.pallas_call(
        _mix_kernel,
        grid=(t // tm,),
        in_specs=[
            pl.BlockSpec((tm, d), lambda i: (i, 0)),
            pl.BlockSpec((tm, d), lambda i: (i, 1)),
            pl.BlockSpec((tm, ds), lambda i: (i, 0)),
            pl.BlockSpec((tm, ds), lambda i: (i, 0)),
            pl.BlockSpec((tm, ds), lambda i: (i, u_col_block)),
            pl.BlockSpec((1, ds), const),
            pl.BlockSpec((1, ds), const),
            _resident((ds, ds), const),
            _resident((ds, d), const),
            _resident((ds, d), const),
        ],
        out_specs=pl.BlockSpec((tm, d), lambda i: (i, 0)),
        out_shape=jax.ShapeDtypeStruct((t, d), BF16),
        compiler_params=_cparams(("parallel",)),
        name="mix",
    )(gates, gates, o_h, y_s, main, d_skip.reshape(1, ds), b_glu.reshape(1, ds), w_glu, w_bh, w_bs)


def _rms(x, w):
    ms = jnp.mean(x * x, axis=-1, keepdims=True)
    return x * lax.rsqrt(ms + EPS) * w


def _ffn_kernel(x_ref, mix_ref, wout_ref, n2_ref, wg_ref, wu_ref, wd_ref, fn_ref,
                out_ref, h_scr, *, final_norm):
    j = pl.program_id(1)
    nj = pl.num_programs(1)

    @pl.when(j == 0)
    def _():
        xn = x_ref[...] + jnp.dot(mix_ref[...], wout_ref[...], preferred_element_type=F32)
        out_ref[...] = xn
        h_scr[...] = _rms(xn, n2_ref[...]).astype(BF16)

    h = h_scr[...]
    ga = jnp.dot(h, wg_ref[...], preferred_element_type=F32)
    up = jnp.dot(h, wu_ref[...], preferred_element_type=F32)
    act = (jax.nn.silu(ga) * up).astype(BF16)
    out_ref[...] += jnp.dot(act, wd_ref[...], preferred_element_type=F32)

    if final_norm:
        @pl.when(j == nj - 1)
        def _():
            out_ref[...] = _rms(out_ref[...], fn_ref[...])


def _ffn(x, mix, w_out, norm2, w_gate_up, w_down, fnorm, final_norm, tm, tf):
    t, d = x.shape
    dff = w_down.shape[0]
    nj = dff // tf
    return pl.pallas_call(
        functools.partial(_ffn_kernel, final_norm=final_norm),
        grid=(t // tm, nj),
        in_specs=[
            pl.BlockSpec((tm, d), lambda i, j: (i, 0)),
            pl.BlockSpec((tm, d), lambda i, j: (i, 0)),
            _resident((d, d), lambda i, j: (0, 0)),
            pl.BlockSpec((1, d), lambda i, j: (0, 0)),
            pl.BlockSpec((d, tf), lambda i, j: (0, j)),
            pl.BlockSpec((d, tf), lambda i, j: (0, nj + j)),
            pl.BlockSpec((tf, d), lambda i, j: (j, 0)),
            pl.BlockSpec((1, d), lambda i, j: (0, 0)),
        ],
        out_specs=pl.BlockSpec((tm, d), lambda i, j: (i, 0)),
        out_shape=jax.ShapeDtypeStruct((t, d), F32),
        scratch_shapes=[pltpu.VMEM((tm, d), BF16)],
        compiler_params=_cparams(("parallel", "arbitrary")),
        name="ffn",
    )(x, mix, w_out, norm2.reshape(1, d), w_gate_up, w_gate_up, w_down, fnorm.reshape(1, d))


def _pick(n, pref):
    t = pref
    while n % t:
        t //= 2
    return t


def _lower_bounds(lb_logits):
    p = jax.nn.softmax(lb_logits.astype(F32), axis=0)
    cs = jnp.cumsum(p, axis=0)
    return cs - cs[:1]


def kernel(x_prompt, x_sample, state_hgrn, state_s5_re, state_s5_im, lb_logits, norm1, w_in, hgrn_norm, w_bh, s5_a_log_neg_re, s5_a_im, s5_log_dt, s5_b_re, s5_b_im, s5_c_re, s5_c_im, s5_d, w_glu, b_glu, w_bs, w_out, norm2, w_gate_up, w_down, final_norm):
    depth = w_in.shape[0]
    bp, tp, d = x_prompt.shape
    bs, ts, _ = x_sample.shape
    assert bp == 1
    dh = lb_logits.shape[1]
    ds = s5_d.shape[1]
    n_groups = ds // S5_GROUP
    n_heads = dh // HEAD_DIM
    n_tok_p = bp * tp
    n_tok_s = bs * ts
    n_tok = n_tok_p + n_tok_s

    n_main = 4 * dh + ds
    w_in_b = jnp.concatenate([w_in[:, :, n_main:], w_in[:, :, :n_main]], axis=-1).astype(BF16)
    w_bh_b = w_bh.astype(BF16)
    w_glu_b = w_glu.astype(BF16)
    w_bs_b = w_bs.astype(BF16)
    w_out_b = w_out.astype(BF16)
    w_gu_b = w_gate_up.astype(BF16)
    w_dn_b = w_down.astype(BF16)
    lbs = _lower_bounds(lb_logits)

    n_p = tp // S5_CHUNK
    n_cs = ts // S5_CHUNK
    m_big, b_big, c_big, lam_big = _s5_tables(s5_a_log_neg_re, s5_a_im, s5_log_dt, s5_b_re,
                                              s5_b_im, s5_c_re, s5_c_im, int(math.log2(n_p)))
    n_slabs = n_groups // S5_SLAB
    sw = S5_SLAB * 2 * S5_STATE

    def pack_state(re, im):
        x0 = jnp.concatenate([re, im], axis=-1)
        return x0.reshape(x0.shape[0], n_slabs, sw).transpose(1, 0, 2)

    def unpack_state(st):
        st = st.transpose(1, 0, 2).reshape(st.shape[1], n_groups, 2, S5_STATE)
        return st[:, :, 0], st[:, :, 1]

    tm_in = _pick(n_tok, 1024)
    tn_in = 1024
    tm_mix = _pick(n_tok, 512)
    tm_ffn = _pick(n_tok, 512)
    tf = _pick(w_down.shape[1], 512)
    c_p = _pick(tp, 128)
    c_s = _pick(ts, 128)

    x = jnp.concatenate([x_prompt.reshape(n_tok_p, d), x_sample.reshape(n_tok_s, d)], axis=0)
    zero_h = jnp.zeros((bp, n_heads, HEAD_DIM, HEAD_DIM), F32)

    new_h_p, new_re_p, new_im_p, new_h_s, new_re_s, new_im_s = [], [], [], [], [], []
    for l in range(depth):
        gates, main = _inproj(x, norm1[l], w_in_b[l], 2 * d, tm_in, tn_in)

        o_h = jnp.zeros((n_tok, dh), BF16)
        o_h, sh_p = _hgrn(main, lbs[l], hgrn_norm[l], zero_h, o_h,
                          row_off=0, n_seq=bp, seq_len=tp, c=c_p, nh=2)
        o_h, sh_s = _hgrn(main, lbs[l], hgrn_norm[l], state_hgrn[l], o_h,
                          row_off=n_tok_p, n_seq=bs, seq_len=ts, c=c_s, nh=n_heads)

        y, st_p, st_s = _s5(main, 4 * dh, m_big[l], b_big[l], c_big[l], lam_big[l],
                            pack_state(state_s5_re[l], state_s5_im[l]),
                            n_p=n_p, n_b=bs, n_cs=n_cs)

        mix = _mix(gates, o_h, y, main, (4 * dh) // ds, s5_d[l], b_glu[l],
                   w_glu_b[l], w_bh_b[l], w_bs_b[l], tm_mix)
        x = _ffn(x, mix, w_out_b[l], norm2[l], w_gu_b[l], w_dn_b[l], final_norm,
                 l == depth - 1, tm_ffn, tf)

        new_h_p.append(sh_p)
        new_h_s.append(sh_s)
        re_p, im_p = unpack_state(st_p)
        re_s, im_s = unpack_state(st_s)
        new_re_p.append(re_p)
        new_im_p.append(im_p)
        new_re_s.append(re_s)
        new_im_s.append(im_s)

    y_prompt = x[:n_tok_p].reshape(bp, tp, d)
    y_sample = x[n_tok_p:].reshape(bs, ts, d)
    return (y_prompt, y_sample, jnp.stack(new_h_p), jnp.stack(new_re_p), jnp.stack(new_im_p),
            jnp.stack(new_h_s), jnp.stack(new_re_s), jnp.stack(new_im_s))
```

```python
import functools
import math

import jax
import jax.numpy as jnp
import numpy as np
from jax import lax
from jax.experimental import pallas as pl
from jax.experimental.pallas import tpu as pltpu

F32 = jnp.float32
BF16 = jnp.bfloat16

EPS = 1e-6
HEAD_DIM = 128
S5_GROUP = 16
S5_STATE = 64
S5_CHUNK = 16
S5_SLAB = 8
HGRN_STACK_ROWS = 256
VMEM_LIMIT = 56 * 1024 * 1024
S5_VMEM_LIMIT = 60 * 1024 * 1024


def _cparams(sem):
    return pltpu.CompilerParams(dimension_semantics=sem, vmem_limit_bytes=VMEM_LIMIT)


def _resident(shape, index_map):
    return pl.BlockSpec(shape, index_map, pipeline_mode=pl.Buffered(1))


def _inproj_kernel(x_ref, nw_ref, w_ref, gate_ref, main_ref, h_scr, *, n_gate_tiles):
    j = pl.program_id(1)

    @pl.when(j == 0)
    def _():
        x = x_ref[...]
        ms = jnp.mean(x * x, axis=-1, keepdims=True)
        h_scr[...] = (x * lax.rsqrt(ms + EPS) * nw_ref[...]).astype(BF16)

    acc = jnp.dot(h_scr[...], w_ref[...], preferred_element_type=F32)

    @pl.when(j < n_gate_tiles)
    def _():
        gate_ref[...] = jax.nn.sigmoid(acc).astype(BF16)

    @pl.when(j >= n_gate_tiles)
    def _():
        main_ref[...] = acc


def _inproj(x, nw, w, n_gate_cols, tm, tn):
    t, d = x.shape
    n = w.shape[1]
    n_gate_tiles = n_gate_cols // tn
    n_tiles = n // tn
    return pl.pallas_call(
        functools.partial(_inproj_kernel, n_gate_tiles=n_gate_tiles),
        grid=(t // tm, n_tiles),
        in_specs=[
            pl.BlockSpec((tm, d), lambda i, j: (i, 0)),
            pl.BlockSpec((1, d), lambda i, j: (0, 0)),
            pl.BlockSpec((d, tn), lambda i, j: (0, j)),
        ],
        out_specs=[
            pl.BlockSpec((tm, tn), lambda i, j: (i, jnp.minimum(j, n_gate_tiles - 1))),
            pl.BlockSpec((tm, tn), lambda i, j: (i, jnp.maximum(j - n_gate_tiles, 0))),
        ],
        out_shape=[
            jax.ShapeDtypeStruct((t, n_gate_cols), BF16),
            jax.ShapeDtypeStruct((t, n - n_gate_cols), F32),
        ],
        scratch_shapes=[pltpu.VMEM((tm, d), BF16)],
        compiler_params=_cparams(("parallel", "arbitrary")),
        name="inproj",
    )(x, nw.reshape(1, d), w)


def _hgrn_level_masks(c):
    n_lev = int(math.log2(c)) + 1
    gsz = max(1, min(n_lev, HGRN_STACK_ROWS // c))
    n_groups = -(-n_lev // gsz)
    r = gsz * c
    t = np.arange(c)[:, None]
    s = np.arange(c)[None, :]
    masks = np.zeros((n_groups, r, r), np.float32)
    for lev in range(n_lev):
        if lev == 0:
            m = (t == s)
        else:
            h = 1 << (lev - 1)
            m = ((t & h) != 0) & ((s & h) == 0) & ((t // (2 * h)) == (s // (2 * h)))
        g, i = divmod(lev, gsz)
        masks[g, i * c:(i + 1) * c, i * c:(i + 1) * c] = m
    return masks, n_lev, gsz, n_groups


def _hgrn_kernel(q_ref, f_ref, i_ref, g_ref, lb_ref, gain_ref, s0_ref, mask_ref, o_all_ref,
                 o_ref, s_ref, st_scr, *, c, nh, n_lev, gsz, n_groups):
    ci = pl.program_id(2)
    nc = pl.num_programs(2)

    @pl.when(ci == 0)
    def _():
        for h in range(nh):
            st_scr[h] = s0_ref[0, h].T

    row = lax.broadcasted_iota(jnp.int32, (c, HEAD_DIM), 0)
    nt = (((1,), (1,)), ((), ()))
    tn = (((0,), (0,)), ((), ()))

    for h in range(nh):
        sl = slice(h * HEAD_DIM, (h + 1) * HEAD_DIM)
        z = f_ref[:, sl]
        lb = lb_ref[:, sl]
        log_sig = jnp.minimum(z, 0.0) - jnp.log1p(jnp.exp(-jnp.abs(z)))
        a0 = jnp.log(lb)
        a1 = jnp.log1p(-lb) + log_sig
        logf = jnp.maximum(a0, a1) + jnp.log1p(jnp.exp(-jnp.abs(a0 - a1)))
        k = (1.0 - lb) * jax.nn.sigmoid(-z)
        q = q_ref[:, sl]
        v = jax.nn.silu(i_ref[:, sl])
        v_bf = v.astype(BF16)

        p = logf
        tb = logf
        a_lev = [q.astype(BF16)]
        b_lev = [k.astype(BF16)]
        for lev in range(1, n_lev):
            hs = 1 << (lev - 1)
            right = (row & hs) != 0
            e = jnp.exp(jnp.where(right, p, tb - p))
            a_lev.append((q * e).astype(BF16))
            b_lev.append((k * e).astype(BF16))
            dn = pltpu.roll(tb, hs, 0)
            up = pltpu.roll(tb, c - hs, 0)
            p = p + jnp.where(right, dn, 0.0)
            tb = tb + jnp.where(right, dn, up)

        o = None
        for g in range(n_groups):
            lo, hi = g * gsz, min((g + 1) * gsz, n_lev)
            n_in = hi - lo
            a_g = jnp.concatenate(a_lev[lo:hi], axis=0) if n_in > 1 else a_lev[lo]
            b_g = jnp.concatenate(b_lev[lo:hi], axis=0) if n_in > 1 else b_lev[lo]
            sc = lax.dot_general(a_g, b_g, nt, preferred_element_type=F32)
            r = n_in * c
            sc = sc * mask_ref[g][:r, :r]
            fold = sc[0:c]
            for i in range(1, n_in):
                fold = fold + sc[i * c:(i + 1) * c]
            v_g = jnp.concatenate([v_bf] * n_in, axis=0) if n_in > 1 else v_bf
            part = jnp.dot(fold.astype(BF16), v_g, preferred_element_type=F32)
            o = part if o is None else o + part

        st = st_scr[h]
        q_in = (q * jnp.exp(p)).astype(BF16)
        o = o + lax.dot_general(q_in, st.astype(BF16), nt, preferred_element_type=F32)
        k_out = (k * jnp.exp(tb - p)).astype(BF16)
        upd = lax.dot_general(v_bf, k_out, tn, preferred_element_type=F32)
        st_scr[h] = st * jnp.exp(tb[0:1, :]) + upd

        ms = jnp.mean(o * o, axis=-1, keepdims=True)
        o = o * lax.rsqrt(ms + EPS) * gain_ref[:, sl] * jax.nn.silu(g_ref[:, sl])
        o_ref[:, sl] = o.astype(o_ref.dtype)

    @pl.when(ci == nc - 1)
    def _():
        for h in range(nh):
            s_ref[0, h] = st_scr[h].T


def _hgrn(main, lb, gain, s0, o_all, *, row_off, n_seq, seq_len, c, nh):
    dh = lb.shape[-1]
    n_heads = dh // HEAD_DIM
    w = nh * HEAD_DIM
    n_hg = n_heads // nh
    n_chunks = seq_len // c
    rb0 = row_off // c
    masks, n_lev, gsz, n_groups = _hgrn_level_masks(c)
    r = masks.shape[-1]

    def col_spec(group):
        return pl.BlockSpec(
            (c, w), lambda b, hg, ci: (rb0 + b * n_chunks + ci, group * n_hg + hg))

    vec_spec = pl.BlockSpec((1, w), lambda b, hg, ci: (0, hg))
    st_spec = pl.BlockSpec((1, nh, HEAD_DIM, HEAD_DIM), lambda b, hg, ci: (b, hg, 0, 0))
    o, s_new = pl.pallas_call(
        functools.partial(_hgrn_kernel, c=c, nh=nh, n_lev=n_lev, gsz=gsz, n_groups=n_groups),
        grid=(n_seq, n_hg, n_chunks),
        in_specs=[col_spec(0), col_spec(1), col_spec(2), col_spec(3), vec_spec, vec_spec,
                  st_spec, _resident((n_groups, r, r), lambda b, hg, ci: (0, 0, 0)),
                  pl.BlockSpec(memory_space=pl.ANY)],
        out_specs=[
            pl.BlockSpec((c, w), lambda b, hg, ci: (rb0 + b * n_chunks + ci, hg)),
            st_spec,
        ],
        out_shape=[
            jax.ShapeDtypeStruct(o_all.shape, o_all.dtype),
            jax.ShapeDtypeStruct((n_seq, n_heads, HEAD_DIM, HEAD_DIM), F32),
        ],
        input_output_aliases={8: 0},
        scratch_shapes=[pltpu.VMEM((nh, HEAD_DIM, HEAD_DIM), F32)],
        compiler_params=_cparams(("parallel", "parallel", "arbitrary")),
        name="hgrn_c%d" % c,
    )(main, main, main, main, lb.reshape(1, dh), gain.reshape(1, dh), s0, jnp.asarray(masks), o_all)
    return o, s_new


def _cmul(x, la, lb):
    return x * la + pltpu.roll(x, S5_STATE, 1) * lb


def _s5_kernel(u_ref, kt_ref, pb_ref, ct_ref, lam_ref, x0_ref, y_ref, sp_ref, ss_ref,
               m_scr, b_scr, ct_scr, acat_scr, z_scr, xprev_scr, *, n_p, n_b, n_cs, rb):
    n_rows = n_p + n_b * n_cs
    sw = 2 * S5_STATE
    cw = S5_SLAB * S5_GROUP

    gi = lax.broadcasted_iota(jnp.int32, (cw, cw), 0) // S5_GROUP
    gj = lax.broadcasted_iota(jnp.int32, (cw, cw), 1) // S5_GROUP
    same_group = gi == gj
    lag_tiles = [
        jnp.where(same_group, jnp.concatenate([kt_ref[0, j]] * S5_SLAB, axis=0), 0.0).astype(BF16)
        for j in range(S5_CHUNK)]
    zero_tile = jnp.zeros((cw, cw), BF16)
    for s in range(S5_CHUNK):
        for t in range(S5_CHUNK):
            m_scr[s * cw:(s + 1) * cw, t * cw:(t + 1) * cw] = (
                lag_tiles[t - s] if t >= s else zero_tile)
    ri = lax.broadcasted_iota(jnp.int32, (cw, S5_SLAB * sw), 0) // S5_GROUP
    ci = lax.broadcasted_iota(jnp.int32, (cw, S5_SLAB * sw), 1) // sw
    own_state = ri == ci
    for s in range(S5_CHUNK):
        b_scr[s * cw:(s + 1) * cw, :] = jnp.where(
            own_state, jnp.concatenate([pb_ref[0, s]] * S5_SLAB, axis=1), 0.0).astype(BF16)
        ct_scr[s * cw:(s + 1) * cw, :] = jnp.where(
            own_state, jnp.concatenate([ct_ref[0, s]] * S5_SLAB, axis=1), 0.0).astype(BF16)

    for r0 in range(0, n_rows, rb):
        a = jnp.concatenate(
            [u_ref[pl.ds(S5_CHUNK * r0 + s, rb, stride=S5_CHUNK), :].astype(BF16)
             for s in range(S5_CHUNK)], axis=1)
        acat_scr[r0:r0 + rb, :] = a
        z = jnp.dot(a, b_scr[...], preferred_element_type=F32)
        for g in range(S5_SLAB):
            z_scr[g, r0:r0 + rb, :] = z[:, g * sw:(g + 1) * sw]

    rowi = lax.broadcasted_iota(jnp.int32, (n_p, sw), 0)
    for g in range(S5_SLAB):
        ls = slice(g * sw, (g + 1) * sw)
        x = z_scr[g, 0:n_p, :]
        for lev in range(int(math.log2(n_p))):
            sh = 1 << lev
            xs = jnp.where(rowi >= sh, pltpu.roll(x, sh, 0), 0.0)
            x = x + _cmul(xs, lam_ref[0, lev, 0:1, ls], lam_ref[0, lev, 1:2, ls])
        xprev_scr[g, 0:n_p, :] = jnp.where(rowi >= 1, pltpu.roll(x, 1, 0), 0.0)
        sp_ref[0, :, ls] = x[n_p - 1:n_p]

        la0 = lam_ref[0, 0, 0:1, ls]
        lb0 = lam_ref[0, 0, 1:2, ls]
        xs = x0_ref[0, :, ls]
        for ci in range(n_cs):
            rows = pl.ds(n_p + ci, n_b, stride=n_cs)
            xprev_scr[g, rows, :] = xs
            xs = _cmul(xs, la0, lb0) + z_scr[g, rows, :]
        ss_ref[0, :, ls] = xs

    for r0 in range(0, n_rows, rb):
        x_prev = jnp.concatenate(
            [xprev_scr[g, r0:r0 + rb, :].astype(BF16) for g in range(S5_SLAB)], axis=1)
        y = (jnp.dot(acat_scr[r0:r0 + rb, :], m_scr[...], preferred_element_type=F32)
             + lax.dot_general(x_prev, ct_scr[...], (((1,), (1,)), ((), ())),
                               preferred_element_type=F32))
        for t in range(S5_CHUNK):
            y_ref[pl.ds(S5_CHUNK * r0 + t, rb, stride=S5_CHUNK), :] = y[:, t * cw:(t + 1) * cw]


def _s5(main, u_col0, kt, pb, ct, lam, x0, *, n_p, n_b, n_cs):
    n_slabs = kt.shape[0]
    n_tok = main.shape[0]
    n_rows = n_p + n_b * n_cs
    assert n_rows * S5_CHUNK == n_tok
    n_lev = lam.shape[1]
    cw = S5_SLAB * S5_GROUP
    kw = S5_CHUNK * cw
    sw = S5_SLAB * 2 * S5_STATE
    rb = max(r for r in range(16, 273, 16) if n_rows % r == 0)
    cb0 = u_col0 // cw

    def per_slab(shape):
        nd = len(shape)
        return pl.BlockSpec((1,) + shape, lambda v: (v,) + (0,) * nd)

    return pl.pallas_call(
        functools.partial(_s5_kernel, n_p=n_p, n_b=n_b, n_cs=n_cs, rb=rb),
        grid=(n_slabs,),
        in_specs=[_resident((n_tok, cw), lambda v: (0, cb0 + v)),
                  per_slab((S5_CHUNK, S5_GROUP, cw)),
                  per_slab((S5_CHUNK, cw, 2 * S5_STATE)),
                  per_slab((S5_CHUNK, cw, 2 * S5_STATE)),
                  per_slab((n_lev, 2, sw)), per_slab((n_b, sw))],
        out_specs=[_resident((n_tok, cw), lambda v: (0, v)),
                   pl.BlockSpec((1, 1, sw), lambda v: (v, 0, 0)),
                   pl.BlockSpec((1, n_b, sw), lambda v: (v, 0, 0))],
        out_shape=[
            jax.ShapeDtypeStruct((n_tok, n_slabs * cw), F32),
            jax.ShapeDtypeStruct((n_slabs, 1, sw), F32),
            jax.ShapeDtypeStruct((n_slabs, n_b, sw), F32),
        ],
        scratch_shapes=[pltpu.VMEM((kw, kw), BF16), pltpu.VMEM((kw, sw), BF16),
                        pltpu.VMEM((kw, sw), BF16), pltpu.VMEM((n_rows, kw), BF16),
                        pltpu.VMEM((S5_SLAB, n_rows, 2 * S5_STATE), F32),
                        pltpu.VMEM((S5_SLAB, n_rows, 2 * S5_STATE), F32)],
        compiler_params=pltpu.CompilerParams(dimension_semantics=("parallel",),
                                             vmem_limit_bytes=S5_VMEM_LIMIT),
        name="s5",
    )(main, kt, pb, ct, lam, x0)


def _s5_tables(a_log_neg_re, a_im, log_dt, b_re, b_im, c_re, c_im, n_scan_lev):
    hp = lax.Precision.HIGHEST
    lam_re = -jnp.exp(a_log_neg_re.astype(F32))
    lam_im = a_im.astype(F32)
    dt = jnp.exp(log_dt.astype(F32))[..., None]

    def powers(jj):
        e = jj[None, None, :, None]
        mag = jnp.exp((lam_re * dt)[:, :, None, :] * e)
        ang = (lam_im * dt)[:, :, None, :] * e
        return mag * jnp.cos(ang), mag * jnp.sin(ang)

    pw_re, pw_im = powers(jnp.arange(S5_CHUNK + 1, dtype=F32))
    num_re, num_im = pw_re[:, :, 1] - 1.0, pw_im[:, :, 1]
    den = lam_re * lam_re + lam_im * lam_im
    zoh_re = ((num_re * lam_re + num_im * lam_im) / den)[..., None]
    zoh_im = ((num_im * lam_re - num_re * lam_im) / den)[..., None]
    b_re, b_im = b_re.astype(F32), b_im.astype(F32)
    bb_re = zoh_re * b_re - zoh_im * b_im
    bb_im = zoh_re * b_im + zoh_im * b_re
    c_re, c_im = c_re.astype(F32)[:, :, None], c_im.astype(F32)[:, :, None]
    pr, pi = pw_re[:, :, :, None, :], pw_im[:, :, :, None, :]
    cp_re = c_re * pr - c_im * pi
    cp_im = c_re * pi + c_im * pr
    kj = (jnp.einsum('lgjpn,lgnq->lgjpq', cp_re[:, :, :S5_CHUNK], bb_re, precision=hp)
          - jnp.einsum('lgjpn,lgnq->lgjpq', cp_im[:, :, :S5_CHUNK], bb_im, precision=hp))
    qr, qi = powers(jnp.asarray(np.arange(S5_CHUNK - 1, -1, -1), F32))
    qr, qi = qr[..., None], qi[..., None]
    pb_re = (qr * bb_re[:, :, None] - qi * bb_im[:, :, None]).transpose(0, 1, 2, 4, 3)
    pb_im = (qr * bb_im[:, :, None] + qi * bb_re[:, :, None]).transpose(0, 1, 2, 4, 3)
    bst = jnp.concatenate([pb_re, pb_im], axis=-1)
    cst = jnp.concatenate([cp_re[:, :, 1:], -cp_im[:, :, 1:]], axis=-1)
    lp_re, lp_im = powers(S5_CHUNK * (2.0 ** jnp.arange(n_scan_lev, dtype=F32)))
    la = jnp.concatenate([lp_re, lp_re], axis=-1)
    lb = jnp.concatenate([-lp_im, lp_im], axis=-1)
    lam_tab = jnp.stack([la, lb], axis=3)

    l, g = kj.shape[:2]
    v, sg = g // S5_SLAB, S5_SLAB
    cw = sg * S5_GROUP
    sw = sg * 2 * S5_STATE
    kt = kj.reshape(l, v, sg, S5_CHUNK, S5_GROUP, S5_GROUP)
    kt = kt.transpose(0, 1, 3, 5, 2, 4).reshape(l, v, S5_CHUNK, S5_GROUP, cw)
    pb = bst.reshape(l, v, sg, S5_CHUNK, S5_GROUP, 2 * S5_STATE)
    pb = pb.transpose(0, 1, 3, 2, 4, 5).reshape(l, v, S5_CHUNK, cw, 2 * S5_STATE)
    ct = cst.reshape(l, v, sg, S5_CHUNK, S5_GROUP, 2 * S5_STATE)
    ct = ct.transpose(0, 1, 3, 2, 4, 5).reshape(l, v, S5_CHUNK, cw, 2 * S5_STATE)
    lam_big = lam_tab.reshape(l, v, sg, n_scan_lev, 2, 2 * S5_STATE)
    lam_big = lam_big.transpose(0, 1, 3, 4, 2, 5).reshape(l, v, n_scan_lev, 2, sw)
    return kt, pb, ct, lam_big


def _mix_kernel(gh_ref, gs_ref, oh_ref, y_ref, u_ref, d_ref, bglu_ref,
                wglu_ref, wbh_ref, wbs_ref, out_ref):
    ys = jax.nn.gelu(y_ref[...] + d_ref[...] * u_ref[...])
    gate = jnp.dot(ys.astype(BF16), wglu_ref[...], preferred_element_type=F32) + bglu_ref[...]
    glu = (ys * jax.nn.sigmoid(gate)).astype(BF16)
    t_h = jnp.dot(oh_ref[...], wbh_ref[...], preferred_element_type=F32)
    t_s = jnp.dot(glu, wbs_ref[...], preferred_element_type=F32)
    mix = gh_ref[...].astype(F32) * t_h + gs_ref[...].astype(F32) * t_s
    out_ref[...] = mix.astype(BF16)


def _mix(gates, o_h, y_s, main, u_col_block, d_skip, b_glu, w_glu, w_bh, w_bs, tm):
    t, ds = y_s.shape
    d = w_bh.shape[1]
    const = lambda i: (0, 0)
    return pl.pallas_call(
        _mix_kernel,
        grid=(t // tm,),
        in_specs=[
            pl.BlockSpec((tm, d), lambda i: (i, 0)),
            pl.BlockSpec((tm, d), lambda i: (i, 1)),
            pl.BlockSpec((tm, ds), lambda i: (i, 0)),
            pl.BlockSpec((tm, ds), lambda i: (i, 0)),
            pl.BlockSpec((tm, ds), lambda i: (i, u_col_block)),
            pl.BlockSpec((1, ds), const),
            pl.BlockSpec((1, ds), const),
            _resident((ds, ds), const),
            _resident((ds, d), const),
            _resident((ds, d), const),
        ],
        out_specs=pl.BlockSpec((tm, d), lambda i: (i, 0)),
        out_shape=jax.ShapeDtypeStruct((t, d), BF16),
        compiler_params=_cparams(("parallel",)),
        name="mix",
    )(gates, gates, o_h, y_s, main, d_skip.reshape(1, ds), b_glu.reshape(1, ds), w_glu, w_bh, w_bs)


def _rms(x, w):
    ms = jnp.mean(x * x, axis=-1, keepdims=True)
    return x * lax.rsqrt(ms + EPS) * w


def _ffn_kernel(x_ref, mix_ref, wout_ref, n2_ref, wg_ref, wu_ref, wd_ref, fn_ref,
                out_ref, h_scr, *, final_norm):
    j = pl.program_id(1)
    nj = pl.num_programs(1)

    @pl.when(j == 0)
    def _():
        xn = x_ref[...] + jnp.dot(mix_ref[...], wout_ref[...], preferred_element_type=F32)
        out_ref[...] = xn
        h_scr[...] = _rms(xn, n2_ref[...]).astype(BF16)

    h = h_scr[...]
    ga = jnp.dot(h, wg_ref[...], preferred_element_type=F32)
    up = jnp.dot(h, wu_ref[...], preferred_element_type=F32)
    act = (jax.nn.silu(ga) * up).astype(BF16)
    out_ref[...] += jnp.dot(act, wd_ref[...], preferred_element_type=F32)

    if final_norm:
        @pl.when(j == nj - 1)
        def _():
            out_ref[...] = _rms(out_ref[...], fn_ref[...])


def _ffn(x, mix, w_out, norm2, w_gate_up, w_down, fnorm, final_norm, tm, tf):
    t, d = x.shape
    dff = w_down.shape[0]
    nj = dff // tf
    return pl.pallas_call(
        functools.partial(_ffn_kernel, final_norm=final_norm),
        grid=(t // tm, nj),
        in_specs=[
            pl.BlockSpec((tm, d), lambda i, j: (i, 0)),
            pl.BlockSpec((tm, d), lambda i, j: (i, 0)),
            _resident((d, d), lambda i, j: (0, 0)),
            pl.BlockSpec((1, d), lambda i, j: (0, 0)),
            pl.BlockSpec((d, tf), lambda i, j: (0, j)),
            pl.BlockSpec((d, tf), lambda i, j: (0, nj + j)),
            pl.BlockSpec((tf, d), lambda i, j: (j, 0)),
            pl.BlockSpec((1, d), lambda i, j: (0, 0)),
        ],
        out_specs=pl.BlockSpec((tm, d), lambda i, j: (i, 0)),
        out_shape=jax.ShapeDtypeStruct((t, d), F32),
        scratch_shapes=[pltpu.VMEM((tm, d), BF16)],
        compiler_params=_cparams(("parallel", "arbitrary")),
        name="ffn",
    )(x, mix, w_out, norm2.reshape(1, d), w_gate_up, w_gate_up, w_down, fnorm.reshape(1, d))


def _pick(n, pref):
    t = pref
    while n % t:
        t //= 2
    return t


def _lower_bounds(lb_logits):
    p = jax.nn.softmax(lb_logits.astype(F32), axis=0)
    cs = jnp.cumsum(p, axis=0)
    return cs - cs[:1]


def kernel(x_prompt, x_sample, state_hgrn, state_s5_re, state_s5_im, lb_logits, norm1, w_in, hgrn_norm, w_bh, s5_a_log_neg_re, s5_a_im, s5_log_dt, s5_b_re, s5_b_im, s5_c_re, s5_c_im, s5_d, w_glu, b_glu, w_bs, w_out, norm2, w_gate_up, w_down, final_norm):
    depth = w_in.shape[0]
    bp, tp, d = x_prompt.shape
    bs, ts, _ = x_sample.shape
    assert bp == 1
    dh = lb_logits.shape[1]
    ds = s5_d.shape[1]
    n_groups = ds // S5_GROUP
    n_heads = dh // HEAD_DIM
    n_tok_p = bp * tp
    n_tok_s = bs * ts
    n_tok = n_tok_p + n_tok_s

    n_main = 4 * dh + ds
    w_in_b = jnp.concatenate([w_in[:, :, n_main:], w_in[:, :, :n_main]], axis=-1).astype(BF16)
    w_bh_b = w_bh.astype(BF16)
    w_glu_b = w_glu.astype(BF16)
    w_bs_b = w_bs.astype(BF16)
    w_out_b = w_out.astype(BF16)
    w_gu_b = w_gate_up.astype(BF16)
    w_dn_b = w_down.astype(BF16)
    lbs = _lower_bounds(lb_logits)

    n_p = tp // S5_CHUNK
    n_cs = ts // S5_CHUNK
    s5_kt, s5_pb, s5_ct, s5_lam = _s5_tables(s5_a_log_neg_re, s5_a_im, s5_log_dt, s5_b_re,
                                             s5_b_im, s5_c_re, s5_c_im, int(math.log2(n_p)))
    n_slabs = n_groups // S5_SLAB
    sw = S5_SLAB * 2 * S5_STATE

    def pack_state(re, im):
        x0 = jnp.concatenate([re, im], axis=-1)
        return x0.reshape(x0.shape[0], n_slabs, sw).transpose(1, 0, 2)

    def unpack_state(st):
        st = st.transpose(1, 0, 2).reshape(st.shape[1], n_groups, 2, S5_STATE)
        return st[:, :, 0], st[:, :, 1]

    tm_in = _pick(n_tok, 1024)
    tn_in = 1024
    tm_mix = _pick(n_tok, 512)
    tm_ffn = _pick(n_tok, 512)
    tf = _pick(w_down.shape[1], 512)
    c_p = _pick(tp, 128)
    c_s = _pick(ts, 128)

    x = jnp.concatenate([x_prompt.reshape(n_tok_p, d), x_sample.reshape(n_tok_s, d)], axis=0)
    zero_h = jnp.zeros((bp, n_heads, HEAD_DIM, HEAD_DIM), F32)

    new_h_p, new_re_p, new_im_p, new_h_s, new_re_s, new_im_s = [], [], [], [], [], []
    for l in range(depth):
        gates, main = _inproj(x, norm1[l], w_in_b[l], 2 * d, tm_in, tn_in)

        o_h = jnp.zeros((n_tok, dh), BF16)
        o_h, sh_p = _hgrn(main, lbs[l], hgrn_norm[l], zero_h, o_h,
                          row_off=0, n_seq=bp, seq_len=tp, c=c_p, nh=2)
        o_h, sh_s = _hgrn(main, lbs[l], hgrn_norm[l], state_hgrn[l], o_h,
                          row_off=n_tok_p, n_seq=bs, seq_len=ts, c=c_s, nh=n_heads)

        y, st_p, st_s = _s5(main, 4 * dh, s5_kt[l], s5_pb[l], s5_ct[l], s5_lam[l],
                            pack_state(state_s5_re[l], state_s5_im[l]),
                            n_p=n_p, n_b=bs, n_cs=n_cs)

        mix = _mix(gates, o_h, y, main, (4 * dh) // ds, s5_d[l], b_glu[l],
                   w_glu_b[l], w_bh_b[l], w_bs_b[l], tm_mix)
        x = _ffn(x, mix, w_out_b[l], norm2[l], w_gu_b[l], w_dn_b[l], final_norm,
                 l == depth - 1, tm_ffn, tf)

        new_h_p.append(sh_p)
        new_h_s.append(sh_s)
        re_p, im_p = unpack_state(st_p)
        re_s, im_s = unpack_state(st_s)
        new_re_p.append(re_p)
        new_im_p.append(im_p)
        new_re_s.append(re_s)
        new_im_s.append(im_s)

    y_prompt = x[:n_tok_p].reshape(bp, tp, d)
    y_sample = x[n_tok_p:].reshape(bs, ts, d)
    return (y_prompt, y_sample, jnp.stack(new_h_p), jnp.stack(new_re_p), jnp.stack(new_im_p),
            jnp.stack(new_h_s), jnp.stack(new_re_s), jnp.stack(new_im_s))
```

```python
import functools
import math

import jax
import jax.numpy as jnp
import numpy as np
from jax import lax
from jax.experimental import pallas as pl
from jax.experimental.pallas import tpu as pltpu

F32 = jnp.float32
BF16 = jnp.bfloat16

EPS = 1e-6
LOG2_E = 1.4426950408889634
HEAD_DIM = 128
S5_GROUP = 16
S5_STATE = 64
S5_CHUNK = 16
S5_SLAB = 8
HGRN_STACK_ROWS = 256
VMEM_LIMIT = 56 * 1024 * 1024
S5_VMEM_LIMIT = 60 * 1024 * 1024


def _cparams(sem):
    return pltpu.CompilerParams(dimension_semantics=sem, vmem_limit_bytes=VMEM_LIMIT)


def _resident(shape, index_map):
    return pl.BlockSpec(shape, index_map, pipeline_mode=pl.Buffered(1))


def _inproj_kernel(x_ref, nw_ref, w_ref, gate_ref, main_ref, h_scr, *, n_gate_tiles):
    j = pl.program_id(1)

    @pl.when(j == 0)
    def _():
        x = x_ref[...]
        ms = jnp.mean(x * x, axis=-1, keepdims=True)
        h_scr[...] = (x * lax.rsqrt(ms + EPS) * nw_ref[...]).astype(BF16)

    acc = jnp.dot(h_scr[...], w_ref[...], preferred_element_type=F32)

    @pl.when(j < n_gate_tiles)
    def _():
        gate_ref[...] = jax.nn.sigmoid(acc).astype(BF16)

    @pl.when(j >= n_gate_tiles)
    def _():
        main_ref[...] = acc


def _inproj(x, nw, w, n_gate_cols, tm, tn):
    t, d = x.shape
    n = w.shape[1]
    n_gate_tiles = n_gate_cols // tn
    n_tiles = n // tn
    return pl.pallas_call(
        functools.partial(_inproj_kernel, n_gate_tiles=n_gate_tiles),
        grid=(t // tm, n_tiles),
        in_specs=[
            pl.BlockSpec((tm, d), lambda i, j: (i, 0)),
            pl.BlockSpec((1, d), lambda i, j: (0, 0)),
            pl.BlockSpec((d, tn), lambda i, j: (0, j)),
        ],
        out_specs=[
            pl.BlockSpec((tm, tn), lambda i, j: (i, jnp.minimum(j, n_gate_tiles - 1))),
            pl.BlockSpec((tm, tn), lambda i, j: (i, jnp.maximum(j - n_gate_tiles, 0))),
        ],
        out_shape=[
            jax.ShapeDtypeStruct((t, n_gate_cols), BF16),
            jax.ShapeDtypeStruct((t, n - n_gate_cols), F32),
        ],
        scratch_shapes=[pltpu.VMEM((tm, d), BF16)],
        compiler_params=_cparams(("parallel", "arbitrary")),
        name="inproj",
    )(x, nw.reshape(1, d), w)


def _hgrn_level_masks(c):
    n_lev = int(math.log2(c)) + 1
    gsz = max(1, min(n_lev, HGRN_STACK_ROWS // c))
    n_groups = -(-n_lev // gsz)
    r = gsz * c
    t = np.arange(c)[:, None]
    s = np.arange(c)[None, :]
    masks = np.zeros((n_groups, r, r), np.float32)
    for lev in range(n_lev):
        if lev == 0:
            m = (t == s)
        else:
            h = 1 << (lev - 1)
            m = ((t & h) != 0) & ((s & h) == 0) & ((t // (2 * h)) == (s // (2 * h)))
        g, i = divmod(lev, gsz)
        masks[g, i * c:(i + 1) * c, i * c:(i + 1) * c] = m
    return masks, n_lev, gsz, n_groups


def _hgrn_kernel(q_ref, f_ref, i_ref, g_ref, lb_ref, gain_ref, s0_ref, mask_ref, o_all_ref,
                 o_ref, s_ref, st_scr, *, c, nh, hgrp, n_lev, gsz, n_groups):
    ci = pl.program_id(2)
    nc = pl.num_programs(2)

    @pl.when(ci == 0)
    def _():
        st_scr[...] = s0_ref[0]

    row = lax.broadcasted_iota(jnp.int32, (c, HEAD_DIM), 0)
    nt = (((1,), (1,)), ((), ()))
    tn = (((0,), (0,)), ((), ()))

    def head_operands(h):
        sl = slice(h * HEAD_DIM, (h + 1) * HEAD_DIM)
        z = f_ref[:, sl]
        lb = lb_ref[:, sl]
        ez = jnp.exp(-jnp.abs(z))
        log_sig = jnp.minimum(z, 0.0) - jnp.log(1.0 + ez)
        a0 = jnp.log(lb)
        a1 = jnp.log1p(-lb) + log_sig
        logf = jnp.maximum(a0, a1) + jnp.log(1.0 + jnp.exp(-jnp.abs(a0 - a1)))
        k = (1.0 - lb) * (jnp.where(z >= 0.0, ez, 1.0) / (1.0 + ez))
        q = q_ref[:, sl]
        v = jax.nn.silu(i_ref[:, sl])
        v_bf = v.astype(BF16)

        p = logf * LOG2_E
        tb = p
        a_lev = [q.astype(BF16)]
        b_lev = [k.astype(BF16)]
        for lev in range(1, n_lev):
            hs = 1 << (lev - 1)
            right = (row & hs) != 0
            e = jnp.exp2(jnp.where(right, p, tb - p))
            zl = (jnp.where(right, q, k) * e).astype(BF16)
            a_lev.append(zl)
            b_lev.append(zl)
            dn = pltpu.roll(tb, hs, 0)
            up = pltpu.roll(tb, c - hs, 0)
            p = p + jnp.where(right, dn, 0.0)
            tb = tb + jnp.where(right, dn, up)
        ops = []
        for g in range(n_groups):
            lo, hi = g * gsz, min((g + 1) * gsz, n_lev)
            n_in = hi - lo
            a_g = jnp.concatenate(a_lev[lo:hi], axis=0) if n_in > 1 else a_lev[lo]
            b_g = a_g if lo > 0 else (
                jnp.concatenate(b_lev[lo:hi], axis=0) if n_in > 1 else b_lev[lo])
            ops.append((a_g, b_g, n_in))
        q_in = (q * jnp.exp2(p)).astype(BF16)
        k_out = (k * jnp.exp2(tb - p)).astype(BF16)
        decay = jnp.exp2(jnp.broadcast_to(tb[0:1, :], (HEAD_DIM, HEAD_DIM)).T)
        return ops, v_bf, q_in, k_out, decay

    def fold_scores(g, sc, n_in):
        if c % HEAD_DIM == 0:
            return jnp.concatenate(
                [sc[i * c:(i + 1) * c, i * c:(i + 1) * c].astype(BF16)
                 * mask_ref[g, i * c:(i + 1) * c, i * c:(i + 1) * c]
                 for i in range(n_in)], axis=1)
        r = n_in * c
        sc = sc * mask_ref[g][:r, :r]
        fold = sc[0:c]
        for i in range(1, n_in):
            fold = fold + sc[i * c:(i + 1) * c]
        return fold.astype(BF16)

    for h0 in range(0, nh, hgrp):
        heads = range(h0, min(h0 + hgrp, nh))
        prep = {h: head_operands(h) for h in heads}
        scores = {h: [lax.dot_general(a_g, b_g, nt, preferred_element_type=F32)
                      for a_g, b_g, _ in prep[h][0]] for h in heads}
        upd = {h: lax.dot_general(prep[h][3], prep[h][1], tn, preferred_element_type=F32)
               for h in heads}
        for h in heads:
            ops, v_bf, q_in, _, decay = prep[h]
            st = st_scr[h]
            lhs = [q_in]
            rhs = [st.astype(BF16)]
            for g, (sc, (_, _, n_in)) in enumerate(zip(scores[h], ops)):
                lhs.append(fold_scores(g, sc, n_in))
                rhs.extend([v_bf] * n_in)
            o = jnp.dot(jnp.concatenate(lhs, axis=1), jnp.concatenate(rhs, axis=0),
                        preferred_element_type=F32)
            st_scr[h] = st * decay + upd[h]
            sl = slice(h * HEAD_DIM, (h + 1) * HEAD_DIM)
            ms = jnp.mean(o * o, axis=-1, keepdims=True)
            o = o * lax.rsqrt(ms + EPS) * gain_ref[:, sl] * jax.nn.silu(g_ref[:, sl])
            o_ref[:, sl] = o.astype(o_ref.dtype)

    @pl.when(ci == nc - 1)
    def _():
        s_ref[0] = st_scr[...]


def _hgrn(main, lb, gain, s0, o_all, *, row_off, n_seq, seq_len, c, nh, hgrp):
    dh = lb.shape[-1]
    n_heads = dh // HEAD_DIM
    w = nh * HEAD_DIM
    n_hg = n_heads // nh
    n_chunks = seq_len // c
    rb0 = row_off // c
    masks, n_lev, gsz, n_groups = _hgrn_level_masks(c)
    r = masks.shape[-1]

    def col_spec(group):
        return pl.BlockSpec(
            (c, w), lambda b, hg, ci: (rb0 + b * n_chunks + ci, group * n_hg + hg))

    vec_spec = pl.BlockSpec((1, w), lambda b, hg, ci: (0, hg))
    st_spec = pl.BlockSpec((1, nh, HEAD_DIM, HEAD_DIM), lambda b, hg, ci: (b, hg, 0, 0))
    o, s_new = pl.pallas_call(
        functools.partial(_hgrn_kernel, c=c, nh=nh, hgrp=hgrp, n_lev=n_lev, gsz=gsz,
                          n_groups=n_groups),
        grid=(n_seq, n_hg, n_chunks),
        in_specs=[col_spec(0), col_spec(1), col_spec(2), col_spec(3), vec_spec, vec_spec,
                  st_spec, _resident((n_groups, r, r), lambda b, hg, ci: (0, 0, 0)),
                  pl.BlockSpec(memory_space=pl.ANY)],
        out_specs=[
            pl.BlockSpec((c, w), lambda b, hg, ci: (rb0 + b * n_chunks + ci, hg)),
            st_spec,
        ],
        out_shape=[
            jax.ShapeDtypeStruct(o_all.shape, o_all.dtype),
            jax.ShapeDtypeStruct((n_seq, n_heads, HEAD_DIM, HEAD_DIM), F32),
        ],
        input_output_aliases={8: 0},
        scratch_shapes=[pltpu.VMEM((nh, HEAD_DIM, HEAD_DIM), F32)],
        compiler_params=_cparams(("parallel", "parallel", "arbitrary")),
        name="hgrn_c%d" % c,
    )(main, main, main, main, lb.reshape(1, dh), gain.reshape(1, dh), s0,
      jnp.asarray(masks, BF16 if c % HEAD_DIM == 0 else F32), o_all)
    return o, s_new


def _cmul(x, la, lb):
    return x * la + pltpu.roll(x, S5_STATE, 1) * lb


def _s5_kernel(u_ref, kt_ref, pb_ref, ct_ref, lam_ref, x0_ref, y_ref, sp_ref, ss_ref,
               m_scr, b_scr, ct_scr, acat_scr, z_scr, xprev_scr, *, n_p, n_b, n_cs, rb):
    n_rows = n_p + n_b * n_cs
    sw = 2 * S5_STATE
    cw = S5_SLAB * S5_GROUP

    gi = lax.broadcasted_iota(jnp.int32, (cw, cw), 0) // S5_GROUP
    gj = lax.broadcasted_iota(jnp.int32, (cw, cw), 1) // S5_GROUP
    same_group = gi == gj
    lag_tiles = [
        jnp.where(same_group, jnp.concatenate([kt_ref[0, j]] * S5_SLAB, axis=0), 0.0).astype(BF16)
        for j in range(S5_CHUNK)]
    zero_tile = jnp.zeros((cw, cw), BF16)
    for s in range(S5_CHUNK):
        for t in range(S5_CHUNK):
            m_scr[s * cw:(s + 1) * cw, t * cw:(t + 1) * cw] = (
                lag_tiles[t - s] if t >= s else zero_tile)
    ri = lax.broadcasted_iota(jnp.int32, (cw, S5_SLAB * sw), 0) // S5_GROUP
    ci = lax.broadcasted_iota(jnp.int32, (cw, S5_SLAB * sw), 1) // sw
    own_state = ri == ci
    for s in range(S5_CHUNK):
        b_scr[s * cw:(s + 1) * cw, :] = jnp.where(
            own_state, jnp.concatenate([pb_ref[0, s]] * S5_SLAB, axis=1), 0.0).astype(BF16)
        ct_scr[s * cw:(s + 1) * cw, :] = jnp.where(
            own_state, jnp.concatenate([ct_ref[0, s]] * S5_SLAB, axis=1), 0.0).astype(BF16)

    for r0 in range(0, n_rows, rb):
        a = jnp.concatenate(
            [u_ref[pl.ds(S5_CHUNK * r0 + s, rb, stride=S5_CHUNK), :].astype(BF16)
             for s in range(S5_CHUNK)], axis=1)
        acat_scr[r0:r0 + rb, :] = a
        z = jnp.dot(a, b_scr[...], preferred_element_type=F32)
        for g in range(S5_SLAB):
            z_scr[g, r0:r0 + rb, :] = z[:, g * sw:(g + 1) * sw]

    rowi = lax.broadcasted_iota(jnp.int32, (n_p, sw), 0)

    def scan_group(g, carry):
        lam_g = lam_ref[0, g]
        x = z_scr[g, 0:n_p, :]
        for lev in range(int(math.log2(n_p))):
            sh = 1 << lev
            xs = jnp.where(rowi >= sh, pltpu.roll(x, sh, 0), 0.0)
            x = x + _cmul(xs, lam_g[2 * lev:2 * lev + 1], lam_g[2 * lev + 1:2 * lev + 2])
        xprev_scr[g, 0:n_p, :] = jnp.where(rowi >= 1, pltpu.roll(x, 1, 0), 0.0)
        sp_ref[0, g] = x[n_p - 1:n_p]

        xs = x0_ref[0, g]
        for ci in range(n_cs):
            rows = pl.ds(n_p + ci, n_b, stride=n_cs)
            xprev_scr[g, rows, :] = xs
            xs = _cmul(xs, lam_g[0:1], lam_g[1:2]) + z_scr[g, rows, :]
        ss_ref[0, g] = xs
        return carry

    lax.fori_loop(0, S5_SLAB, scan_group, 0)

    for r0 in range(0, n_rows, rb):
        x_prev = jnp.concatenate(
            [xprev_scr[g, r0:r0 + rb, :].astype(BF16) for g in range(S5_SLAB)], axis=1)
        y = (jnp.dot(acat_scr[r0:r0 + rb, :], m_scr[...], preferred_element_type=F32)
             + lax.dot_general(x_prev, ct_scr[...], (((1,), (1,)), ((), ())),
                               preferred_element_type=F32))
        for t in range(S5_CHUNK):
            y_ref[pl.ds(S5_CHUNK * r0 + t, rb, stride=S5_CHUNK), :] = y[:, t * cw:(t + 1) * cw]


def _s5(main, u_col0, kt, pb, ct, lam, x0, *, n_p, n_b, n_cs):
    n_slabs = kt.shape[0]
    n_tok = main.shape[0]
    n_rows = n_p + n_b * n_cs
    assert n_rows * S5_CHUNK == n_tok
    n_lam = lam.shape[2]
    cw = S5_SLAB * S5_GROUP
    kw = S5_CHUNK * cw
    gw = 2 * S5_STATE
    sw = S5_SLAB * gw
    rb = max(r for r in range(16, 273, 16) if n_rows % r == 0)
    cb0 = u_col0 // cw

    def per_slab(shape):
        nd = len(shape)
        return pl.BlockSpec((1,) + shape, lambda v: (v,) + (0,) * nd)

    return pl.pallas_call(
        functools.partial(_s5_kernel, n_p=n_p, n_b=n_b, n_cs=n_cs, rb=rb),
        grid=(n_slabs,),
        in_specs=[_resident((n_tok, cw), lambda v: (0, cb0 + v)),
                  per_slab((S5_CHUNK, S5_GROUP, cw)),
                  per_slab((S5_CHUNK, cw, 2 * S5_STATE)),
                  per_slab((S5_CHUNK, cw, 2 * S5_STATE)),
                  per_slab((S5_SLAB, n_lam, gw)), per_slab((S5_SLAB, n_b, gw))],
        out_specs=[_resident((n_tok, cw), lambda v: (0, v)),
                   per_slab((S5_SLAB, 1, gw)), per_slab((S5_SLAB, n_b, gw))],
        out_shape=[
            jax.ShapeDtypeStruct((n_tok, n_slabs * cw), F32),
            jax.ShapeDtypeStruct((n_slabs, S5_SLAB, 1, gw), F32),
            jax.ShapeDtypeStruct((n_slabs, S5_SLAB, n_b, gw), F32),
        ],
        scratch_shapes=[pltpu.VMEM((kw, kw), BF16), pltpu.VMEM((kw, sw), BF16),
                        pltpu.VMEM((kw, sw), BF16), pltpu.VMEM((n_rows, kw), BF16),
                        pltpu.VMEM((S5_SLAB, n_rows, 2 * S5_STATE), F32),
                        pltpu.VMEM((S5_SLAB, n_rows, 2 * S5_STATE), F32)],
        compiler_params=pltpu.CompilerParams(dimension_semantics=("parallel",),
                                             vmem_limit_bytes=S5_VMEM_LIMIT),
        name="s5",
    )(main, kt, pb, ct, lam, x0)


def _s5_tables(a_log_neg_re, a_im, log_dt, b_re, b_im, c_re, c_im, n_scan_lev):
    hp = lax.Precision.HIGHEST
    lam_re = -jnp.exp(a_log_neg_re.astype(F32))
    lam_im = a_im.astype(F32)
    dt = jnp.exp(log_dt.astype(F32))[..., None]

    def powers(jj):
        e = jj[None, None, :, None]
        mag = jnp.exp((lam_re * dt)[:, :, None, :] * e)
        ang = (lam_im * dt)[:, :, None, :] * e
        return mag * jnp.cos(ang), mag * jnp.sin(ang)

    pw_re, pw_im = powers(jnp.arange(S5_CHUNK + 1, dtype=F32))
    num_re, num_im = pw_re[:, :, 1] - 1.0, pw_im[:, :, 1]
    den = lam_re * lam_re + lam_im * lam_im
    zoh_re = ((num_re * lam_re + num_im * lam_im) / den)[..., None]
    zoh_im = ((num_im * lam_re - num_re * lam_im) / den)[..., None]
    b_re, b_im = b_re.astype(F32), b_im.astype(F32)
    bb_re = zoh_re * b_re - zoh_im * b_im
    bb_im = zoh_re * b_im + zoh_im * b_re
    c_re, c_im = c_re.astype(F32)[:, :, None], c_im.astype(F32)[:, :, None]
    pr, pi = pw_re[:, :, :, None, :], pw_im[:, :, :, None, :]
    cp_re = c_re * pr - c_im * pi
    cp_im = c_re * pi + c_im * pr
    kj = (jnp.einsum('lgjpn,lgnq->lgjpq', cp_re[:, :, :S5_CHUNK], bb_re, precision=hp)
          - jnp.einsum('lgjpn,lgnq->lgjpq', cp_im[:, :, :S5_CHUNK], bb_im, precision=hp))
    qr, qi = powers(jnp.asarray(np.arange(S5_CHUNK - 1, -1, -1), F32))
    qr, qi = qr[..., None], qi[..., None]
    pb_re = (qr * bb_re[:, :, None] - qi * bb_im[:, :, None]).transpose(0, 1, 2, 4, 3)
    pb_im = (qr * bb_im[:, :, None] + qi * bb_re[:, :, None]).transpose(0, 1, 2, 4, 3)
    bst = jnp.concatenate([pb_re, pb_im], axis=-1)
    cst = jnp.concatenate([cp_re[:, :, 1:], -cp_im[:, :, 1:]], axis=-1)
    lp_re, lp_im = powers(S5_CHUNK * (2.0 ** jnp.arange(n_scan_lev, dtype=F32)))
    la = jnp.concatenate([lp_re, lp_re], axis=-1)
    lb = jnp.concatenate([-lp_im, lp_im], axis=-1)
    lam_tab = jnp.stack([la, lb], axis=3)

    l, g = kj.shape[:2]
    v, sg = g // S5_SLAB, S5_SLAB
    cw = sg * S5_GROUP
    sw = sg * 2 * S5_STATE
    kt = kj.reshape(l, v, sg, S5_CHUNK, S5_GROUP, S5_GROUP)
    kt = kt.transpose(0, 1, 3, 5, 2, 4).reshape(l, v, S5_CHUNK, S5_GROUP, cw)
    pb = bst.reshape(l, v, sg, S5_CHUNK, S5_GROUP, 2 * S5_STATE)
    pb = pb.transpose(0, 1, 3, 2, 4, 5).reshape(l, v, S5_CHUNK, cw, 2 * S5_STATE)
    ct = cst.reshape(l, v, sg, S5_CHUNK, S5_GROUP, 2 * S5_STATE)
    ct = ct.transpose(0, 1, 3, 2, 4, 5).reshape(l, v, S5_CHUNK, cw, 2 * S5_STATE)
    lam_big = lam_tab.reshape(l, v, sg, 2 * n_scan_lev, 2 * S5_STATE)
    return kt, pb, ct, lam_big


def _mix_kernel(gh_ref, gs_ref, oh_ref, y_ref, u_ref, d_ref, bglu_ref,
                wglu_ref, wbh_ref, wbs_ref, out_ref):
    ys = jax.nn.gelu(y_ref[...] + d_ref[...] * u_ref[...])
    gate = jnp.dot(ys.astype(BF16), wglu_ref[...], preferred_element_type=F32) + bglu_ref[...]
    glu = (ys * jax.nn.sigmoid(gate)).astype(BF16)
    t_h = jnp.dot(oh_ref[...], wbh_ref[...], preferred_element_type=F32)
    t_s = jnp.dot(glu, wbs_ref[...], preferred_element_type=F32)
    mix = gh_ref[...].astype(F32) * t_h + gs_ref[...].astype(F32) * t_s
    out_ref[...] = mix.astype(BF16)


def _mix(gates, o_h, y_s, main, u_col_block, d_skip, b_glu, w_glu, w_bh, w_bs, tm):
    t, ds = y_s.shape
    d = w_bh.shape[1]
    const = lambda i: (0, 0)
    return pl.pallas_call(
        _mix_kernel,
        grid=(t // tm,),
        in_specs=[
            pl.BlockSpec((tm, d), lambda i: (i, 0)),
            pl.BlockSpec((tm, d), lambda i: (i, 1)),
            pl.BlockSpec((tm, ds), lambda i: (i, 0)),
            pl.BlockSpec((tm, ds), lambda i: (i, 0)),
            pl.BlockSpec((tm, ds), lambda i: (i, u_col_block)),
            pl.BlockSpec((1, ds), const),
            pl.BlockSpec((1, ds), const),
            _resident((ds, ds), const),
            _resident((ds, d), const),
            _resident((ds, d), const),
        ],
        out_specs=pl.BlockSpec((tm, d), lambda i: (i, 0)),
        out_shape=jax.ShapeDtypeStruct((t, d), BF16),
        compiler_params=_cparams(("parallel",)),
        name="mix",
    )(gates, gates, o_h, y_s, main, d_skip.reshape(1, ds), b_glu.reshape(1, ds), w_glu, w_bh, w_bs)


def _rms(x, w):
    ms = jnp.mean(x * x, axis=-1, keepdims=True)
    return x * lax.rsqrt(ms + EPS) * w


def _ffn_kernel(x_ref, mix_ref, wout_ref, n2_ref, wg_ref, wu_ref, wd_ref, fn_ref,
                out_ref, h_scr, *, final_norm):
    j = pl.program_id(1)
    nj = pl.num_programs(1)

    @pl.when(j == 0)
    def _():
        xn = x_ref[...] + jnp.dot(mix_ref[...], wout_ref[...], preferred_element_type=F32)
        out_ref[...] = xn
        h_scr[...] = _rms(xn, n2_ref[...]).astype(BF16)

    h = h_scr[...]
    ga = jnp.dot(h, wg_ref[...], preferred_element_type=F32)
    up = jnp.dot(h, wu_ref[...], preferred_element_type=F32)
    act = (jax.nn.silu(ga) * up).astype(BF16)
    out_ref[...] += jnp.dot(act, wd_ref[...], preferred_element_type=F32)

    if final_norm:
        @pl.when(j == nj - 1)
        def _():
            out_ref[...] = _rms(out_ref[...], fn_ref[...])


def _ffn(x, mix, w_out, norm2, w_gate_up, w_down, fnorm, final_norm, tm, tf):
    t, d = x.shape
    dff = w_down.shape[0]
    nj = dff // tf
    return pl.pallas_call(
        functools.partial(_ffn_kernel, final_norm=final_norm),
        grid=(t // tm, nj),
        in_specs=[
            pl.BlockSpec((tm, d), lambda i, j: (i, 0)),
            pl.BlockSpec((tm, d), lambda i, j: (i, 0)),
            _resident((d, d), lambda i, j: (0, 0)),
            pl.BlockSpec((1, d), lambda i, j: (0, 0)),
            pl.BlockSpec((d, tf), lambda i, j: (0, j)),
            pl.BlockSpec((d, tf), lambda i, j: (0, nj + j)),
            pl.BlockSpec((tf, d), lambda i, j: (j, 0)),
            pl.BlockSpec((1, d), lambda i, j: (0, 0)),
        ],
        out_specs=pl.BlockSpec((tm, d), lambda i, j: (i, 0)),
        out_shape=jax.ShapeDtypeStruct((t, d), F32),
        scratch_shapes=[pltpu.VMEM((tm, d), BF16)],
        compiler_params=_cparams(("parallel", "arbitrary")),
        name="ffn",
    )(x, mix, w_out, norm2.reshape(1, d), w_gate_up, w_gate_up, w_down, fnorm.reshape(1, d))


def _pick(n, pref):
    t = pref
    while n % t:
        t //= 2
    return t


def _lower_bounds(lb_logits):
    p = jax.nn.softmax(lb_logits.astype(F32), axis=0)
    cs = jnp.cumsum(p, axis=0)
    return cs - cs[:1]


def kernel(x_prompt, x_sample, state_hgrn, state_s5_re, state_s5_im, lb_logits, norm1, w_in, hgrn_norm, w_bh, s5_a_log_neg_re, s5_a_im, s5_log_dt, s5_b_re, s5_b_im, s5_c_re, s5_c_im, s5_d, w_glu, b_glu, w_bs, w_out, norm2, w_gate_up, w_down, final_norm):
    depth = w_in.shape[0]
    bp, tp, d = x_prompt.shape
    bs, ts, _ = x_sample.shape
    assert bp == 1
    dh = lb_logits.shape[1]
    ds = s5_d.shape[1]
    n_groups = ds // S5_GROUP
    n_heads = dh // HEAD_DIM
    n_tok_p = bp * tp
    n_tok_s = bs * ts
    n_tok = n_tok_p + n_tok_s

    n_main = 4 * dh + ds
    w_in_b = jnp.concatenate([w_in[:, :, n_main:], w_in[:, :, :n_main]], axis=-1).astype(BF16)
    w_bh_b = w_bh.astype(BF16)
    w_glu_b = w_glu.astype(BF16)
    w_bs_b = w_bs.astype(BF16)
    w_out_b = w_out.astype(BF16)
    w_gu_b = w_gate_up.astype(BF16)
    w_dn_b = w_down.astype(BF16)
    lbs = _lower_bounds(lb_logits)

    n_p = tp // S5_CHUNK
    n_cs = ts // S5_CHUNK
    s5_kt, s5_pb, s5_ct, s5_lam = _s5_tables(s5_a_log_neg_re, s5_a_im, s5_log_dt, s5_b_re,
                                             s5_b_im, s5_c_re, s5_c_im, int(math.log2(n_p)))
    n_slabs = n_groups // S5_SLAB
    sw = S5_SLAB * 2 * S5_STATE

    def pack_state(re, im):
        x0 = jnp.concatenate([re, im], axis=-1)
        return x0.reshape(x0.shape[0], n_slabs, S5_SLAB, 2 * S5_STATE).transpose(1, 2, 0, 3)

    def unpack_state(st):
        st = st.transpose(2, 0, 1, 3).reshape(st.shape[2], n_groups, 2, S5_STATE)
        return st[:, :, 0], st[:, :, 1]

    tm_in = _pick(n_tok, 1024)
    tn_in = 1024
    tm_mix = _pick(n_tok, 512)
    tm_ffn = _pick(n_tok, 512)
    tf = _pick(w_down.shape[1], 512)
    c_p = _pick(tp, 128)
    c_s = _pick(ts, 128)

    x = jnp.concatenate([x_prompt.reshape(n_tok_p, d), x_sample.reshape(n_tok_s, d)], axis=0)
    zero_h = jnp.zeros((bp, n_heads, HEAD_DIM, HEAD_DIM), F32)

    new_h_p, new_re_p, new_im_p, new_h_s, new_re_s, new_im_s = [], [], [], [], [], []
    for l in range(depth):
        gates, main = _inproj(x, norm1[l], w_in_b[l], 2 * d, tm_in, tn_in)

        o_h = jnp.zeros((n_tok, dh), BF16)
        o_h, sh_p = _hgrn(main, lbs[l], hgrn_norm[l], zero_h, o_h,
                          row_off=0, n_seq=bp, seq_len=tp, c=c_p, nh=n_heads, hgrp=n_heads)
        o_h, sh_s = _hgrn(main, lbs[l], hgrn_norm[l], state_hgrn[l], o_h,
                          row_off=n_tok_p, n_seq=bs, seq_len=ts, c=c_s, nh=n_heads, hgrp=n_heads)

        y, st_p, st_s = _s5(main, 4 * dh, s5_kt[l], s5_pb[l], s5_ct[l], s5_lam[l],
                            pack_state(state_s5_re[l], state_s5_im[l]),
                            n_p=n_p, n_b=bs, n_cs=n_cs)

        mix = _mix(gates, o_h, y, main, (4 * dh) // ds, s5_d[l], b_glu[l],
                   w_glu_b[l], w_bh_b[l], w_bs_b[l], tm_mix)
        x = _ffn(x, mix, w_out_b[l], norm2[l], w_gu_b[l], w_dn_b[l], final_norm,
                 l == depth - 1, tm_ffn, tf)

        new_h_p.append(sh_p)
        new_h_s.append(sh_s)
        re_p, im_p = unpack_state(st_p)
        re_s, im_s = unpack_state(st_s)
        new_re_p.append(re_p)
        new_im_p.append(im_p)
        new_re_s.append(re_s)
        new_im_s.append(im_s)

    y_prompt = x[:n_tok_p].reshape(bp, tp, d)
    y_sample = x[n_tok_p:].reshape(bs, ts, d)
    return (y_prompt, y_sample, jnp.stack(new_h_p), jnp.stack(new_re_p), jnp.stack(new_im_p),
            jnp.stack(new_h_s), jnp.stack(new_re_s), jnp.stack(new_im_s))
```

```python
import functools
import math

import jax
import jax.numpy as jnp
import numpy as np
from jax import lax
from jax.experimental import pallas as pl
from jax.experimental.pallas import tpu as pltpu

F32 = jnp.float32
BF16 = jnp.bfloat16

EPS = 1e-6
LOG2_E = 1.4426950408889634
HEAD_DIM = 128
S5_GROUP = 16
S5_STATE = 64
S5_CHUNK = 16
S5_SLAB = 8
HGRN_STACK_ROWS = 256
VMEM_LIMIT = 56 * 1024 * 1024
S5_VMEM_LIMIT = 60 * 1024 * 1024


def _cparams(sem):
    return pltpu.CompilerParams(dimension_semantics=sem, vmem_limit_bytes=VMEM_LIMIT)


def _resident(shape, index_map):
    return pl.BlockSpec(shape, index_map, pipeline_mode=pl.Buffered(1))


def _inproj_kernel(x_ref, nw_ref, w_ref, gate_ref, main_ref, h_scr, *, n_gate_tiles):
    j = pl.program_id(1)

    @pl.when(j == 0)
    def _():
        x = x_ref[...]
        ms = jnp.mean(x * x, axis=-1, keepdims=True)
        h_scr[...] = (x * lax.rsqrt(ms + EPS) * nw_ref[...]).astype(BF16)

    @pl.when(j < n_gate_tiles)
    def _():
        acc = jnp.dot(h_scr[...], w_ref[...], preferred_element_type=F32)
        gate_ref[...] = jax.nn.sigmoid(acc).astype(BF16)

    @pl.when(j >= n_gate_tiles)
    def _():
        main_ref[...] = jnp.dot(h_scr[...], w_ref[...], preferred_element_type=F32)


def _inproj(x, nw, w, n_gate_cols, tm, tn):
    t, d = x.shape
    n = w.shape[1]
    n_gate_tiles = n_gate_cols // tn
    n_tiles = n // tn
    return pl.pallas_call(
        functools.partial(_inproj_kernel, n_gate_tiles=n_gate_tiles),
        grid=(t // tm, n_tiles),
        in_specs=[
            pl.BlockSpec((tm, d), lambda i, j: (i, 0)),
            pl.BlockSpec((1, d), lambda i, j: (0, 0)),
            pl.BlockSpec((d, tn), lambda i, j: (0, j)),
        ],
        out_specs=[
            pl.BlockSpec((tm, tn), lambda i, j: (i, jnp.minimum(j, n_gate_tiles - 1))),
            pl.BlockSpec((tm, tn), lambda i, j: (i, jnp.maximum(j - n_gate_tiles, 0))),
        ],
        out_shape=[
            jax.ShapeDtypeStruct((t, n_gate_cols), BF16),
            jax.ShapeDtypeStruct((t, n - n_gate_cols), F32),
        ],
        scratch_shapes=[pltpu.VMEM((tm, d), BF16)],
        compiler_params=_cparams(("parallel", "arbitrary")),
        name="inproj",
    )(x, nw.reshape(1, d), w)


def _hgrn_level_masks(c):
    n_lev = int(math.log2(c)) + 1
    gsz = max(1, min(n_lev, HGRN_STACK_ROWS // c))
    n_groups = -(-n_lev // gsz)
    r = gsz * c
    t = np.arange(c)[:, None]
    s = np.arange(c)[None, :]
    masks = np.zeros((n_groups, r, r), np.float32)
    for lev in range(n_lev):
        if lev == 0:
            m = (t == s)
        else:
            h = 1 << (lev - 1)
            m = ((t & h) != 0) & ((s & h) == 0) & ((t // (2 * h)) == (s // (2 * h)))
        g, i = divmod(lev, gsz)
        masks[g, i * c:(i + 1) * c, i * c:(i + 1) * c] = m
    return masks, n_lev, gsz, n_groups


def _hgrn_kernel(q_ref, f_ref, i_ref, g_ref, lb_ref, gain_ref, s0_ref, mask_ref, o_all_ref,
                 o_ref, s_ref, st_scr, *, c, nh, hgrp, n_lev, gsz, n_groups):
    ci = pl.program_id(2)
    nc = pl.num_programs(2)

    @pl.when(ci == 0)
    def _():
        st_scr[...] = s0_ref[0]

    row = lax.broadcasted_iota(jnp.int32, (c, HEAD_DIM), 0)
    nt = (((1,), (1,)), ((), ()))
    tn = (((0,), (0,)), ((), ()))

    def head_operands(h):
        sl = slice(h * HEAD_DIM, (h + 1) * HEAD_DIM)
        z = f_ref[:, sl]
        lb = lb_ref[:, sl]
        ez = jnp.exp(-jnp.abs(z))
        log_sig = jnp.minimum(z, 0.0) - jnp.log(1.0 + ez)
        a0 = jnp.log(lb)
        a1 = jnp.log1p(-lb) + log_sig
        logf = jnp.maximum(a0, a1) + jnp.log(1.0 + jnp.exp(-jnp.abs(a0 - a1)))
        k = (1.0 - lb) * (jnp.where(z >= 0.0, ez, 1.0) / (1.0 + ez))
        q = q_ref[:, sl]
        v = jax.nn.silu(i_ref[:, sl])
        v_bf = v.astype(BF16)

        p = logf * LOG2_E
        tb = p
        a_lev = [q.astype(BF16)]
        b_lev = [k.astype(BF16)]
        for lev in range(1, n_lev):
            hs = 1 << (lev - 1)
            right = (row & hs) != 0
            e = jnp.exp2(jnp.where(right, p, tb - p))
            zl = (jnp.where(right, q, k) * e).astype(BF16)
            a_lev.append(zl)
            b_lev.append(zl)
            dn = pltpu.roll(tb, hs, 0)
            up = pltpu.roll(tb, c - hs, 0)
            p = p + jnp.where(right, dn, 0.0)
            tb = tb + jnp.where(right, dn, up)
        ops = []
        for g in range(n_groups):
            lo, hi = g * gsz, min((g + 1) * gsz, n_lev)
            n_in = hi - lo
            a_g = jnp.concatenate(a_lev[lo:hi], axis=0) if n_in > 1 else a_lev[lo]
            b_g = a_g if lo > 0 else (
                jnp.concatenate(b_lev[lo:hi], axis=0) if n_in > 1 else b_lev[lo])
            ops.append((a_g, b_g, n_in))
        q_in = (q * jnp.exp2(p)).astype(BF16)
        k_out = (k * jnp.exp2(tb - p)).astype(BF16)
        decay = jnp.exp2(jnp.broadcast_to(tb[0:1, :], (HEAD_DIM, HEAD_DIM)).T)
        return ops, v_bf, q_in, k_out, decay

    def fold_scores(g, sc, n_in):
        if c % HEAD_DIM == 0:
            return jnp.concatenate(
                [sc[i * c:(i + 1) * c, i * c:(i + 1) * c].astype(BF16)
                 * mask_ref[g, i * c:(i + 1) * c, i * c:(i + 1) * c]
                 for i in range(n_in)], axis=1)
        r = n_in * c
        sc = sc * mask_ref[g][:r, :r]
        fold = sc[0:c]
        for i in range(1, n_in):
            fold = fold + sc[i * c:(i + 1) * c]
        return fold.astype(BF16)

    for h0 in range(0, nh, hgrp):
        heads = range(h0, min(h0 + hgrp, nh))
        prep = {h: head_operands(h) for h in heads}
        scores = {h: [lax.dot_general(a_g, b_g, nt, preferred_element_type=F32)
                      for a_g, b_g, _ in prep[h][0]] for h in heads}
        upd = {h: lax.dot_general(prep[h][3], prep[h][1], tn, preferred_element_type=F32)
               for h in heads}
        for h in heads:
            ops, v_bf, q_in, _, decay = prep[h]
            st = st_scr[h]
            lhs = [q_in]
            rhs = [st.astype(BF16)]
            for g, (sc, (_, _, n_in)) in enumerate(zip(scores[h], ops)):
                lhs.append(fold_scores(g, sc, n_in))
                rhs.extend([v_bf] * n_in)
            o = jnp.dot(jnp.concatenate(lhs, axis=1), jnp.concatenate(rhs, axis=0),
                        preferred_element_type=F32)
            st_scr[h] = st * decay + upd[h]
            sl = slice(h * HEAD_DIM, (h + 1) * HEAD_DIM)
            ms = jnp.mean(o * o, axis=-1, keepdims=True)
            o = o * lax.rsqrt(ms + EPS) * gain_ref[:, sl] * jax.nn.silu(g_ref[:, sl])
            o_ref[:, sl] = o.astype(o_ref.dtype)

    @pl.when(ci == nc - 1)
    def _():
        s_ref[0] = st_scr[...]


def _hgrn(main, lb, gain, s0, o_all, *, row_off, n_seq, seq_len, c, nh, hgrp):
    dh = lb.shape[-1]
    n_heads = dh // HEAD_DIM
    w = nh * HEAD_DIM
    n_hg = n_heads // nh
    n_chunks = seq_len // c
    rb0 = row_off // c
    masks, n_lev, gsz, n_groups = _hgrn_level_masks(c)
    r = masks.shape[-1]

    def col_spec(group):
        return pl.BlockSpec(
            (c, w), lambda b, hg, ci: (rb0 + b * n_chunks + ci, group * n_hg + hg))

    vec_spec = pl.BlockSpec((1, w), lambda b, hg, ci: (0, hg))
    st_spec = pl.BlockSpec((1, nh, HEAD_DIM, HEAD_DIM), lambda b, hg, ci: (b, hg, 0, 0))
    o, s_new = pl.pallas_call(
        functools.partial(_hgrn_kernel, c=c, nh=nh, hgrp=hgrp, n_lev=n_lev, gsz=gsz,
                          n_groups=n_groups),
        grid=(n_seq, n_hg, n_chunks),
        in_specs=[col_spec(0), col_spec(1), col_spec(2), col_spec(3), vec_spec, vec_spec,
                  st_spec, _resident((n_groups, r, r), lambda b, hg, ci: (0, 0, 0)),
                  pl.BlockSpec(memory_space=pl.ANY)],
        out_specs=[
            pl.BlockSpec((c, w), lambda b, hg, ci: (rb0 + b * n_chunks + ci, hg)),
            st_spec,
        ],
        out_shape=[
            jax.ShapeDtypeStruct(o_all.shape, o_all.dtype),
            jax.ShapeDtypeStruct((n_seq, n_heads, HEAD_DIM, HEAD_DIM), F32),
        ],
        input_output_aliases={8: 0},
        scratch_shapes=[pltpu.VMEM((nh, HEAD_DIM, HEAD_DIM), F32)],
        compiler_params=_cparams(("parallel", "parallel", "arbitrary")),
        name="hgrn_c%d" % c,
    )(main, main, main, main, lb.reshape(1, dh), gain.reshape(1, dh), s0,
      jnp.asarray(masks, BF16 if c % HEAD_DIM == 0 else F32), o_all)
    return o, s_new


def _cmul(x, la, lb):
    return x * la + pltpu.roll(x, S5_STATE, 1) * lb


def _s5_kernel(u_ref, kt_ref, pb_ref, ct_ref, lam_ref, x0_ref, y_ref, sp_ref, ss_ref,
               m_scr, b_scr, ct_scr, acat_scr, z_scr, xprev_scr, *, n_p, n_b, n_cs, rb):
    n_rows = n_p + n_b * n_cs
    sw = 2 * S5_STATE
    cw = S5_SLAB * S5_GROUP

    gi = lax.broadcasted_iota(jnp.int32, (cw, cw), 0) // S5_GROUP
    gj = lax.broadcasted_iota(jnp.int32, (cw, cw), 1) // S5_GROUP
    same_group = gi == gj
    lag_tiles = [
        jnp.where(same_group, jnp.concatenate([kt_ref[0, j]] * S5_SLAB, axis=0), 0.0).astype(BF16)
        for j in range(S5_CHUNK)]
    zero_tile = jnp.zeros((cw, cw), BF16)
    for s in range(S5_CHUNK):
        for t in range(S5_CHUNK):
            m_scr[s * cw:(s + 1) * cw, t * cw:(t + 1) * cw] = (
                lag_tiles[t - s] if t >= s else zero_tile)
    ri = lax.broadcasted_iota(jnp.int32, (cw, S5_SLAB * sw), 0) // S5_GROUP
    ci = lax.broadcasted_iota(jnp.int32, (cw, S5_SLAB * sw), 1) // sw
    own_state = ri == ci
    for s in range(S5_CHUNK):
        b_scr[s * cw:(s + 1) * cw, :] = jnp.where(
            own_state, jnp.concatenate([pb_ref[0, s]] * S5_SLAB, axis=1), 0.0).astype(BF16)
        ct_scr[s * cw:(s + 1) * cw, :] = jnp.where(
            own_state, jnp.concatenate([ct_ref[0, s]] * S5_SLAB, axis=1), 0.0).astype(BF16)

    for r0 in range(0, n_rows, rb):
        a = jnp.concatenate(
            [u_ref[pl.ds(S5_CHUNK * r0 + s, rb, stride=S5_CHUNK), :].astype(BF16)
             for s in range(S5_CHUNK)], axis=1)
        acat_scr[r0:r0 + rb, :] = a
        z = jnp.dot(a, b_scr[...], preferred_element_type=F32)
        for g in range(S5_SLAB):
            z_scr[g, r0:r0 + rb, :] = z[:, g * sw:(g + 1) * sw]

    rowi = lax.broadcasted_iota(jnp.int32, (n_p, sw), 0)

    def scan_group(g, carry):
        lam_g = lam_ref[0, g]
        x = z_scr[g, 0:n_p, :]
        for lev in range(int(math.log2(n_p))):
            sh = 1 << lev
            xs = jnp.where(rowi >= sh, pltpu.roll(x, sh, 0), 0.0)
            x = x + _cmul(xs, lam_g[2 * lev:2 * lev + 1], lam_g[2 * lev + 1:2 * lev + 2])
        xprev_scr[g, 0:n_p, :] = jnp.where(rowi >= 1, pltpu.roll(x, 1, 0), 0.0)
        sp_ref[0, g] = x[n_p - 1:n_p]

        xs = x0_ref[0, g]
        for ci in range(n_cs):
            rows = pl.ds(n_p + ci, n_b, stride=n_cs)
            xprev_scr[g, rows, :] = xs
            xs = _cmul(xs, lam_g[0:1], lam_g[1:2]) + z_scr[g, rows, :]
        ss_ref[0, g] = xs
        return carry

    lax.fori_loop(0, S5_SLAB, scan_group, 0)

    for r0 in range(0, n_rows, rb):
        x_prev = jnp.concatenate(
            [xprev_scr[g, r0:r0 + rb, :].astype(BF16) for g in range(S5_SLAB)], axis=1)
        y = (jnp.dot(acat_scr[r0:r0 + rb, :], m_scr[...], preferred_element_type=F32)
             + lax.dot_general(x_prev, ct_scr[...], (((1,), (1,)), ((), ())),
                               preferred_element_type=F32))
        for t in range(S5_CHUNK):
            y_ref[pl.ds(S5_CHUNK * r0 + t, rb, stride=S5_CHUNK), :] = y[:, t * cw:(t + 1) * cw]


def _s5(main, u_col0, kt, pb, ct, lam, x0, *, n_p, n_b, n_cs):
    n_slabs = kt.shape[0]
    n_tok = main.shape[0]
    n_rows = n_p + n_b * n_cs
    assert n_rows * S5_CHUNK == n_tok
    n_lam = lam.shape[2]
    cw = S5_SLAB * S5_GROUP
    kw = S5_CHUNK * cw
    gw = 2 * S5_STATE
    sw = S5_SLAB * gw
    rb = max(r for r in range(16, 273, 16) if n_rows % r == 0)
    cb0 = u_col0 // cw

    def per_slab(shape):
        nd = len(shape)
        return pl.BlockSpec((1,) + shape, lambda v: (v,) + (0,) * nd)

    return pl.pallas_call(
        functools.partial(_s5_kernel, n_p=n_p, n_b=n_b, n_cs=n_cs, rb=rb),
        grid=(n_slabs,),
        in_specs=[_resident((n_tok, cw), lambda v: (0, cb0 + v)),
                  per_slab((S5_CHUNK, S5_GROUP, cw)),
                  per_slab((S5_CHUNK, cw, 2 * S5_STATE)),
                  per_slab((S5_CHUNK, cw, 2 * S5_STATE)),
                  per_slab((S5_SLAB, n_lam, gw)), per_slab((S5_SLAB, n_b, gw))],
        out_specs=[_resident((n_tok, cw), lambda v: (0, v)),
                   per_slab((S5_SLAB, 1, gw)), per_slab((S5_SLAB, n_b, gw))],
        out_shape=[
            jax.ShapeDtypeStruct((n_tok, n_slabs * cw), F32),
            jax.ShapeDtypeStruct((n_slabs, S5_SLAB, 1, gw), F32),
            jax.ShapeDtypeStruct((n_slabs, S5_SLAB, n_b, gw), F32),
        ],
        scratch_shapes=[pltpu.VMEM((kw, kw), BF16), pltpu.VMEM((kw, sw), BF16),
                        pltpu.VMEM((kw, sw), BF16), pltpu.VMEM((n_rows, kw), BF16),
                        pltpu.VMEM((S5_SLAB, n_rows, 2 * S5_STATE), F32),
                        pltpu.VMEM((S5_SLAB, n_rows, 2 * S5_STATE), F32)],
        compiler_params=pltpu.CompilerParams(dimension_semantics=("parallel",),
                                             vmem_limit_bytes=S5_VMEM_LIMIT),
        name="s5",
    )(main, kt, pb, ct, lam, x0)


def _s5_tables(a_log_neg_re, a_im, log_dt, b_re, b_im, c_re, c_im, n_scan_lev):
    hp = lax.Precision.HIGHEST
    lam_re = -jnp.exp(a_log_neg_re.astype(F32))
    lam_im = a_im.astype(F32)
    dt = jnp.exp(log_dt.astype(F32))[..., None]

    def powers(jj):
        e = jj[None, None, :, None]
        mag = jnp.exp((lam_re * dt)[:, :, None, :] * e)
        ang = (lam_im * dt)[:, :, None, :] * e
        return mag * jnp.cos(ang), mag * jnp.sin(ang)

    pw_re, pw_im = powers(jnp.arange(S5_CHUNK + 1, dtype=F32))
    num_re, num_im = pw_re[:, :, 1] - 1.0, pw_im[:, :, 1]
    den = lam_re * lam_re + lam_im * lam_im
    zoh_re = ((num_re * lam_re + num_im * lam_im) / den)[..., None]
    zoh_im = ((num_im * lam_re - num_re * lam_im) / den)[..., None]
    b_re, b_im = b_re.astype(F32), b_im.astype(F32)
    bb_re = zoh_re * b_re - zoh_im * b_im
    bb_im = zoh_re * b_im + zoh_im * b_re
    c_re, c_im = c_re.astype(F32)[:, :, None], c_im.astype(F32)[:, :, None]
    pr, pi = pw_re[:, :, :, None, :], pw_im[:, :, :, None, :]
    cp_re = c_re * pr - c_im * pi
    cp_im = c_re * pi + c_im * pr
    kj = (jnp.einsum('lgjpn,lgnq->lgjpq', cp_re[:, :, :S5_CHUNK], bb_re, precision=hp)
          - jnp.einsum('lgjpn,lgnq->lgjpq', cp_im[:, :, :S5_CHUNK], bb_im, precision=hp))
    qr, qi = powers(jnp.asarray(np.arange(S5_CHUNK - 1, -1, -1), F32))
    qr, qi = qr[..., None], qi[..., None]
    pb_re = (qr * bb_re[:, :, None] - qi * bb_im[:, :, None]).transpose(0, 1, 2, 4, 3)
    pb_im = (qr * bb_im[:, :, None] + qi * bb_re[:, :, None]).transpose(0, 1, 2, 4, 3)
    bst = jnp.concatenate([pb_re, pb_im], axis=-1)
    cst = jnp.concatenate([cp_re[:, :, 1:], -cp_im[:, :, 1:]], axis=-1)
    lp_re, lp_im = powers(S5_CHUNK * (2.0 ** jnp.arange(n_scan_lev, dtype=F32)))
    la = jnp.concatenate([lp_re, lp_re], axis=-1)
    lb = jnp.concatenate([-lp_im, lp_im], axis=-1)
    lam_tab = jnp.stack([la, lb], axis=3)

    l, g = kj.shape[:2]
    v, sg = g // S5_SLAB, S5_SLAB
    cw = sg * S5_GROUP
    sw = sg * 2 * S5_STATE
    kt = kj.reshape(l, v, sg, S5_CHUNK, S5_GROUP, S5_GROUP)
    kt = kt.transpose(0, 1, 3, 5, 2, 4).reshape(l, v, S5_CHUNK, S5_GROUP, cw)
    pb = bst.reshape(l, v, sg, S5_CHUNK, S5_GROUP, 2 * S5_STATE)
    pb = pb.transpose(0, 1, 3, 2, 4, 5).reshape(l, v, S5_CHUNK, cw, 2 * S5_STATE)
    ct = cst.reshape(l, v, sg, S5_CHUNK, S5_GROUP, 2 * S5_STATE)
    ct = ct.transpose(0, 1, 3, 2, 4, 5).reshape(l, v, S5_CHUNK, cw, 2 * S5_STATE)
    lam_big = lam_tab.reshape(l, v, sg, 2 * n_scan_lev, 2 * S5_STATE)
    return kt, pb, ct, lam_big


def _mix_kernel(x_ref, gh_ref, gs_ref, oh_ref, y_ref, u_ref, d_ref, bglu_ref,
                wglu_ref, wbh_ref, wbs_ref, wout_ref, out_ref):
    ys = jax.nn.gelu(y_ref[...] + d_ref[...] * u_ref[...])
    gate = jnp.dot(ys.astype(BF16), wglu_ref[...], preferred_element_type=F32) + bglu_ref[...]
    glu = (ys * jax.nn.sigmoid(gate)).astype(BF16)
    t_h = jnp.dot(oh_ref[...], wbh_ref[...], preferred_element_type=F32)
    t_s = jnp.dot(glu, wbs_ref[...], preferred_element_type=F32)
    mix = gh_ref[...].astype(F32) * t_h + gs_ref[...].astype(F32) * t_s
    out_ref[...] = x_ref[...] + jnp.dot(mix.astype(BF16), wout_ref[...],
                                        preferred_element_type=F32)


def _mix(x, gates, o_h, y_s, main, u_col_block, d_skip, b_glu, w_glu, w_bh, w_bs, w_out, tm):
    t, ds = y_s.shape
    d = w_bh.shape[1]
    const = lambda i: (0, 0)
    return pl.pallas_call(
        _mix_kernel,
        grid=(t // tm,),
        in_specs=[
            pl.BlockSpec((tm, d), lambda i: (i, 0)),
            pl.BlockSpec((tm, d), lambda i: (i, 0)),
            pl.BlockSpec((tm, d), lambda i: (i, 1)),
            pl.BlockSpec((tm, ds), lambda i: (i, 0)),
            pl.BlockSpec((tm, ds), lambda i: (i, 0)),
            pl.BlockSpec((tm, ds), lambda i: (i, u_col_block)),
            pl.BlockSpec((1, ds), const),
            pl.BlockSpec((1, ds), const),
            _resident((ds, ds), const),
            _resident((ds, d), const),
            _resident((ds, d), const),
            _resident((d, d), const),
        ],
        out_specs=pl.BlockSpec((tm, d), lambda i: (i, 0)),
        out_shape=jax.ShapeDtypeStruct((t, d), F32),
        compiler_params=_cparams(("parallel",)),
        name="mix",
    )(x, gates, gates, o_h, y_s, main, d_skip.reshape(1, ds), b_glu.reshape(1, ds),
      w_glu, w_bh, w_bs, w_out)


def _rms(x, w):
    ms = jnp.mean(x * x, axis=-1, keepdims=True)
    return x * lax.rsqrt(ms + EPS) * w


def _ffn_kernel(x_ref, n2_ref, wg_ref, wu_ref, wd_ref, fn_ref, out_ref, h_scr, *, final_norm):
    j = pl.program_id(1)
    nj = pl.num_programs(1)

    @pl.when(j == 0)
    def _():
        xn = x_ref[...]
        out_ref[...] = xn
        h_scr[...] = _rms(xn, n2_ref[...]).astype(BF16)

    h = h_scr[...]
    ga = jnp.dot(h, wg_ref[...], preferred_element_type=F32)
    up = jnp.dot(h, wu_ref[...], preferred_element_type=F32)
    act = (jax.nn.silu(ga) * up).astype(BF16)
    out_ref[...] += jnp.dot(act, wd_ref[...], preferred_element_type=F32)

    if final_norm:
        @pl.when(j == nj - 1)
        def _():
            out_ref[...] = _rms(out_ref[...], fn_ref[...])


def _ffn(x, norm2, w_gate_up, w_down, fnorm, final_norm, tm, tf):
    t, d = x.shape
    dff = w_down.shape[0]
    nj = dff // tf
    return pl.pallas_call(
        functools.partial(_ffn_kernel, final_norm=final_norm),
        grid=(t // tm, nj),
        in_specs=[
            pl.BlockSpec((tm, d), lambda i, j: (i, 0)),
            pl.BlockSpec((1, d), lambda i, j: (0, 0)),
            pl.BlockSpec((d, tf), lambda i, j: (0, j)),
            pl.BlockSpec((d, tf), lambda i, j: (0, nj + j)),
            pl.BlockSpec((tf, d), lambda i, j: (j, 0)),
            pl.BlockSpec((1, d), lambda i, j: (0, 0)),
        ],
        out_specs=pl.BlockSpec((tm, d), lambda i, j: (i, 0)),
        out_shape=jax.ShapeDtypeStruct((t, d), F32),
        scratch_shapes=[pltpu.VMEM((tm, d), BF16)],
        compiler_params=_cparams(("parallel", "arbitrary")),
        name="ffn",
    )(x, norm2.reshape(1, d), w_gate_up, w_gate_up, w_down, fnorm.reshape(1, d))


def _pick(n, pref):
    t = pref
    while n % t:
        t //= 2
    return t


def _lower_bounds(lb_logits):
    p = jax.nn.softmax(lb_logits.astype(F32), axis=0)
    cs = jnp.cumsum(p, axis=0)
    return cs - cs[:1]


def kernel(x_prompt, x_sample, state_hgrn, state_s5_re, state_s5_im, lb_logits, norm1, w_in, hgrn_norm, w_bh, s5_a_log_neg_re, s5_a_im, s5_log_dt, s5_b_re, s5_b_im, s5_c_re, s5_c_im, s5_d, w_glu, b_glu, w_bs, w_out, norm2, w_gate_up, w_down, final_norm):
    depth = w_in.shape[0]
    bp, tp, d = x_prompt.shape
    bs, ts, _ = x_sample.shape
    assert bp == 1
    dh = lb_logits.shape[1]
    ds = s5_d.shape[1]
    n_groups = ds // S5_GROUP
    n_heads = dh // HEAD_DIM
    n_tok_p = bp * tp
    n_tok_s = bs * ts
    n_tok = n_tok_p + n_tok_s

    n_main = 4 * dh + ds
    w_in_b = jnp.concatenate([w_in[:, :, n_main:], w_in[:, :, :n_main]], axis=-1).astype(BF16)
    w_bh_b = w_bh.astype(BF16)
    w_glu_b = w_glu.astype(BF16)
    w_bs_b = w_bs.astype(BF16)
    w_out_b = w_out.astype(BF16)
    w_gu_b = w_gate_up.astype(BF16)
    w_dn_b = w_down.astype(BF16)
    lbs = _lower_bounds(lb_logits)

    n_p = tp // S5_CHUNK
    n_cs = ts // S5_CHUNK
    s5_kt, s5_pb, s5_ct, s5_lam = _s5_tables(s5_a_log_neg_re, s5_a_im, s5_log_dt, s5_b_re,
                                             s5_b_im, s5_c_re, s5_c_im, int(math.log2(n_p)))
    n_slabs = n_groups // S5_SLAB

    def pack_state(re, im):
        x0 = jnp.concatenate([re, im], axis=-1)
        return x0.reshape(x0.shape[0], n_slabs, S5_SLAB, 2 * S5_STATE).transpose(1, 2, 0, 3)

    def unpack_state(st):
        st = st.transpose(2, 0, 1, 3).reshape(st.shape[2], n_groups, 2, S5_STATE)
        return st[:, :, 0], st[:, :, 1]

    tm_in = _pick(n_tok, 1024)
    tn_in = 1024
    tm_mix = _pick(n_tok, 256)
    tm_ffn = _pick(n_tok, 1024)
    tf = _pick(w_down.shape[1], 512)
    c_p = _pick(tp, 128)
    c_s = _pick(ts, 128)

    x = jnp.concatenate([x_prompt.reshape(n_tok_p, d), x_sample.reshape(n_tok_s, d)], axis=0)
    zero_h = jnp.zeros((bp, n_heads, HEAD_DIM, HEAD_DIM), F32)

    new_h_p, new_re_p, new_im_p, new_h_s, new_re_s, new_im_s = [], [], [], [], [], []
    for l in range(depth):
        gates, main = _inproj(x, norm1[l], w_in_b[l], 2 * d, tm_in, tn_in)

        o_h = jnp.zeros((n_tok, dh), BF16)
        o_h, sh_p = _hgrn(main, lbs[l], hgrn_norm[l], zero_h, o_h,
                          row_off=0, n_seq=bp, seq_len=tp, c=c_p, nh=n_heads, hgrp=n_heads)
        o_h, sh_s = _hgrn(main, lbs[l], hgrn_norm[l], state_hgrn[l], o_h,
                          row_off=n_tok_p, n_seq=bs, seq_len=ts, c=c_s, nh=n_heads, hgrp=n_heads)

        y, st_p, st_s = _s5(main, 4 * dh, s5_kt[l], s5_pb[l], s5_ct[l], s5_lam[l],
                            pack_state(state_s5_re[l], state_s5_im[l]),
                            n_p=n_p, n_b=bs, n_cs=n_cs)

        x = _mix(x, gates, o_h, y, main, (4 * dh) // ds, s5_d[l], b_glu[l],
                 w_glu_b[l], w_bh_b[l], w_bs_b[l], w_out_b[l], tm_mix)
        x = _ffn(x, norm2[l], w_gu_b[l], w_dn_b[l], final_norm, l == depth - 1, tm_ffn, tf)

        new_h_p.append(sh_p)
        new_h_s.append(sh_s)
        re_p, im_p = unpack_state(st_p)
        re_s, im_s = unpack_state(st_s)
        new_re_p.append(re_p)
        new_im_p.append(im_p)
        new_re_s.append(re_s)
        new_im_s.append(im_s)

    y_prompt = x[:n_tok_p].reshape(bp, tp, d)
    y_sample = x[n_tok_p:].reshape(bs, ts, d)
    return (y_prompt, y_sample, jnp.stack(new_h_p), jnp.stack(new_re_p), jnp.stack(new_im_p),
            jnp.stack(new_h_s), jnp.stack(new_re_s), jnp.stack(new_im_s))
```

```python
import functools
import math

import jax
import jax.numpy as jnp
import numpy as np
from jax import lax
from jax.experimental import pallas as pl
from jax.experimental.pallas import tpu as pltpu

F32 = jnp.float32
BF16 = jnp.bfloat16

EPS = 1e-6
LOG2_E = 1.4426950408889634
HEAD_DIM = 128
S5_GROUP = 16
S5_STATE = 64
S5_CHUNK = 16
S5_SLAB = 8
HGRN_STACK_ROWS = 256
VMEM_LIMIT = 56 * 1024 * 1024
S5_VMEM_LIMIT = 60 * 1024 * 1024


def _cparams(sem):
    return pltpu.CompilerParams(dimension_semantics=sem, vmem_limit_bytes=VMEM_LIMIT)


def _resident(shape, index_map):
    return pl.BlockSpec(shape, index_map, pipeline_mode=pl.Buffered(1))


def _row_tile_specs(xs, tm):
    d = xs[0].shape[1]
    tiles_a = xs[0].shape[0] // tm
    if len(xs) == 1:
        return [pl.BlockSpec((tm, d), lambda i, *_: (i, 0))], tiles_a
    assert len(xs) == 2 and xs[0].shape[0] % tm == 0 and xs[1].shape[0] % tm == 0
    return [
        pl.BlockSpec((tm, d), lambda i, *_: (jnp.minimum(i, tiles_a - 1), 0)),
        pl.BlockSpec((tm, d), lambda i, *_: (jnp.maximum(i - tiles_a, 0), 0),
                     pipeline_mode=pl.Buffered(1)),
    ], tiles_a


def _for_row_source(x_refs, tiles_a, pred, body):
    i = pl.program_id(0)
    if len(x_refs) == 1:
        conds = [pred]
    else:
        conds = [pred & (i < tiles_a), pred & (i >= tiles_a)]
    for cond, x_ref in zip(conds, x_refs):
        pl.when(cond)(functools.partial(body, x_ref))


def _inproj_kernel(*refs, n_src, tiles_a, n_gate_tiles):
    x_refs = refs[:n_src]
    nw_ref, w_ref, gate_ref, main_ref, h_scr = refs[n_src:]
    j = pl.program_id(1)

    def normalize(x_ref):
        x = x_ref[...]
        ms = jnp.mean(x * x, axis=-1, keepdims=True)
        h_scr[...] = (x * lax.rsqrt(ms + EPS) * nw_ref[...]).astype(BF16)

    _for_row_source(x_refs, tiles_a, j == 0, normalize)

    @pl.when(j < n_gate_tiles)
    def _():
        acc = jnp.dot(h_scr[...], w_ref[...], preferred_element_type=F32)
        gate_ref[...] = jax.nn.sigmoid(acc).astype(BF16)

    @pl.when(j >= n_gate_tiles)
    def _():
        main_ref[...] = jnp.dot(h_scr[...], w_ref[...], preferred_element_type=F32)


def _inproj(xs, nw, w, n_gate_cols, tm, tn):
    t = sum(x.shape[0] for x in xs)
    d = xs[0].shape[1]
    n = w.shape[1]
    n_gate_tiles = n_gate_cols // tn
    n_tiles = n // tn
    x_specs, tiles_a = _row_tile_specs(xs, tm)
    return pl.pallas_call(
        functools.partial(_inproj_kernel, n_src=len(xs), tiles_a=tiles_a,
                          n_gate_tiles=n_gate_tiles),
        grid=(t // tm, n_tiles),
        in_specs=x_specs + [
            pl.BlockSpec((1, d), lambda i, j: (0, 0)),
            pl.BlockSpec((d, tn), lambda i, j: (0, j)),
        ],
        out_specs=[
            pl.BlockSpec((tm, tn), lambda i, j: (i, jnp.minimum(j, n_gate_tiles - 1))),
            pl.BlockSpec((tm, tn), lambda i, j: (i, jnp.maximum(j - n_gate_tiles, 0))),
        ],
        out_shape=[
            jax.ShapeDtypeStruct((t, n_gate_cols), BF16),
            jax.ShapeDtypeStruct((t, n - n_gate_cols), F32),
        ],
        scratch_shapes=[pltpu.VMEM((tm, d), BF16)],
        compiler_params=_cparams(("parallel", "arbitrary")),
        name="inproj",
    )(*xs, nw.reshape(1, d), w)


def _hgrn_level_masks(c):
    n_lev = int(math.log2(c)) + 1
    gsz = max(1, min(n_lev, HGRN_STACK_ROWS // c))
    n_groups = -(-n_lev // gsz)
    r = gsz * c
    t = np.arange(c)[:, None]
    s = np.arange(c)[None, :]
    masks = np.zeros((n_groups, r, r), np.float32)
    for lev in range(n_lev):
        if lev == 0:
            m = (t == s)
        else:
            h = 1 << (lev - 1)
            m = ((t & h) != 0) & ((s & h) == 0) & ((t // (2 * h)) == (s // (2 * h)))
        g, i = divmod(lev, gsz)
        masks[g, i * c:(i + 1) * c, i * c:(i + 1) * c] = m
    return masks, n_lev, gsz, n_groups


def _hgrn_kernel(q_ref, f_ref, i_ref, g_ref, lb_ref, gain_ref, s0_ref, mask_ref, o_all_ref,
                 o_ref, s_ref, st_scr, *, c, nh, hgrp, n_lev, gsz, n_groups):
    ci = pl.program_id(2)
    nc = pl.num_programs(2)

    @pl.when(ci == 0)
    def _():
        st_scr[...] = s0_ref[0]

    row = lax.broadcasted_iota(jnp.int32, (c, HEAD_DIM), 0)
    nt = (((1,), (1,)), ((), ()))
    tn = (((0,), (0,)), ((), ()))

    def head_operands(h):
        sl = slice(h * HEAD_DIM, (h + 1) * HEAD_DIM)
        z = f_ref[:, sl]
        lb = lb_ref[:, sl]
        ez = jnp.exp(-jnp.abs(z))
        log_sig = jnp.minimum(z, 0.0) - jnp.log(1.0 + ez)
        a0 = jnp.log(lb)
        a1 = jnp.log1p(-lb) + log_sig
        logf = jnp.maximum(a0, a1) + jnp.log(1.0 + jnp.exp(-jnp.abs(a0 - a1)))
        k = (1.0 - lb) * (jnp.where(z >= 0.0, ez, 1.0) / (1.0 + ez))
        q = q_ref[:, sl]
        v = jax.nn.silu(i_ref[:, sl])
        v_bf = v.astype(BF16)

        p = logf * LOG2_E
        tb = p
        a_lev = [q.astype(BF16)]
        b_lev = [k.astype(BF16)]
        for lev in range(1, n_lev):
            hs = 1 << (lev - 1)
            right = (row & hs) != 0
            e = jnp.exp2(jnp.where(right, p, tb - p))
            zl = (jnp.where(right, q, k) * e).astype(BF16)
            a_lev.append(zl)
            b_lev.append(zl)
            dn = pltpu.roll(tb, hs, 0)
            up = pltpu.roll(tb, c - hs, 0)
            p = p + jnp.where(right, dn, 0.0)
            tb = tb + jnp.where(right, dn, up)
        ops = []
        for g in range(n_groups):
            lo, hi = g * gsz, min((g + 1) * gsz, n_lev)
            n_in = hi - lo
            a_g = jnp.concatenate(a_lev[lo:hi], axis=0) if n_in > 1 else a_lev[lo]
            b_g = a_g if lo > 0 else (
                jnp.concatenate(b_lev[lo:hi], axis=0) if n_in > 1 else b_lev[lo])
            ops.append((a_g, b_g, n_in))
        q_in = (q * jnp.exp2(p)).astype(BF16)
        k_out = (k * jnp.exp2(tb - p)).astype(BF16)
        decay = jnp.exp2(jnp.broadcast_to(tb[0:1, :], (HEAD_DIM, HEAD_DIM)).T)
        return ops, v_bf, q_in, k_out, decay

    def fold_scores(g, sc, n_in):
        if c % HEAD_DIM == 0:
            return jnp.concatenate(
                [sc[i * c:(i + 1) * c, i * c:(i + 1) * c].astype(BF16)
                 * mask_ref[g, i * c:(i + 1) * c, i * c:(i + 1) * c]
                 for i in range(n_in)], axis=1)
        r = n_in * c
        sc = sc * mask_ref[g][:r, :r]
        fold = sc[0:c]
        for i in range(1, n_in):
            fold = fold + sc[i * c:(i + 1) * c]
        return fold.astype(BF16)

    for h0 in range(0, nh, hgrp):
        heads = range(h0, min(h0 + hgrp, nh))
        prep = {h: head_operands(h) for h in heads}
        scores = {h: [lax.dot_general(a_g, b_g, nt, preferred_element_type=F32)
                      for a_g, b_g, _ in prep[h][0]] for h in heads}
        upd = {h: lax.dot_general(prep[h][3], prep[h][1], tn, preferred_element_type=F32)
               for h in heads}
        for h in heads:
            ops, v_bf, q_in, _, decay = prep[h]
            st = st_scr[h]
            lhs = [q_in]
            rhs = [st.astype(BF16)]
            for g, (sc, (_, _, n_in)) in enumerate(zip(scores[h], ops)):
                lhs.append(fold_scores(g, sc, n_in))
                rhs.extend([v_bf] * n_in)
            o = jnp.dot(jnp.concatenate(lhs, axis=1), jnp.concatenate(rhs, axis=0),
                        preferred_element_type=F32)
            st_scr[h] = st * decay + upd[h]
            sl = slice(h * HEAD_DIM, (h + 1) * HEAD_DIM)
            ms = jnp.mean(o * o, axis=-1, keepdims=True)
            o = o * lax.rsqrt(ms + EPS) * gain_ref[:, sl] * jax.nn.silu(g_ref[:, sl])
            o_ref[:, sl] = o.astype(o_ref.dtype)

    @pl.when(ci == nc - 1)
    def _():
        s_ref[0] = st_scr[...]


def _hgrn(main, lb, gain, s0, o_all, *, row_off, n_seq, seq_len, c, nh, hgrp):
    dh = lb.shape[-1]
    n_heads = dh // HEAD_DIM
    w = nh * HEAD_DIM
    n_hg = n_heads // nh
    n_chunks = seq_len // c
    rb0 = row_off // c
    masks, n_lev, gsz, n_groups = _hgrn_level_masks(c)
    r = masks.shape[-1]

    def col_spec(group):
        return pl.BlockSpec(
            (c, w), lambda b, hg, ci: (rb0 + b * n_chunks + ci, group * n_hg + hg))

    vec_spec = pl.BlockSpec((1, w), lambda b, hg, ci: (0, hg))
    st_spec = pl.BlockSpec((1, nh, HEAD_DIM, HEAD_DIM), lambda b, hg, ci: (b, hg, 0, 0))
    o, s_new = pl.pallas_call(
        functools.partial(_hgrn_kernel, c=c, nh=nh, hgrp=hgrp, n_lev=n_lev, gsz=gsz,
                          n_groups=n_groups),
        grid=(n_seq, n_hg, n_chunks),
        in_specs=[col_spec(0), col_spec(1), col_spec(2), col_spec(3), vec_spec, vec_spec,
                  st_spec, _resident((n_groups, r, r), lambda b, hg, ci: (0, 0, 0)),
                  pl.BlockSpec(memory_space=pl.ANY)],
        out_specs=[
            pl.BlockSpec((c, w), lambda b, hg, ci: (rb0 + b * n_chunks + ci, hg)),
            st_spec,
        ],
        out_shape=[
            jax.ShapeDtypeStruct(o_all.shape, o_all.dtype),
            jax.ShapeDtypeStruct((n_seq, n_heads, HEAD_DIM, HEAD_DIM), F32),
        ],
        input_output_aliases={8: 0},
        scratch_shapes=[pltpu.VMEM((nh, HEAD_DIM, HEAD_DIM), F32)],
        compiler_params=_cparams(("parallel", "parallel", "arbitrary")),
        name="hgrn_c%d" % c,
    )(main, main, main, main, lb.reshape(1, dh), gain.reshape(1, dh), s0,
      jnp.asarray(masks, BF16 if c % HEAD_DIM == 0 else F32), o_all)
    return o, s_new


def _cmul(x, la, lb):
    return x * la + pltpu.roll(x, S5_STATE, 1) * lb


def _s5_kernel(u_ref, kt_ref, pb_ref, ct_ref, lam_ref, x0_ref, y_ref, sp_ref, ss_ref,
               m_scr, b_scr, ct_scr, acat_scr, z_scr, xprev_scr, *, n_p, n_b, n_cs, rb):
    n_rows = n_p + n_b * n_cs
    sw = 2 * S5_STATE
    cw = S5_SLAB * S5_GROUP

    gi = lax.broadcasted_iota(jnp.int32, (cw, cw), 0) // S5_GROUP
    gj = lax.broadcasted_iota(jnp.int32, (cw, cw), 1) // S5_GROUP
    same_group = gi == gj
    lag_tiles = [
        jnp.where(same_group, jnp.concatenate([kt_ref[0, j]] * S5_SLAB, axis=0), 0.0).astype(BF16)
        for j in range(S5_CHUNK)]
    zero_tile = jnp.zeros((cw, cw), BF16)
    for s in range(S5_CHUNK):
        for t in range(S5_CHUNK):
            m_scr[s * cw:(s + 1) * cw, t * cw:(t + 1) * cw] = (
                lag_tiles[t - s] if t >= s else zero_tile)
    ri = lax.broadcasted_iota(jnp.int32, (cw, S5_SLAB * sw), 0) // S5_GROUP
    ci = lax.broadcasted_iota(jnp.int32, (cw, S5_SLAB * sw), 1) // sw
    own_state = ri == ci
    for s in range(S5_CHUNK):
        b_scr[s * cw:(s + 1) * cw, :] = jnp.where(
            own_state, jnp.concatenate([pb_ref[0, s]] * S5_SLAB, axis=1), 0.0).astype(BF16)
        ct_scr[s * cw:(s + 1) * cw, :] = jnp.where(
            own_state, jnp.concatenate([ct_ref[0, s]] * S5_SLAB, axis=1), 0.0).astype(BF16)

    for r0 in range(0, n_rows, rb):
        a = jnp.concatenate(
            [u_ref[pl.ds(S5_CHUNK * r0 + s, rb, stride=S5_CHUNK), :].astype(BF16)
             for s in range(S5_CHUNK)], axis=1)
        acat_scr[r0:r0 + rb, :] = a
        z = jnp.dot(a, b_scr[...], preferred_element_type=F32)
        for g in range(S5_SLAB):
            z_scr[g, r0:r0 + rb, :] = z[:, g * sw:(g + 1) * sw]

    rowi = lax.broadcasted_iota(jnp.int32, (n_p, sw), 0)

    def scan_group(g, carry):
        lam_g = lam_ref[0, g]
        x = z_scr[g, 0:n_p, :]
        for lev in range(int(math.log2(n_p))):
            sh = 1 << lev
            xs = jnp.where(rowi >= sh, pltpu.roll(x, sh, 0), 0.0)
            x = x + _cmul(xs, lam_g[2 * lev:2 * lev + 1], lam_g[2 * lev + 1:2 * lev + 2])
        xprev_scr[g, 0:n_p, :] = jnp.where(rowi >= 1, pltpu.roll(x, 1, 0), 0.0)
        sp_ref[0, g] = x[n_p - 1:n_p]

        xs = x0_ref[0, g]
        for ci in range(n_cs):
            rows = pl.ds(n_p + ci, n_b, stride=n_cs)
            xprev_scr[g, rows, :] = xs
            xs = _cmul(xs, lam_g[0:1], lam_g[1:2]) + z_scr[g, rows, :]
        ss_ref[0, g] = xs
        return carry

    lax.fori_loop(0, S5_SLAB, scan_group, 0)

    for r0 in range(0, n_rows, rb):
        x_prev = jnp.concatenate(
            [xprev_scr[g, r0:r0 + rb, :].astype(BF16) for g in range(S5_SLAB)], axis=1)
        y = (jnp.dot(acat_scr[r0:r0 + rb, :], m_scr[...], preferred_element_type=F32)
             + lax.dot_general(x_prev, ct_scr[...], (((1,), (1,)), ((), ())),
                               preferred_element_type=F32))
        for t in range(S5_CHUNK):
            y_ref[pl.ds(S5_CHUNK * r0 + t, rb, stride=S5_CHUNK), :] = y[:, t * cw:(t + 1) * cw]


def _s5(main, u_col0, kt, pb, ct, lam, x0, *, n_p, n_b, n_cs):
    n_slabs = kt.shape[0]
    n_tok = main.shape[0]
    n_rows = n_p + n_b * n_cs
    assert n_rows * S5_CHUNK == n_tok
    n_lam = lam.shape[2]
    cw = S5_SLAB * S5_GROUP
    kw = S5_CHUNK * cw
    gw = 2 * S5_STATE
    sw = S5_SLAB * gw
    rb = max(r for r in range(16, 273, 16) if n_rows % r == 0)
    cb0 = u_col0 // cw

    def per_slab(shape):
        nd = len(shape)
        return pl.BlockSpec((1,) + shape, lambda v: (v,) + (0,) * nd)

    return pl.pallas_call(
        functools.partial(_s5_kernel, n_p=n_p, n_b=n_b, n_cs=n_cs, rb=rb),
        grid=(n_slabs,),
        in_specs=[_resident((n_tok, cw), lambda v: (0, cb0 + v)),
                  per_slab((S5_CHUNK, S5_GROUP, cw)),
                  per_slab((S5_CHUNK, cw, 2 * S5_STATE)),
                  per_slab((S5_CHUNK, cw, 2 * S5_STATE)),
                  per_slab((S5_SLAB, n_lam, gw)), per_slab((S5_SLAB, n_b, gw))],
        out_specs=[_resident((n_tok, cw), lambda v: (0, v)),
                   per_slab((S5_SLAB, 1, gw)), per_slab((S5_SLAB, n_b, gw))],
        out_shape=[
            jax.ShapeDtypeStruct((n_tok, n_slabs * cw), F32),
            jax.ShapeDtypeStruct((n_slabs, S5_SLAB, 1, gw), F32),
            jax.ShapeDtypeStruct((n_slabs, S5_SLAB, n_b, gw), F32),
        ],
        scratch_shapes=[pltpu.VMEM((kw, kw), BF16), pltpu.VMEM((kw, sw), BF16),
                        pltpu.VMEM((kw, sw), BF16), pltpu.VMEM((n_rows, kw), BF16),
                        pltpu.VMEM((S5_SLAB, n_rows, 2 * S5_STATE), F32),
                        pltpu.VMEM((S5_SLAB, n_rows, 2 * S5_STATE), F32)],
        compiler_params=pltpu.CompilerParams(dimension_semantics=("parallel",),
                                             vmem_limit_bytes=S5_VMEM_LIMIT),
        name="s5",
    )(main, kt, pb, ct, lam, x0)


def _s5_tables(a_log_neg_re, a_im, log_dt, b_re, b_im, c_re, c_im, n_scan_lev):
    hp = lax.Precision.HIGHEST
    lam_re = -jnp.exp(a_log_neg_re.astype(F32))
    lam_im = a_im.astype(F32)
    dt = jnp.exp(log_dt.astype(F32))[..., None]

    def powers(jj):
        e = jj[None, None, :, None]
        mag = jnp.exp((lam_re * dt)[:, :, None, :] * e)
        ang = (lam_im * dt)[:, :, None, :] * e
        return mag * jnp.cos(ang), mag * jnp.sin(ang)

    pw_re, pw_im = powers(jnp.arange(S5_CHUNK + 1, dtype=F32))
    num_re, num_im = pw_re[:, :, 1] - 1.0, pw_im[:, :, 1]
    den = lam_re * lam_re + lam_im * lam_im
    zoh_re = ((num_re * lam_re + num_im * lam_im) / den)[..., None]
    zoh_im = ((num_im * lam_re - num_re * lam_im) / den)[..., None]
    b_re, b_im = b_re.astype(F32), b_im.astype(F32)
    bb_re = zoh_re * b_re - zoh_im * b_im
    bb_im = zoh_re * b_im + zoh_im * b_re
    c_re, c_im = c_re.astype(F32)[:, :, None], c_im.astype(F32)[:, :, None]
    pr, pi = pw_re[:, :, :, None, :], pw_im[:, :, :, None, :]
    cp_re = c_re * pr - c_im * pi
    cp_im = c_re * pi + c_im * pr
    kj = (jnp.einsum('lgjpn,lgnq->lgjpq', cp_re[:, :, :S5_CHUNK], bb_re, precision=hp)
          - jnp.einsum('lgjpn,lgnq->lgjpq', cp_im[:, :, :S5_CHUNK], bb_im, precision=hp))
    qr, qi = powers(jnp.asarray(np.arange(S5_CHUNK - 1, -1, -1), F32))
    qr, qi = qr[..., None], qi[..., None]
    pb_re = (qr * bb_re[:, :, None] - qi * bb_im[:, :, None]).transpose(0, 1, 2, 4, 3)
    pb_im = (qr * bb_im[:, :, None] + qi * bb_re[:, :, None]).transpose(0, 1, 2, 4, 3)
    bst = jnp.concatenate([pb_re, pb_im], axis=-1)
    cst = jnp.concatenate([cp_re[:, :, 1:], -cp_im[:, :, 1:]], axis=-1)
    lp_re, lp_im = powers(S5_CHUNK * (2.0 ** jnp.arange(n_scan_lev, dtype=F32)))
    la = jnp.concatenate([lp_re, lp_re], axis=-1)
    lb = jnp.concatenate([-lp_im, lp_im], axis=-1)
    lam_tab = jnp.stack([la, lb], axis=3)

    l, g = kj.shape[:2]
    v, sg = g // S5_SLAB, S5_SLAB
    cw = sg * S5_GROUP
    sw = sg * 2 * S5_STATE
    kt = kj.reshape(l, v, sg, S5_CHUNK, S5_GROUP, S5_GROUP)
    kt = kt.transpose(0, 1, 3, 5, 2, 4).reshape(l, v, S5_CHUNK, S5_GROUP, cw)
    pb = bst.reshape(l, v, sg, S5_CHUNK, S5_GROUP, 2 * S5_STATE)
    pb = pb.transpose(0, 1, 3, 2, 4, 5).reshape(l, v, S5_CHUNK, cw, 2 * S5_STATE)
    ct = cst.reshape(l, v, sg, S5_CHUNK, S5_GROUP, 2 * S5_STATE)
    ct = ct.transpose(0, 1, 3, 2, 4, 5).reshape(l, v, S5_CHUNK, cw, 2 * S5_STATE)
    lam_big = lam_tab.reshape(l, v, sg, 2 * n_scan_lev, 2 * S5_STATE)
    return kt, pb, ct, lam_big


def _mix_kernel(*refs, n_src, tiles_a):
    x_refs = refs[:n_src]
    (gh_ref, gs_ref, oh_ref, y_ref, u_ref, d_ref, bglu_ref,
     wglu_ref, wbh_ref, wbs_ref, wout_ref, out_ref) = refs[n_src:]
    ys = jax.nn.gelu(y_ref[...] + d_ref[...] * u_ref[...])
    gate = jnp.dot(ys.astype(BF16), wglu_ref[...], preferred_element_type=F32) + bglu_ref[...]
    glu = (ys * jax.nn.sigmoid(gate)).astype(BF16)
    t_h = jnp.dot(oh_ref[...], wbh_ref[...], preferred_element_type=F32)
    t_s = jnp.dot(glu, wbs_ref[...], preferred_element_type=F32)
    mix = gh_ref[...].astype(F32) * t_h + gs_ref[...].astype(F32) * t_s
    x = x_refs[0][...]
    if n_src == 2:
        x = jnp.where(pl.program_id(0) < tiles_a, x, x_refs[1][...])
    out_ref[...] = x + jnp.dot(mix.astype(BF16), wout_ref[...], preferred_element_type=F32)


def _mix(xs, gates, o_h, y_s, main, u_col_block, d_skip, b_glu, w_glu, w_bh, w_bs, w_out, tm):
    t, ds = y_s.shape
    d = w_bh.shape[1]
    const = lambda i: (0, 0)
    x_specs, tiles_a = _row_tile_specs(xs, tm)
    return pl.pallas_call(
        functools.partial(_mix_kernel, n_src=len(xs), tiles_a=tiles_a),
        grid=(t // tm,),
        in_specs=x_specs + [
            pl.BlockSpec((tm, d), lambda i: (i, 0)),
            pl.BlockSpec((tm, d), lambda i: (i, 1)),
            pl.BlockSpec((tm, ds), lambda i: (i, 0)),
            pl.BlockSpec((tm, ds), lambda i: (i, 0)),
            pl.BlockSpec((tm, ds), lambda i: (i, u_col_block)),
            pl.BlockSpec((1, ds), const),
            pl.BlockSpec((1, ds), const),
            _resident((ds, ds), const),
            _resident((ds, d), const),
            _resident((ds, d), const),
            _resident((d, d), const),
        ],
        out_specs=pl.BlockSpec((tm, d), lambda i: (i, 0)),
        out_shape=jax.ShapeDtypeStruct((t, d), F32),
        compiler_params=_cparams(("parallel",)),
        name="mix",
    )(*xs, gates, gates, o_h, y_s, main, d_skip.reshape(1, ds), b_glu.reshape(1, ds),
      w_glu, w_bh, w_bs, w_out)


def _rms(x, w):
    ms = jnp.mean(x * x, axis=-1, keepdims=True)
    return x * lax.rsqrt(ms + EPS) * w


def _ffn_kernel(x_ref, n2_ref, wg_ref, wu_ref, wd_ref, fn_ref, *rest, final_norm, tiles_a):
    out_refs, h_scr = rest[:-1], rest[-1]
    i = pl.program_id(0)
    j = pl.program_id(1)
    nj = pl.num_programs(1)

    def body(out_ref):
        @pl.when(j == 0)
        def _():
            xn = x_ref[...]
            out_ref[...] = xn
            h_scr[...] = _rms(xn, n2_ref[...]).astype(BF16)

        h = h_scr[...]
        ga = jnp.dot(h, wg_ref[...], preferred_element_type=F32)
        up = jnp.dot(h, wu_ref[...], preferred_element_type=F32)
        act = (jax.nn.silu(ga) * up).astype(BF16)
        out_ref[...] += jnp.dot(act, wd_ref[...], preferred_element_type=F32)

        if final_norm:
            @pl.when(j == nj - 1)
            def _():
                out_ref[...] = _rms(out_ref[...], fn_ref[...])

    if len(out_refs) == 1:
        body(out_refs[0])
    else:
        pl.when(i < tiles_a)(functools.partial(body, out_refs[0]))
        pl.when(i >= tiles_a)(functools.partial(body, out_refs[1]))


def _ffn(x, norm2, w_gate_up, w_down, fnorm, final_norm, tm, tf, split_rows=None):
    t, d = x.shape
    dff = w_down.shape[0]
    nj = dff // tf
    if split_rows is None:
        tiles_a = t // tm
        out_specs = [pl.BlockSpec((tm, d), lambda i, j: (i, 0))]
        out_shape = [jax.ShapeDtypeStruct((t, d), F32)]
    else:
        assert split_rows % tm == 0 and (t - split_rows) % tm == 0
        tiles_a = split_rows // tm
        out_specs = [
            pl.BlockSpec((tm, d), lambda i, j: (jnp.minimum(i, tiles_a - 1), 0),
                         pipeline_mode=pl.Buffered(1)),
            pl.BlockSpec((tm, d), lambda i, j: (jnp.maximum(i - tiles_a, 0), 0),
                         pipeline_mode=pl.Buffered(1)),
        ]
        out_shape = [jax.ShapeDtypeStruct((split_rows, d), F32),
                     jax.ShapeDtypeStruct((t - split_rows, d), F32)]
    return pl.pallas_call(
        functools.partial(_ffn_kernel, final_norm=final_norm, tiles_a=tiles_a),
        grid=(t // tm, nj),
        in_specs=[
            pl.BlockSpec((tm, d), lambda i, j: (i, 0)),
            pl.BlockSpec((1, d), lambda i, j: (0, 0)),
            pl.BlockSpec((d, tf), lambda i, j: (0, j)),
            pl.BlockSpec((d, tf), lambda i, j: (0, nj + j)),
            pl.BlockSpec((tf, d), lambda i, j: (j, 0)),
            pl.BlockSpec((1, d), lambda i, j: (0, 0)),
        ],
        out_specs=out_specs,
        out_shape=out_shape,
        scratch_shapes=[pltpu.VMEM((tm, d), BF16)],
        compiler_params=_cparams(("parallel" if split_rows is None else "arbitrary", "arbitrary")),
        name="ffn",
    )(x, norm2.reshape(1, d), w_gate_up, w_gate_up, w_down, fnorm.reshape(1, d))


def _pick(n, pref):
    t = pref
    while n % t:
        t //= 2
    return t


def _lower_bounds(lb_logits):
    p = jax.nn.softmax(lb_logits.astype(F32), axis=0)
    cs = jnp.cumsum(p, axis=0)
    return cs - cs[:1]


def kernel(x_prompt, x_sample, state_hgrn, state_s5_re, state_s5_im, lb_logits, norm1, w_in, hgrn_norm, w_bh, s5_a_log_neg_re, s5_a_im, s5_log_dt, s5_b_re, s5_b_im, s5_c_re, s5_c_im, s5_d, w_glu, b_glu, w_bs, w_out, norm2, w_gate_up, w_down, final_norm):
    depth = w_in.shape[0]
    bp, tp, d = x_prompt.shape
    bs, ts, _ = x_sample.shape
    assert bp == 1
    dh = lb_logits.shape[1]
    ds = s5_d.shape[1]
    n_groups = ds // S5_GROUP
    n_heads = dh // HEAD_DIM
    n_tok_p = bp * tp
    n_tok_s = bs * ts
    n_tok = n_tok_p + n_tok_s

    n_main = 4 * dh + ds

    def layer_weights(l):
        w = w_in[l]
        return dict(
            w_in=jnp.concatenate([w[:, n_main:], w[:, :n_main]], axis=-1).astype(BF16),
            w_bh=w_bh[l].astype(BF16), w_glu=w_glu[l].astype(BF16), w_bs=w_bs[l].astype(BF16),
            w_out=w_out[l].astype(BF16), w_gu=w_gate_up[l].astype(BF16),
            w_dn=w_down[l].astype(BF16))

    lbs = _lower_bounds(lb_logits)

    n_p = tp // S5_CHUNK
    n_cs = ts // S5_CHUNK
    s5_kt, s5_pb, s5_ct, s5_lam = _s5_tables(s5_a_log_neg_re, s5_a_im, s5_log_dt, s5_b_re,
                                             s5_b_im, s5_c_re, s5_c_im, int(math.log2(n_p)))
    n_slabs = n_groups // S5_SLAB

    def pack_state(re, im):
        x0 = jnp.concatenate([re, im], axis=-1)
        return x0.reshape(x0.shape[0], n_slabs, S5_SLAB, 2 * S5_STATE).transpose(1, 2, 0, 3)

    def unpack_state(st):
        st = st.transpose(2, 0, 1, 3).reshape(st.shape[2], n_groups, 2, S5_STATE)
        return st[:, :, 0], st[:, :, 1]

    tm_in = _pick(n_tok, 1024)
    tn_in = 1024
    tm_mix = _pick(n_tok, 256)
    tm_ffn = _pick(n_tok, 1024)
    tf = _pick(w_down.shape[1], 512)
    c_p = _pick(tp, 128)
    c_s = _pick(ts, 128)

    xs = (x_prompt.reshape(n_tok_p, d), x_sample.reshape(n_tok_s, d))
    if n_tok_p % tm_in or n_tok_s % tm_in or n_tok_p % tm_mix or n_tok_s % tm_mix:
        xs = (jnp.concatenate(xs, axis=0),)
    zero_h = jnp.zeros((bp, n_heads, HEAD_DIM, HEAD_DIM), F32)

    new_h_p, new_re_p, new_im_p, new_h_s, new_re_s, new_im_s = [], [], [], [], [], []
    for l in range(depth):
        wl = layer_weights(l)
        gates, main = _inproj(xs, norm1[l], wl['w_in'], 2 * d, tm_in, tn_in)

        o_h = jnp.zeros((n_tok, dh), BF16)
        o_h, sh_p = _hgrn(main, lbs[l], hgrn_norm[l], zero_h, o_h,
                          row_off=0, n_seq=bp, seq_len=tp, c=c_p, nh=n_heads, hgrp=n_heads)
        o_h, sh_s = _hgrn(main, lbs[l], hgrn_norm[l], state_hgrn[l], o_h,
                          row_off=n_tok_p, n_seq=bs, seq_len=ts, c=c_s, nh=n_heads, hgrp=n_heads)

        y, st_p, st_s = _s5(main, 4 * dh, s5_kt[l], s5_pb[l], s5_ct[l], s5_lam[l],
                            pack_state(state_s5_re[l], state_s5_im[l]),
                            n_p=n_p, n_b=bs, n_cs=n_cs)

        x = _mix(xs, gates, o_h, y, main, (4 * dh) // ds, s5_d[l], b_glu[l],
                 wl['w_glu'], wl['w_bh'], wl['w_bs'], wl['w_out'], tm_mix)
        last = l == depth - 1
        split = n_tok_p if last and n_tok_p % tm_ffn == 0 and n_tok_s % tm_ffn == 0 else None
        xs = tuple(_ffn(x, norm2[l], wl['w_gu'], wl['w_dn'], final_norm, last, tm_ffn, tf, split))

        new_h_p.append(sh_p)
        new_h_s.append(sh_s)
        re_p, im_p = unpack_state(st_p)
        re_s, im_s = unpack_state(st_s)
        new_re_p.append(re_p)
        new_im_p.append(im_p)
        new_re_s.append(re_s)
        new_im_s.append(im_s)

    if len(xs) == 1:
        xs = (xs[0][:n_tok_p], xs[0][n_tok_p:])
    y_prompt = xs[0].reshape(bp, tp, d)
    y_sample = xs[1].reshape(bs, ts, d)
    return (y_prompt, y_sample, jnp.stack(new_h_p), jnp.stack(new_re_p), jnp.stack(new_im_p),
            jnp.stack(new_h_s), jnp.stack(new_re_s), jnp.stack(new_im_s))
```

```python
import functools
import math

import jax
import jax.numpy as jnp
import numpy as np
from jax import lax
from jax.experimental import pallas as pl
from jax.experimental.pallas import tpu as pltpu

F32 = jnp.float32
BF16 = jnp.bfloat16

EPS = 1e-6
LOG2_E = 1.4426950408889634
HEAD_DIM = 128
S5_GROUP = 16
S5_STATE = 64
S5_CHUNK = 16
S5_SLAB = 8
MXU_TILE = 256
HGRN_STACK_ROWS = MXU_TILE
VMEM_LIMIT = 56 * 1024 * 1024
S5_VMEM_LIMIT = 60 * 1024 * 1024


def _cparams(sem):
    return pltpu.CompilerParams(dimension_semantics=sem, vmem_limit_bytes=VMEM_LIMIT)


def _resident(shape, index_map):
    return pl.BlockSpec(shape, index_map, pipeline_mode=pl.Buffered(1))


def _row_tile_specs(xs, tm):
    d = xs[0].shape[1]
    tiles_a = xs[0].shape[0] // tm
    if len(xs) == 1:
        return [pl.BlockSpec((tm, d), lambda i, *_: (i, 0))], tiles_a
    assert len(xs) == 2 and xs[0].shape[0] % tm == 0 and xs[1].shape[0] % tm == 0
    return [
        pl.BlockSpec((tm, d), lambda i, *_: (jnp.minimum(i, tiles_a - 1), 0)),
        pl.BlockSpec((tm, d), lambda i, *_: (jnp.maximum(i - tiles_a, 0), 0),
                     pipeline_mode=pl.Buffered(1)),
    ], tiles_a


def _for_row_source(x_refs, tiles_a, pred, body):
    i = pl.program_id(0)
    if len(x_refs) == 1:
        conds = [pred]
    else:
        conds = [pred & (i < tiles_a), pred & (i >= tiles_a)]
    for cond, x_ref in zip(conds, x_refs):
        pl.when(cond)(functools.partial(body, x_ref))


def _inproj_kernel(*refs, n_src, tiles_a, n_gate_tiles):
    x_refs = refs[:n_src]
    nw_ref, w_ref, gate_ref, main_ref, h_scr = refs[n_src:]
    j = pl.program_id(1)

    def normalize(x_ref):
        x = x_ref[...]
        ms = jnp.mean(x * x, axis=-1, keepdims=True)
        h_scr[...] = (x * lax.rsqrt(ms + EPS) * nw_ref[...]).astype(BF16)

    _for_row_source(x_refs, tiles_a, j == 0, normalize)

    @pl.when(j < n_gate_tiles)
    def _():
        acc = jnp.dot(h_scr[...], w_ref[...], preferred_element_type=F32)
        gate_ref[...] = jax.nn.sigmoid(acc).astype(BF16)

    @pl.when(j >= n_gate_tiles)
    def _():
        main_ref[...] = jnp.dot(h_scr[...], w_ref[...], preferred_element_type=F32)


def _inproj(xs, nw, w, l, n_gate_cols, tm, tn):
    t = sum(x.shape[0] for x in xs)
    d = xs[0].shape[1]
    n = w.shape[2]
    n_gate_tiles = n_gate_cols // tn
    n_tiles = n // tn
    x_specs, tiles_a = _row_tile_specs(xs, tm)
    return pl.pallas_call(
        functools.partial(_inproj_kernel, n_src=len(xs), tiles_a=tiles_a,
                          n_gate_tiles=n_gate_tiles),
        grid=(t // tm, n_tiles),
        in_specs=x_specs + [
            pl.BlockSpec((1, d), lambda i, j: (0, 0)),
            pl.BlockSpec((None, d, tn), lambda i, j: (l, 0, j)),
        ],
        out_specs=[
            pl.BlockSpec((tm, tn), lambda i, j: (i, jnp.minimum(j, n_gate_tiles - 1))),
            pl.BlockSpec((tm, tn), lambda i, j: (i, jnp.maximum(j - n_gate_tiles, 0))),
        ],
        out_shape=[
            jax.ShapeDtypeStruct((t, n_gate_cols), BF16),
            jax.ShapeDtypeStruct((t, n - n_gate_cols), F32),
        ],
        scratch_shapes=[pltpu.VMEM((tm, d), BF16)],
        compiler_params=_cparams(("parallel", "arbitrary")),
        name="inproj",
    )(*xs, nw.reshape(1, d), w)


def _hgrn_level_masks(c):
    n_lev = int(math.log2(c)) + 1
    gsz = max(1, min(n_lev, HGRN_STACK_ROWS // c))
    n_groups = -(-n_lev // gsz)
    r = gsz * c
    t = np.arange(c)[:, None]
    s = np.arange(c)[None, :]
    masks = np.zeros((n_groups, r, r), np.float32)
    for lev in range(n_lev):
        if lev == 0:
            m = (t == s)
        else:
            h = 1 << (lev - 1)
            m = ((t & h) != 0) & ((s & h) == 0) & ((t // (2 * h)) == (s // (2 * h)))
        g, i = divmod(lev, gsz)
        masks[g, i * c:(i + 1) * c, i * c:(i + 1) * c] = m
    return masks, n_lev, gsz, n_groups


def _hgrn_kernel(q_ref, f_ref, i_ref, g_ref, lb_ref, gain_ref, s0_ref, mask_ref, *rest,
                 c, nh, hgrp, n_lev, gsz, n_groups):
    o_ref, s_ref, st_scr = rest[-3:]
    ci = pl.program_id(2)
    nc = pl.num_programs(2)

    @pl.when(ci == 0)
    def _():
        st_scr[...] = s0_ref[0]

    row = lax.broadcasted_iota(jnp.int32, (c, HEAD_DIM), 0)
    nt = (((1,), (1,)), ((), ()))
    tn = (((0,), (0,)), ((), ()))

    def head_operands(h):
        sl = slice(h * HEAD_DIM, (h + 1) * HEAD_DIM)
        z = f_ref[:, sl]
        lb = lb_ref[:, sl]
        ez = jnp.exp(-jnp.abs(z))
        log_sig = jnp.minimum(z, 0.0) - jnp.log(1.0 + ez)
        a0 = jnp.log(lb)
        a1 = jnp.log1p(-lb) + log_sig
        logf = jnp.maximum(a0, a1) + jnp.log(1.0 + jnp.exp(-jnp.abs(a0 - a1)))
        k = (1.0 - lb) * (jnp.where(z >= 0.0, ez, 1.0) / (1.0 + ez))
        q = q_ref[:, sl]
        v = jax.nn.silu(i_ref[:, sl])
        v_bf = v.astype(BF16)

        p = logf * LOG2_E
        tb = p
        a_lev = [q.astype(BF16)]
        b_lev = [k.astype(BF16)]
        for lev in range(1, n_lev):
            hs = 1 << (lev - 1)
            right = (row & hs) != 0
            e = jnp.exp2(jnp.where(right, p, tb - p))
            zl = (jnp.where(right, q, k) * e).astype(BF16)
            a_lev.append(zl)
            b_lev.append(zl)
            dn = pltpu.roll(tb, hs, 0)
            up = pltpu.roll(tb, c - hs, 0)
            p = p + jnp.where(right, dn, 0.0)
            tb = tb + jnp.where(right, dn, up)
        ops = []
        for g in range(n_groups):
            lo, hi = g * gsz, min((g + 1) * gsz, n_lev)
            n_in = hi - lo
            a_g = jnp.concatenate(a_lev[lo:hi], axis=0) if n_in > 1 else a_lev[lo]
            b_g = a_g if lo > 0 else (
                jnp.concatenate(b_lev[lo:hi], axis=0) if n_in > 1 else b_lev[lo])
            ops.append((a_g, b_g, n_in))
        q_in = (q * jnp.exp2(p)).astype(BF16)
        k_out = (k * jnp.exp2(tb - p)).astype(BF16)
        decay = jnp.exp2(jnp.broadcast_to(tb[0:1, :], (HEAD_DIM, HEAD_DIM)).T)
        return ops, v_bf, q_in, k_out, decay

    def fold_scores(g, sc, n_in):
        if c % HEAD_DIM == 0:
            return jnp.concatenate(
                [sc[i * c:(i + 1) * c, i * c:(i + 1) * c].astype(BF16)
                 * mask_ref[g, i * c:(i + 1) * c, i * c:(i + 1) * c]
                 for i in range(n_in)], axis=1)
        r = n_in * c
        sc = sc * mask_ref[g][:r, :r]
        fold = sc[0:c]
        for i in range(1, n_in):
            fold = fold + sc[i * c:(i + 1) * c]
        return fold.astype(BF16)

    for h0 in range(0, nh, hgrp):
        heads = range(h0, min(h0 + hgrp, nh))
        prep = {h: head_operands(h) for h in heads}
        scores = {h: [lax.dot_general(a_g, b_g, nt, preferred_element_type=F32)
                      for a_g, b_g, _ in prep[h][0]] for h in heads}
        upd = {h: lax.dot_general(prep[h][3], prep[h][1], tn, preferred_element_type=F32)
               for h in heads}
        for h in heads:
            ops, v_bf, q_in, _, decay = prep[h]
            st = st_scr[h]
            lhs = [q_in]
            rhs = [st.astype(BF16)]
            for g, (sc, (_, _, n_in)) in enumerate(zip(scores[h], ops)):
                lhs.append(fold_scores(g, sc, n_in))
                rhs.extend([v_bf] * n_in)
            o = jnp.dot(jnp.concatenate(lhs, axis=1), jnp.concatenate(rhs, axis=0),
                        preferred_element_type=F32)
            st_scr[h] = st * decay + upd[h]
            sl = slice(h * HEAD_DIM, (h + 1) * HEAD_DIM)
            ms = jnp.mean(o * o, axis=-1, keepdims=True)
            o = o * lax.rsqrt(ms + EPS) * gain_ref[:, sl] * jax.nn.silu(g_ref[:, sl])
            o_ref[:, sl] = o.astype(o_ref.dtype)

    @pl.when(ci == nc - 1)
    def _():
        s_ref[0] = st_scr[...]


def _hgrn(main, lb, gain, s0, o_all, s_all, *, layer, n_layers, row_off, n_seq, seq_len, c, nh,
          hgrp):
    n_tok = main.shape[0]
    dh = lb.shape[-1]
    n_heads = dh // HEAD_DIM
    w = nh * HEAD_DIM
    n_hg = n_heads // nh
    n_chunks = seq_len // c
    rb0 = row_off // c
    masks, n_lev, gsz, n_groups = _hgrn_level_masks(c)
    r = masks.shape[-1]

    def col_spec(group):
        return pl.BlockSpec(
            (c, w), lambda b, hg, ci: (rb0 + b * n_chunks + ci, group * n_hg + hg))

    vec_spec = pl.BlockSpec((1, w), lambda b, hg, ci: (0, hg))
    st_spec = pl.BlockSpec((1, nh, HEAD_DIM, HEAD_DIM), lambda b, hg, ci: (b, hg, 0, 0))
    operands = [main, main, main, main, lb.reshape(1, dh), gain.reshape(1, dh), s0,
                jnp.asarray(masks, BF16 if c % HEAD_DIM == 0 else F32)]
    in_specs = [col_spec(0), col_spec(1), col_spec(2), col_spec(3), vec_spec, vec_spec,
                st_spec, _resident((n_groups, r, r), lambda b, hg, ci: (0, 0, 0))]
    aliases = {}
    for out_idx, buf in enumerate((o_all, s_all)):
        if buf is not None:
            aliases[len(operands)] = out_idx
            operands.append(buf)
            in_specs.append(pl.BlockSpec(memory_space=pl.ANY))
    o, s_new = pl.pallas_call(
        functools.partial(_hgrn_kernel, c=c, nh=nh, hgrp=hgrp, n_lev=n_lev, gsz=gsz,
                          n_groups=n_groups),
        grid=(n_seq, n_hg, n_chunks),
        in_specs=in_specs,
        out_specs=[
            pl.BlockSpec((c, w), lambda b, hg, ci: (rb0 + b * n_chunks + ci, hg)),
            pl.BlockSpec((None, 1, nh, HEAD_DIM, HEAD_DIM),
                         lambda b, hg, ci: (layer, b, hg, 0, 0)),
        ],
        out_shape=[
            jax.ShapeDtypeStruct((n_tok, dh), BF16),
            jax.ShapeDtypeStruct((n_layers, n_seq, n_heads, HEAD_DIM, HEAD_DIM), F32),
        ],
        input_output_aliases=aliases,
        scratch_shapes=[pltpu.VMEM((nh, HEAD_DIM, HEAD_DIM), F32)],
        compiler_params=_cparams(("parallel", "parallel", "arbitrary")),
        name="hgrn_c%d" % c,
    )(*operands)
    return o, s_new


def _cmul(x, la, lb):
    return x * la + pltpu.roll(x, S5_STATE, 1) * lb


def _s5_kernel(u_ref, kt_ref, pb_ref, ct_ref, lam_ref, x0_ref, y_ref, sp_ref, ss_ref,
               m_scr, b_scr, ct_scr, z_scr, *, n_p, n_b, n_cs, rb):
    n_rows = n_p + n_b * n_cs
    sw = 2 * S5_STATE
    cw = S5_SLAB * S5_GROUP

    gi = lax.broadcasted_iota(jnp.int32, (cw, cw), 0) // S5_GROUP
    gj = lax.broadcasted_iota(jnp.int32, (cw, cw), 1) // S5_GROUP
    same_group = gi == gj
    lag_tiles = [
        jnp.where(same_group, jnp.concatenate([kt_ref[0, j]] * S5_SLAB, axis=0), 0.0).astype(BF16)
        for j in range(S5_CHUNK)]
    zero_tile = jnp.zeros((cw, cw), BF16)
    tpb = MXU_TILE // cw
    for t in range(S5_CHUNK):
        for s in range((t // tpb + 1) * tpb):
            m_scr[s * cw:(s + 1) * cw, t * cw:(t + 1) * cw] = (
                lag_tiles[t - s] if t >= s else zero_tile)
    ri = lax.broadcasted_iota(jnp.int32, (cw, S5_SLAB * sw), 0) // S5_GROUP
    ci = lax.broadcasted_iota(jnp.int32, (cw, S5_SLAB * sw), 1) // sw
    own_state = ri == ci
    for s in range(S5_CHUNK):
        b_scr[s * cw:(s + 1) * cw, :] = jnp.where(
            own_state, jnp.concatenate([pb_ref[0, s]] * S5_SLAB, axis=1), 0.0).astype(BF16)
        ct_scr[s * cw:(s + 1) * cw, :] = jnp.where(
            own_state, jnp.concatenate([ct_ref[0, s]] * S5_SLAB, axis=1), 0.0).astype(BF16)

    def chunk_operand(r0):
        return jnp.concatenate(
            [u_ref[pl.ds(S5_CHUNK * r0 + s, rb, stride=S5_CHUNK), :].astype(BF16)
             for s in range(S5_CHUNK)], axis=1)

    for r0 in range(0, n_rows, rb):
        z = jnp.dot(chunk_operand(r0), b_scr[...], preferred_element_type=F32)
        for g in range(S5_SLAB):
            z_scr[g, r0:r0 + rb, :] = z[:, g * sw:(g + 1) * sw]

    rowi = lax.broadcasted_iota(jnp.int32, (n_p, sw), 0)

    def scan_group(g, carry):
        lam_g = lam_ref[0, g]
        x = z_scr[g, 0:n_p, :]
        for lev in range(int(math.log2(n_p))):
            sh = 1 << lev
            la, lb = lam_g[2 * lev:2 * lev + 1], lam_g[2 * lev + 1:2 * lev + 2]
            if sh % 8 == 0 and sh < n_p:
                x = jnp.concatenate([x[:sh], x[sh:] + _cmul(x[:n_p - sh], la, lb)], axis=0)
            else:
                x = x + _cmul(jnp.where(rowi >= sh, pltpu.roll(x, sh, 0), 0.0), la, lb)
        z_scr[g, 0:n_p, :] = jnp.where(rowi >= 1, pltpu.roll(x, 1, 0), 0.0)
        sp_ref[0, g] = x[n_p - 1:n_p]

        xs = x0_ref[0, g]
        for ci in range(n_cs):
            rows = pl.ds(n_p + ci, n_b, stride=n_cs)
            zc = z_scr[g, rows, :]
            z_scr[g, rows, :] = xs
            xs = _cmul(xs, lam_g[0:1], lam_g[1:2]) + zc
        ss_ref[0, g] = xs
        return carry

    lax.fori_loop(0, S5_SLAB, scan_group, 0)

    for r0 in range(0, n_rows, rb):
        x_prev = jnp.concatenate(
            [z_scr[g, r0:r0 + rb, :].astype(BF16) for g in range(S5_SLAB)], axis=1)
        a = chunk_operand(r0)
        for j in range(S5_CHUNK // tpb):
            cols = slice(j * tpb * cw, (j + 1) * tpb * cw)
            y = (jnp.dot(a[:, :(j + 1) * tpb * cw], m_scr[0:(j + 1) * tpb * cw, cols],
                         preferred_element_type=F32)
                 + lax.dot_general(x_prev, ct_scr[cols, :], (((1,), (1,)), ((), ())),
                                   preferred_element_type=F32))
            for ti in range(tpb):
                t = j * tpb + ti
                y_ref[pl.ds(S5_CHUNK * r0 + t, rb, stride=S5_CHUNK), :] = (
                    y[:, ti * cw:(ti + 1) * cw])


def _s5(main, u_col0, kt, pb, ct, lam, x0, *, n_p, n_b, n_cs):
    n_slabs = kt.shape[0]
    n_tok = main.shape[0]
    n_rows = n_p + n_b * n_cs
    assert n_rows * S5_CHUNK == n_tok
    n_lam = lam.shape[2]
    cw = S5_SLAB * S5_GROUP
    kw = S5_CHUNK * cw
    gw = 2 * S5_STATE
    sw = S5_SLAB * gw
    rb = max(r for r in range(16, 273, 16) if n_rows % r == 0)
    cb0 = u_col0 // cw

    def per_slab(shape):
        nd = len(shape)
        return pl.BlockSpec((1,) + shape, lambda v: (v,) + (0,) * nd)

    return pl.pallas_call(
        functools.partial(_s5_kernel, n_p=n_p, n_b=n_b, n_cs=n_cs, rb=rb),
        grid=(n_slabs,),
        in_specs=[pl.BlockSpec((n_tok, cw), lambda v: (0, cb0 + v)),
                  per_slab((S5_CHUNK, S5_GROUP, cw)),
                  per_slab((S5_CHUNK, cw, 2 * S5_STATE)),
                  per_slab((S5_CHUNK, cw, 2 * S5_STATE)),
                  per_slab((S5_SLAB, n_lam, gw)), per_slab((S5_SLAB, n_b, gw))],
        out_specs=[_resident((n_tok, cw), lambda v: (0, v)),
                   per_slab((S5_SLAB, 1, gw)), per_slab((S5_SLAB, n_b, gw))],
        out_shape=[
            jax.ShapeDtypeStruct((n_tok, n_slabs * cw), F32),
            jax.ShapeDtypeStruct((n_slabs, S5_SLAB, 1, gw), F32),
            jax.ShapeDtypeStruct((n_slabs, S5_SLAB, n_b, gw), F32),
        ],
        scratch_shapes=[pltpu.VMEM((kw, kw), BF16), pltpu.VMEM((kw, sw), BF16),
                        pltpu.VMEM((kw, sw), BF16),
                        pltpu.VMEM((S5_SLAB, n_rows, 2 * S5_STATE), F32)],
        compiler_params=pltpu.CompilerParams(dimension_semantics=("parallel",),
                                             vmem_limit_bytes=S5_VMEM_LIMIT),
        name="s5",
    )(main, kt, pb, ct, lam, x0)


def _s5_tables(a_log_neg_re, a_im, log_dt, b_re, b_im, c_re, c_im, n_scan_lev):
    hp = lax.Precision.HIGHEST
    lam_re = -jnp.exp(a_log_neg_re.astype(F32))
    lam_im = a_im.astype(F32)
    dt = jnp.exp(log_dt.astype(F32))[..., None]

    def powers(jj):
        e = jj[None, None, :, None]
        mag = jnp.exp((lam_re * dt)[:, :, None, :] * e)
        ang = (lam_im * dt)[:, :, None, :] * e
        return mag * jnp.cos(ang), mag * jnp.sin(ang)

    pw_re, pw_im = powers(jnp.arange(S5_CHUNK + 1, dtype=F32))
    num_re, num_im = pw_re[:, :, 1] - 1.0, pw_im[:, :, 1]
    den = lam_re * lam_re + lam_im * lam_im
    zoh_re = ((num_re * lam_re + num_im * lam_im) / den)[..., None]
    zoh_im = ((num_im * lam_re - num_re * lam_im) / den)[..., None]
    b_re, b_im = b_re.astype(F32), b_im.astype(F32)
    bb_re = zoh_re * b_re - zoh_im * b_im
    bb_im = zoh_re * b_im + zoh_im * b_re
    c_re, c_im = c_re.astype(F32)[:, :, None], c_im.astype(F32)[:, :, None]
    pr, pi = pw_re[:, :, :, None, :], pw_im[:, :, :, None, :]
    cp_re = c_re * pr - c_im * pi
    cp_im = c_re * pi + c_im * pr
    kj = (jnp.einsum('lgjpn,lgnq->lgjpq', cp_re[:, :, :S5_CHUNK], bb_re, precision=hp)
          - jnp.einsum('lgjpn,lgnq->lgjpq', cp_im[:, :, :S5_CHUNK], bb_im, precision=hp))
    qr, qi = powers(jnp.asarray(np.arange(S5_CHUNK - 1, -1, -1), F32))
    qr, qi = qr[..., None], qi[..., None]
    pb_re = (qr * bb_re[:, :, None] - qi * bb_im[:, :, None]).transpose(0, 1, 2, 4, 3)
    pb_im = (qr * bb_im[:, :, None] + qi * bb_re[:, :, None]).transpose(0, 1, 2, 4, 3)
    bst = jnp.concatenate([pb_re, pb_im], axis=-1)
    cst = jnp.concatenate([cp_re[:, :, 1:], -cp_im[:, :, 1:]], axis=-1)
    lp_re, lp_im = powers(S5_CHUNK * (2.0 ** jnp.arange(n_scan_lev, dtype=F32)))
    la = jnp.concatenate([lp_re, lp_re], axis=-1)
    lb = jnp.concatenate([-lp_im, lp_im], axis=-1)
    lam_tab = jnp.stack([la, lb], axis=3)

    l, g = kj.shape[:2]
    v, sg = g // S5_SLAB, S5_SLAB
    cw = sg * S5_GROUP
    sw = sg * 2 * S5_STATE
    kt = kj.reshape(l, v, sg, S5_CHUNK, S5_GROUP, S5_GROUP)
    kt = kt.transpose(0, 1, 3, 5, 2, 4).reshape(l, v, S5_CHUNK, S5_GROUP, cw)
    pb = bst.reshape(l, v, sg, S5_CHUNK, S5_GROUP, 2 * S5_STATE)
    pb = pb.transpose(0, 1, 3, 2, 4, 5).reshape(l, v, S5_CHUNK, cw, 2 * S5_STATE)
    ct = cst.reshape(l, v, sg, S5_CHUNK, S5_GROUP, 2 * S5_STATE)
    ct = ct.transpose(0, 1, 3, 2, 4, 5).reshape(l, v, S5_CHUNK, cw, 2 * S5_STATE)
    lam_big = lam_tab.reshape(l, v, sg, 2 * n_scan_lev, 2 * S5_STATE)
    return kt, pb, ct, lam_big


def _mix_kernel(*refs, n_src, tiles_a):
    x_refs = refs[:n_src]
    (gh_ref, gs_ref, oh_ref, y_ref, u_ref, d_ref, bglu_ref,
     wglu_ref, wbh_ref, wbs_ref, wout_ref, out_ref) = refs[n_src:]
    ys = jax.nn.gelu(y_ref[...] + d_ref[...] * u_ref[...])
    gate = jnp.dot(ys.astype(BF16), wglu_ref[...], preferred_element_type=F32) + bglu_ref[...]
    glu = (ys * jax.nn.sigmoid(gate)).astype(BF16)
    t_h = jnp.dot(oh_ref[...], wbh_ref[...], preferred_element_type=F32)
    t_s = jnp.dot(glu, wbs_ref[...], preferred_element_type=F32)
    mix = gh_ref[...].astype(F32) * t_h + gs_ref[...].astype(F32) * t_s
    x = x_refs[0][...]
    if n_src == 2:
        x = jnp.where(pl.program_id(0) < tiles_a, x, x_refs[1][...])
    out_ref[...] = x + jnp.dot(mix.astype(BF16), wout_ref[...], preferred_element_type=F32)


def _mix(xs, gates, o_h, y_s, main, u_col_block, d_skip, b_glu, w_glu, w_bh, w_bs, w_out, l, tm):
    t, ds = y_s.shape
    d = w_bh.shape[2]
    layer = lambda i: (l, 0, 0)
    const = lambda i: (0, 0)
    x_specs, tiles_a = _row_tile_specs(xs, tm)
    return pl.pallas_call(
        functools.partial(_mix_kernel, n_src=len(xs), tiles_a=tiles_a),
        grid=(t // tm,),
        in_specs=x_specs + [
            pl.BlockSpec((tm, d), lambda i: (i, 0)),
            pl.BlockSpec((tm, d), lambda i: (i, 1)),
            pl.BlockSpec((tm, ds), lambda i: (i, 0)),
            pl.BlockSpec((tm, ds), lambda i: (i, 0)),
            pl.BlockSpec((tm, ds), lambda i: (i, u_col_block)),
            pl.BlockSpec((1, ds), const),
            pl.BlockSpec((1, ds), const),
            _resident((None, ds, ds), layer),
            _resident((None, ds, d), layer),
            _resident((None, ds, d), layer),
            _resident((None, d, d), layer),
        ],
        out_specs=pl.BlockSpec((tm, d), lambda i: (i, 0)),
        out_shape=jax.ShapeDtypeStruct((t, d), F32),
        compiler_params=_cparams(("parallel",)),
        name="mix",
    )(*xs, gates, gates, o_h, y_s, main, d_skip.reshape(1, ds), b_glu.reshape(1, ds),
      w_glu, w_bh, w_bs, w_out)


def _rms(x, w):
    ms = jnp.mean(x * x, axis=-1, keepdims=True)
    return x * lax.rsqrt(ms + EPS) * w


def _ffn_kernel(x_ref, n2_ref, wg_ref, wu_ref, wd_ref, fn_ref, *rest, final_norm, tiles_a):
    out_refs, h_scr = rest[:-1], rest[-1]
    i = pl.program_id(0)
    j = pl.program_id(1)
    nj = pl.num_programs(1)

    def body(out_ref):
        @pl.when(j == 0)
        def _():
            xn = x_ref[...]
            out_ref[...] = xn
            h_scr[...] = _rms(xn, n2_ref[...]).astype(BF16)

        h = h_scr[...]
        ga = jnp.dot(h, wg_ref[...], preferred_element_type=F32)
        up = jnp.dot(h, wu_ref[...], preferred_element_type=F32)
        act = (jax.nn.silu(ga) * up).astype(BF16)
        out_ref[...] += jnp.dot(act, wd_ref[...], preferred_element_type=F32)

        if final_norm:
            @pl.when(j == nj - 1)
            def _():
                out_ref[...] = _rms(out_ref[...], fn_ref[...])

    if len(out_refs) == 1:
        body(out_refs[0])
    else:
        pl.when(i < tiles_a)(functools.partial(body, out_refs[0]))
        pl.when(i >= tiles_a)(functools.partial(body, out_refs[1]))


def _ffn(x, norm2, w_gate_up, w_down, l, fnorm, final_norm, tm, tf, split_rows=None):
    t, d = x.shape
    dff = w_down.shape[1]
    nj = dff // tf
    if split_rows is None:
        tiles_a = t // tm
        out_specs = [pl.BlockSpec((tm, d), lambda i, j: (i, 0))]
        out_shape = [jax.ShapeDtypeStruct((t, d), F32)]
    else:
        assert split_rows % tm == 0 and (t - split_rows) % tm == 0
        tiles_a = split_rows // tm
        out_specs = [
            pl.BlockSpec((tm, d), lambda i, j: (jnp.minimum(i, tiles_a - 1), 0),
                         pipeline_mode=pl.Buffered(1)),
            pl.BlockSpec((tm, d), lambda i, j: (jnp.maximum(i - tiles_a, 0), 0),
                         pipeline_mode=pl.Buffered(1)),
        ]
        out_shape = [jax.ShapeDtypeStruct((split_rows, d), F32),
                     jax.ShapeDtypeStruct((t - split_rows, d), F32)]
    return pl.pallas_call(
        functools.partial(_ffn_kernel, final_norm=final_norm, tiles_a=tiles_a),
        grid=(t // tm, nj),
        in_specs=[
            pl.BlockSpec((tm, d), lambda i, j: (i, 0)),
            pl.BlockSpec((1, d), lambda i, j: (0, 0)),
            pl.BlockSpec((None, d, tf), lambda i, j: (l, 0, j)),
            pl.BlockSpec((None, d, tf), lambda i, j: (l, 0, nj + j)),
            pl.BlockSpec((None, tf, d), lambda i, j: (l, j, 0)),
            pl.BlockSpec((1, d), lambda i, j: (0, 0)),
        ],
        out_specs=out_specs,
        out_shape=out_shape,
        scratch_shapes=[pltpu.VMEM((tm, d), BF16)],
        compiler_params=_cparams(("parallel" if split_rows is None else "arbitrary", "arbitrary")),
        name="ffn",
    )(x, norm2.reshape(1, d), w_gate_up, w_gate_up, w_down, fnorm.reshape(1, d))


def _pick(n, pref):
    t = pref
    while n % t:
        t //= 2
    return t


def _lower_bounds(lb_logits):
    p = jax.nn.softmax(lb_logits.astype(F32), axis=0)
    cs = jnp.cumsum(p, axis=0)
    return cs - cs[:1]


def kernel(x_prompt, x_sample, state_hgrn, state_s5_re, state_s5_im, lb_logits, norm1, w_in, hgrn_norm, w_bh, s5_a_log_neg_re, s5_a_im, s5_log_dt, s5_b_re, s5_b_im, s5_c_re, s5_c_im, s5_d, w_glu, b_glu, w_bs, w_out, norm2, w_gate_up, w_down, final_norm):
    depth = w_in.shape[0]
    bp, tp, d = x_prompt.shape
    bs, ts, _ = x_sample.shape
    assert bp == 1
    dh = lb_logits.shape[1]
    ds = s5_d.shape[1]
    n_groups = ds // S5_GROUP
    n_heads = dh // HEAD_DIM
    n_tok_p = bp * tp
    n_tok_s = bs * ts
    n_tok = n_tok_p + n_tok_s

    n_main = 4 * dh + ds

    w_in_b = jnp.concatenate([w_in[:, :, n_main:], w_in[:, :, :n_main]], axis=-1).astype(BF16)
    w_bh_b = w_bh.astype(BF16)
    w_glu_b = w_glu.astype(BF16)
    w_bs_b = w_bs.astype(BF16)
    w_out_b = w_out.astype(BF16)
    w_gu_b = w_gate_up.astype(BF16)
    w_dn_b = w_down.astype(BF16)
    lbs = _lower_bounds(lb_logits)

    n_p = tp // S5_CHUNK
    n_cs = ts // S5_CHUNK
    s5_kt, s5_pb, s5_ct, s5_lam = _s5_tables(s5_a_log_neg_re, s5_a_im, s5_log_dt, s5_b_re,
                                             s5_b_im, s5_c_re, s5_c_im, int(math.log2(n_p)))
    n_slabs = n_groups // S5_SLAB

    def pack_state(re, im):
        x0 = jnp.concatenate([re, im], axis=-1)
        return x0.reshape(x0.shape[0], n_slabs, S5_SLAB, 2 * S5_STATE).transpose(1, 2, 0, 3)

    def unpack_state(st):
        st = st.transpose(2, 0, 1, 3).reshape(st.shape[2], n_groups, 2, S5_STATE)
        return st[:, :, 0], st[:, :, 1]

    tm_in = _pick(n_tok, 1024)
    tn_in = 1024
    tm_mix = _pick(n_tok, 256)
    tm_ffn = _pick(n_tok, 1024)
    tf = _pick(w_down.shape[1], 512)
    c_p = _pick(tp, 128)
    c_s = _pick(ts, 128)

    xs = (x_prompt.reshape(n_tok_p, d), x_sample.reshape(n_tok_s, d))
    if n_tok_p % tm_in or n_tok_s % tm_in or n_tok_p % tm_mix or n_tok_s % tm_mix:
        xs = (jnp.concatenate(xs, axis=0),)
    zero_h = jnp.zeros((bp, n_heads, HEAD_DIM, HEAD_DIM), F32)

    new_re_p, new_im_p, new_re_s, new_im_s = [], [], [], []
    hgrn_p = hgrn_s = None
    for l in range(depth):
        gates, main = _inproj(xs, norm1[l], w_in_b, l, 2 * d, tm_in, tn_in)

        o_h, hgrn_p = _hgrn(main, lbs[l], hgrn_norm[l], zero_h, None, hgrn_p,
                            layer=l, n_layers=depth, row_off=0, n_seq=bp, seq_len=tp, c=c_p,
                            nh=n_heads, hgrp=n_heads)
        o_h, hgrn_s = _hgrn(main, lbs[l], hgrn_norm[l], state_hgrn[l], o_h, hgrn_s,
                            layer=l, n_layers=depth, row_off=n_tok_p, n_seq=bs, seq_len=ts, c=c_s,
                            nh=n_heads, hgrp=n_heads)

        y, st_p, st_s = _s5(main, 4 * dh, s5_kt[l], s5_pb[l], s5_ct[l], s5_lam[l],
                            pack_state(state_s5_re[l], state_s5_im[l]),
                            n_p=n_p, n_b=bs, n_cs=n_cs)

        x = _mix(xs, gates, o_h, y, main, (4 * dh) // ds, s5_d[l], b_glu[l],
                 w_glu_b, w_bh_b, w_bs_b, w_out_b, l, tm_mix)
        last = l == depth - 1
        split = n_tok_p if last and n_tok_p % tm_ffn == 0 and n_tok_s % tm_ffn == 0 else None
        xs = tuple(_ffn(x, norm2[l], w_gu_b, w_dn_b, l, final_norm, last, tm_ffn, tf, split))

        re_p, im_p = unpack_state(st_p)
        re_s, im_s = unpack_state(st_s)
        new_re_p.append(re_p)
        new_im_p.append(im_p)
        new_re_s.append(re_s)
        new_im_s.append(im_s)

    if len(xs) == 1:
        xs = (xs[0][:n_tok_p], xs[0][n_tok_p:])
    y_prompt = xs[0].reshape(bp, tp, d)
    y_sample = xs[1].reshape(bs, ts, d)
    return (y_prompt, y_sample, hgrn_p, jnp.stack(new_re_p), jnp.stack(new_im_p),
            hgrn_s, jnp.stack(new_re_s), jnp.stack(new_im_s))
```

```python
import functools
import math

import jax
import jax.numpy as jnp
import numpy as np
from jax import lax
from jax.experimental import pallas as pl
from jax.experimental.pallas import tpu as pltpu

F32 = jnp.float32
BF16 = jnp.bfloat16

EPS = 1e-6
LOG2_E = 1.4426950408889634
HEAD_DIM = 128
S5_GROUP = 16
S5_STATE = 64
S5_CHUNK = 16
S5_SLAB = 8
MXU_TILE = 256
HGRN_STACK_ROWS = MXU_TILE
VMEM_LIMIT = 56 * 1024 * 1024
BIG_VMEM_LIMIT = 60 * 1024 * 1024


def _cparams(sem):
    return pltpu.CompilerParams(dimension_semantics=sem, vmem_limit_bytes=VMEM_LIMIT)


def _resident(shape, index_map):
    return pl.BlockSpec(shape, index_map, pipeline_mode=pl.Buffered(1))


def _row_tile_specs(xs, tm):
    d = xs[0].shape[1]
    tiles_a = xs[0].shape[0] // tm
    if len(xs) == 1:
        return [pl.BlockSpec((tm, d), lambda i, *_: (i, 0))], tiles_a
    assert len(xs) == 2 and xs[0].shape[0] % tm == 0 and xs[1].shape[0] % tm == 0
    return [
        pl.BlockSpec((tm, d), lambda i, *_: (jnp.minimum(i, tiles_a - 1), 0)),
        pl.BlockSpec((tm, d), lambda i, *_: (jnp.maximum(i - tiles_a, 0), 0),
                     pipeline_mode=pl.Buffered(1)),
    ], tiles_a


def _for_row_source(x_refs, tiles_a, pred, body):
    i = pl.program_id(0)
    if len(x_refs) == 1:
        conds = [pred]
    else:
        conds = [pred & (i < tiles_a), pred & (i >= tiles_a)]
    for cond, x_ref in zip(conds, x_refs):
        pl.when(cond)(functools.partial(body, x_ref))


def _inproj_kernel(*refs, n_src, tiles_a, n_gate_tiles):
    x_refs = refs[:n_src]
    nw_ref, w_ref, gate_ref, main_ref, h_scr = refs[n_src:]
    j = pl.program_id(1)

    def normalize(x_ref):
        x = x_ref[...]
        ms = jnp.mean(x * x, axis=-1, keepdims=True)
        h_scr[...] = (x * lax.rsqrt(ms + EPS) * nw_ref[...]).astype(BF16)

    _for_row_source(x_refs, tiles_a, j == 0, normalize)

    @pl.when(j < n_gate_tiles)
    def _():
        acc = jnp.dot(h_scr[...], w_ref[...], preferred_element_type=F32)
        gate_ref[...] = jax.nn.sigmoid(acc).astype(BF16)

    @pl.when(j >= n_gate_tiles)
    def _():
        main_ref[...] = jnp.dot(h_scr[...], w_ref[...], preferred_element_type=F32)


def _inproj(xs, nw, w, l, n_gate_cols, tm, tn):
    t = sum(x.shape[0] for x in xs)
    d = xs[0].shape[1]
    n = w.shape[2]
    n_gate_tiles = n_gate_cols // tn
    n_tiles = n // tn
    x_specs, tiles_a = _row_tile_specs(xs, tm)
    return pl.pallas_call(
        functools.partial(_inproj_kernel, n_src=len(xs), tiles_a=tiles_a,
                          n_gate_tiles=n_gate_tiles),
        grid=(t // tm, n_tiles),
        in_specs=x_specs + [
            pl.BlockSpec((1, d), lambda i, j: (0, 0)),
            pl.BlockSpec((None, d, tn), lambda i, j: (l, 0, j)),
        ],
        out_specs=[
            pl.BlockSpec((tm, tn), lambda i, j: (i, jnp.minimum(j, n_gate_tiles - 1))),
            pl.BlockSpec((tm, tn), lambda i, j: (i, jnp.maximum(j - n_gate_tiles, 0))),
        ],
        out_shape=[
            jax.ShapeDtypeStruct((t, n_gate_cols), BF16),
            jax.ShapeDtypeStruct((t, n - n_gate_cols), F32),
        ],
        scratch_shapes=[pltpu.VMEM((tm, d), BF16)],
        compiler_params=_cparams(("parallel", "arbitrary")),
        name="inproj",
    )(*xs, nw.reshape(1, d), w)


def _hgrn_level_masks(c):
    n_lev = int(math.log2(c)) + 1
    gsz = max(1, min(n_lev, HGRN_STACK_ROWS // c))
    n_groups = -(-n_lev // gsz)
    r = gsz * c
    t = np.arange(c)[:, None]
    s = np.arange(c)[None, :]
    masks = np.zeros((n_groups, r, r), np.float32)
    for lev in range(n_lev):
        if lev == 0:
            m = (t == s)
        else:
            h = 1 << (lev - 1)
            m = ((t & h) != 0) & ((s & h) == 0) & ((t // (2 * h)) == (s // (2 * h)))
        g, i = divmod(lev, gsz)
        masks[g, i * c:(i + 1) * c, i * c:(i + 1) * c] = m
    return masks, n_lev, gsz, n_groups


def _hgrn_kernel(q_ref, f_ref, i_ref, g_ref, lb_ref, gain_ref, s0_ref, mask_ref, *rest,
                 c, nh, hgrp, n_lev, gsz, n_groups):
    o_ref, s_ref, st_scr = rest[-3:]
    ci = pl.program_id(2)
    nc = pl.num_programs(2)

    @pl.when(ci == 0)
    def _():
        st_scr[...] = s0_ref[0]

    row = lax.broadcasted_iota(jnp.int32, (c, HEAD_DIM), 0)
    nt = (((1,), (1,)), ((), ()))
    tn = (((0,), (0,)), ((), ()))

    def head_operands(h):
        sl = slice(h * HEAD_DIM, (h + 1) * HEAD_DIM)
        z = f_ref[:, sl]
        lb = lb_ref[:, sl]
        ez = jnp.exp(-jnp.abs(z))
        log_sig = jnp.minimum(z, 0.0) - jnp.log(1.0 + ez)
        a0 = jnp.log(lb)
        a1 = jnp.log1p(-lb) + log_sig
        logf = jnp.maximum(a0, a1) + jnp.log(1.0 + jnp.exp(-jnp.abs(a0 - a1)))
        k = (1.0 - lb) * (jnp.where(z >= 0.0, ez, 1.0) / (1.0 + ez))
        q = q_ref[:, sl]
        v = jax.nn.silu(i_ref[:, sl])
        v_bf = v.astype(BF16)

        p = logf * LOG2_E
        tb = p
        a_lev = [q.astype(BF16)]
        b_lev = [k.astype(BF16)]
        for lev in range(1, n_lev):
            hs = 1 << (lev - 1)
            right = (row & hs) != 0
            e = jnp.exp2(jnp.where(right, p, tb - p))
            zl = (jnp.where(right, q, k) * e).astype(BF16)
            a_lev.append(zl)
            b_lev.append(zl)
            dn = pltpu.roll(tb, hs, 0)
            up = pltpu.roll(tb, c - hs, 0)
            p = p + jnp.where(right, dn, 0.0)
            tb = tb + jnp.where(right, dn, up)
        ops = []
        for g in range(n_groups):
            lo, hi = g * gsz, min((g + 1) * gsz, n_lev)
            n_in = hi - lo
            a_g = jnp.concatenate(a_lev[lo:hi], axis=0) if n_in > 1 else a_lev[lo]
            b_g = a_g if lo > 0 else (
                jnp.concatenate(b_lev[lo:hi], axis=0) if n_in > 1 else b_lev[lo])
            ops.append((a_g, b_g, n_in))
        q_in = (q * jnp.exp2(p)).astype(BF16)
        k_out = (k * jnp.exp2(tb - p)).astype(BF16)
        decay = jnp.exp2(jnp.broadcast_to(tb[0:1, :], (HEAD_DIM, HEAD_DIM)).T)
        return ops, v_bf, q_in, k_out, decay

    def fold_scores(g, sc, n_in):
        if c % HEAD_DIM == 0:
            return jnp.concatenate(
                [sc[i * c:(i + 1) * c, i * c:(i + 1) * c].astype(BF16)
                 * mask_ref[g, i * c:(i + 1) * c, i * c:(i + 1) * c]
                 for i in range(n_in)], axis=1)
        r = n_in * c
        sc = sc * mask_ref[g][:r, :r]
        fold = sc[0:c]
        for i in range(1, n_in):
            fold = fold + sc[i * c:(i + 1) * c]
        return fold.astype(BF16)

    for h0 in range(0, nh, hgrp):
        heads = range(h0, min(h0 + hgrp, nh))
        prep = {h: head_operands(h) for h in heads}
        scores = {h: [lax.dot_general(a_g, b_g, nt, preferred_element_type=F32)
                      for a_g, b_g, _ in prep[h][0]] for h in heads}
        upd = {h: lax.dot_general(prep[h][3], prep[h][1], tn, preferred_element_type=F32)
               for h in heads}
        for h in heads:
            ops, v_bf, q_in, _, decay = prep[h]
            st = st_scr[h]
            lhs = [q_in]
            rhs = [st.astype(BF16)]
            for g, (sc, (_, _, n_in)) in enumerate(zip(scores[h], ops)):
                lhs.append(fold_scores(g, sc, n_in))
                rhs.extend([v_bf] * n_in)
            o = jnp.dot(jnp.concatenate(lhs, axis=1), jnp.concatenate(rhs, axis=0),
                        preferred_element_type=F32)
            st_scr[h] = st * decay + upd[h]
            sl = slice(h * HEAD_DIM, (h + 1) * HEAD_DIM)
            ms = jnp.mean(o * o, axis=-1, keepdims=True)
            o = o * lax.rsqrt(ms + EPS) * gain_ref[:, sl] * jax.nn.silu(g_ref[:, sl])
            o_ref[:, sl] = o.astype(o_ref.dtype)

    @pl.when(ci == nc - 1)
    def _():
        s_ref[0] = st_scr[...]


def _hgrn(main, lb, gain, s0, o_all, s_all, *, layer, n_layers, row_off, n_seq, seq_len, c, nh,
          hgrp):
    n_tok = main.shape[0]
    dh = lb.shape[-1]
    n_heads = dh // HEAD_DIM
    w = nh * HEAD_DIM
    n_hg = n_heads // nh
    n_chunks = seq_len // c
    rb0 = row_off // c
    masks, n_lev, gsz, n_groups = _hgrn_level_masks(c)
    r = masks.shape[-1]

    def col_spec(group):
        return pl.BlockSpec(
            (c, w), lambda b, hg, ci: (rb0 + b * n_chunks + ci, group * n_hg + hg))

    vec_spec = pl.BlockSpec((1, w), lambda b, hg, ci: (0, hg))
    st_spec = pl.BlockSpec((1, nh, HEAD_DIM, HEAD_DIM), lambda b, hg, ci: (b, hg, 0, 0))
    operands = [main, main, main, main, lb.reshape(1, dh), gain.reshape(1, dh), s0,
                jnp.asarray(masks, BF16 if c % HEAD_DIM == 0 else F32)]
    in_specs = [col_spec(0), col_spec(1), col_spec(2), col_spec(3), vec_spec, vec_spec,
                st_spec, _resident((n_groups, r, r), lambda b, hg, ci: (0, 0, 0))]
    aliases = {}
    for out_idx, buf in enumerate((o_all, s_all)):
        if buf is not None:
            aliases[len(operands)] = out_idx
            operands.append(buf)
            in_specs.append(pl.BlockSpec(memory_space=pl.ANY))
    o, s_new = pl.pallas_call(
        functools.partial(_hgrn_kernel, c=c, nh=nh, hgrp=hgrp, n_lev=n_lev, gsz=gsz,
                          n_groups=n_groups),
        grid=(n_seq, n_hg, n_chunks),
        in_specs=in_specs,
        out_specs=[
            pl.BlockSpec((c, w), lambda b, hg, ci: (rb0 + b * n_chunks + ci, hg)),
            pl.BlockSpec((None, 1, nh, HEAD_DIM, HEAD_DIM),
                         lambda b, hg, ci: (layer, b, hg, 0, 0)),
        ],
        out_shape=[
            jax.ShapeDtypeStruct((n_tok, dh), BF16),
            jax.ShapeDtypeStruct((n_layers, n_seq, n_heads, HEAD_DIM, HEAD_DIM), F32),
        ],
        input_output_aliases=aliases,
        scratch_shapes=[pltpu.VMEM((nh, HEAD_DIM, HEAD_DIM), F32)],
        compiler_params=_cparams(("parallel", "parallel", "arbitrary")),
        name="hgrn_c%d" % c,
    )(*operands)
    return o, s_new


def _cmul(xr, xi, lr, li):
    return xr * lr - xi * li, xr * li + xi * lr


def _s5_kernel(u_ref, kt_ref, pb_ref, ct_ref, lam_ref, x0_ref, y_ref, sp_ref, ss_ref,
               m_scr, b_scr, ct_scr, z_scr, *, n_p, n_b, n_cs, rb):
    n_rows = n_p + n_b * n_cs
    sw = 2 * S5_STATE
    cw = S5_SLAB * S5_GROUP

    gi = lax.broadcasted_iota(jnp.int32, (cw, cw), 0) // S5_GROUP
    gj = lax.broadcasted_iota(jnp.int32, (cw, cw), 1) // S5_GROUP
    same_group = gi == gj
    lag_tiles = [
        jnp.where(same_group, jnp.concatenate([kt_ref[0, j]] * S5_SLAB, axis=0), 0.0).astype(BF16)
        for j in range(S5_CHUNK)]
    zero_tile = jnp.zeros((cw, cw), BF16)
    tpb = MXU_TILE // cw
    for t in range(S5_CHUNK):
        for s in range((t // tpb + 1) * tpb):
            m_scr[s * cw:(s + 1) * cw, t * cw:(t + 1) * cw] = (
                lag_tiles[t - s] if t >= s else zero_tile)
    rg = lax.broadcasted_iota(jnp.int32, (cw, S5_SLAB * sw), 0) // S5_GROUP
    lane = lax.broadcasted_iota(jnp.int32, (cw, S5_SLAB * sw), 1)
    plane, half = lane // sw, (lane % sw) // S5_STATE
    own_state = (plane // 2 == rg // 2) & (half == rg % 2)
    straight = plane % 2 == rg % 2

    def expand(tab):
        tiled = jnp.concatenate([tab] * S5_SLAB, axis=1)
        swapped = jnp.concatenate([pltpu.roll(tab, S5_STATE, 1)] * S5_SLAB, axis=1)
        return jnp.where(own_state, jnp.where(straight, tiled, swapped), 0.0).astype(BF16)

    for s in range(S5_CHUNK):
        b_scr[s * cw:(s + 1) * cw, :] = expand(pb_ref[0, s])
        ct_scr[s * cw:(s + 1) * cw, :] = expand(ct_ref[0, s])

    def chunk_operand(r0):
        return jnp.concatenate(
            [u_ref[pl.ds(S5_CHUNK * r0 + s, rb, stride=S5_CHUNK), :].astype(BF16)
             for s in range(S5_CHUNK)], axis=1)

    for r0 in range(0, n_rows, rb):
        z = jnp.dot(chunk_operand(r0), b_scr[...], preferred_element_type=F32)
        for g in range(S5_SLAB):
            z_scr[g, r0:r0 + rb, :] = z[:, g * sw:(g + 1) * sw]

    rowi = lax.broadcasted_iota(jnp.int32, (n_p, sw), 0)

    def shift_rows(x, sh):
        return jnp.where(rowi >= sh, pltpu.roll(x, sh, 0), 0.0)

    def scan_pair(k, carry):
        pr, pi = 2 * k, 2 * k + 1
        lam_k = lam_ref[0, k]
        xr = z_scr[pr, 0:n_p, :]
        xi = z_scr[pi, 0:n_p, :]
        for lev in range(int(math.log2(n_p))):
            sh = 1 << lev
            lr, li = lam_k[2 * lev:2 * lev + 1], lam_k[2 * lev + 1:2 * lev + 2]
            if sh % 8 == 0 and sh < n_p:
                dr, di = _cmul(xr[:n_p - sh], xi[:n_p - sh], lr, li)
                xr = jnp.concatenate([xr[:sh], xr[sh:] + dr], axis=0)
                xi = jnp.concatenate([xi[:sh], xi[sh:] + di], axis=0)
            else:
                dr, di = _cmul(shift_rows(xr, sh), shift_rows(xi, sh), lr, li)
                xr, xi = xr + dr, xi + di
        z_scr[pr, 0:n_p, :] = shift_rows(xr, 1)
        z_scr[pi, 0:n_p, :] = shift_rows(xi, 1)
        sp_ref[0, pr] = xr[n_p - 1:n_p]
        sp_ref[0, pi] = xi[n_p - 1:n_p]

        sr, si = x0_ref[0, pr], x0_ref[0, pi]
        for ci in range(n_cs):
            rows = pl.ds(n_p + ci, n_b, stride=n_cs)
            zr, zi = z_scr[pr, rows, :], z_scr[pi, rows, :]
            z_scr[pr, rows, :] = sr
            z_scr[pi, rows, :] = si
            dr, di = _cmul(sr, si, lam_k[0:1], lam_k[1:2])
            sr, si = dr + zr, di + zi
        ss_ref[0, pr] = sr
        ss_ref[0, pi] = si
        return carry

    lax.fori_loop(0, S5_SLAB // 2, scan_pair, 0)

    for r0 in range(0, n_rows, rb):
        x_prev = jnp.concatenate(
            [z_scr[g, r0:r0 + rb, :].astype(BF16) for g in range(S5_SLAB)], axis=1)
        a = chunk_operand(r0)
        for j in range(S5_CHUNK // tpb):
            cols = slice(j * tpb * cw, (j + 1) * tpb * cw)
            y = (jnp.dot(a[:, :(j + 1) * tpb * cw], m_scr[0:(j + 1) * tpb * cw, cols],
                         preferred_element_type=F32)
                 + lax.dot_general(x_prev, ct_scr[cols, :], (((1,), (1,)), ((), ())),
                                   preferred_element_type=F32))
            for ti in range(tpb):
                t = j * tpb + ti
                y_ref[pl.ds(S5_CHUNK * r0 + t, rb, stride=S5_CHUNK), :] = (
                    y[:, ti * cw:(ti + 1) * cw])


def _s5(main, u_col0, kt, pb, ct, lam, x0, *, n_p, n_b, n_cs):
    n_slabs = kt.shape[0]
    n_tok = main.shape[0]
    n_rows = n_p + n_b * n_cs
    assert n_rows * S5_CHUNK == n_tok
    n_lam = lam.shape[2]
    cw = S5_SLAB * S5_GROUP
    kw = S5_CHUNK * cw
    gw = 2 * S5_STATE
    sw = S5_SLAB * gw
    rb = max(r for r in range(16, 273, 16) if n_rows % r == 0)
    cb0 = u_col0 // cw

    def per_slab(shape):
        nd = len(shape)
        return pl.BlockSpec((1,) + shape, lambda v: (v,) + (0,) * nd)

    return pl.pallas_call(
        functools.partial(_s5_kernel, n_p=n_p, n_b=n_b, n_cs=n_cs, rb=rb),
        grid=(n_slabs,),
        in_specs=[pl.BlockSpec((n_tok, cw), lambda v: (0, cb0 + v)),
                  per_slab((S5_CHUNK, S5_GROUP, cw)),
                  per_slab((S5_CHUNK, cw, 2 * S5_STATE)),
                  per_slab((S5_CHUNK, cw, 2 * S5_STATE)),
                  per_slab((S5_SLAB // 2, n_lam, gw)), per_slab((S5_SLAB, n_b, gw))],
        out_specs=[_resident((n_tok, cw), lambda v: (0, v)),
                   per_slab((S5_SLAB, 1, gw)), per_slab((S5_SLAB, n_b, gw))],
        out_shape=[
            jax.ShapeDtypeStruct((n_tok, n_slabs * cw), F32),
            jax.ShapeDtypeStruct((n_slabs, S5_SLAB, 1, gw), F32),
            jax.ShapeDtypeStruct((n_slabs, S5_SLAB, n_b, gw), F32),
        ],
        scratch_shapes=[pltpu.VMEM((kw, kw), BF16), pltpu.VMEM((kw, sw), BF16),
                        pltpu.VMEM((kw, sw), BF16),
                        pltpu.VMEM((S5_SLAB, n_rows, 2 * S5_STATE), F32)],
        compiler_params=pltpu.CompilerParams(dimension_semantics=("parallel",),
                                             vmem_limit_bytes=BIG_VMEM_LIMIT),
        name="s5",
    )(main, kt, pb, ct, lam, x0)


def _s5_tables(a_log_neg_re, a_im, log_dt, b_re, b_im, c_re, c_im, n_scan_lev):
    hp = lax.Precision.HIGHEST
    lam_re = -jnp.exp(a_log_neg_re.astype(F32))
    lam_im = a_im.astype(F32)
    dt = jnp.exp(log_dt.astype(F32))[..., None]

    def powers(jj):
        e = jj[None, None, :, None]
        mag = jnp.exp((lam_re * dt)[:, :, None, :] * e)
        ang = (lam_im * dt)[:, :, None, :] * e
        return mag * jnp.cos(ang), mag * jnp.sin(ang)

    pw_re, pw_im = powers(jnp.arange(S5_CHUNK + 1, dtype=F32))
    num_re, num_im = pw_re[:, :, 1] - 1.0, pw_im[:, :, 1]
    den = lam_re * lam_re + lam_im * lam_im
    zoh_re = ((num_re * lam_re + num_im * lam_im) / den)[..., None]
    zoh_im = ((num_im * lam_re - num_re * lam_im) / den)[..., None]
    b_re, b_im = b_re.astype(F32), b_im.astype(F32)
    bb_re = zoh_re * b_re - zoh_im * b_im
    bb_im = zoh_re * b_im + zoh_im * b_re
    c_re, c_im = c_re.astype(F32)[:, :, None], c_im.astype(F32)[:, :, None]
    pr, pi = pw_re[:, :, :, None, :], pw_im[:, :, :, None, :]
    cp_re = c_re * pr - c_im * pi
    cp_im = c_re * pi + c_im * pr
    kj = (jnp.einsum('lgjpn,lgnq->lgjpq', cp_re[:, :, :S5_CHUNK], bb_re, precision=hp)
          - jnp.einsum('lgjpn,lgnq->lgjpq', cp_im[:, :, :S5_CHUNK], bb_im, precision=hp))
    qr, qi = powers(jnp.asarray(np.arange(S5_CHUNK - 1, -1, -1), F32))
    qr, qi = qr[..., None], qi[..., None]
    pb_re = (qr * bb_re[:, :, None] - qi * bb_im[:, :, None]).transpose(0, 1, 2, 4, 3)
    pb_im = (qr * bb_im[:, :, None] + qi * bb_re[:, :, None]).transpose(0, 1, 2, 4, 3)
    bst = jnp.concatenate([pb_re, pb_im], axis=-1)
    cst = jnp.concatenate([cp_re[:, :, 1:], -cp_im[:, :, 1:]], axis=-1)
    lp_re, lp_im = powers(S5_CHUNK * (2.0 ** jnp.arange(n_scan_lev, dtype=F32)))

    l, g = kj.shape[:2]
    v, sg = g // S5_SLAB, S5_SLAB
    cw = sg * S5_GROUP
    sw = sg * 2 * S5_STATE
    kt = kj.reshape(l, v, sg, S5_CHUNK, S5_GROUP, S5_GROUP)
    kt = kt.transpose(0, 1, 3, 5, 2, 4).reshape(l, v, S5_CHUNK, S5_GROUP, cw)
    pb = bst.reshape(l, v, sg, S5_CHUNK, S5_GROUP, 2 * S5_STATE)
    pb = pb.transpose(0, 1, 3, 2, 4, 5).reshape(l, v, S5_CHUNK, cw, 2 * S5_STATE)
    ct = cst.reshape(l, v, sg, S5_CHUNK, S5_GROUP, 2 * S5_STATE)
    ct = ct.transpose(0, 1, 3, 2, 4, 5).reshape(l, v, S5_CHUNK, cw, 2 * S5_STATE)
    def pair_lanes(a):
        a = a.reshape(l, v, sg // 2, 2, n_scan_lev, S5_STATE)
        return a.transpose(0, 1, 2, 4, 3, 5).reshape(l, v, sg // 2, n_scan_lev, 2 * S5_STATE)

    lam_big = jnp.stack([pair_lanes(lp_re), pair_lanes(lp_im)], axis=4)
    lam_big = lam_big.reshape(l, v, sg // 2, 2 * n_scan_lev, 2 * S5_STATE)
    return kt, pb, ct, lam_big


def _mix_kernel(*refs, n_src, tiles_a):
    x_refs = refs[:n_src]
    (gh_ref, gs_ref, oh_ref, y_ref, u_ref, d_ref, bglu_ref,
     wglu_ref, wbh_ref, wbs_ref, wout_ref, out_ref) = refs[n_src:]
    ys = jax.nn.gelu(y_ref[...] + d_ref[...] * u_ref[...])
    gate = jnp.dot(ys.astype(BF16), wglu_ref[...], preferred_element_type=F32) + bglu_ref[...]
    glu = (ys * jax.nn.sigmoid(gate)).astype(BF16)
    t_h = jnp.dot(oh_ref[...], wbh_ref[...], preferred_element_type=F32)
    t_s = jnp.dot(glu, wbs_ref[...], preferred_element_type=F32)
    mix = gh_ref[...].astype(F32) * t_h + gs_ref[...].astype(F32) * t_s
    x = x_refs[0][...]
    if n_src == 2:
        x = jnp.where(pl.program_id(0) < tiles_a, x, x_refs[1][...])
    out_ref[...] = x + jnp.dot(mix.astype(BF16), wout_ref[...], preferred_element_type=F32)


def _mix(xs, gates, o_h, y_s, main, u_col_block, d_skip, b_glu, w_glu, w_bh, w_bs, w_out, l, tm):
    t, ds = y_s.shape
    d = w_bh.shape[2]
    layer = lambda i: (l, 0, 0)
    const = lambda i: (0, 0)
    x_specs, tiles_a = _row_tile_specs(xs, tm)
    return pl.pallas_call(
        functools.partial(_mix_kernel, n_src=len(xs), tiles_a=tiles_a),
        grid=(t // tm,),
        in_specs=x_specs + [
            pl.BlockSpec((tm, d), lambda i: (i, 0)),
            pl.BlockSpec((tm, d), lambda i: (i, 1)),
            pl.BlockSpec((tm, ds), lambda i: (i, 0)),
            pl.BlockSpec((tm, ds), lambda i: (i, 0)),
            pl.BlockSpec((tm, ds), lambda i: (i, u_col_block)),
            pl.BlockSpec((1, ds), const),
            pl.BlockSpec((1, ds), const),
            _resident((None, ds, ds), layer),
            _resident((None, ds, d), layer),
            _resident((None, ds, d), layer),
            _resident((None, d, d), layer),
        ],
        out_specs=pl.BlockSpec((tm, d), lambda i: (i, 0)),
        out_shape=jax.ShapeDtypeStruct((t, d), F32),
        compiler_params=_cparams(("parallel",)),
        name="mix",
    )(*xs, gates, gates, o_h, y_s, main, d_skip.reshape(1, ds), b_glu.reshape(1, ds),
      w_glu, w_bh, w_bs, w_out)


def _rms(x, w):
    ms = jnp.mean(x * x, axis=-1, keepdims=True)
    return x * lax.rsqrt(ms + EPS) * w


def _ffn_kernel(x_ref, n2_ref, wg_ref, wu_ref, wd_ref, fn_ref, *rest, final_norm, tiles_a):
    out_refs, h_scr = rest[:-1], rest[-1]
    i = pl.program_id(0)
    j = pl.program_id(1)
    nj = pl.num_programs(1)

    def body(out_ref):
        @pl.when(j == 0)
        def _():
            xn = x_ref[...]
            out_ref[...] = xn
            h_scr[...] = _rms(xn, n2_ref[...]).astype(BF16)

        h = h_scr[...]
        ga = jnp.dot(h, wg_ref[...], preferred_element_type=F32)
        up = jnp.dot(h, wu_ref[...], preferred_element_type=F32)
        act = (jax.nn.silu(ga) * up).astype(BF16)
        out_ref[...] += jnp.dot(act, wd_ref[...], preferred_element_type=F32)

        if final_norm:
            @pl.when(j == nj - 1)
            def _():
                out_ref[...] = _rms(out_ref[...], fn_ref[...])

    if len(out_refs) == 1:
        body(out_refs[0])
    else:
        pl.when(i < tiles_a)(functools.partial(body, out_refs[0]))
        pl.when(i >= tiles_a)(functools.partial(body, out_refs[1]))


def _ffn(x, norm2, w_gate_up, w_down, l, fnorm, final_norm, tm, tf, split_rows=None):
    t, d = x.shape
    dff = w_down.shape[1]
    nj = dff // tf
    if split_rows is None:
        tiles_a = t // tm
        out_specs = [pl.BlockSpec((tm, d), lambda i, j: (i, 0))]
        out_shape = [jax.ShapeDtypeStruct((t, d), F32)]
    else:
        assert split_rows % tm == 0 and (t - split_rows) % tm == 0
        tiles_a = split_rows // tm
        out_specs = [
            pl.BlockSpec((tm, d), lambda i, j: (jnp.minimum(i, tiles_a - 1), 0),
                         pipeline_mode=pl.Buffered(1)),
            pl.BlockSpec((tm, d), lambda i, j: (jnp.maximum(i - tiles_a, 0), 0),
                         pipeline_mode=pl.Buffered(1)),
        ]
        out_shape = [jax.ShapeDtypeStruct((split_rows, d), F32),
                     jax.ShapeDtypeStruct((t - split_rows, d), F32)]
    return pl.pallas_call(
        functools.partial(_ffn_kernel, final_norm=final_norm, tiles_a=tiles_a),
        grid=(t // tm, nj),
        in_specs=[
            pl.BlockSpec((tm, d), lambda i, j: (i, 0)),
            pl.BlockSpec((1, d), lambda i, j: (0, 0)),
            pl.BlockSpec((None, d, tf), lambda i, j: (l, 0, j)),
            pl.BlockSpec((None, d, tf), lambda i, j: (l, 0, nj + j)),
            pl.BlockSpec((None, tf, d), lambda i, j: (l, j, 0)),
            pl.BlockSpec((1, d), lambda i, j: (0, 0)),
        ],
        out_specs=out_specs,
        out_shape=out_shape,
        scratch_shapes=[pltpu.VMEM((tm, d), BF16)],
        compiler_params=_cparams(("parallel" if split_rows is None else "arbitrary", "arbitrary")),
        name="ffn",
    )(x, norm2.reshape(1, d), w_gate_up, w_gate_up, w_down, fnorm.reshape(1, d))


def _pick(n, pref):
    t = pref
    while n % t:
        t //= 2
    return t


def _lower_bounds(lb_logits):
    p = jax.nn.softmax(lb_logits.astype(F32), axis=0)
    cs = jnp.cumsum(p, axis=0)
    return cs - cs[:1]


def kernel(x_prompt, x_sample, state_hgrn, state_s5_re, state_s5_im, lb_logits, norm1, w_in, hgrn_norm, w_bh, s5_a_log_neg_re, s5_a_im, s5_log_dt, s5_b_re, s5_b_im, s5_c_re, s5_c_im, s5_d, w_glu, b_glu, w_bs, w_out, norm2, w_gate_up, w_down, final_norm):
    depth = w_in.shape[0]
    bp, tp, d = x_prompt.shape
    bs, ts, _ = x_sample.shape
    assert bp == 1
    dh = lb_logits.shape[1]
    ds = s5_d.shape[1]
    n_groups = ds // S5_GROUP
    n_heads = dh // HEAD_DIM
    n_tok_p = bp * tp
    n_tok_s = bs * ts
    n_tok = n_tok_p + n_tok_s

    n_main = 4 * dh + ds

    w_in_b = jnp.concatenate([w_in[:, :, n_main:], w_in[:, :, :n_main]], axis=-1).astype(BF16)
    w_bh_b = w_bh.astype(BF16)
    w_glu_b = w_glu.astype(BF16)
    w_bs_b = w_bs.astype(BF16)
    w_out_b = w_out.astype(BF16)
    w_gu_b = w_gate_up.astype(BF16)
    w_dn_b = w_down.astype(BF16)
    lbs = _lower_bounds(lb_logits)

    n_p = tp // S5_CHUNK
    n_cs = ts // S5_CHUNK
    s5_kt, s5_pb, s5_ct, s5_lam = _s5_tables(s5_a_log_neg_re, s5_a_im, s5_log_dt, s5_b_re,
                                             s5_b_im, s5_c_re, s5_c_im, int(math.log2(n_p)))
    n_slabs = n_groups // S5_SLAB

    n_pairs = S5_SLAB // 2

    def pack_state(re, im):
        nb = re.shape[0]
        x0 = jnp.stack([re.reshape(nb, n_slabs, n_pairs, 2 * S5_STATE),
                        im.reshape(nb, n_slabs, n_pairs, 2 * S5_STATE)], axis=3)
        return x0.transpose(1, 2, 3, 0, 4).reshape(n_slabs, S5_SLAB, nb, 2 * S5_STATE)

    def unpack_state(st):
        nb = st.shape[2]
        st = st.reshape(n_slabs, n_pairs, 2, nb, 2 * S5_STATE).transpose(2, 3, 0, 1, 4)
        st = st.reshape(2, nb, n_groups, S5_STATE)
        return st[0], st[1]

    tm_in = _pick(n_tok, 1024)
    tn_in = 1024
    tm_mix = _pick(n_tok, 256)
    tm_ffn = _pick(n_tok, 1024)
    tf = _pick(w_down.shape[1], 512)
    c_p = _pick(tp, 128)
    c_s = _pick(ts, 128)

    xs = (x_prompt.reshape(n_tok_p, d), x_sample.reshape(n_tok_s, d))
    if n_tok_p % tm_in or n_tok_s % tm_in or n_tok_p % tm_mix or n_tok_s % tm_mix:
        xs = (jnp.concatenate(xs, axis=0),)
    zero_h = jnp.zeros((bp, n_heads, HEAD_DIM, HEAD_DIM), F32)

    new_re_p, new_im_p, new_re_s, new_im_s = [], [], [], []
    hgrn_p = hgrn_s = None
    for l in range(depth):
        gates, main = _inproj(xs, norm1[l], w_in_b, l, 2 * d, tm_in, tn_in)

        o_h, hgrn_p = _hgrn(main, lbs[l], hgrn_norm[l], zero_h, None, hgrn_p,
                            layer=l, n_layers=depth, row_off=0, n_seq=bp, seq_len=tp, c=c_p,
                            nh=n_heads, hgrp=n_heads)
        o_h, hgrn_s = _hgrn(main, lbs[l], hgrn_norm[l], state_hgrn[l], o_h, hgrn_s,
                            layer=l, n_layers=depth, row_off=n_tok_p, n_seq=bs, seq_len=ts, c=c_s,
                            nh=n_heads, hgrp=n_heads)

        y, st_p, st_s = _s5(main, 4 * dh, s5_kt[l], s5_pb[l], s5_ct[l], s5_lam[l],
                            pack_state(state_s5_re[l], state_s5_im[l]),
                            n_p=n_p, n_b=bs, n_cs=n_cs)

        x = _mix(xs, gates, o_h, y, main, (4 * dh) // ds, s5_d[l], b_glu[l],
                 w_glu_b, w_bh_b, w_bs_b, w_out_b, l, tm_mix)
        last = l == depth - 1
        split = n_tok_p if last and n_tok_p % tm_ffn == 0 and n_tok_s % tm_ffn == 0 else None
        xs = tuple(_ffn(x, norm2[l], w_gu_b, w_dn_b, l, final_norm, last, tm_ffn, tf, split))

        re_p, im_p = unpack_state(st_p)
        re_s, im_s = unpack_state(st_s)
        new_re_p.append(re_p)
        new_im_p.append(im_p)
        new_re_s.append(re_s)
        new_im_s.append(im_s)

    if len(xs) == 1:
        xs = (xs[0][:n_tok_p], xs[0][n_tok_p:])
    y_prompt = xs[0].reshape(bp, tp, d)
    y_sample = xs[1].reshape(bs, ts, d)
    return (y_prompt, y_sample, hgrn_p, jnp.stack(new_re_p), jnp.stack(new_im_p),
            hgrn_s, jnp.stack(new_re_s), jnp.stack(new_im_s))
```

```python
import functools
import math

import jax
import jax.numpy as jnp
import numpy as np
from jax import lax
from jax.experimental import pallas as pl
from jax.experimental.pallas import tpu as pltpu

F32 = jnp.float32
BF16 = jnp.bfloat16

EPS = 1e-6
LOG2_E = 1.4426950408889634
HEAD_DIM = 128
S5_GROUP = 16
S5_STATE = 64
S5_CHUNK = 16
S5_SLAB = 8
MXU_TILE = 256
HGRN_STACK_ROWS = MXU_TILE
VMEM_LIMIT = 56 * 1024 * 1024
BIG_VMEM_LIMIT = 60 * 1024 * 1024


def _cparams(sem):
    return pltpu.CompilerParams(dimension_semantics=sem, vmem_limit_bytes=VMEM_LIMIT)


def _resident(shape, index_map):
    return pl.BlockSpec(shape, index_map, pipeline_mode=pl.Buffered(1))


def _row_tile_specs(xs, tm):
    d = xs[0].shape[1]
    tiles_a = xs[0].shape[0] // tm
    if len(xs) == 1:
        return [pl.BlockSpec((tm, d), lambda i, *_: (i, 0))], tiles_a
    assert len(xs) == 2 and xs[0].shape[0] % tm == 0 and xs[1].shape[0] % tm == 0
    return [
        pl.BlockSpec((tm, d), lambda i, *_: (jnp.minimum(i, tiles_a - 1), 0)),
        pl.BlockSpec((tm, d), lambda i, *_: (jnp.maximum(i - tiles_a, 0), 0),
                     pipeline_mode=pl.Buffered(1)),
    ], tiles_a


def _for_row_source(x_refs, tiles_a, pred, body):
    i = pl.program_id(0)
    if len(x_refs) == 1:
        conds = [pred]
    else:
        conds = [pred & (i < tiles_a), pred & (i >= tiles_a)]
    for cond, x_ref in zip(conds, x_refs):
        pl.when(cond)(functools.partial(body, x_ref))


def _inproj_kernel(*refs, n_src, tiles_a, n_main_tiles):
    x_refs = refs[:n_src]
    nw_ref, w_ref, gate_ref, main_ref, h_scr = refs[n_src:]
    j = pl.program_id(1)

    def normalize(x_ref):
        x = x_ref[...]
        ms = jnp.mean(x * x, axis=-1, keepdims=True)
        h_scr[...] = (x * lax.rsqrt(ms + EPS) * nw_ref[...]).astype(BF16)

    _for_row_source(x_refs, tiles_a, j == 0, normalize)

    @pl.when(j < n_main_tiles)
    def _():
        main_ref[...] = jnp.dot(h_scr[...], w_ref[...], preferred_element_type=F32)

    @pl.when(j >= n_main_tiles)
    def _():
        acc = jnp.dot(h_scr[...], w_ref[...], preferred_element_type=F32)
        gate_ref[...] = jax.nn.sigmoid(acc).astype(BF16)


def _inproj(xs, nw, w, l, n_gate_cols, tm, tn):
    t = sum(x.shape[0] for x in xs)
    d = xs[0].shape[1]
    n = w.shape[2]
    n_tiles = n // tn
    n_main_tiles = (n - n_gate_cols) // tn
    x_specs, tiles_a = _row_tile_specs(xs, tm)
    return pl.pallas_call(
        functools.partial(_inproj_kernel, n_src=len(xs), tiles_a=tiles_a,
                          n_main_tiles=n_main_tiles),
        grid=(t // tm, n_tiles),
        in_specs=x_specs + [
            pl.BlockSpec((1, d), lambda i, j: (0, 0)),
            pl.BlockSpec((None, d, tn), lambda i, j: (l, 0, j)),
        ],
        out_specs=[
            pl.BlockSpec((tm, tn), lambda i, j: (i, jnp.maximum(j - n_main_tiles, 0))),
            pl.BlockSpec((tm, tn), lambda i, j: (i, jnp.minimum(j, n_main_tiles - 1))),
        ],
        out_shape=[
            jax.ShapeDtypeStruct((t, n_gate_cols), BF16),
            jax.ShapeDtypeStruct((t, n - n_gate_cols), F32),
        ],
        scratch_shapes=[pltpu.VMEM((tm, d), BF16)],
        compiler_params=_cparams(("parallel", "arbitrary")),
        name="inproj",
    )(*xs, nw.reshape(1, d), w)


def _hgrn_level_masks(c):
    n_lev = int(math.log2(c)) + 1
    gsz = max(1, min(n_lev, HGRN_STACK_ROWS // c))
    n_groups = -(-n_lev // gsz)
    r = gsz * c
    t = np.arange(c)[:, None]
    s = np.arange(c)[None, :]
    masks = np.zeros((n_groups, r, r), np.float32)
    for lev in range(n_lev):
        if lev == 0:
            m = (t == s)
        else:
            h = 1 << (lev - 1)
            m = ((t & h) != 0) & ((s & h) == 0) & ((t // (2 * h)) == (s // (2 * h)))
        g, i = divmod(lev, gsz)
        masks[g, i * c:(i + 1) * c, i * c:(i + 1) * c] = m
    return masks, n_lev, gsz, n_groups


def _hgrn_kernel(q_ref, f_ref, i_ref, g_ref, lb_ref, gain_ref, s0_ref, mask_ref, *rest,
                 c, nh, hgrp, n_lev, gsz, n_groups):
    o_ref, s_ref, st_scr = rest[-3:]
    ci = pl.program_id(2)
    nc = pl.num_programs(2)

    @pl.when(ci == 0)
    def _():
        st_scr[...] = s0_ref[0]

    row = lax.broadcasted_iota(jnp.int32, (c, HEAD_DIM), 0)
    nt = (((1,), (1,)), ((), ()))
    tn = (((0,), (0,)), ((), ()))

    def head_operands(h):
        sl = slice(h * HEAD_DIM, (h + 1) * HEAD_DIM)
        z = f_ref[:, sl]
        lb = lb_ref[:, sl]
        ez = jnp.exp(-jnp.abs(z))
        log_sig = jnp.minimum(z, 0.0) - jnp.log(1.0 + ez)
        a0 = jnp.log(lb)
        a1 = jnp.log1p(-lb) + log_sig
        logf = jnp.maximum(a0, a1) + jnp.log(1.0 + jnp.exp(-jnp.abs(a0 - a1)))
        k = (1.0 - lb) * (jnp.where(z >= 0.0, ez, 1.0) / (1.0 + ez))
        q = q_ref[:, sl]
        v = jax.nn.silu(i_ref[:, sl])
        v_bf = v.astype(BF16)

        p = logf * LOG2_E
        tb = p
        a_lev = [q.astype(BF16)]
        b_lev = [k.astype(BF16)]
        for lev in range(1, n_lev):
            hs = 1 << (lev - 1)
            right = (row & hs) != 0
            e = jnp.exp2(jnp.where(right, p, tb - p))
            zl = (jnp.where(right, q, k) * e).astype(BF16)
            a_lev.append(zl)
            b_lev.append(zl)
            dn = pltpu.roll(tb, hs, 0)
            up = pltpu.roll(tb, c - hs, 0)
            p = p + jnp.where(right, dn, 0.0)
            tb = tb + jnp.where(right, dn, up)
        ops = []
        for g in range(n_groups):
            lo, hi = g * gsz, min((g + 1) * gsz, n_lev)
            n_in = hi - lo
            a_g = jnp.concatenate(a_lev[lo:hi], axis=0) if n_in > 1 else a_lev[lo]
            b_g = a_g if lo > 0 else (
                jnp.concatenate(b_lev[lo:hi], axis=0) if n_in > 1 else b_lev[lo])
            ops.append((a_g, b_g, n_in))
        q_in = (q * jnp.exp2(p)).astype(BF16)
        k_out = (k * jnp.exp2(tb - p)).astype(BF16)
        decay = jnp.exp2(jnp.broadcast_to(tb[0:1, :], (HEAD_DIM, HEAD_DIM)).T)
        return ops, v_bf, q_in, k_out, decay

    def fold_scores(g, sc, n_in):
        if c % HEAD_DIM == 0:
            return jnp.concatenate(
                [sc[i * c:(i + 1) * c, i * c:(i + 1) * c].astype(BF16)
                 * mask_ref[g, i * c:(i + 1) * c, i * c:(i + 1) * c]
                 for i in range(n_in)], axis=1)
        r = n_in * c
        sc = sc * mask_ref[g][:r, :r]
        fold = sc[0:c]
        for i in range(1, n_in):
            fold = fold + sc[i * c:(i + 1) * c]
        return fold.astype(BF16)

    for h0 in range(0, nh, hgrp):
        heads = range(h0, min(h0 + hgrp, nh))
        prep = {h: head_operands(h) for h in heads}
        scores = {h: [lax.dot_general(a_g, b_g, nt, preferred_element_type=F32)
                      for a_g, b_g, _ in prep[h][0]] for h in heads}
        upd = {h: lax.dot_general(prep[h][3], prep[h][1], tn, preferred_element_type=F32)
               for h in heads}
        for h in heads:
            ops, v_bf, q_in, _, decay = prep[h]
            st = st_scr[h]
            lhs = [q_in]
            rhs = [st.astype(BF16)]
            for g, (sc, (_, _, n_in)) in enumerate(zip(scores[h], ops)):
                lhs.append(fold_scores(g, sc, n_in))
                rhs.extend([v_bf] * n_in)
            o = jnp.dot(jnp.concatenate(lhs, axis=1), jnp.concatenate(rhs, axis=0),
                        preferred_element_type=F32)
            st_scr[h] = st * decay + upd[h]
            sl = slice(h * HEAD_DIM, (h + 1) * HEAD_DIM)
            ms = jnp.mean(o * o, axis=-1, keepdims=True)
            o = o * lax.rsqrt(ms + EPS) * gain_ref[:, sl] * jax.nn.silu(g_ref[:, sl])
            o_ref[:, sl] = o.astype(o_ref.dtype)

    @pl.when(ci == nc - 1)
    def _():
        s_ref[0] = st_scr[...]


def _hgrn(main, lb, gain, s0, o_all, s_all, *, layer, n_layers, row_off, n_seq, seq_len, c, nh,
          hgrp):
    n_tok = main.shape[0]
    dh = lb.shape[-1]
    n_heads = dh // HEAD_DIM
    w = nh * HEAD_DIM
    n_hg = n_heads // nh
    n_chunks = seq_len // c
    rb0 = row_off // c
    masks, n_lev, gsz, n_groups = _hgrn_level_masks(c)
    r = masks.shape[-1]

    def col_spec(group):
        return pl.BlockSpec(
            (c, w), lambda b, hg, ci: (rb0 + b * n_chunks + ci, group * n_hg + hg))

    vec_spec = pl.BlockSpec((1, w), lambda b, hg, ci: (0, hg))
    s0_layer = min(layer, s0.shape[0] - 1)
    st_spec = pl.BlockSpec((None, 1, nh, HEAD_DIM, HEAD_DIM),
                           lambda b, hg, ci: (s0_layer, b, hg, 0, 0))
    operands = [main, main, main, main, lb.reshape(1, dh), gain.reshape(1, dh), s0,
                jnp.asarray(masks, BF16 if c % HEAD_DIM == 0 else F32)]
    in_specs = [col_spec(0), col_spec(1), col_spec(2), col_spec(3), vec_spec, vec_spec,
                st_spec, _resident((n_groups, r, r), lambda b, hg, ci: (0, 0, 0))]
    aliases = {}
    for out_idx, buf in enumerate((o_all, s_all)):
        if buf is not None:
            aliases[len(operands)] = out_idx
            operands.append(buf)
            in_specs.append(pl.BlockSpec(memory_space=pl.ANY))
    o, s_new = pl.pallas_call(
        functools.partial(_hgrn_kernel, c=c, nh=nh, hgrp=hgrp, n_lev=n_lev, gsz=gsz,
                          n_groups=n_groups),
        grid=(n_seq, n_hg, n_chunks),
        in_specs=in_specs,
        out_specs=[
            pl.BlockSpec((c, w), lambda b, hg, ci: (rb0 + b * n_chunks + ci, hg)),
            pl.BlockSpec((None, 1, nh, HEAD_DIM, HEAD_DIM),
                         lambda b, hg, ci: (layer, b, hg, 0, 0)),
        ],
        out_shape=[
            jax.ShapeDtypeStruct((n_tok, dh), BF16),
            jax.ShapeDtypeStruct((n_layers, n_seq, n_heads, HEAD_DIM, HEAD_DIM), F32),
        ],
        input_output_aliases=aliases,
        scratch_shapes=[pltpu.VMEM((nh, HEAD_DIM, HEAD_DIM), F32)],
        compiler_params=_cparams(("parallel", "parallel", "arbitrary")),
        name="hgrn_c%d" % c,
    )(*operands)
    return o, s_new


def _cmul(xr, xi, lr, li):
    return xr * lr - xi * li, xr * li + xi * lr


def _s5_kernel(u_ref, kt_ref, pb_ref, ct_ref, lam_ref, x0_ref, y_ref, sp_ref, ss_ref,
               m_scr, b_scr, ct_scr, z_scr, *, n_p, n_b, n_cs, rb):
    n_rows = n_p + n_b * n_cs
    sw = 2 * S5_STATE
    cw = S5_SLAB * S5_GROUP

    gi = lax.broadcasted_iota(jnp.int32, (cw, cw), 0) // S5_GROUP
    gj = lax.broadcasted_iota(jnp.int32, (cw, cw), 1) // S5_GROUP
    same_group = gi == gj
    lag_tiles = [
        jnp.where(same_group, jnp.concatenate([kt_ref[0, j]] * S5_SLAB, axis=0), 0.0).astype(BF16)
        for j in range(S5_CHUNK)]
    zero_tile = jnp.zeros((cw, cw), BF16)
    tpb = MXU_TILE // cw
    for t in range(S5_CHUNK):
        for s in range((t // tpb + 1) * tpb):
            m_scr[s * cw:(s + 1) * cw, t * cw:(t + 1) * cw] = (
                lag_tiles[t - s] if t >= s else zero_tile)
    rg = lax.broadcasted_iota(jnp.int32, (cw, S5_SLAB * sw), 0) // S5_GROUP
    lane = lax.broadcasted_iota(jnp.int32, (cw, S5_SLAB * sw), 1)
    plane, half = lane // sw, (lane % sw) // S5_STATE
    own_state = (plane // 2 == rg // 2) & (half == rg % 2)
    straight = plane % 2 == rg % 2

    def expand(tab):
        tiled = jnp.concatenate([tab] * S5_SLAB, axis=1)
        swapped = jnp.concatenate([pltpu.roll(tab, S5_STATE, 1)] * S5_SLAB, axis=1)
        return jnp.where(own_state, jnp.where(straight, tiled, swapped), 0.0).astype(BF16)

    for s in range(S5_CHUNK):
        b_scr[s * cw:(s + 1) * cw, :] = expand(pb_ref[0, s])
        ct_scr[s * cw:(s + 1) * cw, :] = expand(ct_ref[0, s])

    def chunk_operand(r0):
        return jnp.concatenate(
            [u_ref[pl.ds(S5_CHUNK * r0 + s, rb, stride=S5_CHUNK), :].astype(BF16)
             for s in range(S5_CHUNK)], axis=1)

    for r0 in range(0, n_rows, rb):
        z = jnp.dot(chunk_operand(r0), b_scr[...], preferred_element_type=F32)
        for g in range(S5_SLAB):
            z_scr[g, r0:r0 + rb, :] = z[:, g * sw:(g + 1) * sw]

    rowi = lax.broadcasted_iota(jnp.int32, (n_p, sw), 0)

    def shift_rows(x, sh):
        return jnp.where(rowi >= sh, pltpu.roll(x, sh, 0), 0.0)

    def scan_pair(k, carry):
        pr, pi = 2 * k, 2 * k + 1
        lam_k = lam_ref[0, k]
        xr = z_scr[pr, 0:n_p, :]
        xi = z_scr[pi, 0:n_p, :]
        for lev in range(int(math.log2(n_p))):
            sh = 1 << lev
            lr, li = lam_k[2 * lev:2 * lev + 1], lam_k[2 * lev + 1:2 * lev + 2]
            if sh % 8 == 0 and sh < n_p:
                dr, di = _cmul(xr[:n_p - sh], xi[:n_p - sh], lr, li)
                xr = jnp.concatenate([xr[:sh], xr[sh:] + dr], axis=0)
                xi = jnp.concatenate([xi[:sh], xi[sh:] + di], axis=0)
            else:
                dr, di = _cmul(shift_rows(xr, sh), shift_rows(xi, sh), lr, li)
                xr, xi = xr + dr, xi + di
        z_scr[pr, 0:n_p, :] = shift_rows(xr, 1)
        z_scr[pi, 0:n_p, :] = shift_rows(xi, 1)
        sp_ref[0, pr] = xr[n_p - 1:n_p]
        sp_ref[0, pi] = xi[n_p - 1:n_p]

        sr, si = x0_ref[0, pr], x0_ref[0, pi]
        for ci in range(n_cs):
            rows = pl.ds(n_p + ci, n_b, stride=n_cs)
            zr, zi = z_scr[pr, rows, :], z_scr[pi, rows, :]
            z_scr[pr, rows, :] = sr
            z_scr[pi, rows, :] = si
            dr, di = _cmul(sr, si, lam_k[0:1], lam_k[1:2])
            sr, si = dr + zr, di + zi
        ss_ref[0, pr] = sr
        ss_ref[0, pi] = si
        return carry

    lax.fori_loop(0, S5_SLAB // 2, scan_pair, 0)

    for r0 in range(0, n_rows, rb):
        x_prev = jnp.concatenate(
            [z_scr[g, r0:r0 + rb, :].astype(BF16) for g in range(S5_SLAB)], axis=1)
        a = chunk_operand(r0)
        for j in range(S5_CHUNK // tpb):
            cols = slice(j * tpb * cw, (j + 1) * tpb * cw)
            y = (jnp.dot(a[:, :(j + 1) * tpb * cw], m_scr[0:(j + 1) * tpb * cw, cols],
                         preferred_element_type=F32)
                 + lax.dot_general(x_prev, ct_scr[cols, :], (((1,), (1,)), ((), ())),
                                   preferred_element_type=F32))
            for ti in range(tpb):
                t = j * tpb + ti
                y_ref[pl.ds(S5_CHUNK * r0 + t, rb, stride=S5_CHUNK), :] = (
                    y[:, ti * cw:(ti + 1) * cw])


def _s5(main, u_col0, kt, pb, ct, lam, l, x0, *, n_p, n_b, n_cs):
    n_slabs = kt.shape[1]
    n_tok = main.shape[0]
    n_rows = n_p + n_b * n_cs
    assert n_rows * S5_CHUNK == n_tok
    n_lam = lam.shape[3]
    cw = S5_SLAB * S5_GROUP
    kw = S5_CHUNK * cw
    gw = 2 * S5_STATE
    sw = S5_SLAB * gw
    rb = max(r for r in range(16, 273, 16) if n_rows % r == 0)
    cb0 = u_col0 // cw

    def per_slab(shape):
        nd = len(shape)
        return pl.BlockSpec((1,) + shape, lambda v: (v,) + (0,) * nd)

    def table(shape):
        nd = len(shape)
        return pl.BlockSpec((None, 1) + shape, lambda v: (l, v) + (0,) * nd)

    return pl.pallas_call(
        functools.partial(_s5_kernel, n_p=n_p, n_b=n_b, n_cs=n_cs, rb=rb),
        grid=(n_slabs,),
        in_specs=[pl.BlockSpec((n_tok, cw), lambda v: (0, cb0 + v)),
                  table((S5_CHUNK, S5_GROUP, cw)),
                  table((S5_CHUNK, cw, 2 * S5_STATE)),
                  table((S5_CHUNK, cw, 2 * S5_STATE)),
                  table((S5_SLAB // 2, n_lam, gw)), per_slab((S5_SLAB, n_b, gw))],
        out_specs=[_resident((n_tok, cw), lambda v: (0, v)),
                   per_slab((S5_SLAB, 1, gw)), per_slab((S5_SLAB, n_b, gw))],
        out_shape=[
            jax.ShapeDtypeStruct((n_tok, n_slabs * cw), F32),
            jax.ShapeDtypeStruct((n_slabs, S5_SLAB, 1, gw), F32),
            jax.ShapeDtypeStruct((n_slabs, S5_SLAB, n_b, gw), F32),
        ],
        scratch_shapes=[pltpu.VMEM((kw, kw), BF16), pltpu.VMEM((kw, sw), BF16),
                        pltpu.VMEM((kw, sw), BF16),
                        pltpu.VMEM((S5_SLAB, n_rows, 2 * S5_STATE), F32)],
        compiler_params=pltpu.CompilerParams(dimension_semantics=("parallel",),
                                             vmem_limit_bytes=BIG_VMEM_LIMIT),
        name="s5",
    )(main, kt, pb, ct, lam, x0)


def _s5_tables(a_log_neg_re, a_im, log_dt, b_re, b_im, c_re, c_im, n_scan_lev):
    hp = lax.Precision.HIGHEST
    lam_re = -jnp.exp(a_log_neg_re.astype(F32))
    lam_im = a_im.astype(F32)
    dt = jnp.exp(log_dt.astype(F32))[..., None]

    def powers(jj):
        e = jj[None, None, :, None]
        mag = jnp.exp((lam_re * dt)[:, :, None, :] * e)
        ang = (lam_im * dt)[:, :, None, :] * e
        return mag * jnp.cos(ang), mag * jnp.sin(ang)

    pw_re, pw_im = powers(jnp.arange(S5_CHUNK + 1, dtype=F32))
    num_re, num_im = pw_re[:, :, 1] - 1.0, pw_im[:, :, 1]
    den = lam_re * lam_re + lam_im * lam_im
    zoh_re = ((num_re * lam_re + num_im * lam_im) / den)[..., None]
    zoh_im = ((num_im * lam_re - num_re * lam_im) / den)[..., None]
    b_re, b_im = b_re.astype(F32), b_im.astype(F32)
    bb_re = zoh_re * b_re - zoh_im * b_im
    bb_im = zoh_re * b_im + zoh_im * b_re
    c_re, c_im = c_re.astype(F32)[:, :, None], c_im.astype(F32)[:, :, None]
    pr, pi = pw_re[:, :, :, None, :], pw_im[:, :, :, None, :]
    cp_re = c_re * pr - c_im * pi
    cp_im = c_re * pi + c_im * pr
    kj = (jnp.einsum('lgjpn,lgnq->lgjpq', cp_re[:, :, :S5_CHUNK], bb_re, precision=hp)
          - jnp.einsum('lgjpn,lgnq->lgjpq', cp_im[:, :, :S5_CHUNK], bb_im, precision=hp))
    qr, qi = powers(jnp.asarray(np.arange(S5_CHUNK - 1, -1, -1), F32))
    qr, qi = qr[..., None], qi[..., None]
    pb_re = (qr * bb_re[:, :, None] - qi * bb_im[:, :, None]).transpose(0, 1, 2, 4, 3)
    pb_im = (qr * bb_im[:, :, None] + qi * bb_re[:, :, None]).transpose(0, 1, 2, 4, 3)
    bst = jnp.concatenate([pb_re, pb_im], axis=-1)
    cst = jnp.concatenate([cp_re[:, :, 1:], -cp_im[:, :, 1:]], axis=-1)
    lp_re, lp_im = powers(S5_CHUNK * (2.0 ** jnp.arange(n_scan_lev, dtype=F32)))

    l, g = kj.shape[:2]
    v, sg = g // S5_SLAB, S5_SLAB
    cw = sg * S5_GROUP
    sw = sg * 2 * S5_STATE
    kt = kj.reshape(l, v, sg, S5_CHUNK, S5_GROUP, S5_GROUP)
    kt = kt.transpose(0, 1, 3, 5, 2, 4).reshape(l, v, S5_CHUNK, S5_GROUP, cw)
    pb = bst.reshape(l, v, sg, S5_CHUNK, S5_GROUP, 2 * S5_STATE)
    pb = pb.transpose(0, 1, 3, 2, 4, 5).reshape(l, v, S5_CHUNK, cw, 2 * S5_STATE)
    ct = cst.reshape(l, v, sg, S5_CHUNK, S5_GROUP, 2 * S5_STATE)
    ct = ct.transpose(0, 1, 3, 2, 4, 5).reshape(l, v, S5_CHUNK, cw, 2 * S5_STATE)
    def pair_lanes(a):
        a = a.reshape(l, v, sg // 2, 2, n_scan_lev, S5_STATE)
        return a.transpose(0, 1, 2, 4, 3, 5).reshape(l, v, sg // 2, n_scan_lev, 2 * S5_STATE)

    lam_big = jnp.stack([pair_lanes(lp_re), pair_lanes(lp_im)], axis=4)
    lam_big = lam_big.reshape(l, v, sg // 2, 2 * n_scan_lev, 2 * S5_STATE)
    return kt, pb, ct, lam_big


def _mix_kernel(*refs, n_src, tiles_a):
    x_refs = refs[:n_src]
    (gh_ref, gs_ref, oh_ref, y_ref, u_ref, d_ref, bglu_ref,
     wglu_ref, wbh_ref, wbs_ref, wout_ref, out_ref) = refs[n_src:]
    ys = jax.nn.gelu(y_ref[...] + d_ref[...] * u_ref[...])
    gate = jnp.dot(ys.astype(BF16), wglu_ref[...], preferred_element_type=F32) + bglu_ref[...]
    glu = (ys * jax.nn.sigmoid(gate)).astype(BF16)
    t_h = jnp.dot(oh_ref[...], wbh_ref[...], preferred_element_type=F32)
    t_s = jnp.dot(glu, wbs_ref[...], preferred_element_type=F32)
    mix = gh_ref[...].astype(F32) * t_h + gs_ref[...].astype(F32) * t_s
    x = x_refs[0][...]
    if n_src == 2:
        x = jnp.where(pl.program_id(0) < tiles_a, x, x_refs[1][...])
    out_ref[...] = x + jnp.dot(mix.astype(BF16), wout_ref[...], preferred_element_type=F32)


def _mix(xs, gates, o_h, y_s, main, u_col_block, d_skip, b_glu, w_glu, w_bh, w_bs, w_out, l, tm):
    t, ds = y_s.shape
    d = w_bh.shape[2]
    layer = lambda i: (l, 0, 0)
    const = lambda i: (0, 0)
    x_specs, tiles_a = _row_tile_specs(xs, tm)
    return pl.pallas_call(
        functools.partial(_mix_kernel, n_src=len(xs), tiles_a=tiles_a),
        grid=(t // tm,),
        in_specs=x_specs + [
            pl.BlockSpec((tm, d), lambda i: (i, 0)),
            pl.BlockSpec((tm, d), lambda i: (i, 1)),
            pl.BlockSpec((tm, ds), lambda i: (i, 0)),
            pl.BlockSpec((tm, ds), lambda i: (i, 0)),
            pl.BlockSpec((tm, ds), lambda i: (i, u_col_block)),
            pl.BlockSpec((1, ds), const),
            pl.BlockSpec((1, ds), const),
            _resident((None, ds, ds), layer),
            _resident((None, ds, d), layer),
            _resident((None, ds, d), layer),
            _resident((None, d, d), layer),
        ],
        out_specs=pl.BlockSpec((tm, d), lambda i: (i, 0)),
        out_shape=jax.ShapeDtypeStruct((t, d), F32),
        compiler_params=_cparams(("parallel",)),
        name="mix",
    )(*xs, gates, gates, o_h, y_s, main, d_skip.reshape(1, ds), b_glu.reshape(1, ds),
      w_glu, w_bh, w_bs, w_out)


def _rms(x, w):
    ms = jnp.mean(x * x, axis=-1, keepdims=True)
    return x * lax.rsqrt(ms + EPS) * w


def _ffn_kernel(x_ref, n2_ref, wg_ref, wu_ref, wd_ref, fn_ref, *rest, final_norm, tiles_a):
    out_refs, h_scr = rest[:-1], rest[-1]
    i = pl.program_id(0)
    j = pl.program_id(1)
    nj = pl.num_programs(1)

    def body(out_ref):
        @pl.when(j == 0)
        def _():
            xn = x_ref[...]
            out_ref[...] = xn
            h_scr[...] = _rms(xn, n2_ref[...]).astype(BF16)

        h = h_scr[...]
        ga = jnp.dot(h, wg_ref[...], preferred_element_type=F32)
        up = jnp.dot(h, wu_ref[...], preferred_element_type=F32)
        act = (jax.nn.silu(ga) * up).astype(BF16)
        out_ref[...] += jnp.dot(act, wd_ref[...], preferred_element_type=F32)

        if final_norm:
            @pl.when(j == nj - 1)
            def _():
                out_ref[...] = _rms(out_ref[...], fn_ref[...])

    if len(out_refs) == 1:
        body(out_refs[0])
    else:
        pl.when(i < tiles_a)(functools.partial(body, out_refs[0]))
        pl.when(i >= tiles_a)(functools.partial(body, out_refs[1]))


def _ffn(x, norm2, w_gate_up, w_down, l, fnorm, final_norm, tm, tf, split_rows=None):
    t, d = x.shape
    dff = w_down.shape[1]
    nj = dff // tf
    if split_rows is None:
        tiles_a = t // tm
        out_specs = [pl.BlockSpec((tm, d), lambda i, j: (i, 0))]
        out_shape = [jax.ShapeDtypeStruct((t, d), F32)]
    else:
        assert split_rows % tm == 0 and (t - split_rows) % tm == 0
        tiles_a = split_rows // tm
        out_specs = [
            pl.BlockSpec((tm, d), lambda i, j: (jnp.minimum(i, tiles_a - 1), 0),
                         pipeline_mode=pl.Buffered(1)),
            pl.BlockSpec((tm, d), lambda i, j: (jnp.maximum(i - tiles_a, 0), 0),
                         pipeline_mode=pl.Buffered(1)),
        ]
        out_shape = [jax.ShapeDtypeStruct((split_rows, d), F32),
                     jax.ShapeDtypeStruct((t - split_rows, d), F32)]
    return pl.pallas_call(
        functools.partial(_ffn_kernel, final_norm=final_norm, tiles_a=tiles_a),
        grid=(t // tm, nj),
        in_specs=[
            pl.BlockSpec((tm, d), lambda i, j: (i, 0)),
            pl.BlockSpec((1, d), lambda i, j: (0, 0)),
            pl.BlockSpec((None, d, tf), lambda i, j: (l, 0, j)),
            pl.BlockSpec((None, d, tf), lambda i, j: (l, 0, nj + j)),
            pl.BlockSpec((None, tf, d), lambda i, j: (l, j, 0)),
            pl.BlockSpec((1, d), lambda i, j: (0, 0)),
        ],
        out_specs=out_specs,
        out_shape=out_shape,
        scratch_shapes=[pltpu.VMEM((tm, d), BF16)],
        compiler_params=_cparams(("parallel" if split_rows is None else "arbitrary", "arbitrary")),
        name="ffn",
    )(x, norm2.reshape(1, d), w_gate_up, w_gate_up, w_down, fnorm.reshape(1, d))


def _pick(n, pref):
    t = pref
    while n % t:
        t //= 2
    return t


def _lower_bounds(lb_logits):
    p = jax.nn.softmax(lb_logits.astype(F32), axis=0)
    cs = jnp.cumsum(p, axis=0)
    return cs - cs[:1]


def kernel(x_prompt, x_sample, state_hgrn, state_s5_re, state_s5_im, lb_logits, norm1, w_in, hgrn_norm, w_bh, s5_a_log_neg_re, s5_a_im, s5_log_dt, s5_b_re, s5_b_im, s5_c_re, s5_c_im, s5_d, w_glu, b_glu, w_bs, w_out, norm2, w_gate_up, w_down, final_norm):
    depth = w_in.shape[0]
    bp, tp, d = x_prompt.shape
    bs, ts, _ = x_sample.shape
    assert bp == 1
    dh = lb_logits.shape[1]
    ds = s5_d.shape[1]
    n_groups = ds // S5_GROUP
    n_heads = dh // HEAD_DIM
    n_tok_p = bp * tp
    n_tok_s = bs * ts
    n_tok = n_tok_p + n_tok_s

    w_in_b = w_in.astype(BF16)
    w_bh_b = w_bh.astype(BF16)
    w_glu_b = w_glu.astype(BF16)
    w_bs_b = w_bs.astype(BF16)
    w_out_b = w_out.astype(BF16)
    w_gu_b = w_gate_up.astype(BF16)
    w_dn_b = w_down.astype(BF16)
    lbs = _lower_bounds(lb_logits)

    n_p = tp // S5_CHUNK
    n_cs = ts // S5_CHUNK
    s5_kt, s5_pb, s5_ct, s5_lam = _s5_tables(s5_a_log_neg_re, s5_a_im, s5_log_dt, s5_b_re,
                                             s5_b_im, s5_c_re, s5_c_im, int(math.log2(n_p)))
    n_slabs = n_groups // S5_SLAB

    n_pairs = S5_SLAB // 2

    def pack_state(re, im):
        nb = re.shape[0]
        x0 = jnp.stack([re.reshape(nb, n_slabs, n_pairs, 2 * S5_STATE),
                        im.reshape(nb, n_slabs, n_pairs, 2 * S5_STATE)], axis=3)
        return x0.transpose(1, 2, 3, 0, 4).reshape(n_slabs, S5_SLAB, nb, 2 * S5_STATE)

    def unpack_state(st):
        nb = st.shape[2]
        st = st.reshape(n_slabs, n_pairs, 2, nb, 2 * S5_STATE).transpose(2, 3, 0, 1, 4)
        st = st.reshape(2, nb, n_groups, S5_STATE)
        return st[0], st[1]

    tm_in = _pick(n_tok, 1024)
    tn_in = 1024
    tm_mix = _pick(n_tok, 256)
    tm_ffn = _pick(n_tok, 1024)
    tf = _pick(w_down.shape[1], 512)
    c_p = _pick(tp, 128)
    c_s = _pick(ts, 128)

    xs = (x_prompt.reshape(n_tok_p, d), x_sample.reshape(n_tok_s, d))
    if n_tok_p % tm_in or n_tok_s % tm_in or n_tok_p % tm_mix or n_tok_s % tm_mix:
        xs = (jnp.concatenate(xs, axis=0),)
    zero_h = jnp.zeros((1, bp, n_heads, HEAD_DIM, HEAD_DIM), F32)

    new_re_p, new_im_p, new_re_s, new_im_s = [], [], [], []
    hgrn_p = hgrn_s = None
    for l in range(depth):
        gates, main = _inproj(xs, norm1[l], w_in_b, l, 2 * d, tm_in, tn_in)

        o_h, hgrn_p = _hgrn(main, lbs[l], hgrn_norm[l], zero_h, None, hgrn_p,
                            layer=l, n_layers=depth, row_off=0, n_seq=bp, seq_len=tp, c=c_p,
                            nh=n_heads, hgrp=n_heads)
        o_h, hgrn_s = _hgrn(main, lbs[l], hgrn_norm[l], state_hgrn, o_h, hgrn_s,
                            layer=l, n_layers=depth, row_off=n_tok_p, n_seq=bs, seq_len=ts, c=c_s,
                            nh=n_heads, hgrp=n_heads)

        y, st_p, st_s = _s5(main, 4 * dh, s5_kt, s5_pb, s5_ct, s5_lam, l,
                            pack_state(state_s5_re[l], state_s5_im[l]),
                            n_p=n_p, n_b=bs, n_cs=n_cs)

        x = _mix(xs, gates, o_h, y, main, (4 * dh) // ds, s5_d[l], b_glu[l],
                 w_glu_b, w_bh_b, w_bs_b, w_out_b, l, tm_mix)
        last = l == depth - 1
        split = n_tok_p if last and n_tok_p % tm_ffn == 0 and n_tok_s % tm_ffn == 0 else None
        xs = tuple(_ffn(x, norm2[l], w_gu_b, w_dn_b, l, final_norm, last, tm_ffn, tf, split))

        re_p, im_p = unpack_state(st_p)
        re_s, im_s = unpack_state(st_s)
        new_re_p.append(re_p)
        new_im_p.append(im_p)
        new_re_s.append(re_s)
        new_im_s.append(im_s)

    if len(xs) == 1:
        xs = (xs[0][:n_tok_p], xs[0][n_tok_p:])
    y_prompt = xs[0].reshape(bp, tp, d)
    y_sample = xs[1].reshape(bs, ts, d)
    return (y_prompt, y_sample, hgrn_p, jnp.stack(new_re_p), jnp.stack(new_im_p),
            hgrn_s, jnp.stack(new_re_s), jnp.stack(new_im_s))
```

```python
import functools
import math

import jax
import jax.numpy as jnp
import numpy as np
from jax import lax
from jax.experimental import pallas as pl
from jax.experimental.pallas import tpu as pltpu

F32 = jnp.float32
BF16 = jnp.bfloat16

EPS = 1e-6
LOG2_E = 1.4426950408889634
HEAD_DIM = 128
S5_GROUP = 16
S5_STATE = 64
S5_CHUNK = 16
S5_SLAB = 8
S5_ROW_BLOCK = 272
MXU_TILE = 256
HGRN_STACK_ROWS = MXU_TILE
VMEM_LIMIT = 56 * 1024 * 1024
BIG_VMEM_LIMIT = 60 * 1024 * 1024


def _cparams(sem):
    return pltpu.CompilerParams(dimension_semantics=sem, vmem_limit_bytes=VMEM_LIMIT)


def _resident(shape, index_map):
    return pl.BlockSpec(shape, index_map, pipeline_mode=pl.Buffered(1))


def _row_tile_specs(xs, tm):
    d = xs[0].shape[1]
    tiles_a = xs[0].shape[0] // tm
    if len(xs) == 1:
        return [pl.BlockSpec((tm, d), lambda i, *_: (i, 0))], tiles_a
    assert len(xs) == 2 and xs[0].shape[0] % tm == 0 and xs[1].shape[0] % tm == 0
    return [
        pl.BlockSpec((tm, d), lambda i, *_: (jnp.minimum(i, tiles_a - 1), 0)),
        pl.BlockSpec((tm, d), lambda i, *_: (jnp.maximum(i - tiles_a, 0), 0),
                     pipeline_mode=pl.Buffered(1)),
    ], tiles_a


def _for_row_source(x_refs, tiles_a, pred, body):
    i = pl.program_id(0)
    if len(x_refs) == 1:
        conds = [pred]
    else:
        conds = [pred & (i < tiles_a), pred & (i >= tiles_a)]
    for cond, x_ref in zip(conds, x_refs):
        pl.when(cond)(functools.partial(body, x_ref))


def _inproj_kernel(*refs, n_src, tiles_a, n_main_tiles):
    x_refs = refs[:n_src]
    nw_ref, w_ref, gate_ref, main_ref, h_scr = refs[n_src:]
    j = pl.program_id(1)

    def normalize(x_ref):
        x = x_ref[...]
        ms = jnp.mean(x * x, axis=-1, keepdims=True)
        h_scr[...] = (x * lax.rsqrt(ms + EPS) * nw_ref[...]).astype(BF16)

    _for_row_source(x_refs, tiles_a, j == 0, normalize)

    @pl.when(j < n_main_tiles)
    def _():
        main_ref[...] = jnp.dot(h_scr[...], w_ref[...], preferred_element_type=F32)

    @pl.when(j >= n_main_tiles)
    def _():
        acc = jnp.dot(h_scr[...], w_ref[...], preferred_element_type=F32)
        gate_ref[...] = jax.nn.sigmoid(acc).astype(BF16)


def _inproj(xs, nw, w, l, n_gate_cols, tm, tn):
    t = sum(x.shape[0] for x in xs)
    d = xs[0].shape[1]
    n = w.shape[2]
    n_tiles = n // tn
    n_main_tiles = (n - n_gate_cols) // tn
    x_specs, tiles_a = _row_tile_specs(xs, tm)
    return pl.pallas_call(
        functools.partial(_inproj_kernel, n_src=len(xs), tiles_a=tiles_a,
                          n_main_tiles=n_main_tiles),
        grid=(t // tm, n_tiles),
        in_specs=x_specs + [
            pl.BlockSpec((1, d), lambda i, j: (0, 0)),
            pl.BlockSpec((None, d, tn), lambda i, j: (l, 0, j)),
        ],
        out_specs=[
            pl.BlockSpec((tm, tn), lambda i, j: (i, jnp.maximum(j - n_main_tiles, 0))),
            pl.BlockSpec((tm, tn), lambda i, j: (i, jnp.minimum(j, n_main_tiles - 1))),
        ],
        out_shape=[
            jax.ShapeDtypeStruct((t, n_gate_cols), BF16),
            jax.ShapeDtypeStruct((t, n - n_gate_cols), F32),
        ],
        scratch_shapes=[pltpu.VMEM((tm, d), BF16)],
        compiler_params=_cparams(("parallel", "arbitrary")),
        name="inproj",
    )(*xs, nw.reshape(1, d), w)


def _hgrn_level_masks(c):
    n_lev = int(math.log2(c)) + 1
    gsz = max(1, min(n_lev, HGRN_STACK_ROWS // c))
    n_groups = -(-n_lev // gsz)
    r = gsz * c
    t = np.arange(c)[:, None]
    s = np.arange(c)[None, :]
    masks = np.zeros((n_groups, r, r), np.float32)
    for lev in range(n_lev):
        if lev == 0:
            m = (t == s)
        else:
            h = 1 << (lev - 1)
            m = ((t & h) != 0) & ((s & h) == 0) & ((t // (2 * h)) == (s // (2 * h)))
        g, i = divmod(lev, gsz)
        masks[g, i * c:(i + 1) * c, i * c:(i + 1) * c] = m
    return masks, n_lev, gsz, n_groups


def _hgrn_kernel(q_ref, f_ref, i_ref, g_ref, lb_ref, gain_ref, s0_ref, mask_ref, *rest,
                 c, nh, hgrp, n_lev, gsz, n_groups):
    o_ref, s_ref, st_scr = rest[-3:]
    ci = pl.program_id(2)
    nc = pl.num_programs(2)

    @pl.when(ci == 0)
    def _():
        st_scr[...] = s0_ref[0]

    row = lax.broadcasted_iota(jnp.int32, (c, HEAD_DIM), 0)
    nt = (((1,), (1,)), ((), ()))
    tn = (((0,), (0,)), ((), ()))

    def head_operands(h):
        sl = slice(h * HEAD_DIM, (h + 1) * HEAD_DIM)
        z = f_ref[:, sl]
        lb = lb_ref[:, sl]
        ez = jnp.exp(-jnp.abs(z))
        log_sig = jnp.minimum(z, 0.0) - jnp.log(1.0 + ez)
        a0 = jnp.log(lb)
        a1 = jnp.log1p(-lb) + log_sig
        logf = jnp.maximum(a0, a1) + jnp.log(1.0 + jnp.exp(-jnp.abs(a0 - a1)))
        k = (1.0 - lb) * (jnp.where(z >= 0.0, ez, 1.0) / (1.0 + ez))
        q = q_ref[:, sl]
        v = jax.nn.silu(i_ref[:, sl])
        v_bf = v.astype(BF16)

        p = logf * LOG2_E
        tb = p
        a_lev = [q.astype(BF16)]
        b_lev = [k.astype(BF16)]
        for lev in range(1, n_lev):
            hs = 1 << (lev - 1)
            right = (row & hs) != 0
            e = jnp.exp2(jnp.where(right, p, tb - p))
            zl = (jnp.where(right, q, k) * e).astype(BF16)
            a_lev.append(zl)
            b_lev.append(zl)
            dn = pltpu.roll(tb, hs, 0)
            up = pltpu.roll(tb, c - hs, 0)
            p = p + jnp.where(right, dn, 0.0)
            tb = tb + jnp.where(right, dn, up)
        ops = []
        for g in range(n_groups):
            lo, hi = g * gsz, min((g + 1) * gsz, n_lev)
            n_in = hi - lo
            a_g = jnp.concatenate(a_lev[lo:hi], axis=0) if n_in > 1 else a_lev[lo]
            b_g = a_g if lo > 0 else (
                jnp.concatenate(b_lev[lo:hi], axis=0) if n_in > 1 else b_lev[lo])
            ops.append((a_g, b_g, n_in))
        q_in = (q * jnp.exp2(p)).astype(BF16)
        k_out = (k * jnp.exp2(tb - p)).astype(BF16)
        decay = jnp.exp2(jnp.broadcast_to(tb[0:1, :], (HEAD_DIM, HEAD_DIM)).T)
        return ops, v_bf, q_in, k_out, decay

    def fold_scores(g, sc, n_in):
        if c % HEAD_DIM == 0:
            return jnp.concatenate(
                [sc[i * c:(i + 1) * c, i * c:(i + 1) * c].astype(BF16)
                 * mask_ref[g, i * c:(i + 1) * c, i * c:(i + 1) * c]
                 for i in range(n_in)], axis=1)
        r = n_in * c
        sc = sc * mask_ref[g][:r, :r]
        fold = sc[0:c]
        for i in range(1, n_in):
            fold = fold + sc[i * c:(i + 1) * c]
        return fold.astype(BF16)

    for h0 in range(0, nh, hgrp):
        heads = range(h0, min(h0 + hgrp, nh))
        prep = {h: head_operands(h) for h in heads}
        scores = {h: [lax.dot_general(a_g, b_g, nt, preferred_element_type=F32)
                      for a_g, b_g, _ in prep[h][0]] for h in heads}
        upd = {h: lax.dot_general(prep[h][3], prep[h][1], tn, preferred_element_type=F32)
               for h in heads}
        for h in heads:
            ops, v_bf, q_in, _, decay = prep[h]
            st = st_scr[h]
            lhs = [q_in]
            rhs = [st.astype(BF16)]
            for g, (sc, (_, _, n_in)) in enumerate(zip(scores[h], ops)):
                lhs.append(fold_scores(g, sc, n_in))
                rhs.extend([v_bf] * n_in)
            o = jnp.dot(jnp.concatenate(lhs, axis=1), jnp.concatenate(rhs, axis=0),
                        preferred_element_type=F32)
            st_scr[h] = st * decay + upd[h]
            sl = slice(h * HEAD_DIM, (h + 1) * HEAD_DIM)
            ms = jnp.mean(o * o, axis=-1, keepdims=True)
            o = o * lax.rsqrt(ms + EPS) * gain_ref[:, sl] * jax.nn.silu(g_ref[:, sl])
            o_ref[:, sl] = o.astype(o_ref.dtype)

    @pl.when(ci == nc - 1)
    def _():
        s_ref[0] = st_scr[...]


def _hgrn(main, lb, gain, s0, o_all, s_all, *, layer, n_layers, row_off, n_seq, seq_len, c, nh,
          hgrp):
    n_tok = main.shape[0]
    dh = lb.shape[-1]
    n_heads = dh // HEAD_DIM
    w = nh * HEAD_DIM
    n_hg = n_heads // nh
    n_chunks = seq_len // c
    rb0 = row_off // c
    masks, n_lev, gsz, n_groups = _hgrn_level_masks(c)
    r = masks.shape[-1]

    def col_spec(group):
        return pl.BlockSpec(
            (c, w), lambda b, hg, ci: (rb0 + b * n_chunks + ci, group * n_hg + hg))

    vec_spec = pl.BlockSpec((1, w), lambda b, hg, ci: (0, hg))
    s0_layer = min(layer, s0.shape[0] - 1)
    st_spec = pl.BlockSpec((None, 1, nh, HEAD_DIM, HEAD_DIM),
                           lambda b, hg, ci: (s0_layer, b, hg, 0, 0))
    operands = [main, main, main, main, lb.reshape(1, dh), gain.reshape(1, dh), s0,
                jnp.asarray(masks, BF16 if c % HEAD_DIM == 0 else F32)]
    in_specs = [col_spec(0), col_spec(1), col_spec(2), col_spec(3), vec_spec, vec_spec,
                st_spec, _resident((n_groups, r, r), lambda b, hg, ci: (0, 0, 0))]
    aliases = {}
    for out_idx, buf in enumerate((o_all, s_all)):
        if buf is not None:
            aliases[len(operands)] = out_idx
            operands.append(buf)
            in_specs.append(pl.BlockSpec(memory_space=pl.ANY))
    o, s_new = pl.pallas_call(
        functools.partial(_hgrn_kernel, c=c, nh=nh, hgrp=hgrp, n_lev=n_lev, gsz=gsz,
                          n_groups=n_groups),
        grid=(n_seq, n_hg, n_chunks),
        in_specs=in_specs,
        out_specs=[
            pl.BlockSpec((c, w), lambda b, hg, ci: (rb0 + b * n_chunks + ci, hg)),
            pl.BlockSpec((None, 1, nh, HEAD_DIM, HEAD_DIM),
                         lambda b, hg, ci: (layer, b, hg, 0, 0)),
        ],
        out_shape=[
            jax.ShapeDtypeStruct((n_tok, dh), BF16),
            jax.ShapeDtypeStruct((n_layers, n_seq, n_heads, HEAD_DIM, HEAD_DIM), F32),
        ],
        input_output_aliases=aliases,
        scratch_shapes=[pltpu.VMEM((nh, HEAD_DIM, HEAD_DIM), F32)],
        compiler_params=_cparams(("parallel", "parallel", "arbitrary")),
        name="hgrn_c%d" % c,
    )(*operands)
    return o, s_new


def _cmul(xr, xi, lr, li):
    return xr * lr - xi * li, xr * li + xi * lr


def _s5_kernel(u_ref, kt_ref, pb_ref, ct_ref, lam_ref, x0_ref, y_ref, sp_ref, ss_ref,
               m_scr, b_scr, ct_scr, z_scr, *, n_p, n_b, n_cs, rb):
    n_rows = n_p + n_b * n_cs
    sw = 2 * S5_STATE
    cw = S5_SLAB * S5_GROUP

    gi = lax.broadcasted_iota(jnp.int32, (cw, cw), 0) // S5_GROUP
    gj = lax.broadcasted_iota(jnp.int32, (cw, cw), 1) // S5_GROUP
    same_group = gi == gj
    hw = cw // 2
    hk = S5_CHUNK * hw
    hs = (S5_SLAB // 2) * sw
    low_lanes = lax.broadcasted_iota(jnp.int32, (1, cw), 1) < hw

    def split_halves(a, b):
        return (jnp.where(low_lanes, a, pltpu.roll(b, hw, 1)),
                jnp.where(low_lanes, pltpu.roll(a, hw, 1), b))

    lag_tiles = [
        jnp.where(same_group, jnp.concatenate([kt_ref[0, j]] * S5_SLAB, axis=0), 0.0)
        for j in range(S5_CHUNK)]
    zero_tile = jnp.zeros((cw, cw), F32)
    tpb = MXU_TILE // hw
    for t in range(0, S5_CHUNK, 2):
        for s in range((t // tpb + 1) * tpb):
            pair = split_halves(lag_tiles[t - s] if t >= s else zero_tile,
                                lag_tiles[t + 1 - s] if t + 1 >= s else zero_tile)
            for h in range(2):
                m_scr[h, s * hw:(s + 1) * hw, t * hw:(t + 2) * hw] = (
                    pair[h][h * hw:(h + 1) * hw, :].astype(BF16))
    rg = lax.broadcasted_iota(jnp.int32, (cw, S5_SLAB * sw), 0) // S5_GROUP
    lane = lax.broadcasted_iota(jnp.int32, (cw, S5_SLAB * sw), 1)
    plane, half = lane // sw, (lane % sw) // S5_STATE
    own_state = (plane // 2 == rg // 2) & (half == rg % 2)
    straight = plane % 2 == rg % 2

    def expand(tab):
        tiled = jnp.concatenate([tab] * S5_SLAB, axis=1)
        swapped = jnp.concatenate([pltpu.roll(tab, S5_STATE, 1)] * S5_SLAB, axis=1)
        return jnp.where(own_state, jnp.where(straight, tiled, swapped), 0.0).astype(BF16)

    for s in range(S5_CHUNK):
        eb, ec = expand(pb_ref[0, s]), expand(ct_ref[0, s])
        for h in range(2):
            b_scr[h, s * hw:(s + 1) * hw, :] = eb[h * hw:(h + 1) * hw, h * hs:(h + 1) * hs]
            ct_scr[h, s * hw:(s + 1) * hw, :] = ec[h * hw:(h + 1) * hw, h * hs:(h + 1) * hs]

    def chunk_operands(r0):
        tiles = [u_ref[pl.ds(S5_CHUNK * r0 + s, rb, stride=S5_CHUNK), :]
                 for s in range(S5_CHUNK)]
        pairs = [split_halves(tiles[s], tiles[s + 1]) for s in range(0, S5_CHUNK, 2)]
        return [jnp.concatenate([p[h] for p in pairs], axis=1).astype(BF16) for h in range(2)]

    planes_per_half = S5_SLAB // 2
    for r0 in range(0, n_rows, rb):
        for h, a in enumerate(chunk_operands(r0)):
            z = jnp.dot(a, b_scr[h], preferred_element_type=F32)
            for g in range(planes_per_half):
                z_scr[h * planes_per_half + g, r0:r0 + rb, :] = z[:, g * sw:(g + 1) * sw]

    rowi = lax.broadcasted_iota(jnp.int32, (n_p, sw), 0)

    def shift_rows(x, sh):
        return jnp.where(rowi >= sh, pltpu.roll(x, sh, 0), 0.0)

    def scan_pair(k, carry):
        pr, pi = 2 * k, 2 * k + 1
        lam_k = lam_ref[0, k]
        xr = z_scr[pr, 0:n_p, :]
        xi = z_scr[pi, 0:n_p, :]
        for lev in range(int(math.log2(n_p))):
            sh = 1 << lev
            lr, li = lam_k[2 * lev:2 * lev + 1], lam_k[2 * lev + 1:2 * lev + 2]
            if sh % 8 == 0 and sh < n_p:
                dr, di = _cmul(xr[:n_p - sh], xi[:n_p - sh], lr, li)
                xr = jnp.concatenate([xr[:sh], xr[sh:] + dr], axis=0)
                xi = jnp.concatenate([xi[:sh], xi[sh:] + di], axis=0)
            else:
                dr, di = _cmul(shift_rows(xr, sh), shift_rows(xi, sh), lr, li)
                xr, xi = xr + dr, xi + di
        z_scr[pr, 0:n_p, :] = shift_rows(xr, 1)
        z_scr[pi, 0:n_p, :] = shift_rows(xi, 1)
        sp_ref[0, pr] = xr[n_p - 1:n_p]
        sp_ref[0, pi] = xi[n_p - 1:n_p]

        sr, si = x0_ref[0, pr], x0_ref[0, pi]
        for ci in range(n_cs):
            rows = pl.ds(n_p + ci, n_b, stride=n_cs)
            zr, zi = z_scr[pr, rows, :], z_scr[pi, rows, :]
            z_scr[pr, rows, :] = sr
            z_scr[pi, rows, :] = si
            dr, di = _cmul(sr, si, lam_k[0:1], lam_k[1:2])
            sr, si = dr + zr, di + zi
        ss_ref[0, pr] = sr
        ss_ref[0, pi] = si
        return carry

    lax.fori_loop(0, S5_SLAB // 2, scan_pair, 0)

    for r0 in range(0, n_rows, rb):
        ops = chunk_operands(r0)
        x_prev = [jnp.concatenate(
            [z_scr[h * planes_per_half + g, r0:r0 + rb, :].astype(BF16)
             for g in range(planes_per_half)], axis=1) for h in range(2)]
        for j in range(S5_CHUNK // tpb):
            cols = slice(j * MXU_TILE, (j + 1) * MXU_TILE)
            y = [jnp.dot(ops[h][:, :(j + 1) * MXU_TILE], m_scr[h, 0:(j + 1) * MXU_TILE, cols],
                         preferred_element_type=F32)
                 + lax.dot_general(x_prev[h], ct_scr[h, cols, :], (((1,), (1,)), ((), ())),
                                   preferred_element_type=F32)
                 for h in range(2)]
            for ti in range(0, tpb, 2):
                lo = slice(ti * hw, (ti + 2) * hw)
                out_even, out_odd = split_halves(y[0][:, lo], y[1][:, lo])
                for dt, out in ((0, out_even), (1, out_odd)):
                    t = j * tpb + ti + dt
                    y_ref[pl.ds(S5_CHUNK * r0 + t, rb, stride=S5_CHUNK), :] = out


def _s5(main, u_col0, kt, pb, ct, lam, l, x0, *, n_p, n_b, n_cs):
    n_slabs = kt.shape[1]
    n_tok = main.shape[0]
    n_rows = n_p + n_b * n_cs
    assert n_rows * S5_CHUNK == n_tok
    n_lam = lam.shape[3]
    cw = S5_SLAB * S5_GROUP
    kw = S5_CHUNK * cw
    gw = 2 * S5_STATE
    sw = S5_SLAB * gw
    rb = max(r for r in range(16, S5_ROW_BLOCK + 1, 16) if n_rows % r == 0)
    cb0 = u_col0 // cw

    def per_slab(shape):
        nd = len(shape)
        return pl.BlockSpec((1,) + shape, lambda v: (v,) + (0,) * nd)

    def table(shape):
        nd = len(shape)
        return pl.BlockSpec((None, 1) + shape, lambda v: (l, v) + (0,) * nd)

    return pl.pallas_call(
        functools.partial(_s5_kernel, n_p=n_p, n_b=n_b, n_cs=n_cs, rb=rb),
        grid=(n_slabs,),
        in_specs=[pl.BlockSpec((n_tok, cw), lambda v: (0, cb0 + v)),
                  table((S5_CHUNK, S5_GROUP, cw)),
                  table((S5_CHUNK, cw, 2 * S5_STATE)),
                  table((S5_CHUNK, cw, 2 * S5_STATE)),
                  table((S5_SLAB // 2, n_lam, gw)), per_slab((S5_SLAB, n_b, gw))],
        out_specs=[_resident((n_tok, cw), lambda v: (0, v)),
                   per_slab((S5_SLAB, 1, gw)), per_slab((S5_SLAB, n_b, gw))],
        out_shape=[
            jax.ShapeDtypeStruct((n_tok, n_slabs * cw), F32),
            jax.ShapeDtypeStruct((n_slabs, S5_SLAB, 1, gw), F32),
            jax.ShapeDtypeStruct((n_slabs, S5_SLAB, n_b, gw), F32),
        ],
        scratch_shapes=[pltpu.VMEM((2, kw // 2, kw // 2), BF16),
                        pltpu.VMEM((2, kw // 2, sw // 2), BF16),
                        pltpu.VMEM((2, kw // 2, sw // 2), BF16),
                        pltpu.VMEM((S5_SLAB, n_rows, 2 * S5_STATE), F32)],
        compiler_params=pltpu.CompilerParams(dimension_semantics=("parallel",),
                                             vmem_limit_bytes=BIG_VMEM_LIMIT),
        name="s5",
    )(main, kt, pb, ct, lam, x0)


def _s5_tables(a_log_neg_re, a_im, log_dt, b_re, b_im, c_re, c_im, n_scan_lev):
    hp = lax.Precision.HIGHEST
    lam_re = -jnp.exp(a_log_neg_re.astype(F32))
    lam_im = a_im.astype(F32)
    dt = jnp.exp(log_dt.astype(F32))[..., None]

    def powers(jj):
        e = jj[None, None, :, None]
        mag = jnp.exp((lam_re * dt)[:, :, None, :] * e)
        ang = (lam_im * dt)[:, :, None, :] * e
        return mag * jnp.cos(ang), mag * jnp.sin(ang)

    pw_re, pw_im = powers(jnp.arange(S5_CHUNK + 1, dtype=F32))
    num_re, num_im = pw_re[:, :, 1] - 1.0, pw_im[:, :, 1]
    den = lam_re * lam_re + lam_im * lam_im
    zoh_re = ((num_re * lam_re + num_im * lam_im) / den)[..., None]
    zoh_im = ((num_im * lam_re - num_re * lam_im) / den)[..., None]
    b_re, b_im = b_re.astype(F32), b_im.astype(F32)
    bb_re = zoh_re * b_re - zoh_im * b_im
    bb_im = zoh_re * b_im + zoh_im * b_re
    c_re, c_im = c_re.astype(F32)[:, :, None], c_im.astype(F32)[:, :, None]
    pr, pi = pw_re[:, :, :, None, :], pw_im[:, :, :, None, :]
    cp_re = c_re * pr - c_im * pi
    cp_im = c_re * pi + c_im * pr
    kj = (jnp.einsum('lgjpn,lgnq->lgjpq', cp_re[:, :, :S5_CHUNK], bb_re, precision=hp)
          - jnp.einsum('lgjpn,lgnq->lgjpq', cp_im[:, :, :S5_CHUNK], bb_im, precision=hp))
    qr, qi = powers(jnp.asarray(np.arange(S5_CHUNK - 1, -1, -1), F32))
    qr, qi = qr[..., None], qi[..., None]
    pb_re = (qr * bb_re[:, :, None] - qi * bb_im[:, :, None]).transpose(0, 1, 2, 4, 3)
    pb_im = (qr * bb_im[:, :, None] + qi * bb_re[:, :, None]).transpose(0, 1, 2, 4, 3)
    bst = jnp.concatenate([pb_re, pb_im], axis=-1)
    cst = jnp.concatenate([cp_re[:, :, 1:], -cp_im[:, :, 1:]], axis=-1)
    lp_re, lp_im = powers(S5_CHUNK * (2.0 ** jnp.arange(n_scan_lev, dtype=F32)))

    l, g = kj.shape[:2]
    v, sg = g // S5_SLAB, S5_SLAB
    cw = sg * S5_GROUP
    kt = kj.reshape(l, v, sg, S5_CHUNK, S5_GROUP, S5_GROUP)
    kt = kt.transpose(0, 1, 3, 5, 2, 4).reshape(l, v, S5_CHUNK, S5_GROUP, cw)
    pb = bst.reshape(l, v, sg, S5_CHUNK, S5_GROUP, 2 * S5_STATE)
    pb = pb.transpose(0, 1, 3, 2, 4, 5).reshape(l, v, S5_CHUNK, cw, 2 * S5_STATE)
    ct = cst.reshape(l, v, sg, S5_CHUNK, S5_GROUP, 2 * S5_STATE)
    ct = ct.transpose(0, 1, 3, 2, 4, 5).reshape(l, v, S5_CHUNK, cw, 2 * S5_STATE)

    def pair_lanes(a):
        a = a.reshape(l, v, sg // 2, 2, n_scan_lev, S5_STATE)
        return a.transpose(0, 1, 2, 4, 3, 5).reshape(l, v, sg // 2, n_scan_lev, 2 * S5_STATE)

    lam_big = jnp.stack([pair_lanes(lp_re), pair_lanes(lp_im)], axis=4)
    lam_big = lam_big.reshape(l, v, sg // 2, 2 * n_scan_lev, 2 * S5_STATE)
    return kt, pb, ct, lam_big


def _mix_kernel(*refs, n_src, tiles_a):
    x_refs = refs[:n_src]
    (gh_ref, gs_ref, oh_ref, y_ref, u_ref, d_ref, bglu_ref,
     wglu_ref, wbh_ref, wbs_ref, wout_ref, out_ref) = refs[n_src:]
    ys = jax.nn.gelu(y_ref[...] + d_ref[...] * u_ref[...])
    gate = jnp.dot(ys.astype(BF16), wglu_ref[...], preferred_element_type=F32) + bglu_ref[...]
    glu = (ys * jax.nn.sigmoid(gate)).astype(BF16)
    t_h = jnp.dot(oh_ref[...], wbh_ref[...], preferred_element_type=F32)
    t_s = jnp.dot(glu, wbs_ref[...], preferred_element_type=F32)
    mix = gh_ref[...].astype(F32) * t_h + gs_ref[...].astype(F32) * t_s
    x = x_refs[0][...]
    if n_src == 2:
        x = jnp.where(pl.program_id(0) < tiles_a, x, x_refs[1][...])
    out_ref[...] = x + jnp.dot(mix.astype(BF16), wout_ref[...], preferred_element_type=F32)


def _mix(xs, gates, o_h, y_s, main, u_col_block, d_skip, b_glu, w_glu, w_bh, w_bs, w_out, l, tm):
    t, ds = y_s.shape
    d = w_bh.shape[2]
    layer = lambda i: (l, 0, 0)
    const = lambda i: (0, 0)
    x_specs, tiles_a = _row_tile_specs(xs, tm)
    return pl.pallas_call(
        functools.partial(_mix_kernel, n_src=len(xs), tiles_a=tiles_a),
        grid=(t // tm,),
        in_specs=x_specs + [
            pl.BlockSpec((tm, d), lambda i: (i, 0)),
            pl.BlockSpec((tm, d), lambda i: (i, 1)),
            pl.BlockSpec((tm, ds), lambda i: (i, 0)),
            pl.BlockSpec((tm, ds), lambda i: (i, 0)),
            pl.BlockSpec((tm, ds), lambda i: (i, u_col_block)),
            pl.BlockSpec((1, ds), const),
            pl.BlockSpec((1, ds), const),
            _resident((None, ds, ds), layer),
            _resident((None, ds, d), layer),
            _resident((None, ds, d), layer),
            _resident((None, d, d), layer),
        ],
        out_specs=pl.BlockSpec((tm, d), lambda i: (i, 0)),
        out_shape=jax.ShapeDtypeStruct((t, d), F32),
        compiler_params=_cparams(("parallel",)),
        name="mix",
    )(*xs, gates, gates, o_h, y_s, main, d_skip.reshape(1, ds), b_glu.reshape(1, ds),
      w_glu, w_bh, w_bs, w_out)


def _rms(x, w):
    ms = jnp.mean(x * x, axis=-1, keepdims=True)
    return x * lax.rsqrt(ms + EPS) * w


def _ffn_kernel(x_ref, n2_ref, wg_ref, wu_ref, wd_ref, fn_ref, *rest, final_norm, tiles_a):
    out_refs, h_scr = rest[:-1], rest[-1]
    i = pl.program_id(0)
    j = pl.program_id(1)
    nj = pl.num_programs(1)

    def body(out_ref):
        @pl.when(j == 0)
        def _():
            xn = x_ref[...]
            out_ref[...] = xn
            h_scr[...] = _rms(xn, n2_ref[...]).astype(BF16)

        h = h_scr[...]
        ga = jnp.dot(h, wg_ref[...], preferred_element_type=F32)
        up = jnp.dot(h, wu_ref[...], preferred_element_type=F32)
        act = (jax.nn.silu(ga) * up).astype(BF16)
        out_ref[...] += jnp.dot(act, wd_ref[...], preferred_element_type=F32)

        if final_norm:
            @pl.when(j == nj - 1)
            def _():
                out_ref[...] = _rms(out_ref[...], fn_ref[...])

    if len(out_refs) == 1:
        body(out_refs[0])
    else:
        pl.when(i < tiles_a)(functools.partial(body, out_refs[0]))
        pl.when(i >= tiles_a)(functools.partial(body, out_refs[1]))


def _ffn(x, norm2, w_gate_up, w_down, l, fnorm, final_norm, tm, tf, split_rows=None):
    t, d = x.shape
    dff = w_down.shape[1]
    nj = dff // tf
    if split_rows is None:
        tiles_a = t // tm
        out_specs = [pl.BlockSpec((tm, d), lambda i, j: (i, 0))]
        out_shape = [jax.ShapeDtypeStruct((t, d), F32)]
    else:
        assert split_rows % tm == 0 and (t - split_rows) % tm == 0
        tiles_a = split_rows // tm
        out_specs = [
            pl.BlockSpec((tm, d), lambda i, j: (jnp.minimum(i, tiles_a - 1), 0),
                         pipeline_mode=pl.Buffered(1)),
            pl.BlockSpec((tm, d), lambda i, j: (jnp.maximum(i - tiles_a, 0), 0),
                         pipeline_mode=pl.Buffered(1)),
        ]
        out_shape = [jax.ShapeDtypeStruct((split_rows, d), F32),
                     jax.ShapeDtypeStruct((t - split_rows, d), F32)]
    return pl.pallas_call(
        functools.partial(_ffn_kernel, final_norm=final_norm, tiles_a=tiles_a),
        grid=(t // tm, nj),
        in_specs=[
            pl.BlockSpec((tm, d), lambda i, j: (i, 0)),
            pl.BlockSpec((1, d), lambda i, j: (0, 0)),
            pl.BlockSpec((None, d, tf), lambda i, j: (l, 0, j)),
            pl.BlockSpec((None, d, tf), lambda i, j: (l, 0, nj + j)),
            pl.BlockSpec((None, tf, d), lambda i, j: (l, j, 0)),
            pl.BlockSpec((1, d), lambda i, j: (0, 0)),
        ],
        out_specs=out_specs,
        out_shape=out_shape,
        scratch_shapes=[pltpu.VMEM((tm, d), BF16)],
        compiler_params=_cparams(("parallel" if split_rows is None else "arbitrary", "arbitrary")),
        name="ffn",
    )(x, norm2.reshape(1, d), w_gate_up, w_gate_up, w_down, fnorm.reshape(1, d))


def _pick(n, pref):
    t = pref
    while n % t:
        t //= 2
    return t


def _lower_bounds(lb_logits):
    p = jax.nn.softmax(lb_logits.astype(F32), axis=0)
    cs = jnp.cumsum(p, axis=0)
    return cs - cs[:1]


def kernel(x_prompt, x_sample, state_hgrn, state_s5_re, state_s5_im, lb_logits, norm1, w_in, hgrn_norm, w_bh, s5_a_log_neg_re, s5_a_im, s5_log_dt, s5_b_re, s5_b_im, s5_c_re, s5_c_im, s5_d, w_glu, b_glu, w_bs, w_out, norm2, w_gate_up, w_down, final_norm):
    depth = w_in.shape[0]
    bp, tp, d = x_prompt.shape
    bs, ts, _ = x_sample.shape
    assert bp == 1
    dh = lb_logits.shape[1]
    ds = s5_d.shape[1]
    n_groups = ds // S5_GROUP
    n_heads = dh // HEAD_DIM
    n_tok_p = bp * tp
    n_tok_s = bs * ts
    n_tok = n_tok_p + n_tok_s

    w_in_b = w_in.astype(BF16)
    w_bh_b = w_bh.astype(BF16)
    w_glu_b = w_glu.astype(BF16)
    w_bs_b = w_bs.astype(BF16)
    w_out_b = w_out.astype(BF16)
    w_gu_b = w_gate_up.astype(BF16)
    w_dn_b = w_down.astype(BF16)
    lbs = _lower_bounds(lb_logits)

    n_p = tp // S5_CHUNK
    n_cs = ts // S5_CHUNK
    s5_kt, s5_pb, s5_ct, s5_lam = _s5_tables(s5_a_log_neg_re, s5_a_im, s5_log_dt, s5_b_re,
                                             s5_b_im, s5_c_re, s5_c_im, int(math.log2(n_p)))
    n_slabs = n_groups // S5_SLAB
    n_pairs = S5_SLAB // 2

    def pack_state(re, im):
        nb = re.shape[0]
        x0 = jnp.stack([re.reshape(nb, n_slabs, n_pairs, 2 * S5_STATE),
                        im.reshape(nb, n_slabs, n_pairs, 2 * S5_STATE)], axis=3)
        return x0.transpose(1, 2, 3, 0, 4).reshape(n_slabs, S5_SLAB, nb, 2 * S5_STATE)

    def unpack_state(st):
        nb = st.shape[2]
        st = st.reshape(n_slabs, n_pairs, 2, nb, 2 * S5_STATE).transpose(2, 3, 0, 1, 4)
        st = st.reshape(2, nb, n_groups, S5_STATE)
        return st[0], st[1]

    tm_in = _pick(n_tok, 1024)
    tn_in = 1024
    tm_mix = _pick(n_tok, 256)
    tm_ffn = _pick(n_tok, 1024)
    tf = _pick(w_down.shape[1], 512)
    c_p = _pick(tp, 128)
    c_s = _pick(ts, 128)

    xs = (x_prompt.reshape(n_tok_p, d), x_sample.reshape(n_tok_s, d))
    if n_tok_p % tm_in or n_tok_s % tm_in or n_tok_p % tm_mix or n_tok_s % tm_mix:
        xs = (jnp.concatenate(xs, axis=0),)
    zero_h = jnp.zeros((1, bp, n_heads, HEAD_DIM, HEAD_DIM), F32)

    new_re_p, new_im_p, new_re_s, new_im_s = [], [], [], []
    hgrn_p = hgrn_s = None
    for l in range(depth):
        gates, main = _inproj(xs, norm1[l], w_in_b, l, 2 * d, tm_in, tn_in)

        o_h, hgrn_p = _hgrn(main, lbs[l], hgrn_norm[l], zero_h, None, hgrn_p,
                            layer=l, n_layers=depth, row_off=0, n_seq=bp, seq_len=tp, c=c_p,
                            nh=n_heads, hgrp=n_heads)
        o_h, hgrn_s = _hgrn(main, lbs[l], hgrn_norm[l], state_hgrn, o_h, hgrn_s,
                            layer=l, n_layers=depth, row_off=n_tok_p, n_seq=bs, seq_len=ts, c=c_s,
                            nh=n_heads, hgrp=n_heads)

        y, st_p, st_s = _s5(main, 4 * dh, s5_kt, s5_pb, s5_ct, s5_lam, l,
                            pack_state(state_s5_re[l], state_s5_im[l]),
                            n_p=n_p, n_b=bs, n_cs=n_cs)

        x = _mix(xs, gates, o_h, y, main, (4 * dh) // ds, s5_d[l], b_glu[l],
                 w_glu_b, w_bh_b, w_bs_b, w_out_b, l, tm_mix)
        last = l == depth - 1
        split = n_tok_p if last and n_tok_p % tm_ffn == 0 and n_tok_s % tm_ffn == 0 else None
        xs = tuple(_ffn(x, norm2[l], w_gu_b, w_dn_b, l, final_norm, last, tm_ffn, tf, split))

        re_p, im_p = unpack_state(st_p)
        re_s, im_s = unpack_state(st_s)
        new_re_p.append(re_p)
        new_im_p.append(im_p)
        new_re_s.append(re_s)
        new_im_s.append(im_s)

    if len(xs) == 1:
        xs = (xs[0][:n_tok_p], xs[0][n_tok_p:])
    y_prompt = xs[0].reshape(bp, tp, d)
    y_sample = xs[1].reshape(bs, ts, d)
    return (y_prompt, y_sample, hgrn_p, jnp.stack(new_re_p), jnp.stack(new_im_p),
            hgrn_s, jnp.stack(new_re_s), jnp.stack(new_im_s))
```

```python
import functools
import math

import jax
import jax.numpy as jnp
import numpy as np
from jax import lax
from jax.experimental import pallas as pl
from jax.experimental.pallas import tpu as pltpu

F32 = jnp.float32
BF16 = jnp.bfloat16

EPS = 1e-6
LOG2_E = 1.4426950408889634
HEAD_DIM = 128
S5_GROUP = 16
S5_STATE = 64
S5_CHUNK = 16
S5_SLAB = 8
S5_ROW_BLOCK = 272
MXU_TILE = 256
HGRN_STACK_ROWS = MXU_TILE
VMEM_LIMIT = 56 * 1024 * 1024
BIG_VMEM_LIMIT = 60 * 1024 * 1024


def _cparams(sem):
    return pltpu.CompilerParams(dimension_semantics=sem, vmem_limit_bytes=VMEM_LIMIT)


def _resident(shape, index_map):
    return pl.BlockSpec(shape, index_map, pipeline_mode=pl.Buffered(1))


def _row_tile_specs(xs, tm):
    d = xs[0].shape[1]
    tiles_a = xs[0].shape[0] // tm
    if len(xs) == 1:
        return [pl.BlockSpec((tm, d), lambda i, *_: (i, 0))], tiles_a
    assert len(xs) == 2 and xs[0].shape[0] % tm == 0 and xs[1].shape[0] % tm == 0
    return [
        pl.BlockSpec((tm, d), lambda i, *_: (jnp.minimum(i, tiles_a - 1), 0)),
        pl.BlockSpec((tm, d), lambda i, *_: (jnp.maximum(i - tiles_a, 0), 0),
                     pipeline_mode=pl.Buffered(1)),
    ], tiles_a


def _for_row_source(x_refs, tiles_a, pred, body):
    i = pl.program_id(0)
    if len(x_refs) == 1:
        conds = [pred]
    else:
        conds = [pred & (i < tiles_a), pred & (i >= tiles_a)]
    for cond, x_ref in zip(conds, x_refs):
        pl.when(cond)(functools.partial(body, x_ref))


def _inproj_kernel(*refs, n_src, tiles_a, n_main_tiles):
    x_refs = refs[:n_src]
    nw_ref, w_ref, gate_ref, main_ref, h_scr = refs[n_src:]
    j = pl.program_id(1)

    def normalize(x_ref):
        x = x_ref[...]
        ms = jnp.mean(x * x, axis=-1, keepdims=True)
        h_scr[...] = (x * lax.rsqrt(ms + EPS) * nw_ref[...]).astype(BF16)

    _for_row_source(x_refs, tiles_a, j == 0, normalize)

    @pl.when(j < n_main_tiles)
    def _():
        main_ref[...] = jnp.dot(h_scr[...], w_ref[...], preferred_element_type=F32)

    @pl.when(j >= n_main_tiles)
    def _():
        acc = jnp.dot(h_scr[...], w_ref[...], preferred_element_type=F32)
        gate_ref[...] = jax.nn.sigmoid(acc).astype(BF16)


def _inproj(xs, nw, w, l, n_gate_cols, tm, tn):
    t = sum(x.shape[0] for x in xs)
    d = xs[0].shape[1]
    n = w.shape[2]
    n_tiles = n // tn
    n_main_tiles = (n - n_gate_cols) // tn
    x_specs, tiles_a = _row_tile_specs(xs, tm)
    return pl.pallas_call(
        functools.partial(_inproj_kernel, n_src=len(xs), tiles_a=tiles_a,
                          n_main_tiles=n_main_tiles),
        grid=(t // tm, n_tiles),
        in_specs=x_specs + [
            pl.BlockSpec((1, d), lambda i, j: (0, 0)),
            pl.BlockSpec((None, d, tn), lambda i, j: (l, 0, j)),
        ],
        out_specs=[
            pl.BlockSpec((tm, tn), lambda i, j: (i, jnp.maximum(j - n_main_tiles, 0))),
            pl.BlockSpec((tm, tn), lambda i, j: (i, jnp.minimum(j, n_main_tiles - 1))),
        ],
        out_shape=[
            jax.ShapeDtypeStruct((t, n_gate_cols), BF16),
            jax.ShapeDtypeStruct((t, n - n_gate_cols), F32),
        ],
        scratch_shapes=[pltpu.VMEM((tm, d), BF16)],
        compiler_params=_cparams(("parallel", "arbitrary")),
        name="inproj",
    )(*xs, nw.reshape(1, d), w)


def _hgrn_level_masks(c):
    n_lev = int(math.log2(c)) + 1
    gsz = max(1, min(n_lev, HGRN_STACK_ROWS // c))
    n_groups = -(-n_lev // gsz)
    r = gsz * c
    t = np.arange(c)[:, None]
    s = np.arange(c)[None, :]
    masks = np.zeros((n_groups, r, r), np.float32)
    for lev in range(n_lev):
        if lev == 0:
            m = (t == s)
        else:
            h = 1 << (lev - 1)
            m = ((t & h) != 0) & ((s & h) == 0) & ((t // (2 * h)) == (s // (2 * h)))
        g, i = divmod(lev, gsz)
        masks[g, i * c:(i + 1) * c, i * c:(i + 1) * c] = m
    return masks, n_lev, gsz, n_groups


def _hgrn_kernel(q_ref, f_ref, i_ref, g_ref, lb_ref, gain_ref, s0_ref, mask_ref, *rest,
                 c, nh, hgrp, n_lev, gsz, n_groups):
    o_ref, s_ref, st_scr = rest[-3:]
    ci = pl.program_id(2)
    nc = pl.num_programs(2)

    @pl.when(ci == 0)
    def _():
        st_scr[...] = s0_ref[0]

    row = lax.broadcasted_iota(jnp.int32, (c, HEAD_DIM), 0)
    nt = (((1,), (1,)), ((), ()))
    tn = (((0,), (0,)), ((), ()))

    def head_operands(h):
        sl = slice(h * HEAD_DIM, (h + 1) * HEAD_DIM)
        z = f_ref[:, sl]
        lb = lb_ref[:, sl]
        ez = jnp.exp(-jnp.abs(z))
        log_sig = jnp.minimum(z, 0.0) - jnp.log(1.0 + ez)
        a0 = jnp.log(lb)
        a1 = jnp.log1p(-lb) + log_sig
        logf = jnp.maximum(a0, a1) + jnp.log(1.0 + jnp.exp(-jnp.abs(a0 - a1)))
        k = (1.0 - lb) * (jnp.where(z >= 0.0, ez, 1.0) / (1.0 + ez))
        q = q_ref[:, sl]
        v = jax.nn.silu(i_ref[:, sl])
        v_bf = v.astype(BF16)

        p = logf * LOG2_E
        tb = p
        a_lev = [q.astype(BF16)]
        b_lev = [k.astype(BF16)]
        for lev in range(1, n_lev):
            hs = 1 << (lev - 1)
            right = (row & hs) != 0
            e = jnp.exp2(jnp.where(right, p, tb - p))
            zl = (jnp.where(right, q, k) * e).astype(BF16)
            a_lev.append(zl)
            b_lev.append(zl)
            dn = pltpu.roll(tb, hs, 0)
            up = pltpu.roll(tb, c - hs, 0)
            p = p + jnp.where(right, dn, 0.0)
            tb = tb + jnp.where(right, dn, up)
        ops = []
        for g in range(n_groups):
            lo, hi = g * gsz, min((g + 1) * gsz, n_lev)
            n_in = hi - lo
            a_g = jnp.concatenate(a_lev[lo:hi], axis=0) if n_in > 1 else a_lev[lo]
            b_g = a_g if lo > 0 else (
                jnp.concatenate(b_lev[lo:hi], axis=0) if n_in > 1 else b_lev[lo])
            ops.append((a_g, b_g, n_in))
        q_in = (q * jnp.exp2(p)).astype(BF16)
        k_out = (k * jnp.exp2(tb - p)).astype(BF16)
        decay = jnp.exp2(jnp.broadcast_to(tb[0:1, :], (HEAD_DIM, HEAD_DIM)).T)
        return ops, v_bf, q_in, k_out, decay

    def fold_scores(g, sc, n_in):
        if c % HEAD_DIM == 0:
            return jnp.concatenate(
                [sc[i * c:(i + 1) * c, i * c:(i + 1) * c].astype(BF16)
                 * mask_ref[g, i * c:(i + 1) * c, i * c:(i + 1) * c]
                 for i in range(n_in)], axis=1)
        r = n_in * c
        sc = sc * mask_ref[g][:r, :r]
        fold = sc[0:c]
        for i in range(1, n_in):
            fold = fold + sc[i * c:(i + 1) * c]
        return fold.astype(BF16)

    for h0 in range(0, nh, hgrp):
        heads = range(h0, min(h0 + hgrp, nh))
        prep = {h: head_operands(h) for h in heads}
        scores = {h: [lax.dot_general(a_g, b_g, nt, preferred_element_type=F32)
                      for a_g, b_g, _ in prep[h][0]] for h in heads}
        upd = {h: lax.dot_general(prep[h][3], prep[h][1], tn, preferred_element_type=F32)
               for h in heads}
        for h in heads:
            ops, v_bf, q_in, _, decay = prep[h]
            st = st_scr[h]
            lhs = [q_in]
            rhs = [st.astype(BF16)]
            for g, (sc, (_, _, n_in)) in enumerate(zip(scores[h], ops)):
                lhs.append(fold_scores(g, sc, n_in))
                rhs.extend([v_bf] * n_in)
            o = jnp.dot(jnp.concatenate(lhs, axis=1), jnp.concatenate(rhs, axis=0),
                        preferred_element_type=F32)
            st_scr[h] = st * decay + upd[h]
            sl = slice(h * HEAD_DIM, (h + 1) * HEAD_DIM)
            ms = jnp.mean(o * o, axis=-1, keepdims=True)
            o = o * lax.rsqrt(ms + EPS) * gain_ref[:, sl] * jax.nn.silu(g_ref[:, sl])
            o_ref[:, sl] = o.astype(o_ref.dtype)

    @pl.when(ci == nc - 1)
    def _():
        s_ref[0] = st_scr[...]


def _hgrn(main, lb, gain, s0, o_all, s_all, *, layer, n_layers, row_off, n_seq, seq_len, c, nh,
          hgrp):
    n_tok = main.shape[0]
    dh = lb.shape[-1]
    n_heads = dh // HEAD_DIM
    w = nh * HEAD_DIM
    n_hg = n_heads // nh
    n_chunks = seq_len // c
    rb0 = row_off // c
    masks, n_lev, gsz, n_groups = _hgrn_level_masks(c)
    r = masks.shape[-1]

    def col_spec(group):
        return pl.BlockSpec(
            (c, w), lambda b, hg, ci: (rb0 + b * n_chunks + ci, group * n_hg + hg))

    vec_spec = pl.BlockSpec((1, w), lambda b, hg, ci: (0, hg))
    s0_layer = min(layer, s0.shape[0] - 1)
    st_spec = pl.BlockSpec((None, 1, nh, HEAD_DIM, HEAD_DIM),
                           lambda b, hg, ci: (s0_layer, b, hg, 0, 0))
    operands = [main, main, main, main, lb.reshape(1, dh), gain.reshape(1, dh), s0,
                jnp.asarray(masks, BF16 if c % HEAD_DIM == 0 else F32)]
    in_specs = [col_spec(0), col_spec(1), col_spec(2), col_spec(3), vec_spec, vec_spec,
                st_spec, _resident((n_groups, r, r), lambda b, hg, ci: (0, 0, 0))]
    aliases = {}
    for out_idx, buf in enumerate((o_all, s_all)):
        if buf is not None:
            aliases[len(operands)] = out_idx
            operands.append(buf)
            in_specs.append(pl.BlockSpec(memory_space=pl.ANY))
    o, s_new = pl.pallas_call(
        functools.partial(_hgrn_kernel, c=c, nh=nh, hgrp=hgrp, n_lev=n_lev, gsz=gsz,
                          n_groups=n_groups),
        grid=(n_seq, n_hg, n_chunks),
        in_specs=in_specs,
        out_specs=[
            pl.BlockSpec((c, w), lambda b, hg, ci: (rb0 + b * n_chunks + ci, hg)),
            pl.BlockSpec((None, 1, nh, HEAD_DIM, HEAD_DIM),
                         lambda b, hg, ci: (layer, b, hg, 0, 0)),
        ],
        out_shape=[
            jax.ShapeDtypeStruct((n_tok, dh), BF16),
            jax.ShapeDtypeStruct((n_layers, n_seq, n_heads, HEAD_DIM, HEAD_DIM), F32),
        ],
        input_output_aliases=aliases,
        scratch_shapes=[pltpu.VMEM((nh, HEAD_DIM, HEAD_DIM), F32)],
        compiler_params=_cparams(("parallel", "parallel", "arbitrary")),
        name="hgrn_c%d" % c,
    )(*operands)
    return o, s_new


def _cmul(xr, xi, lr, li):
    return xr * lr - xi * li, xr * li + xi * lr


def _s5_kernel(u_ref, kt_ref, pb_ref, ct_ref, lam_ref, x0_ref, y_ref, sp_ref, ss_ref,
               m_scr, b_scr, ct_scr, z_scr, *, n_p, n_b, n_cs, rb):
    n_rows = n_p + n_b * n_cs
    sw = 2 * S5_STATE
    cw = S5_SLAB * S5_GROUP

    gi = lax.broadcasted_iota(jnp.int32, (cw, cw), 0) // S5_GROUP
    gj = lax.broadcasted_iota(jnp.int32, (cw, cw), 1) // S5_GROUP
    same_group = gi == gj
    hw = cw // 2
    hk = S5_CHUNK * hw
    hs = (S5_SLAB // 2) * sw
    low_lanes = lax.broadcasted_iota(jnp.int32, (1, cw), 1) < hw

    def split_halves(a, b):
        return (jnp.where(low_lanes, a, pltpu.roll(b, hw, 1)),
                jnp.where(low_lanes, pltpu.roll(a, hw, 1), b))

    lag_tiles = [
        jnp.where(same_group, jnp.concatenate([kt_ref[0, j]] * S5_SLAB, axis=0), 0.0)
        for j in range(S5_CHUNK)]
    zero_tile = jnp.zeros((cw, cw), F32)
    tpb = MXU_TILE // hw
    for t in range(0, S5_CHUNK, 2):
        for s in range((t // tpb + 1) * tpb):
            pair = split_halves(lag_tiles[t - s] if t >= s else zero_tile,
                                lag_tiles[t + 1 - s] if t + 1 >= s else zero_tile)
            for h in range(2):
                m_scr[h, s * hw:(s + 1) * hw, t * hw:(t + 2) * hw] = (
                    pair[h][h * hw:(h + 1) * hw, :].astype(BF16))
    rg = lax.broadcasted_iota(jnp.int32, (cw, S5_SLAB * sw), 0) // S5_GROUP
    lane = lax.broadcasted_iota(jnp.int32, (cw, S5_SLAB * sw), 1)
    plane, half = lane // sw, (lane % sw) // S5_STATE
    own_state = (plane // 2 == rg // 2) & (half == rg % 2)
    straight = plane % 2 == rg % 2

    def expand(tab):
        tiled = jnp.concatenate([tab] * S5_SLAB, axis=1)
        swapped = jnp.concatenate([pltpu.roll(tab, S5_STATE, 1)] * S5_SLAB, axis=1)
        return jnp.where(own_state, jnp.where(straight, tiled, swapped), 0.0).astype(BF16)

    for s in range(S5_CHUNK):
        eb, ec = expand(pb_ref[0, s]), expand(ct_ref[0, s])
        for h in range(2):
            b_scr[h, s * hw:(s + 1) * hw, :] = eb[h * hw:(h + 1) * hw, h * hs:(h + 1) * hs]
            ct_scr[h, s * hw:(s + 1) * hw, :] = ec[h * hw:(h + 1) * hw, h * hs:(h + 1) * hs]

    def chunk_operands(r0):
        tiles = [u_ref[pl.ds(S5_CHUNK * r0 + s, rb, stride=S5_CHUNK), :]
                 for s in range(S5_CHUNK)]
        pairs = [split_halves(tiles[s], tiles[s + 1]) for s in range(0, S5_CHUNK, 2)]
        return [jnp.concatenate([p[h] for p in pairs], axis=1).astype(BF16) for h in range(2)]

    planes_per_half = S5_SLAB // 2
    for r0 in range(0, n_rows, rb):
        for h, a in enumerate(chunk_operands(r0)):
            z = jnp.dot(a, b_scr[h], preferred_element_type=F32)
            for g in range(planes_per_half):
                z_scr[h * planes_per_half + g, r0:r0 + rb, :] = z[:, g * sw:(g + 1) * sw]

    rowi = lax.broadcasted_iota(jnp.int32, (n_p, sw), 0)

    def shift_rows(x, sh):
        return jnp.where(rowi >= sh, pltpu.roll(x, sh, 0), 0.0)

    def scan_pair(k, carry):
        pr, pi = 2 * k, 2 * k + 1
        lam_k = lam_ref[0, k]
        xr = z_scr[pr, 0:n_p, :]
        xi = z_scr[pi, 0:n_p, :]
        for lev in range(int(math.log2(n_p))):
            sh = 1 << lev
            lr, li = lam_k[2 * lev:2 * lev + 1], lam_k[2 * lev + 1:2 * lev + 2]
            if sh % 8 == 0 and sh < n_p:
                dr, di = _cmul(xr[:n_p - sh], xi[:n_p - sh], lr, li)
                xr = jnp.concatenate([xr[:sh], xr[sh:] + dr], axis=0)
                xi = jnp.concatenate([xi[:sh], xi[sh:] + di], axis=0)
            else:
                dr, di = _cmul(shift_rows(xr, sh), shift_rows(xi, sh), lr, li)
                xr, xi = xr + dr, xi + di
        z_scr[pr, 0:n_p, :] = shift_rows(xr, 1)
        z_scr[pi, 0:n_p, :] = shift_rows(xi, 1)
        sp_ref[0, pr] = xr[n_p - 1:n_p]
        sp_ref[0, pi] = xi[n_p - 1:n_p]

        sr, si = x0_ref[0, pr], x0_ref[0, pi]
        for ci in range(n_cs):
            rows = pl.ds(n_p + ci, n_b, stride=n_cs)
            zr, zi = z_scr[pr, rows, :], z_scr[pi, rows, :]
            z_scr[pr, rows, :] = sr
            z_scr[pi, rows, :] = si
            dr, di = _cmul(sr, si, lam_k[0:1], lam_k[1:2])
            sr, si = dr + zr, di + zi
        ss_ref[0, pr] = sr
        ss_ref[0, pi] = si
        return carry

    for k in range(S5_SLAB // 2):
        scan_pair(k, 0)

    for r0 in range(0, n_rows, rb):
        ops = chunk_operands(r0)
        x_prev = [jnp.concatenate(
            [z_scr[h * planes_per_half + g, r0:r0 + rb, :].astype(BF16)
             for g in range(planes_per_half)], axis=1) for h in range(2)]
        for j in range(S5_CHUNK // tpb):
            cols = slice(j * MXU_TILE, (j + 1) * MXU_TILE)
            y = [jnp.dot(ops[h][:, :(j + 1) * MXU_TILE], m_scr[h, 0:(j + 1) * MXU_TILE, cols],
                         preferred_element_type=F32)
                 + lax.dot_general(x_prev[h], ct_scr[h, cols, :], (((1,), (1,)), ((), ())),
                                   preferred_element_type=F32)
                 for h in range(2)]
            for ti in range(0, tpb, 2):
                lo = slice(ti * hw, (ti + 2) * hw)
                out_even, out_odd = split_halves(y[0][:, lo], y[1][:, lo])
                for dt, out in ((0, out_even), (1, out_odd)):
                    t = j * tpb + ti + dt
                    y_ref[pl.ds(S5_CHUNK * r0 + t, rb, stride=S5_CHUNK), :] = out


def _s5(main, u_col0, kt, pb, ct, lam, l, x0, *, n_p, n_b, n_cs):
    n_slabs = kt.shape[1]
    n_tok = main.shape[0]
    n_rows = n_p + n_b * n_cs
    assert n_rows * S5_CHUNK == n_tok
    n_lam = lam.shape[3]
    cw = S5_SLAB * S5_GROUP
    kw = S5_CHUNK * cw
    gw = 2 * S5_STATE
    sw = S5_SLAB * gw
    rb = max(r for r in range(16, S5_ROW_BLOCK + 1, 16) if n_rows % r == 0)
    cb0 = u_col0 // cw

    def per_slab(shape):
        nd = len(shape)
        return pl.BlockSpec((1,) + shape, lambda v: (v,) + (0,) * nd)

    def table(shape):
        nd = len(shape)
        return pl.BlockSpec((None, 1) + shape, lambda v: (l, v) + (0,) * nd)

    return pl.pallas_call(
        functools.partial(_s5_kernel, n_p=n_p, n_b=n_b, n_cs=n_cs, rb=rb),
        grid=(n_slabs,),
        in_specs=[pl.BlockSpec((n_tok, cw), lambda v: (0, cb0 + v)),
                  table((S5_CHUNK, S5_GROUP, cw)),
                  table((S5_CHUNK, cw, 2 * S5_STATE)),
                  table((S5_CHUNK, cw, 2 * S5_STATE)),
                  table((S5_SLAB // 2, n_lam, gw)), per_slab((S5_SLAB, n_b, gw))],
        out_specs=[_resident((n_tok, cw), lambda v: (0, v)),
                   per_slab((S5_SLAB, 1, gw)), per_slab((S5_SLAB, n_b, gw))],
        out_shape=[
            jax.ShapeDtypeStruct((n_tok, n_slabs * cw), F32),
            jax.ShapeDtypeStruct((n_slabs, S5_SLAB, 1, gw), F32),
            jax.ShapeDtypeStruct((n_slabs, S5_SLAB, n_b, gw), F32),
        ],
        scratch_shapes=[pltpu.VMEM((2, kw // 2, kw // 2), BF16),
                        pltpu.VMEM((2, kw // 2, sw // 2), BF16),
                        pltpu.VMEM((2, kw // 2, sw // 2), BF16),
                        pltpu.VMEM((S5_SLAB, n_rows, 2 * S5_STATE), F32)],
        compiler_params=pltpu.CompilerParams(dimension_semantics=("parallel",),
                                             vmem_limit_bytes=BIG_VMEM_LIMIT),
        name="s5",
    )(main, kt, pb, ct, lam, x0)


def _s5_tables(a_log_neg_re, a_im, log_dt, b_re, b_im, c_re, c_im, n_scan_lev):
    hp = lax.Precision.HIGHEST
    lam_re = -jnp.exp(a_log_neg_re.astype(F32))
    lam_im = a_im.astype(F32)
    dt = jnp.exp(log_dt.astype(F32))[..., None]

    def powers(jj):
        e = jj[None, None, :, None]
        mag = jnp.exp((lam_re * dt)[:, :, None, :] * e)
        ang = (lam_im * dt)[:, :, None, :] * e
        return mag * jnp.cos(ang), mag * jnp.sin(ang)

    pw_re, pw_im = powers(jnp.arange(S5_CHUNK + 1, dtype=F32))
    num_re, num_im = pw_re[:, :, 1] - 1.0, pw_im[:, :, 1]
    den = lam_re * lam_re + lam_im * lam_im
    zoh_re = ((num_re * lam_re + num_im * lam_im) / den)[..., None]
    zoh_im = ((num_im * lam_re - num_re * lam_im) / den)[..., None]
    b_re, b_im = b_re.astype(F32), b_im.astype(F32)
    bb_re = zoh_re * b_re - zoh_im * b_im
    bb_im = zoh_re * b_im + zoh_im * b_re
    c_re, c_im = c_re.astype(F32)[:, :, None], c_im.astype(F32)[:, :, None]
    pr, pi = pw_re[:, :, :, None, :], pw_im[:, :, :, None, :]
    cp_re = c_re * pr - c_im * pi
    cp_im = c_re * pi + c_im * pr
    kj = (jnp.einsum('lgjpn,lgnq->lgjpq', cp_re[:, :, :S5_CHUNK], bb_re, precision=hp)
          - jnp.einsum('lgjpn,lgnq->lgjpq', cp_im[:, :, :S5_CHUNK], bb_im, precision=hp))
    qr, qi = powers(jnp.asarray(np.arange(S5_CHUNK - 1, -1, -1), F32))
    qr, qi = qr[..., None], qi[..., None]
    pb_re = (qr * bb_re[:, :, None] - qi * bb_im[:, :, None]).transpose(0, 1, 2, 4, 3)
    pb_im = (qr * bb_im[:, :, None] + qi * bb_re[:, :, None]).transpose(0, 1, 2, 4, 3)
    bst = jnp.concatenate([pb_re, pb_im], axis=-1)
    cst = jnp.concatenate([cp_re[:, :, 1:], -cp_im[:, :, 1:]], axis=-1)
    lp_re, lp_im = powers(S5_CHUNK * (2.0 ** jnp.arange(n_scan_lev, dtype=F32)))

    l, g = kj.shape[:2]
    v, sg = g // S5_SLAB, S5_SLAB
    cw = sg * S5_GROUP
    kt = kj.reshape(l, v, sg, S5_CHUNK, S5_GROUP, S5_GROUP)
    kt = kt.transpose(0, 1, 3, 5, 2, 4).reshape(l, v, S5_CHUNK, S5_GROUP, cw)
    pb = bst.reshape(l, v, sg, S5_CHUNK, S5_GROUP, 2 * S5_STATE)
    pb = pb.transpose(0, 1, 3, 2, 4, 5).reshape(l, v, S5_CHUNK, cw, 2 * S5_STATE)
    ct = cst.reshape(l, v, sg, S5_CHUNK, S5_GROUP, 2 * S5_STATE)
    ct = ct.transpose(0, 1, 3, 2, 4, 5).reshape(l, v, S5_CHUNK, cw, 2 * S5_STATE)

    def pair_lanes(a):
        a = a.reshape(l, v, sg // 2, 2, n_scan_lev, S5_STATE)
        return a.transpose(0, 1, 2, 4, 3, 5).reshape(l, v, sg // 2, n_scan_lev, 2 * S5_STATE)

    lam_big = jnp.stack([pair_lanes(lp_re), pair_lanes(lp_im)], axis=4)
    lam_big = lam_big.reshape(l, v, sg // 2, 2 * n_scan_lev, 2 * S5_STATE)
    return kt, pb, ct, lam_big


def _mix_kernel(*refs, n_src, tiles_a):
    x_refs = refs[:n_src]
    (gh_ref, gs_ref, oh_ref, y_ref, u_ref, d_ref, bglu_ref,
     wglu_ref, wbh_ref, wbs_ref, wout_ref, out_ref) = refs[n_src:]
    ys = jax.nn.gelu(y_ref[...] + d_ref[...] * u_ref[...])
    gate = jnp.dot(ys.astype(BF16), wglu_ref[...], preferred_element_type=F32) + bglu_ref[...]
    glu = (ys * jax.nn.sigmoid(gate)).astype(BF16)
    t_h = jnp.dot(oh_ref[...], wbh_ref[...], preferred_element_type=F32)
    t_s = jnp.dot(glu, wbs_ref[...], preferred_element_type=F32)
    mix = gh_ref[...].astype(F32) * t_h + gs_ref[...].astype(F32) * t_s
    x = x_refs[0][...]
    if n_src == 2:
        x = jnp.where(pl.program_id(0) < tiles_a, x, x_refs[1][...])
    out_ref[...] = x + jnp.dot(mix.astype(BF16), wout_ref[...], preferred_element_type=F32)


def _mix(xs, gates, o_h, y_s, main, u_col_block, d_skip, b_glu, w_glu, w_bh, w_bs, w_out, l, tm):
    t, ds = y_s.shape
    d = w_bh.shape[2]
    layer = lambda i: (l, 0, 0)
    const = lambda i: (0, 0)
    x_specs, tiles_a = _row_tile_specs(xs, tm)
    return pl.pallas_call(
        functools.partial(_mix_kernel, n_src=len(xs), tiles_a=tiles_a),
        grid=(t // tm,),
        in_specs=x_specs + [
            pl.BlockSpec((tm, d), lambda i: (i, 0)),
            pl.BlockSpec((tm, d), lambda i: (i, 1)),
            pl.BlockSpec((tm, ds), lambda i: (i, 0)),
            pl.BlockSpec((tm, ds), lambda i: (i, 0)),
            pl.BlockSpec((tm, ds), lambda i: (i, u_col_block)),
            pl.BlockSpec((1, ds), const),
            pl.BlockSpec((1, ds), const),
            _resident((None, ds, ds), layer),
            _resident((None, ds, d), layer),
            _resident((None, ds, d), layer),
            _resident((None, d, d), layer),
        ],
        out_specs=pl.BlockSpec((tm, d), lambda i: (i, 0)),
        out_shape=jax.ShapeDtypeStruct((t, d), F32),
        compiler_params=_cparams(("parallel",)),
        name="mix",
    )(*xs, gates, gates, o_h, y_s, main, d_skip.reshape(1, ds), b_glu.reshape(1, ds),
      w_glu, w_bh, w_bs, w_out)


def _rms(x, w):
    ms = jnp.mean(x * x, axis=-1, keepdims=True)
    return x * lax.rsqrt(ms + EPS) * w


def _ffn_kernel(x_ref, n2_ref, wg_ref, wu_ref, wd_ref, fn_ref, *rest, final_norm, tiles_a):
    out_refs, h_scr = rest[:-1], rest[-1]
    i = pl.program_id(0)
    j = pl.program_id(1)
    nj = pl.num_programs(1)

    def body(out_ref):
        @pl.when(j == 0)
        def _():
            xn = x_ref[...]
            out_ref[...] = xn
            h_scr[...] = _rms(xn, n2_ref[...]).astype(BF16)

        h = h_scr[...]
        ga = jnp.dot(h, wg_ref[...], preferred_element_type=F32)
        up = jnp.dot(h, wu_ref[...], preferred_element_type=F32)
        act = (jax.nn.silu(ga) * up).astype(BF16)
        out_ref[...] += jnp.dot(act, wd_ref[...], preferred_element_type=F32)

        if final_norm:
            @pl.when(j == nj - 1)
            def _():
                out_ref[...] = _rms(out_ref[...], fn_ref[...])

    if len(out_refs) == 1:
        body(out_refs[0])
    else:
        pl.when(i < tiles_a)(functools.partial(body, out_refs[0]))
        pl.when(i >= tiles_a)(functools.partial(body, out_refs[1]))


def _ffn(x, norm2, w_gate_up, w_down, l, fnorm, final_norm, tm, tf, split_rows=None):
    t, d = x.shape
    dff = w_down.shape[1]
    nj = dff // tf
    if split_rows is None:
        tiles_a = t // tm
        out_specs = [pl.BlockSpec((tm, d), lambda i, j: (i, 0))]
        out_shape = [jax.ShapeDtypeStruct((t, d), F32)]
    else:
        assert split_rows % tm == 0 and (t - split_rows) % tm == 0
        tiles_a = split_rows // tm
        out_specs = [
            pl.BlockSpec((tm, d), lambda i, j: (jnp.minimum(i, tiles_a - 1), 0),
                         pipeline_mode=pl.Buffered(1)),
            pl.BlockSpec((tm, d), lambda i, j: (jnp.maximum(i - tiles_a, 0), 0),
                         pipeline_mode=pl.Buffered(1)),
        ]
        out_shape = [jax.ShapeDtypeStruct((split_rows, d), F32),
                     jax.ShapeDtypeStruct((t - split_rows, d), F32)]
    return pl.pallas_call(
        functools.partial(_ffn_kernel, final_norm=final_norm, tiles_a=tiles_a),
        grid=(t // tm, nj),
        in_specs=[
            pl.BlockSpec((tm, d), lambda i, j: (i, 0)),
            pl.BlockSpec((1, d), lambda i, j: (0, 0)),
            pl.BlockSpec((None, d, tf), lambda i, j: (l, 0, j)),
            pl.BlockSpec((None, d, tf), lambda i, j: (l, 0, nj + j)),
            pl.BlockSpec((None, tf, d), lambda i, j: (l, j, 0)),
            pl.BlockSpec((1, d), lambda i, j: (0, 0)),
        ],
        out_specs=out_specs,
        out_shape=out_shape,
        scratch_shapes=[pltpu.VMEM((tm, d), BF16)],
        compiler_params=_cparams(("parallel" if split_rows is None else "arbitrary", "arbitrary")),
        name="ffn",
    )(x, norm2.reshape(1, d), w_gate_up, w_gate_up, w_down, fnorm.reshape(1, d))


def _pick(n, pref):
    t = pref
    while n % t:
        t //= 2
    return t


def _lower_bounds(lb_logits):
    p = jax.nn.softmax(lb_logits.astype(F32), axis=0)
    cs = jnp.cumsum(p, axis=0)
    return cs - cs[:1]


def kernel(x_prompt, x_sample, state_hgrn, state_s5_re, state_s5_im, lb_logits, norm1, w_in, hgrn_norm, w_bh, s5_a_log_neg_re, s5_a_im, s5_log_dt, s5_b_re, s5_b_im, s5_c_re, s5_c_im, s5_d, w_glu, b_glu, w_bs, w_out, norm2, w_gate_up, w_down, final_norm):
    depth = w_in.shape[0]
    bp, tp, d = x_prompt.shape
    bs, ts, _ = x_sample.shape
    assert bp == 1
    dh = lb_logits.shape[1]
    ds = s5_d.shape[1]
    n_groups = ds // S5_GROUP
    n_heads = dh // HEAD_DIM
    n_tok_p = bp * tp
    n_tok_s = bs * ts
    n_tok = n_tok_p + n_tok_s

    w_in_b = w_in.astype(BF16)
    w_bh_b = w_bh.astype(BF16)
    w_glu_b = w_glu.astype(BF16)
    w_bs_b = w_bs.astype(BF16)
    w_out_b = w_out.astype(BF16)
    w_gu_b = w_gate_up.astype(BF16)
    w_dn_b = w_down.astype(BF16)
    lbs = _lower_bounds(lb_logits)

    n_p = tp // S5_CHUNK
    n_cs = ts // S5_CHUNK
    s5_kt, s5_pb, s5_ct, s5_lam = _s5_tables(s5_a_log_neg_re, s5_a_im, s5_log_dt, s5_b_re,
                                             s5_b_im, s5_c_re, s5_c_im, int(math.log2(n_p)))
    n_slabs = n_groups // S5_SLAB
    n_pairs = S5_SLAB // 2

    def pack_state(re, im):
        nb = re.shape[0]
        x0 = jnp.stack([re.reshape(nb, n_slabs, n_pairs, 2 * S5_STATE),
                        im.reshape(nb, n_slabs, n_pairs, 2 * S5_STATE)], axis=3)
        return x0.transpose(1, 2, 3, 0, 4).reshape(n_slabs, S5_SLAB, nb, 2 * S5_STATE)

    def unpack_state(st):
        nb = st.shape[2]
        st = st.reshape(n_slabs, n_pairs, 2, nb, 2 * S5_STATE).transpose(2, 3, 0, 1, 4)
        st = st.reshape(2, nb, n_groups, S5_STATE)
        return st[0], st[1]

    tm_in = _pick(n_tok, 1024)
    tn_in = 1024
    tm_mix = _pick(n_tok, 256)
    tm_ffn = _pick(n_tok, 1024)
    tf = _pick(w_down.shape[1], 512)
    c_p = _pick(tp, 128)
    c_s = _pick(ts, 128)

    xs = (x_prompt.reshape(n_tok_p, d), x_sample.reshape(n_tok_s, d))
    if n_tok_p % tm_in or n_tok_s % tm_in or n_tok_p % tm_mix or n_tok_s % tm_mix:
        xs = (jnp.concatenate(xs, axis=0),)
    zero_h = jnp.zeros((1, bp, n_heads, HEAD_DIM, HEAD_DIM), F32)

    new_re_p, new_im_p, new_re_s, new_im_s = [], [], [], []
    hgrn_p = hgrn_s = None
    for l in range(depth):
        gates, main = _inproj(xs, norm1[l], w_in_b, l, 2 * d, tm_in, tn_in)

        o_h, hgrn_p = _hgrn(main, lbs[l], hgrn_norm[l], zero_h, None, hgrn_p,
                            layer=l, n_layers=depth, row_off=0, n_seq=bp, seq_len=tp, c=c_p,
                            nh=n_heads, hgrp=n_heads)
        o_h, hgrn_s = _hgrn(main, lbs[l], hgrn_norm[l], state_hgrn, o_h, hgrn_s,
                            layer=l, n_layers=depth, row_off=n_tok_p, n_seq=bs, seq_len=ts, c=c_s,
                            nh=n_heads, hgrp=n_heads)

        y, st_p, st_s = _s5(main, 4 * dh, s5_kt, s5_pb, s5_ct, s5_lam, l,
                            pack_state(state_s5_re[l], state_s5_im[l]),
                            n_p=n_p, n_b=bs, n_cs=n_cs)

        x = _mix(xs, gates, o_h, y, main, (4 * dh) // ds, s5_d[l], b_glu[l],
                 w_glu_b, w_bh_b, w_bs_b, w_out_b, l, tm_mix)
        last = l == depth - 1
        split = n_tok_p if last and n_tok_p % tm_ffn == 0 and n_tok_s % tm_ffn == 0 else None
        xs = tuple(_ffn(x, norm2[l], w_gu_b, w_dn_b, l, final_norm, last, tm_ffn, tf, split))

        re_p, im_p = unpack_state(st_p)
        re_s, im_s = unpack_state(st_s)
        new_re_p.append(re_p)
        new_im_p.append(im_p)
        new_re_s.append(re_s)
        new_im_s.append(im_s)

    if len(xs) == 1:
        xs = (xs[0][:n_tok_p], xs[0][n_tok_p:])
    y_prompt = xs[0].reshape(bp, tp, d)
    y_sample = xs[1].reshape(bs, ts, d)
    return (y_prompt, y_sample, hgrn_p, jnp.stack(new_re_p), jnp.stack(new_im_p),
            hgrn_s, jnp.stack(new_re_s), jnp.stack(new_im_s))
```

```python
import functools
import math

import jax
import jax.numpy as jnp
import numpy as np
from jax import lax
from jax.experimental import pallas as pl
from jax.experimental.pallas import tpu as pltpu

F32 = jnp.float32
BF16 = jnp.bfloat16

EPS = 1e-6
LOG2_E = 1.4426950408889634
HEAD_DIM = 128
S5_GROUP = 16
S5_STATE = 64
S5_CHUNK = 16
S5_SLAB = 8
S5_ROW_BLOCK = 272
MXU_TILE = 256
HGRN_STACK_ROWS = MXU_TILE
VMEM_LIMIT = 56 * 1024 * 1024
BIG_VMEM_LIMIT = 60 * 1024 * 1024


def _cparams(sem):
    return pltpu.CompilerParams(dimension_semantics=sem, vmem_limit_bytes=VMEM_LIMIT)


def _resident(shape, index_map):
    return pl.BlockSpec(shape, index_map, pipeline_mode=pl.Buffered(1))


def _row_tile_specs(xs, tm):
    d = xs[0].shape[1]
    tiles_a = xs[0].shape[0] // tm
    if len(xs) == 1:
        return [pl.BlockSpec((tm, d), lambda i, *_: (i, 0))], tiles_a
    assert len(xs) == 2 and xs[0].shape[0] % tm == 0 and xs[1].shape[0] % tm == 0
    return [
        pl.BlockSpec((tm, d), lambda i, *_: (jnp.minimum(i, tiles_a - 1), 0)),
        pl.BlockSpec((tm, d), lambda i, *_: (jnp.maximum(i - tiles_a, 0), 0),
                     pipeline_mode=pl.Buffered(1)),
    ], tiles_a


def _for_row_source(x_refs, tiles_a, pred, body):
    i = pl.program_id(0)
    if len(x_refs) == 1:
        conds = [pred]
    else:
        conds = [pred & (i < tiles_a), pred & (i >= tiles_a)]
    for cond, x_ref in zip(conds, x_refs):
        pl.when(cond)(functools.partial(body, x_ref))


def _inproj_kernel(*refs, n_src, tiles_a, n_main_tiles):
    x_refs = refs[:n_src]
    nw_ref, w_ref, gate_ref, main_ref, h_scr = refs[n_src:]
    j = pl.program_id(1)

    def normalize(x_ref):
        x = x_ref[...]
        ms = jnp.mean(x * x, axis=-1, keepdims=True)
        h_scr[...] = (x * lax.rsqrt(ms + EPS) * nw_ref[...]).astype(BF16)

    _for_row_source(x_refs, tiles_a, j == 0, normalize)

    @pl.when(j < n_main_tiles)
    def _():
        main_ref[...] = jnp.dot(h_scr[...], w_ref[...], preferred_element_type=F32)

    @pl.when(j >= n_main_tiles)
    def _():
        acc = jnp.dot(h_scr[...], w_ref[...], preferred_element_type=F32)
        gate_ref[...] = jax.nn.sigmoid(acc).astype(BF16)


def _inproj(xs, nw, w, l, n_gate_cols, tm, tn):
    t = sum(x.shape[0] for x in xs)
    d = xs[0].shape[1]
    n = w.shape[2]
    n_tiles = n // tn
    n_main_tiles = (n - n_gate_cols) // tn
    x_specs, tiles_a = _row_tile_specs(xs, tm)
    return pl.pallas_call(
        functools.partial(_inproj_kernel, n_src=len(xs), tiles_a=tiles_a,
                          n_main_tiles=n_main_tiles),
        grid=(t // tm, n_tiles),
        in_specs=x_specs + [
            pl.BlockSpec((1, d), lambda i, j: (0, 0)),
            pl.BlockSpec((None, d, tn), lambda i, j: (l, 0, j)),
        ],
        out_specs=[
            pl.BlockSpec((tm, tn), lambda i, j: (i, jnp.maximum(j - n_main_tiles, 0))),
            pl.BlockSpec((tm, tn), lambda i, j: (i, jnp.minimum(j, n_main_tiles - 1))),
        ],
        out_shape=[
            jax.ShapeDtypeStruct((t, n_gate_cols), BF16),
            jax.ShapeDtypeStruct((t, n - n_gate_cols), F32),
        ],
        scratch_shapes=[pltpu.VMEM((tm, d), BF16)],
        compiler_params=_cparams(("parallel", "arbitrary")),
        name="inproj",
    )(*xs, nw.reshape(1, d), w)


def _hgrn_level_masks(c):
    n_lev = int(math.log2(c)) + 1
    gsz = max(1, min(n_lev, HGRN_STACK_ROWS // c))
    n_groups = -(-n_lev // gsz)
    r = gsz * c
    t = np.arange(c)[:, None]
    s = np.arange(c)[None, :]
    masks = np.zeros((n_groups, r, r), np.float32)
    for lev in range(n_lev):
        if lev == 0:
            m = (t == s)
        else:
            h = 1 << (lev - 1)
            m = ((t & h) != 0) & ((s & h) == 0) & ((t // (2 * h)) == (s // (2 * h)))
        g, i = divmod(lev, gsz)
        masks[g, i * c:(i + 1) * c, i * c:(i + 1) * c] = m
    return masks, n_lev, gsz, n_groups


def _hgrn_kernel(q_ref, f_ref, i_ref, g_ref, lb_ref, gain_ref, s0_ref, mask_ref, *rest,
                 c, nh, hgrp, n_lev, gsz, n_groups):
    o_ref, s_ref, st_scr = rest[-3:]
    ci = pl.program_id(2)
    nc = pl.num_programs(2)

    @pl.when(ci == 0)
    def _():
        st_scr[...] = s0_ref[0]

    row = lax.broadcasted_iota(jnp.int32, (c, HEAD_DIM), 0)
    nt = (((1,), (1,)), ((), ()))
    tn = (((0,), (0,)), ((), ()))

    def head_operands(h):
        sl = slice(h * HEAD_DIM, (h + 1) * HEAD_DIM)
        z = f_ref[:, sl]
        lb = lb_ref[:, sl]
        ez = jnp.exp(-jnp.abs(z))
        log_sig = jnp.minimum(z, 0.0) - jnp.log(1.0 + ez)
        a0 = jnp.log(lb)
        a1 = jnp.log1p(-lb) + log_sig
        logf = jnp.maximum(a0, a1) + jnp.log(1.0 + jnp.exp(-jnp.abs(a0 - a1)))
        k = (1.0 - lb) * (jnp.where(z >= 0.0, ez, 1.0) / (1.0 + ez))
        q = q_ref[:, sl]
        v = jax.nn.silu(i_ref[:, sl])
        v_bf = v.astype(BF16)

        p = logf * LOG2_E
        tb = p
        a_lev = [q.astype(BF16)]
        b_lev = [k.astype(BF16)]
        for lev in range(1, n_lev):
            hs = 1 << (lev - 1)
            right = (row & hs) != 0
            e = jnp.exp2(jnp.where(right, p, tb - p))
            zl = (jnp.where(right, q, k) * e).astype(BF16)
            a_lev.append(zl)
            b_lev.append(zl)
            dn = pltpu.roll(tb, hs, 0)
            up = pltpu.roll(tb, c - hs, 0)
            p = p + jnp.where(right, dn, 0.0)
            tb = tb + jnp.where(right, dn, up)
        ops = []
        for g in range(n_groups):
            lo, hi = g * gsz, min((g + 1) * gsz, n_lev)
            n_in = hi - lo
            a_g = jnp.concatenate(a_lev[lo:hi], axis=0) if n_in > 1 else a_lev[lo]
            b_g = a_g if lo > 0 else (
                jnp.concatenate(b_lev[lo:hi], axis=0) if n_in > 1 else b_lev[lo])
            ops.append((a_g, b_g, n_in))
        q_in = (q * jnp.exp2(p)).astype(BF16)
        k_out = (k * jnp.exp2(tb - p)).astype(BF16)
        decay = jnp.exp2(jnp.broadcast_to(tb[0:1, :], (HEAD_DIM, HEAD_DIM)).T)
        return ops, v_bf, q_in, k_out, decay

    def fold_scores(g, sc, n_in):
        if c % HEAD_DIM == 0:
            return jnp.concatenate(
                [sc[i * c:(i + 1) * c, i * c:(i + 1) * c].astype(BF16)
                 * mask_ref[g, i * c:(i + 1) * c, i * c:(i + 1) * c]
                 for i in range(n_in)], axis=1)
        r = n_in * c
        sc = sc * mask_ref[g][:r, :r]
        fold = sc[0:c]
        for i in range(1, n_in):
            fold = fold + sc[i * c:(i + 1) * c]
        return fold.astype(BF16)

    for h0 in range(0, nh, hgrp):
        heads = range(h0, min(h0 + hgrp, nh))
        prep = {h: head_operands(h) for h in heads}
        scores = {h: [lax.dot_general(a_g, b_g, nt, preferred_element_type=F32)
                      for a_g, b_g, _ in prep[h][0]] for h in heads}
        upd = {h: lax.dot_general(prep[h][3], prep[h][1], tn, preferred_element_type=F32)
               for h in heads}
        for h in heads:
            ops, v_bf, q_in, _, decay = prep[h]
            st = st_scr[h]
            lhs = [q_in]
            rhs = [st.astype(BF16)]
            for g, (sc, (_, _, n_in)) in enumerate(zip(scores[h], ops)):
                lhs.append(fold_scores(g, sc, n_in))
                rhs.extend([v_bf] * n_in)
            o = jnp.dot(jnp.concatenate(lhs, axis=1), jnp.concatenate(rhs, axis=0),
                        preferred_element_type=F32)
            st_scr[h] = st * decay + upd[h]
            sl = slice(h * HEAD_DIM, (h + 1) * HEAD_DIM)
            ms = jnp.mean(o * o, axis=-1, keepdims=True)
            o = o * lax.rsqrt(ms + EPS) * gain_ref[:, sl] * jax.nn.silu(g_ref[:, sl])
            o_ref[:, sl] = o.astype(o_ref.dtype)

    @pl.when(ci == nc - 1)
    def _():
        s_ref[0] = st_scr[...]


def _hgrn(main, lb, gain, s0, o_all, s_all, *, layer, n_layers, row_off, n_seq, seq_len, c, nh,
          hgrp):
    n_tok = main.shape[0]
    dh = lb.shape[-1]
    n_heads = dh // HEAD_DIM
    w = nh * HEAD_DIM
    n_hg = n_heads // nh
    n_chunks = seq_len // c
    rb0 = row_off // c
    masks, n_lev, gsz, n_groups = _hgrn_level_masks(c)
    r = masks.shape[-1]

    def col_spec(group):
        return pl.BlockSpec(
            (c, w), lambda b, hg, ci: (rb0 + b * n_chunks + ci, group * n_hg + hg))

    vec_spec = pl.BlockSpec((1, w), lambda b, hg, ci: (0, hg))
    s0_layer = min(layer, s0.shape[0] - 1)
    st_spec = pl.BlockSpec((None, 1, nh, HEAD_DIM, HEAD_DIM),
                           lambda b, hg, ci: (s0_layer, b, hg, 0, 0))
    operands = [main, main, main, main, lb.reshape(1, dh), gain.reshape(1, dh), s0,
                jnp.asarray(masks, BF16 if c % HEAD_DIM == 0 else F32)]
    in_specs = [col_spec(0), col_spec(1), col_spec(2), col_spec(3), vec_spec, vec_spec,
                st_spec, _resident((n_groups, r, r), lambda b, hg, ci: (0, 0, 0))]
    aliases = {}
    for out_idx, buf in enumerate((o_all, s_all)):
        if buf is not None:
            aliases[len(operands)] = out_idx
            operands.append(buf)
            in_specs.append(pl.BlockSpec(memory_space=pl.ANY))
    o, s_new = pl.pallas_call(
        functools.partial(_hgrn_kernel, c=c, nh=nh, hgrp=hgrp, n_lev=n_lev, gsz=gsz,
                          n_groups=n_groups),
        grid=(n_seq, n_hg, n_chunks),
        in_specs=in_specs,
        out_specs=[
            pl.BlockSpec((c, w), lambda b, hg, ci: (rb0 + b * n_chunks + ci, hg)),
            pl.BlockSpec((None, 1, nh, HEAD_DIM, HEAD_DIM),
                         lambda b, hg, ci: (layer, b, hg, 0, 0)),
        ],
        out_shape=[
            jax.ShapeDtypeStruct((n_tok, dh), BF16),
            jax.ShapeDtypeStruct((n_layers, n_seq, n_heads, HEAD_DIM, HEAD_DIM), F32),
        ],
        input_output_aliases=aliases,
        scratch_shapes=[pltpu.VMEM((nh, HEAD_DIM, HEAD_DIM), F32)],
        compiler_params=_cparams(("parallel", "parallel", "arbitrary")),
        name="hgrn_c%d" % c,
    )(*operands)
    return o, s_new


def _cmul(xr, xi, lr, li):
    return xr * lr - xi * li, xr * li + xi * lr


def _s5_kernel(u_ref, kt_ref, pb_ref, ct_ref, lam_ref, x0_ref, y_ref, sp_ref, ss_ref,
               m_scr, b_scr, ct_scr, z_scr, *, n_p, n_b, n_cs, rb):
    n_rows = n_p + n_b * n_cs
    sw = 2 * S5_STATE
    cw = S5_SLAB * S5_GROUP

    gi = lax.broadcasted_iota(jnp.int32, (cw, cw), 0) // S5_GROUP
    gj = lax.broadcasted_iota(jnp.int32, (cw, cw), 1) // S5_GROUP
    same_group = gi == gj
    hw = cw // 2
    hk = S5_CHUNK * hw
    hs = (S5_SLAB // 2) * sw
    low_lanes = lax.broadcasted_iota(jnp.int32, (1, cw), 1) < hw

    def split_halves(a, b):
        return (jnp.where(low_lanes, a, pltpu.roll(b, hw, 1)),
                jnp.where(low_lanes, pltpu.roll(a, hw, 1), b))

    lag_tiles = [
        jnp.where(same_group, jnp.concatenate([kt_ref[0, j]] * S5_SLAB, axis=0), 0.0)
        for j in range(S5_CHUNK)]
    zero_tile = jnp.zeros((cw, cw), F32)
    tpb = MXU_TILE // hw
    for t in range(0, S5_CHUNK, 2):
        for s in range((t // tpb + 1) * tpb):
            pair = split_halves(lag_tiles[t - s] if t >= s else zero_tile,
                                lag_tiles[t + 1 - s] if t + 1 >= s else zero_tile)
            for h in range(2):
                m_scr[h, s * hw:(s + 1) * hw, t * hw:(t + 2) * hw] = (
                    pair[h][h * hw:(h + 1) * hw, :].astype(BF16))
    rg = lax.broadcasted_iota(jnp.int32, (cw, S5_SLAB * sw), 0) // S5_GROUP
    lane = lax.broadcasted_iota(jnp.int32, (cw, S5_SLAB * sw), 1)
    plane, half = lane // sw, (lane % sw) // S5_STATE
    own_state = (plane // 2 == rg // 2) & (half == rg % 2)
    straight = plane % 2 == rg % 2

    def expand(tab):
        tiled = jnp.concatenate([tab] * S5_SLAB, axis=1)
        swapped = jnp.concatenate([pltpu.roll(tab, S5_STATE, 1)] * S5_SLAB, axis=1)
        return jnp.where(own_state, jnp.where(straight, tiled, swapped), 0.0).astype(BF16)

    for s in range(S5_CHUNK):
        eb, ec = expand(pb_ref[0, s]), expand(ct_ref[0, s])
        for h in range(2):
            b_scr[h, s * hw:(s + 1) * hw, :] = eb[h * hw:(h + 1) * hw, h * hs:(h + 1) * hs]
            ct_scr[h, s * hw:(s + 1) * hw, :] = ec[h * hw:(h + 1) * hw, h * hs:(h + 1) * hs]

    def chunk_operands(r0):
        tiles = [u_ref[pl.ds(S5_CHUNK * r0 + s, rb, stride=S5_CHUNK), :]
                 for s in range(S5_CHUNK)]
        pairs = [split_halves(tiles[s], tiles[s + 1]) for s in range(0, S5_CHUNK, 2)]
        return [jnp.concatenate([p[h] for p in pairs], axis=1).astype(BF16) for h in range(2)]

    planes_per_half = S5_SLAB // 2
    for r0 in range(0, n_rows, rb):
        for h, a in enumerate(chunk_operands(r0)):
            z = jnp.dot(a, b_scr[h], preferred_element_type=F32)
            for g in range(planes_per_half):
                z_scr[h * planes_per_half + g, r0:r0 + rb, :] = z[:, g * sw:(g + 1) * sw]

    rowi = lax.broadcasted_iota(jnp.int32, (n_p, sw), 0)

    def shift_rows(x, sh):
        return jnp.where(rowi >= sh, pltpu.roll(x, sh, 0), 0.0)

    def scan_pair(k, carry):
        pr, pi = 2 * k, 2 * k + 1
        lam_k = lam_ref[0, k]
        xr = z_scr[pr, 0:n_p, :]
        xi = z_scr[pi, 0:n_p, :]
        for lev in range(int(math.log2(n_p))):
            sh = 1 << lev
            lr, li = lam_k[2 * lev:2 * lev + 1], lam_k[2 * lev + 1:2 * lev + 2]
            if sh % 8 == 0 and sh < n_p:
                dr, di = _cmul(xr[:n_p - sh], xi[:n_p - sh], lr, li)
                xr = jnp.concatenate([xr[:sh], xr[sh:] + dr], axis=0)
                xi = jnp.concatenate([xi[:sh], xi[sh:] + di], axis=0)
            else:
                dr, di = _cmul(shift_rows(xr, sh), shift_rows(xi, sh), lr, li)
                xr, xi = xr + dr, xi + di
        z_scr[pr, 0:n_p, :] = shift_rows(xr, 1)
        z_scr[pi, 0:n_p, :] = shift_rows(xi, 1)
        sp_ref[0, pr] = xr[n_p - 1:n_p]
        sp_ref[0, pi] = xi[n_p - 1:n_p]

        sr, si = x0_ref[0, pr], x0_ref[0, pi]
        for ci in range(n_cs):
            rows = pl.ds(n_p + ci, n_b, stride=n_cs)
            zr, zi = z_scr[pr, rows, :], z_scr[pi, rows, :]
            z_scr[pr, rows, :] = sr
            z_scr[pi, rows, :] = si
            dr, di = _cmul(sr, si, lam_k[0:1], lam_k[1:2])
            sr, si = dr + zr, di + zi
        ss_ref[0, pr] = sr
        ss_ref[0, pi] = si
        return carry

    for k in range(S5_SLAB // 2):
        scan_pair(k, 0)

    for r0 in range(0, n_rows, rb):
        ops = chunk_operands(r0)
        x_prev = [jnp.concatenate(
            [z_scr[h * planes_per_half + g, r0:r0 + rb, :].astype(BF16)
             for g in range(planes_per_half)], axis=1) for h in range(2)]
        for j in range(S5_CHUNK // tpb):
            cols = slice(j * MXU_TILE, (j + 1) * MXU_TILE)
            y = [jnp.dot(ops[h][:, :(j + 1) * MXU_TILE], m_scr[h, 0:(j + 1) * MXU_TILE, cols],
                         preferred_element_type=F32)
                 + lax.dot_general(x_prev[h], ct_scr[h, cols, :], (((1,), (1,)), ((), ())),
                                   preferred_element_type=F32)
                 for h in range(2)]
            for ti in range(0, tpb, 2):
                lo = slice(ti * hw, (ti + 2) * hw)
                out_even, out_odd = split_halves(y[0][:, lo], y[1][:, lo])
                for dt, out in ((0, out_even), (1, out_odd)):
                    t = j * tpb + ti + dt
                    y_ref[pl.ds(S5_CHUNK * r0 + t, rb, stride=S5_CHUNK), :] = out


def _s5(main, u_col0, kt, pb, ct, lam, l, x0, *, n_p, n_b, n_cs):
    n_slabs = kt.shape[1]
    n_tok = main.shape[0]
    n_rows = n_p + n_b * n_cs
    assert n_rows * S5_CHUNK == n_tok
    n_lam = lam.shape[3]
    cw = S5_SLAB * S5_GROUP
    kw = S5_CHUNK * cw
    gw = 2 * S5_STATE
    sw = S5_SLAB * gw
    rb = max(r for r in range(16, S5_ROW_BLOCK + 1, 16) if n_rows % r == 0)
    cb0 = u_col0 // cw

    def per_slab(shape):
        nd = len(shape)
        return pl.BlockSpec((1,) + shape, lambda v: (v,) + (0,) * nd)

    def table(shape):
        nd = len(shape)
        return pl.BlockSpec((None, 1) + shape, lambda v: (l, v) + (0,) * nd)

    return pl.pallas_call(
        functools.partial(_s5_kernel, n_p=n_p, n_b=n_b, n_cs=n_cs, rb=rb),
        grid=(n_slabs,),
        in_specs=[pl.BlockSpec((n_tok, cw), lambda v: (0, cb0 + v)),
                  table((S5_CHUNK, S5_GROUP, cw)),
                  table((S5_CHUNK, cw, 2 * S5_STATE)),
                  table((S5_CHUNK, cw, 2 * S5_STATE)),
                  table((S5_SLAB // 2, n_lam, gw)), per_slab((S5_SLAB, n_b, gw))],
        out_specs=[_resident((n_tok, cw), lambda v: (0, v)),
                   per_slab((S5_SLAB, 1, gw)), per_slab((S5_SLAB, n_b, gw))],
        out_shape=[
            jax.ShapeDtypeStruct((n_tok, n_slabs * cw), F32),
            jax.ShapeDtypeStruct((n_slabs, S5_SLAB, 1, gw), F32),
            jax.ShapeDtypeStruct((n_slabs, S5_SLAB, n_b, gw), F32),
        ],
        scratch_shapes=[pltpu.VMEM((2, kw // 2, kw // 2), BF16),
                        pltpu.VMEM((2, kw // 2, sw // 2), BF16),
                        pltpu.VMEM((2, kw // 2, sw // 2), BF16),
                        pltpu.VMEM((S5_SLAB, n_rows, 2 * S5_STATE), F32)],
        compiler_params=pltpu.CompilerParams(dimension_semantics=("parallel",),
                                             vmem_limit_bytes=BIG_VMEM_LIMIT),
        name="s5",
    )(main, kt, pb, ct, lam, x0)


def _s5_tables(a_log_neg_re, a_im, log_dt, b_re, b_im, c_re, c_im, n_scan_lev):
    hp = lax.Precision.HIGHEST
    lam_re = -jnp.exp(a_log_neg_re.astype(F32))
    lam_im = a_im.astype(F32)
    dt = jnp.exp(log_dt.astype(F32))[..., None]

    def powers(jj):
        e = jj[None, None, :, None]
        mag = jnp.exp((lam_re * dt)[:, :, None, :] * e)
        ang = (lam_im * dt)[:, :, None, :] * e
        return mag * jnp.cos(ang), mag * jnp.sin(ang)

    pw_re, pw_im = powers(jnp.arange(S5_CHUNK + 1, dtype=F32))
    num_re, num_im = pw_re[:, :, 1] - 1.0, pw_im[:, :, 1]
    den = lam_re * lam_re + lam_im * lam_im
    zoh_re = ((num_re * lam_re + num_im * lam_im) / den)[..., None]
    zoh_im = ((num_im * lam_re - num_re * lam_im) / den)[..., None]
    b_re, b_im = b_re.astype(F32), b_im.astype(F32)
    bb_re = zoh_re * b_re - zoh_im * b_im
    bb_im = zoh_re * b_im + zoh_im * b_re
    c_re, c_im = c_re.astype(F32)[:, :, None], c_im.astype(F32)[:, :, None]
    pr, pi = pw_re[:, :, :, None, :], pw_im[:, :, :, None, :]
    cp_re = c_re * pr - c_im * pi
    cp_im = c_re * pi + c_im * pr
    kj = (jnp.einsum('lgjpn,lgnq->lgjpq', cp_re[:, :, :S5_CHUNK], bb_re, precision=hp)
          - jnp.einsum('lgjpn,lgnq->lgjpq', cp_im[:, :, :S5_CHUNK], bb_im, precision=hp))
    qr, qi = powers(jnp.asarray(np.arange(S5_CHUNK - 1, -1, -1), F32))
    qr, qi = qr[..., None], qi[..., None]
    pb_re = (qr * bb_re[:, :, None] - qi * bb_im[:, :, None]).transpose(0, 1, 2, 4, 3)
    pb_im = (qr * bb_im[:, :, None] + qi * bb_re[:, :, None]).transpose(0, 1, 2, 4, 3)
    bst = jnp.concatenate([pb_re, pb_im], axis=-1)
    cst = jnp.concatenate([cp_re[:, :, 1:], -cp_im[:, :, 1:]], axis=-1)
    lp_re, lp_im = powers(S5_CHUNK * (2.0 ** jnp.arange(n_scan_lev, dtype=F32)))

    l, g = kj.shape[:2]
    v, sg = g // S5_SLAB, S5_SLAB
    cw = sg * S5_GROUP
    kt = kj.reshape(l, v, sg, S5_CHUNK, S5_GROUP, S5_GROUP)
    kt = kt.transpose(0, 1, 3, 5, 2, 4).reshape(l, v, S5_CHUNK, S5_GROUP, cw)
    pb = bst.reshape(l, v, sg, S5_CHUNK, S5_GROUP, 2 * S5_STATE)
    pb = pb.transpose(0, 1, 3, 2, 4, 5).reshape(l, v, S5_CHUNK, cw, 2 * S5_STATE)
    ct = cst.reshape(l, v, sg, S5_CHUNK, S5_GROUP, 2 * S5_STATE)
    ct = ct.transpose(0, 1, 3, 2, 4, 5).reshape(l, v, S5_CHUNK, cw, 2 * S5_STATE)

    def pair_lanes(a):
        a = a.reshape(l, v, sg // 2, 2, n_scan_lev, S5_STATE)
        return a.transpose(0, 1, 2, 4, 3, 5).reshape(l, v, sg // 2, n_scan_lev, 2 * S5_STATE)

    lam_big = jnp.stack([pair_lanes(lp_re), pair_lanes(lp_im)], axis=4)
    lam_big = lam_big.reshape(l, v, sg // 2, 2 * n_scan_lev, 2 * S5_STATE)
    return kt, pb, ct, lam_big


def _mix_kernel(*refs, n_src, tiles_a):
    x_refs = refs[:n_src]
    (gh_ref, gs_ref, oh_ref, y_ref, u_ref, d_ref, bglu_ref,
     wglu_ref, wbh_ref, wbs_ref, wout_ref, out_ref) = refs[n_src:]
    ys = jax.nn.gelu(y_ref[...] + d_ref[...] * u_ref[...])
    gate = jnp.dot(ys.astype(BF16), wglu_ref[...], preferred_element_type=F32) + bglu_ref[...]
    glu = (ys * jax.nn.sigmoid(gate)).astype(BF16)
    t_h = jnp.dot(oh_ref[...], wbh_ref[...], preferred_element_type=F32)
    t_s = jnp.dot(glu, wbs_ref[...], preferred_element_type=F32)
    mix = gh_ref[...].astype(F32) * t_h + gs_ref[...].astype(F32) * t_s
    x = x_refs[0][...]
    if n_src == 2:
        x = jnp.where(pl.program_id(0) < tiles_a, x, x_refs[1][...])
    out_ref[...] = x + jnp.dot(mix.astype(BF16), wout_ref[...], preferred_element_type=F32)


def _mix(xs, gates, o_h, y_s, main, u_col_block, d_skip, b_glu, w_glu, w_bh, w_bs, w_out, l, tm):
    t, ds = y_s.shape
    d = w_bh.shape[2]
    layer = lambda i: (l, 0, 0)
    const = lambda i: (0, 0)
    x_specs, tiles_a = _row_tile_specs(xs, tm)
    return pl.pallas_call(
        functools.partial(_mix_kernel, n_src=len(xs), tiles_a=tiles_a),
        grid=(t // tm,),
        in_specs=x_specs + [
            pl.BlockSpec((tm, d), lambda i: (i, 0)),
            pl.BlockSpec((tm, d), lambda i: (i, 1)),
            pl.BlockSpec((tm, ds), lambda i: (i, 0)),
            pl.BlockSpec((tm, ds), lambda i: (i, 0)),
            pl.BlockSpec((tm, ds), lambda i: (i, u_col_block)),
            pl.BlockSpec((1, ds), const),
            pl.BlockSpec((1, ds), const),
            _resident((None, ds, ds), layer),
            _resident((None, ds, d), layer),
            _resident((None, ds, d), layer),
            _resident((None, d, d), layer),
        ],
        out_specs=pl.BlockSpec((tm, d), lambda i: (i, 0)),
        out_shape=jax.ShapeDtypeStruct((t, d), F32),
        compiler_params=_cparams(("parallel",)),
        name="mix",
    )(*xs, gates, gates, o_h, y_s, main, d_skip.reshape(1, ds), b_glu.reshape(1, ds),
      w_glu, w_bh, w_bs, w_out)


def _rms(x, w):
    ms = jnp.mean(x * x, axis=-1, keepdims=True)
    return x * lax.rsqrt(ms + EPS) * w


def _ffn_kernel(x_ref, n2_ref, wg_ref, wu_ref, wd_ref, fn_ref, out_ref, h_scr, *, final_norm):
    j = pl.program_id(1)
    nj = pl.num_programs(1)

    @pl.when(j == 0)
    def _():
        xn = x_ref[...]
        out_ref[...] = xn
        h_scr[...] = _rms(xn, n2_ref[...]).astype(BF16)

    h = h_scr[...]
    ga = jnp.dot(h, wg_ref[...], preferred_element_type=F32)
    up = jnp.dot(h, wu_ref[...], preferred_element_type=F32)
    act = (jax.nn.silu(ga) * up).astype(BF16)
    out_ref[...] += jnp.dot(act, wd_ref[...], preferred_element_type=F32)

    if final_norm:
        @pl.when(j == nj - 1)
        def _():
            out_ref[...] = _rms(out_ref[...], fn_ref[...])


def _ffn(x, norm2, w_gate_up, w_down, l, fnorm, final_norm, tm, tf, row0=0, n_rows=None):
    d = x.shape[1]
    n_rows = x.shape[0] - row0 if n_rows is None else n_rows
    assert row0 % tm == 0 and n_rows % tm == 0
    tile0 = row0 // tm
    dff = w_down.shape[1]
    nj = dff // tf
    return pl.pallas_call(
        functools.partial(_ffn_kernel, final_norm=final_norm),
        grid=(n_rows // tm, nj),
        in_specs=[
            pl.BlockSpec((tm, d), lambda i, j: (tile0 + i, 0)),
            pl.BlockSpec((1, d), lambda i, j: (0, 0)),
            pl.BlockSpec((None, d, tf), lambda i, j: (l, 0, j)),
            pl.BlockSpec((None, d, tf), lambda i, j: (l, 0, nj + j)),
            pl.BlockSpec((None, tf, d), lambda i, j: (l, j, 0)),
            pl.BlockSpec((1, d), lambda i, j: (0, 0)),
        ],
        out_specs=pl.BlockSpec((tm, d), lambda i, j: (i, 0)),
        out_shape=jax.ShapeDtypeStruct((n_rows, d), F32),
        scratch_shapes=[pltpu.VMEM((tm, d), BF16)],
        compiler_params=_cparams(("parallel", "arbitrary")),
        name="ffn",
    )(x, norm2.reshape(1, d), w_gate_up, w_gate_up, w_down, fnorm.reshape(1, d))


def _pick(n, pref):
    t = pref
    while n % t:
        t //= 2
    return t


def _lower_bounds(lb_logits):
    p = jax.nn.softmax(lb_logits.astype(F32), axis=0)
    cs = jnp.cumsum(p, axis=0)
    return cs - cs[:1]


def kernel(x_prompt, x_sample, state_hgrn, state_s5_re, state_s5_im, lb_logits, norm1, w_in, hgrn_norm, w_bh, s5_a_log_neg_re, s5_a_im, s5_log_dt, s5_b_re, s5_b_im, s5_c_re, s5_c_im, s5_d, w_glu, b_glu, w_bs, w_out, norm2, w_gate_up, w_down, final_norm):
    depth = w_in.shape[0]
    bp, tp, d = x_prompt.shape
    bs, ts, _ = x_sample.shape
    assert bp == 1
    dh = lb_logits.shape[1]
    ds = s5_d.shape[1]
    n_groups = ds // S5_GROUP
    n_heads = dh // HEAD_DIM
    n_tok_p = bp * tp
    n_tok_s = bs * ts
    n_tok = n_tok_p + n_tok_s

    w_in_b = w_in.astype(BF16)
    w_bh_b = w_bh.astype(BF16)
    w_glu_b = w_glu.astype(BF16)
    w_bs_b = w_bs.astype(BF16)
    w_out_b = w_out.astype(BF16)
    w_gu_b = w_gate_up.astype(BF16)
    w_dn_b = w_down.astype(BF16)
    lbs = _lower_bounds(lb_logits)

    n_p = tp // S5_CHUNK
    n_cs = ts // S5_CHUNK
    s5_kt, s5_pb, s5_ct, s5_lam = _s5_tables(s5_a_log_neg_re, s5_a_im, s5_log_dt, s5_b_re,
                                             s5_b_im, s5_c_re, s5_c_im, int(math.log2(n_p)))
    n_slabs = n_groups // S5_SLAB
    n_pairs = S5_SLAB // 2

    def pack_state(re, im):
        nb = re.shape[0]
        x0 = jnp.stack([re.reshape(nb, n_slabs, n_pairs, 2 * S5_STATE),
                        im.reshape(nb, n_slabs, n_pairs, 2 * S5_STATE)], axis=3)
        return x0.transpose(1, 2, 3, 0, 4).reshape(n_slabs, S5_SLAB, nb, 2 * S5_STATE)

    def unpack_state(st):
        nb = st.shape[2]
        st = st.reshape(n_slabs, n_pairs, 2, nb, 2 * S5_STATE).transpose(2, 3, 0, 1, 4)
        st = st.reshape(2, nb, n_groups, S5_STATE)
        return st[0], st[1]

    tm_in = _pick(n_tok, 1024)
    tn_in = 1024
    tm_mix = _pick(n_tok, 256)
    tm_ffn = _pick(n_tok, 1024)
    tf = _pick(w_down.shape[1], 512)
    c_p = _pick(tp, 128)
    c_s = _pick(ts, 128)

    xs = (x_prompt.reshape(n_tok_p, d), x_sample.reshape(n_tok_s, d))
    if n_tok_p % tm_in or n_tok_s % tm_in or n_tok_p % tm_mix or n_tok_s % tm_mix:
        xs = (jnp.concatenate(xs, axis=0),)
    zero_h = jnp.zeros((1, bp, n_heads, HEAD_DIM, HEAD_DIM), F32)

    new_re_p, new_im_p, new_re_s, new_im_s = [], [], [], []
    hgrn_p = hgrn_s = None
    for l in range(depth):
        gates, main = _inproj(xs, norm1[l], w_in_b, l, 2 * d, tm_in, tn_in)

        o_h, hgrn_p = _hgrn(main, lbs[l], hgrn_norm[l], zero_h, None, hgrn_p,
                            layer=l, n_layers=depth, row_off=0, n_seq=bp, seq_len=tp, c=c_p,
                            nh=n_heads, hgrp=n_heads)
        o_h, hgrn_s = _hgrn(main, lbs[l], hgrn_norm[l], state_hgrn, o_h, hgrn_s,
                            layer=l, n_layers=depth, row_off=n_tok_p, n_seq=bs, seq_len=ts, c=c_s,
                            nh=n_heads, hgrp=n_heads)

        y, st_p, st_s = _s5(main, 4 * dh, s5_kt, s5_pb, s5_ct, s5_lam, l,
                            pack_state(state_s5_re[l], state_s5_im[l]),
                            n_p=n_p, n_b=bs, n_cs=n_cs)

        x = _mix(xs, gates, o_h, y, main, (4 * dh) // ds, s5_d[l], b_glu[l],
                 w_glu_b, w_bh_b, w_bs_b, w_out_b, l, tm_mix)
        last = l == depth - 1
        if last and n_tok_p % tm_ffn == 0 and n_tok_s % tm_ffn == 0:
            xs = (_ffn(x, norm2[l], w_gu_b, w_dn_b, l, final_norm, True, tm_ffn, tf, 0, n_tok_p),
                  _ffn(x, norm2[l], w_gu_b, w_dn_b, l, final_norm, True, tm_ffn, tf, n_tok_p,
                       n_tok_s))
        else:
            xs = (_ffn(x, norm2[l], w_gu_b, w_dn_b, l, final_norm, last, tm_ffn, tf),)

        re_p, im_p = unpack_state(st_p)
        re_s, im_s = unpack_state(st_s)
        new_re_p.append(re_p)
        new_im_p.append(im_p)
        new_re_s.append(re_s)
        new_im_s.append(im_s)

    if len(xs) == 1:
        xs = (xs[0][:n_tok_p], xs[0][n_tok_p:])
    y_prompt = xs[0].reshape(bp, tp, d)
    y_sample = xs[1].reshape(bs, ts, d)
    return (y_prompt, y_sample, hgrn_p, jnp.stack(new_re_p), jnp.stack(new_im_p),
            hgrn_s, jnp.stack(new_re_s), jnp.stack(new_im_s))
```

```python
import functools
import math

import jax
import jax.numpy as jnp
import numpy as np
from jax import lax
from jax.experimental import pallas as pl
from jax.experimental.pallas import tpu as pltpu

F32 = jnp.float32
BF16 = jnp.bfloat16

EPS = 1e-6
LOG2_E = 1.4426950408889634
HEAD_DIM = 128
S5_GROUP = 16
S5_STATE = 64
S5_CHUNK = 16
S5_SLAB = 8
S5_ROW_BLOCK = 272
MXU_TILE = 256
HGRN_STACK_ROWS = MXU_TILE
VMEM_LIMIT = 56 * 1024 * 1024
BIG_VMEM_LIMIT = 60 * 1024 * 1024


def _cparams(sem):
    return pltpu.CompilerParams(dimension_semantics=sem, vmem_limit_bytes=VMEM_LIMIT)


def _resident(shape, index_map):
    return pl.BlockSpec(shape, index_map, pipeline_mode=pl.Buffered(1))


def _row_tile_specs(xs, tm):
    d = xs[0].shape[1]
    tiles_a = xs[0].shape[0] // tm
    if len(xs) == 1:
        return [pl.BlockSpec((tm, d), lambda i, *_: (i, 0))], tiles_a
    assert len(xs) == 2 and xs[0].shape[0] % tm == 0 and xs[1].shape[0] % tm == 0
    return [
        pl.BlockSpec((tm, d), lambda i, *_: (jnp.minimum(i, tiles_a - 1), 0)),
        pl.BlockSpec((tm, d), lambda i, *_: (jnp.maximum(i - tiles_a, 0), 0),
                     pipeline_mode=pl.Buffered(1)),
    ], tiles_a


def _for_row_source(x_refs, tiles_a, pred, body):
    i = pl.program_id(0)
    if len(x_refs) == 1:
        conds = [pred]
    else:
        conds = [pred & (i < tiles_a), pred & (i >= tiles_a)]
    for cond, x_ref in zip(conds, x_refs):
        pl.when(cond)(functools.partial(body, x_ref))


def _inproj_kernel(*refs, n_src, tiles_a, n_main_tiles):
    x_refs = refs[:n_src]
    nw_ref, w_ref, gate_ref, main_ref, h_scr = refs[n_src:]
    j = pl.program_id(1)

    def normalize(x_ref):
        x = x_ref[...]
        ms = jnp.mean(x * x, axis=-1, keepdims=True)
        h_scr[...] = (x * lax.rsqrt(ms + EPS) * nw_ref[...]).astype(BF16)

    _for_row_source(x_refs, tiles_a, j == 0, normalize)

    @pl.when(j < n_main_tiles)
    def _():
        main_ref[...] = jnp.dot(h_scr[...], w_ref[...], preferred_element_type=F32)

    @pl.when(j >= n_main_tiles)
    def _():
        acc = jnp.dot(h_scr[...], w_ref[...], preferred_element_type=F32)
        gate_ref[...] = jax.nn.sigmoid(acc).astype(BF16)


def _inproj(xs, nw, w, l, n_gate_cols, tm, tn):
    t = sum(x.shape[0] for x in xs)
    d = xs[0].shape[1]
    n = w.shape[2]
    n_tiles = n // tn
    n_main_tiles = (n - n_gate_cols) // tn
    x_specs, tiles_a = _row_tile_specs(xs, tm)
    return pl.pallas_call(
        functools.partial(_inproj_kernel, n_src=len(xs), tiles_a=tiles_a,
                          n_main_tiles=n_main_tiles),
        grid=(t // tm, n_tiles),
        in_specs=x_specs + [
            pl.BlockSpec((1, d), lambda i, j: (0, 0)),
            pl.BlockSpec((None, d, tn), lambda i, j: (l, 0, j)),
        ],
        out_specs=[
            pl.BlockSpec((tm, tn), lambda i, j: (i, jnp.maximum(j - n_main_tiles, 0))),
            pl.BlockSpec((tm, tn), lambda i, j: (i, jnp.minimum(j, n_main_tiles - 1))),
        ],
        out_shape=[
            jax.ShapeDtypeStruct((t, n_gate_cols), BF16),
            jax.ShapeDtypeStruct((t, n - n_gate_cols), F32),
        ],
        scratch_shapes=[pltpu.VMEM((tm, d), BF16)],
        compiler_params=_cparams(("parallel", "arbitrary")),
        name="inproj",
    )(*xs, nw.reshape(1, d), w)


def _hgrn_level_masks(c):
    n_lev = int(math.log2(c)) + 1
    gsz = max(1, min(n_lev, HGRN_STACK_ROWS // c))
    n_groups = -(-n_lev // gsz)
    r = gsz * c
    t = np.arange(c)[:, None]
    s = np.arange(c)[None, :]
    masks = np.zeros((n_groups, r, r), np.float32)
    for lev in range(n_lev):
        if lev == 0:
            m = (t == s)
        else:
            h = 1 << (lev - 1)
            m = ((t & h) != 0) & ((s & h) == 0) & ((t // (2 * h)) == (s // (2 * h)))
        g, i = divmod(lev, gsz)
        masks[g, i * c:(i + 1) * c, i * c:(i + 1) * c] = m
    return masks, n_lev, gsz, n_groups


def _hgrn_kernel(q_ref, f_ref, i_ref, g_ref, lb_ref, gain_ref, s0_ref, mask_ref, *rest,
                 c, nh, hgrp, n_lev, gsz, n_groups):
    o_ref, s_ref, st_scr = rest[-3:]
    ci = pl.program_id(2)
    nc = pl.num_programs(2)

    @pl.when(ci == 0)
    def _():
        st_scr[...] = s0_ref[0]

    row = lax.broadcasted_iota(jnp.int32, (c, HEAD_DIM), 0)
    nt = (((1,), (1,)), ((), ()))
    tn = (((0,), (0,)), ((), ()))

    def head_operands(h):
        sl = slice(h * HEAD_DIM, (h + 1) * HEAD_DIM)
        z = f_ref[:, sl]
        lb = lb_ref[:, sl]
        ez = jnp.exp(-jnp.abs(z))
        log_sig = jnp.minimum(z, 0.0) - jnp.log(1.0 + ez)
        a0 = jnp.log(lb)
        a1 = jnp.log1p(-lb) + log_sig
        logf = jnp.maximum(a0, a1) + jnp.log(1.0 + jnp.exp(-jnp.abs(a0 - a1)))
        k = (1.0 - lb) * (jnp.where(z >= 0.0, ez, 1.0) / (1.0 + ez))
        q = q_ref[:, sl]
        v = jax.nn.silu(i_ref[:, sl])
        v_bf = v.astype(BF16)

        p = logf * LOG2_E
        tb = p
        a_lev = [q.astype(BF16)]
        b_lev = [k.astype(BF16)]
        for lev in range(1, n_lev):
            hs = 1 << (lev - 1)
            right = (row & hs) != 0
            e = jnp.exp2(jnp.where(right, p, tb - p))
            zl = (jnp.where(right, q, k) * e).astype(BF16)
            a_lev.append(zl)
            b_lev.append(zl)
            dn = pltpu.roll(tb, hs, 0)
            up = pltpu.roll(tb, c - hs, 0)
            p = p + jnp.where(right, dn, 0.0)
            tb = tb + jnp.where(right, dn, up)
        ops = []
        for g in range(n_groups):
            lo, hi = g * gsz, min((g + 1) * gsz, n_lev)
            n_in = hi - lo
            a_g = jnp.concatenate(a_lev[lo:hi], axis=0) if n_in > 1 else a_lev[lo]
            b_g = a_g if lo > 0 else (
                jnp.concatenate(b_lev[lo:hi], axis=0) if n_in > 1 else b_lev[lo])
            ops.append((a_g, b_g, n_in))
        q_in = (q * jnp.exp2(p)).astype(BF16)
        k_out = (k * jnp.exp2(tb - p)).astype(BF16)
        decay = jnp.exp2(jnp.broadcast_to(tb[0:1, :], (HEAD_DIM, HEAD_DIM)).T)
        return ops, v_bf, q_in, k_out, decay

    def fold_scores(g, sc, n_in):
        if c % HEAD_DIM == 0:
            return jnp.concatenate(
                [sc[i * c:(i + 1) * c, i * c:(i + 1) * c].astype(BF16)
                 * mask_ref[g, i * c:(i + 1) * c, i * c:(i + 1) * c]
                 for i in range(n_in)], axis=1)
        r = n_in * c
        sc = sc * mask_ref[g][:r, :r]
        fold = sc[0:c]
        for i in range(1, n_in):
            fold = fold + sc[i * c:(i + 1) * c]
        return fold.astype(BF16)

    for h0 in range(0, nh, hgrp):
        heads = range(h0, min(h0 + hgrp, nh))
        prep = {h: head_operands(h) for h in heads}
        scores = {h: [lax.dot_general(a_g, b_g, nt, preferred_element_type=F32)
                      for a_g, b_g, _ in prep[h][0]] for h in heads}
        upd = {h: lax.dot_general(prep[h][3], prep[h][1], tn, preferred_element_type=F32)
               for h in heads}
        for h in heads:
            ops, v_bf, q_in, _, decay = prep[h]
            st = st_scr[h]
            lhs = [q_in]
            rhs = [st.astype(BF16)]
            for g, (sc, (_, _, n_in)) in enumerate(zip(scores[h], ops)):
                lhs.append(fold_scores(g, sc, n_in))
                rhs.extend([v_bf] * n_in)
            o = jnp.dot(jnp.concatenate(lhs, axis=1), jnp.concatenate(rhs, axis=0),
                        preferred_element_type=F32)
            st_scr[h] = st * decay + upd[h]
            sl = slice(h * HEAD_DIM, (h + 1) * HEAD_DIM)
            ms = jnp.mean(o * o, axis=-1, keepdims=True)
            o = o * lax.rsqrt(ms + EPS) * gain_ref[:, sl] * jax.nn.silu(g_ref[:, sl])
            o_ref[:, sl] = o.astype(o_ref.dtype)

    @pl.when(ci == nc - 1)
    def _():
        s_ref[0] = st_scr[...]


def _hgrn(main, lb, gain, s0, o_all, s_all, *, layer, n_layers, row_off, n_seq, seq_len, c, nh,
          hgrp):
    n_tok = main.shape[0]
    dh = lb.shape[-1]
    n_heads = dh // HEAD_DIM
    w = nh * HEAD_DIM
    n_hg = n_heads // nh
    n_chunks = seq_len // c
    rb0 = row_off // c
    masks, n_lev, gsz, n_groups = _hgrn_level_masks(c)
    r = masks.shape[-1]

    def col_spec(group):
        return pl.BlockSpec(
            (c, w), lambda b, hg, ci: (rb0 + b * n_chunks + ci, group * n_hg + hg))

    vec_spec = pl.BlockSpec((1, w), lambda b, hg, ci: (0, hg))
    s0_layer = min(layer, s0.shape[0] - 1)
    st_spec = pl.BlockSpec((None, 1, nh, HEAD_DIM, HEAD_DIM),
                           lambda b, hg, ci: (s0_layer, b, hg, 0, 0))
    operands = [main, main, main, main, lb.reshape(1, dh), gain.reshape(1, dh), s0,
                jnp.asarray(masks, BF16 if c % HEAD_DIM == 0 else F32)]
    in_specs = [col_spec(0), col_spec(1), col_spec(2), col_spec(3), vec_spec, vec_spec,
                st_spec, _resident((n_groups, r, r), lambda b, hg, ci: (0, 0, 0))]
    aliases = {}
    for out_idx, buf in enumerate((o_all, s_all)):
        if buf is not None:
            aliases[len(operands)] = out_idx
            operands.append(buf)
            in_specs.append(pl.BlockSpec(memory_space=pl.ANY))
    o, s_new = pl.pallas_call(
        functools.partial(_hgrn_kernel, c=c, nh=nh, hgrp=hgrp, n_lev=n_lev, gsz=gsz,
                          n_groups=n_groups),
        grid=(n_seq, n_hg, n_chunks),
        in_specs=in_specs,
        out_specs=[
            pl.BlockSpec((c, w), lambda b, hg, ci: (rb0 + b * n_chunks + ci, hg)),
            pl.BlockSpec((None, 1, nh, HEAD_DIM, HEAD_DIM),
                         lambda b, hg, ci: (layer, b, hg, 0, 0)),
        ],
        out_shape=[
            jax.ShapeDtypeStruct((n_tok, dh), BF16),
            jax.ShapeDtypeStruct((n_layers, n_seq, n_heads, HEAD_DIM, HEAD_DIM), F32),
        ],
        input_output_aliases=aliases,
        scratch_shapes=[pltpu.VMEM((nh, HEAD_DIM, HEAD_DIM), F32)],
        compiler_params=_cparams(("parallel", "parallel", "arbitrary")),
        name="hgrn_c%d" % c,
    )(*operands)
    return o, s_new


def _cmul(xr, xi, lr, li):
    return xr * lr - xi * li, xr * li + xi * lr


def _s5_kernel(u_ref, kt_ref, pb_ref, ct_ref, lam_ref, x0_ref, y_ref, sp_ref, ss_ref,
               m_scr, b_scr, ct_scr, z_scr, *, n_p, n_b, n_cs, rb):
    n_rows = n_p + n_b * n_cs
    sw = 2 * S5_STATE
    cw = S5_SLAB * S5_GROUP

    gi = lax.broadcasted_iota(jnp.int32, (cw, cw), 0) // S5_GROUP
    gj = lax.broadcasted_iota(jnp.int32, (cw, cw), 1) // S5_GROUP
    same_group = gi == gj
    hw = cw // 2
    hk = S5_CHUNK * hw
    hs = (S5_SLAB // 2) * sw
    low_lanes = lax.broadcasted_iota(jnp.int32, (1, cw), 1) < hw

    def split_halves(a, b):
        return (jnp.where(low_lanes, a, pltpu.roll(b, hw, 1)),
                jnp.where(low_lanes, pltpu.roll(a, hw, 1), b))

    lag_tiles = [
        jnp.where(same_group, jnp.concatenate([kt_ref[0, j]] * S5_SLAB, axis=0), 0.0)
        for j in range(S5_CHUNK)]
    zero_tile = jnp.zeros((cw, cw), F32)
    tpb = MXU_TILE // hw
    for t in range(0, S5_CHUNK, 2):
        for s in range((t // tpb + 1) * tpb):
            pair = split_halves(lag_tiles[t - s] if t >= s else zero_tile,
                                lag_tiles[t + 1 - s] if t + 1 >= s else zero_tile)
            for h in range(2):
                m_scr[h, s * hw:(s + 1) * hw, t * hw:(t + 2) * hw] = (
                    pair[h][h * hw:(h + 1) * hw, :].astype(BF16))
    rg = lax.broadcasted_iota(jnp.int32, (cw, S5_SLAB * sw), 0) // S5_GROUP
    lane = lax.broadcasted_iota(jnp.int32, (cw, S5_SLAB * sw), 1)
    plane, half = lane // sw, (lane % sw) // S5_STATE
    own_state = (plane // 2 == rg // 2) & (half == rg % 2)
    straight = plane % 2 == rg % 2

    def expand(tab):
        tiled = jnp.concatenate([tab] * S5_SLAB, axis=1)
        swapped = jnp.concatenate([pltpu.roll(tab, S5_STATE, 1)] * S5_SLAB, axis=1)
        return jnp.where(own_state, jnp.where(straight, tiled, swapped), 0.0).astype(BF16)

    for s in range(S5_CHUNK):
        eb, ec = expand(pb_ref[0, s]), expand(ct_ref[0, s])
        for h in range(2):
            b_scr[h, s * hw:(s + 1) * hw, :] = eb[h * hw:(h + 1) * hw, h * hs:(h + 1) * hs]
            ct_scr[h, s * hw:(s + 1) * hw, :] = ec[h * hw:(h + 1) * hw, h * hs:(h + 1) * hs]

    def chunk_operands(r0):
        tiles = [u_ref[pl.ds(S5_CHUNK * r0 + s, rb, stride=S5_CHUNK), :]
                 for s in range(S5_CHUNK)]
        pairs = [split_halves(tiles[s], tiles[s + 1]) for s in range(0, S5_CHUNK, 2)]
        return [jnp.concatenate([p[h] for p in pairs], axis=1).astype(BF16) for h in range(2)]

    planes_per_half = S5_SLAB // 2
    for r0 in range(0, n_rows, rb):
        for h, a in enumerate(chunk_operands(r0)):
            z = jnp.dot(a, b_scr[h], preferred_element_type=F32)
            for g in range(planes_per_half):
                z_scr[h * planes_per_half + g, r0:r0 + rb, :] = z[:, g * sw:(g + 1) * sw]

    rowi = lax.broadcasted_iota(jnp.int32, (n_p, sw), 0)

    def shift_rows(x, sh):
        return jnp.where(rowi >= sh, pltpu.roll(x, sh, 0), 0.0)

    def scan_pair(k, carry):
        pr, pi = 2 * k, 2 * k + 1
        lam_k = lam_ref[0, k]
        xr = z_scr[pr, 0:n_p, :]
        xi = z_scr[pi, 0:n_p, :]
        for lev in range(int(math.log2(n_p))):
            sh = 1 << lev
            lr, li = lam_k[2 * lev:2 * lev + 1], lam_k[2 * lev + 1:2 * lev + 2]
            if sh % 8 == 0 and sh < n_p:
                dr, di = _cmul(xr[:n_p - sh], xi[:n_p - sh], lr, li)
                xr = jnp.concatenate([xr[:sh], xr[sh:] + dr], axis=0)
                xi = jnp.concatenate([xi[:sh], xi[sh:] + di], axis=0)
            else:
                dr, di = _cmul(shift_rows(xr, sh), shift_rows(xi, sh), lr, li)
                xr, xi = xr + dr, xi + di
        z_scr[pr, 0:n_p, :] = shift_rows(xr, 1)
        z_scr[pi, 0:n_p, :] = shift_rows(xi, 1)
        sp_ref[0, pr] = xr[n_p - 1:n_p]
        sp_ref[0, pi] = xi[n_p - 1:n_p]

        sr, si = x0_ref[0, pr], x0_ref[0, pi]
        for ci in range(n_cs):
            rows = pl.ds(n_p + ci, n_b, stride=n_cs)
            zr, zi = z_scr[pr, rows, :], z_scr[pi, rows, :]
            z_scr[pr, rows, :] = sr
            z_scr[pi, rows, :] = si
            dr, di = _cmul(sr, si, lam_k[0:1], lam_k[1:2])
            sr, si = dr + zr, di + zi
        ss_ref[0, pr] = sr
        ss_ref[0, pi] = si
        return carry

    for k in range(S5_SLAB // 2):
        scan_pair(k, 0)

    for r0 in range(0, n_rows, rb):
        ops = chunk_operands(r0)
        x_prev = [jnp.concatenate(
            [z_scr[h * planes_per_half + g, r0:r0 + rb, :].astype(BF16)
             for g in range(planes_per_half)], axis=1) for h in range(2)]
        for j in range(S5_CHUNK // tpb):
            cols = slice(j * MXU_TILE, (j + 1) * MXU_TILE)
            y = [jnp.dot(ops[h][:, :(j + 1) * MXU_TILE], m_scr[h, 0:(j + 1) * MXU_TILE, cols],
                         preferred_element_type=F32)
                 + lax.dot_general(x_prev[h], ct_scr[h, cols, :], (((1,), (1,)), ((), ())),
                                   preferred_element_type=F32)
                 for h in range(2)]
            for ti in range(0, tpb, 2):
                lo = slice(ti * hw, (ti + 2) * hw)
                out_even, out_odd = split_halves(y[0][:, lo], y[1][:, lo])
                for dt, out in ((0, out_even), (1, out_odd)):
                    t = j * tpb + ti + dt
                    y_ref[pl.ds(S5_CHUNK * r0 + t, rb, stride=S5_CHUNK), :] = out


def _s5(main, u_col0, kt, pb, ct, lam, l, x0, *, n_p, n_b, n_cs):
    n_slabs = kt.shape[1]
    n_tok = main.shape[0]
    n_rows = n_p + n_b * n_cs
    assert n_rows * S5_CHUNK == n_tok
    n_lam = lam.shape[3]
    cw = S5_SLAB * S5_GROUP
    kw = S5_CHUNK * cw
    gw = 2 * S5_STATE
    sw = S5_SLAB * gw
    rb = max(r for r in range(16, S5_ROW_BLOCK + 1, 16) if n_rows % r == 0)
    cb0 = u_col0 // cw

    def per_slab(shape):
        nd = len(shape)
        return pl.BlockSpec((1,) + shape, lambda v: (v,) + (0,) * nd)

    def table(shape):
        nd = len(shape)
        return pl.BlockSpec((None, 1) + shape, lambda v: (l, v) + (0,) * nd)

    return pl.pallas_call(
        functools.partial(_s5_kernel, n_p=n_p, n_b=n_b, n_cs=n_cs, rb=rb),
        grid=(n_slabs,),
        in_specs=[pl.BlockSpec((n_tok, cw), lambda v: (0, cb0 + v)),
                  table((S5_CHUNK, S5_GROUP, cw)),
                  table((S5_CHUNK, cw, 2 * S5_STATE)),
                  table((S5_CHUNK, cw, 2 * S5_STATE)),
                  table((S5_SLAB // 2, n_lam, gw)), per_slab((S5_SLAB, n_b, gw))],
        out_specs=[_resident((n_tok, cw), lambda v: (0, v)),
                   per_slab((S5_SLAB, 1, gw)), per_slab((S5_SLAB, n_b, gw))],
        out_shape=[
            jax.ShapeDtypeStruct((n_tok, n_slabs * cw), F32),
            jax.ShapeDtypeStruct((n_slabs, S5_SLAB, 1, gw), F32),
            jax.ShapeDtypeStruct((n_slabs, S5_SLAB, n_b, gw), F32),
        ],
        scratch_shapes=[pltpu.VMEM((2, kw // 2, kw // 2), BF16),
                        pltpu.VMEM((2, kw // 2, sw // 2), BF16),
                        pltpu.VMEM((2, kw // 2, sw // 2), BF16),
                        pltpu.VMEM((S5_SLAB, n_rows, 2 * S5_STATE), F32)],
        compiler_params=pltpu.CompilerParams(dimension_semantics=("parallel",),
                                             vmem_limit_bytes=BIG_VMEM_LIMIT),
        name="s5",
    )(main, kt, pb, ct, lam, x0)


def _s5_tables(a_log_neg_re, a_im, log_dt, b_re, b_im, c_re, c_im, n_scan_lev):
    hp = lax.Precision.HIGHEST
    lam_re = -jnp.exp(a_log_neg_re.astype(F32))
    lam_im = a_im.astype(F32)
    dt = jnp.exp(log_dt.astype(F32))[..., None]

    def powers(jj):
        e = jj[None, None, :, None]
        mag = jnp.exp((lam_re * dt)[:, :, None, :] * e)
        ang = (lam_im * dt)[:, :, None, :] * e
        return mag * jnp.cos(ang), mag * jnp.sin(ang)

    pw_re, pw_im = powers(jnp.arange(S5_CHUNK + 1, dtype=F32))
    num_re, num_im = pw_re[:, :, 1] - 1.0, pw_im[:, :, 1]
    den = lam_re * lam_re + lam_im * lam_im
    zoh_re = ((num_re * lam_re + num_im * lam_im) / den)[..., None]
    zoh_im = ((num_im * lam_re - num_re * lam_im) / den)[..., None]
    b_re, b_im = b_re.astype(F32), b_im.astype(F32)
    bb_re = zoh_re * b_re - zoh_im * b_im
    bb_im = zoh_re * b_im + zoh_im * b_re
    c_re, c_im = c_re.astype(F32)[:, :, None], c_im.astype(F32)[:, :, None]
    pr, pi = pw_re[:, :, :, None, :], pw_im[:, :, :, None, :]
    cp_re = c_re * pr - c_im * pi
    cp_im = c_re * pi + c_im * pr
    kj = (jnp.einsum('lgjpn,lgnq->lgjpq', cp_re[:, :, :S5_CHUNK], bb_re, precision=hp)
          - jnp.einsum('lgjpn,lgnq->lgjpq', cp_im[:, :, :S5_CHUNK], bb_im, precision=hp))
    qr, qi = powers(jnp.asarray(np.arange(S5_CHUNK - 1, -1, -1), F32))
    qr, qi = qr[:, :, :, None, :], qi[:, :, :, None, :]
    bt_re = bb_re.transpose(0, 1, 3, 2)[:, :, None]
    bt_im = bb_im.transpose(0, 1, 3, 2)[:, :, None]
    bst = jnp.concatenate([qr * bt_re - qi * bt_im, qr * bt_im + qi * bt_re],
                          axis=-1)
    cst = jnp.concatenate([cp_re[:, :, 1:], -cp_im[:, :, 1:]], axis=-1)
    lp_re, lp_im = powers(S5_CHUNK * (2.0 ** jnp.arange(n_scan_lev, dtype=F32)))

    l, g = kj.shape[:2]
    v, sg = g // S5_SLAB, S5_SLAB
    cw = sg * S5_GROUP
    kt = kj.reshape(l, v, sg, S5_CHUNK, S5_GROUP, S5_GROUP)
    kt = kt.transpose(0, 1, 3, 5, 2, 4).reshape(l, v, S5_CHUNK, S5_GROUP, cw)
    pb = bst.reshape(l, v, sg, S5_CHUNK, S5_GROUP, 2 * S5_STATE)
    pb = pb.transpose(0, 1, 3, 2, 4, 5).reshape(l, v, S5_CHUNK, cw, 2 * S5_STATE)
    ct = cst.reshape(l, v, sg, S5_CHUNK, S5_GROUP, 2 * S5_STATE)
    ct = ct.transpose(0, 1, 3, 2, 4, 5).reshape(l, v, S5_CHUNK, cw, 2 * S5_STATE)

    def pair_lanes(a):
        a = a.reshape(l, v, sg // 2, 2, n_scan_lev, S5_STATE)
        return a.transpose(0, 1, 2, 4, 3, 5).reshape(l, v, sg // 2, n_scan_lev, 2 * S5_STATE)

    lam_big = jnp.stack([pair_lanes(lp_re), pair_lanes(lp_im)], axis=4)
    lam_big = lam_big.reshape(l, v, sg // 2, 2 * n_scan_lev, 2 * S5_STATE)
    return kt, pb, ct, lam_big


def _mix_kernel(*refs, n_src, tiles_a):
    x_refs = refs[:n_src]
    (gh_ref, gs_ref, oh_ref, y_ref, u_ref, d_ref, bglu_ref,
     wglu_ref, wbh_ref, wbs_ref, wout_ref, out_ref) = refs[n_src:]
    ys = jax.nn.gelu(y_ref[...] + d_ref[...] * u_ref[...])
    gate = jnp.dot(ys.astype(BF16), wglu_ref[...], preferred_element_type=F32) + bglu_ref[...]
    glu = (ys * jax.nn.sigmoid(gate)).astype(BF16)
    t_h = jnp.dot(oh_ref[...], wbh_ref[...], preferred_element_type=F32)
    t_s = jnp.dot(glu, wbs_ref[...], preferred_element_type=F32)
    mix = gh_ref[...].astype(F32) * t_h + gs_ref[...].astype(F32) * t_s
    x = x_refs[0][...]
    if n_src == 2:
        x = jnp.where(pl.program_id(0) < tiles_a, x, x_refs[1][...])
    out_ref[...] = x + jnp.dot(mix.astype(BF16), wout_ref[...], preferred_element_type=F32)


def _mix(xs, gates, o_h, y_s, main, u_col_block, d_skip, b_glu, w_glu, w_bh, w_bs, w_out, l, tm):
    t, ds = y_s.shape
    d = w_bh.shape[2]
    layer = lambda i: (l, 0, 0)
    const = lambda i: (0, 0)
    x_specs, tiles_a = _row_tile_specs(xs, tm)
    return pl.pallas_call(
        functools.partial(_mix_kernel, n_src=len(xs), tiles_a=tiles_a),
        grid=(t // tm,),
        in_specs=x_specs + [
            pl.BlockSpec((tm, d), lambda i: (i, 0)),
            pl.BlockSpec((tm, d), lambda i: (i, 1)),
            pl.BlockSpec((tm, ds), lambda i: (i, 0)),
            pl.BlockSpec((tm, ds), lambda i: (i, 0)),
            pl.BlockSpec((tm, ds), lambda i: (i, u_col_block)),
            pl.BlockSpec((1, ds), const),
            pl.BlockSpec((1, ds), const),
            _resident((None, ds, ds), layer),
            _resident((None, ds, d), layer),
            _resident((None, ds, d), layer),
            _resident((None, d, d), layer),
        ],
        out_specs=pl.BlockSpec((tm, d), lambda i: (i, 0)),
        out_shape=jax.ShapeDtypeStruct((t, d), F32),
        compiler_params=_cparams(("parallel",)),
        name="mix",
    )(*xs, gates, gates, o_h, y_s, main, d_skip.reshape(1, ds), b_glu.reshape(1, ds),
      w_glu, w_bh, w_bs, w_out)


def _rms(x, w):
    ms = jnp.mean(x * x, axis=-1, keepdims=True)
    return x * lax.rsqrt(ms + EPS) * w


def _ffn_kernel(x_ref, n2_ref, wg_ref, wu_ref, wd_ref, fn_ref, out_ref, h_scr, *, final_norm):
    j = pl.program_id(1)
    nj = pl.num_programs(1)

    @pl.when(j == 0)
    def _():
        xn = x_ref[...]
        out_ref[...] = xn
        h_scr[...] = _rms(xn, n2_ref[...]).astype(BF16)

    h = h_scr[...]
    ga = jnp.dot(h, wg_ref[...], preferred_element_type=F32)
    up = jnp.dot(h, wu_ref[...], preferred_element_type=F32)
    act = (jax.nn.silu(ga) * up).astype(BF16)
    out_ref[...] += jnp.dot(act, wd_ref[...], preferred_element_type=F32)

    if final_norm:
        @pl.when(j == nj - 1)
        def _():
            out_ref[...] = _rms(out_ref[...], fn_ref[...])


def _ffn(x, norm2, w_gate_up, w_down, l, fnorm, final_norm, tm, tf, row0=0, n_rows=None):
    d = x.shape[1]
    n_rows = x.shape[0] - row0 if n_rows is None else n_rows
    assert row0 % tm == 0 and n_rows % tm == 0
    tile0 = row0 // tm
    dff = w_down.shape[1]
    nj = dff // tf
    return pl.pallas_call(
        functools.partial(_ffn_kernel, final_norm=final_norm),
        grid=(n_rows // tm, nj),
        in_specs=[
            pl.BlockSpec((tm, d), lambda i, j: (tile0 + i, 0)),
            pl.BlockSpec((1, d), lambda i, j: (0, 0)),
            pl.BlockSpec((None, d, tf), lambda i, j: (l, 0, j)),
            pl.BlockSpec((None, d, tf), lambda i, j: (l, 0, nj + j)),
            pl.BlockSpec((None, tf, d), lambda i, j: (l, j, 0)),
            pl.BlockSpec((1, d), lambda i, j: (0, 0)),
        ],
        out_specs=pl.BlockSpec((tm, d), lambda i, j: (i, 0)),
        out_shape=jax.ShapeDtypeStruct((n_rows, d), F32),
        scratch_shapes=[pltpu.VMEM((tm, d), BF16)],
        compiler_params=_cparams(("parallel", "arbitrary")),
        name="ffn",
    )(x, norm2.reshape(1, d), w_gate_up, w_gate_up, w_down, fnorm.reshape(1, d))


def _pick(n, pref):
    t = pref
    while n % t:
        t //= 2
    return t


def _lower_bounds(lb_logits):
    p = jax.nn.softmax(lb_logits.astype(F32), axis=0)
    cs = jnp.cumsum(p, axis=0)
    return cs - cs[:1]


def kernel(x_prompt, x_sample, state_hgrn, state_s5_re, state_s5_im, lb_logits, norm1, w_in, hgrn_norm, w_bh, s5_a_log_neg_re, s5_a_im, s5_log_dt, s5_b_re, s5_b_im, s5_c_re, s5_c_im, s5_d, w_glu, b_glu, w_bs, w_out, norm2, w_gate_up, w_down, final_norm):
    depth = w_in.shape[0]
    bp, tp, d = x_prompt.shape
    bs, ts, _ = x_sample.shape
    assert bp == 1
    dh = lb_logits.shape[1]
    ds = s5_d.shape[1]
    n_groups = ds // S5_GROUP
    n_heads = dh // HEAD_DIM
    n_tok_p = bp * tp
    n_tok_s = bs * ts
    n_tok = n_tok_p + n_tok_s

    w_in_b = w_in.astype(BF16)
    w_bh_b = w_bh.astype(BF16)
    w_glu_b = w_glu.astype(BF16)
    w_bs_b = w_bs.astype(BF16)
    w_out_b = w_out.astype(BF16)
    w_gu_b = w_gate_up.astype(BF16)
    w_dn_b = w_down.astype(BF16)
    lbs = _lower_bounds(lb_logits)

    n_p = tp // S5_CHUNK
    n_cs = ts // S5_CHUNK
    s5_kt, s5_pb, s5_ct, s5_lam = _s5_tables(s5_a_log_neg_re, s5_a_im, s5_log_dt, s5_b_re,
                                             s5_b_im, s5_c_re, s5_c_im, int(math.log2(n_p)))
    n_slabs = n_groups // S5_SLAB
    n_pairs = S5_SLAB // 2

    def pack_state(re, im):
        nb = re.shape[0]
        x0 = jnp.stack([re.reshape(nb, n_slabs, n_pairs, 2 * S5_STATE),
                        im.reshape(nb, n_slabs, n_pairs, 2 * S5_STATE)], axis=3)
        return x0.transpose(1, 2, 3, 0, 4).reshape(n_slabs, S5_SLAB, nb, 2 * S5_STATE)

    def unpack_state(st):
        nb = st.shape[2]
        st = st.reshape(n_slabs, n_pairs, 2, nb, 2 * S5_STATE).transpose(2, 3, 0, 1, 4)
        st = st.reshape(2, nb, n_groups, S5_STATE)
        return st[0], st[1]

    tm_in = _pick(n_tok, 1024)
    tn_in = 1024
    tm_mix = _pick(n_tok, 256)
    tm_ffn = _pick(n_tok, 1024)
    tf = _pick(w_down.shape[1], 512)
    c_p = _pick(tp, 128)
    c_s = _pick(ts, 128)

    xs = (x_prompt.reshape(n_tok_p, d), x_sample.reshape(n_tok_s, d))
    if n_tok_p % tm_in or n_tok_s % tm_in or n_tok_p % tm_mix or n_tok_s % tm_mix:
        xs = (jnp.concatenate(xs, axis=0),)
    zero_h = jnp.zeros((1, bp, n_heads, HEAD_DIM, HEAD_DIM), F32)

    new_re_p, new_im_p, new_re_s, new_im_s = [], [], [], []
    hgrn_p = hgrn_s = None
    for l in range(depth):
        gates, main = _inproj(xs, norm1[l], w_in_b, l, 2 * d, tm_in, tn_in)

        o_h, hgrn_p = _hgrn(main, lbs[l], hgrn_norm[l], zero_h, None, hgrn_p,
                            layer=l, n_layers=depth, row_off=0, n_seq=bp, seq_len=tp, c=c_p,
                            nh=n_heads, hgrp=n_heads)
        o_h, hgrn_s = _hgrn(main, lbs[l], hgrn_norm[l], state_hgrn, o_h, hgrn_s,
                            layer=l, n_layers=depth, row_off=n_tok_p, n_seq=bs, seq_len=ts, c=c_s,
                            nh=n_heads, hgrp=n_heads)

        y, st_p, st_s = _s5(main, 4 * dh, s5_kt, s5_pb, s5_ct, s5_lam, l,
                            pack_state(state_s5_re[l], state_s5_im[l]),
                            n_p=n_p, n_b=bs, n_cs=n_cs)

        x = _mix(xs, gates, o_h, y, main, (4 * dh) // ds, s5_d[l], b_glu[l],
                 w_glu_b, w_bh_b, w_bs_b, w_out_b, l, tm_mix)
        last = l == depth - 1
        if last and n_tok_p % tm_ffn == 0 and n_tok_s % tm_ffn == 0:
            xs = (_ffn(x, norm2[l], w_gu_b, w_dn_b, l, final_norm, True, tm_ffn, tf, 0, n_tok_p),
                  _ffn(x, norm2[l], w_gu_b, w_dn_b, l, final_norm, True, tm_ffn, tf, n_tok_p,
                       n_tok_s))
        else:
            xs = (_ffn(x, norm2[l], w_gu_b, w_dn_b, l, final_norm, last, tm_ffn, tf),)

        re_p, im_p = unpack_state(st_p)
        re_s, im_s = unpack_state(st_s)
        new_re_p.append(re_p)
        new_im_p.append(im_p)
        new_re_s.append(re_s)
        new_im_s.append(im_s)

    if len(xs) == 1:
        xs = (xs[0][:n_tok_p], xs[0][n_tok_p:])
    y_prompt = xs[0].reshape(bp, tp, d)
    y_sample = xs[1].reshape(bs, ts, d)
    return (y_prompt, y_sample, hgrn_p, jnp.stack(new_re_p), jnp.stack(new_im_p),
            hgrn_s, jnp.stack(new_re_s), jnp.stack(new_im_s))
```

```python
import functools
import math

import jax
import jax.numpy as jnp
import numpy as np
from jax import lax
from jax.experimental import pallas as pl
from jax.experimental.pallas import tpu as pltpu

F32 = jnp.float32
BF16 = jnp.bfloat16

EPS = 1e-6
LOG2_E = 1.4426950408889634
HEAD_DIM = 128
S5_GROUP = 16
S5_STATE = 64
S5_CHUNK = 16
S5_SLAB = 8
S5_ROW_BLOCK = 272
MXU_TILE = 256
HGRN_STACK_ROWS = MXU_TILE
VMEM_LIMIT = 56 * 1024 * 1024
BIG_VMEM_LIMIT = 60 * 1024 * 1024


def _cparams(sem):
    return pltpu.CompilerParams(dimension_semantics=sem, vmem_limit_bytes=VMEM_LIMIT)


def _resident(shape, index_map):
    return pl.BlockSpec(shape, index_map, pipeline_mode=pl.Buffered(1))


def _row_tile_specs(xs, tm):
    d = xs[0].shape[1]
    tiles_a = xs[0].shape[0] // tm
    if len(xs) == 1:
        return [pl.BlockSpec((tm, d), lambda i, *_: (i, 0))], tiles_a
    assert len(xs) == 2 and xs[0].shape[0] % tm == 0 and xs[1].shape[0] % tm == 0
    return [
        pl.BlockSpec((tm, d), lambda i, *_: (jnp.minimum(i, tiles_a - 1), 0)),
        pl.BlockSpec((tm, d), lambda i, *_: (jnp.maximum(i - tiles_a, 0), 0),
                     pipeline_mode=pl.Buffered(1)),
    ], tiles_a


def _for_row_source(x_refs, tiles_a, pred, body):
    i = pl.program_id(0)
    if len(x_refs) == 1:
        conds = [pred]
    else:
        conds = [pred & (i < tiles_a), pred & (i >= tiles_a)]
    for cond, x_ref in zip(conds, x_refs):
        pl.when(cond)(functools.partial(body, x_ref))


def _inproj_kernel(*refs, n_src, tiles_a, n_main_tiles):
    x_refs = refs[:n_src]
    nw_ref, w_ref, gate_ref, main_ref, h_scr = refs[n_src:]
    j = pl.program_id(1)

    def first_tile(x_ref):
        x = x_ref[...]
        ms = jnp.mean(x * x, axis=-1, keepdims=True)
        h = (x * lax.rsqrt(ms + EPS) * nw_ref[...]).astype(BF16)
        h_scr[...] = h
        main_ref[...] = jnp.dot(h, w_ref[...], preferred_element_type=F32)

    _for_row_source(x_refs, tiles_a, j == 0, first_tile)

    @pl.when((j > 0) & (j < n_main_tiles))
    def _():
        main_ref[...] = jnp.dot(h_scr[...], w_ref[...], preferred_element_type=F32)

    @pl.when(j >= n_main_tiles)
    def _():
        acc = jnp.dot(h_scr[...], w_ref[...], preferred_element_type=F32)
        gate_ref[...] = jax.nn.sigmoid(acc).astype(BF16)


def _inproj(xs, nw, w, l, n_gate_cols, tm, tn):
    t = sum(x.shape[0] for x in xs)
    d = xs[0].shape[1]
    n = w.shape[2]
    n_tiles = n // tn
    n_main_tiles = (n - n_gate_cols) // tn
    assert n_main_tiles >= 1
    x_specs, tiles_a = _row_tile_specs(xs, tm)
    return pl.pallas_call(
        functools.partial(_inproj_kernel, n_src=len(xs), tiles_a=tiles_a,
                          n_main_tiles=n_main_tiles),
        grid=(t // tm, n_tiles),
        in_specs=x_specs + [
            pl.BlockSpec((1, d), lambda i, j: (0, 0)),
            pl.BlockSpec((None, d, tn), lambda i, j: (l, 0, j)),
        ],
        out_specs=[
            pl.BlockSpec((tm, tn), lambda i, j: (i, jnp.maximum(j - n_main_tiles, 0))),
            pl.BlockSpec((tm, tn), lambda i, j: (i, jnp.minimum(j, n_main_tiles - 1))),
        ],
        out_shape=[
            jax.ShapeDtypeStruct((t, n_gate_cols), BF16),
            jax.ShapeDtypeStruct((t, n - n_gate_cols), F32),
        ],
        scratch_shapes=[pltpu.VMEM((tm, d), BF16)],
        compiler_params=_cparams(("parallel", "arbitrary")),
        name="inproj",
    )(*xs, nw.reshape(1, d), w)


def _hgrn_level_masks(c):
    n_lev = int(math.log2(c)) + 1
    gsz = max(1, min(n_lev, HGRN_STACK_ROWS // c))
    n_groups = -(-n_lev // gsz)
    r = gsz * c
    t = np.arange(c)[:, None]
    s = np.arange(c)[None, :]
    masks = np.zeros((n_groups, r, r), np.float32)
    for lev in range(n_lev):
        if lev == 0:
            m = (t == s)
        else:
            h = 1 << (lev - 1)
            m = ((t & h) != 0) & ((s & h) == 0) & ((t // (2 * h)) == (s // (2 * h)))
        g, i = divmod(lev, gsz)
        masks[g, i * c:(i + 1) * c, i * c:(i + 1) * c] = m
    return masks, n_lev, gsz, n_groups


def _hgrn_kernel(q_ref, f_ref, i_ref, g_ref, lb_ref, gain_ref, s0_ref, mask_ref, *rest,
                 c, nh, hgrp, n_lev, gsz, n_groups):
    o_ref, s_ref, st_scr = rest[-3:]
    ci = pl.program_id(2)
    nc = pl.num_programs(2)

    @pl.when(ci == 0)
    def _():
        st_scr[...] = s0_ref[0]

    row = lax.broadcasted_iota(jnp.int32, (c, HEAD_DIM), 0)
    nt = (((1,), (1,)), ((), ()))
    tn = (((0,), (0,)), ((), ()))

    def head_operands(h):
        sl = slice(h * HEAD_DIM, (h + 1) * HEAD_DIM)
        z = f_ref[:, sl]
        lb = lb_ref[:, sl]
        ez = jnp.exp(-jnp.abs(z))
        log_sig = jnp.minimum(z, 0.0) - jnp.log(1.0 + ez)
        a0 = jnp.log(lb)
        a1 = jnp.log1p(-lb) + log_sig
        logf = jnp.maximum(a0, a1) + jnp.log(1.0 + jnp.exp(-jnp.abs(a0 - a1)))
        k = (1.0 - lb) * (jnp.where(z >= 0.0, ez, 1.0) / (1.0 + ez))
        q = q_ref[:, sl]
        v = jax.nn.silu(i_ref[:, sl])
        v_bf = v.astype(BF16)

        p = logf * LOG2_E
        tb = p
        a_lev = [q.astype(BF16)]
        b_lev = [k.astype(BF16)]
        for lev in range(1, n_lev):
            hs = 1 << (lev - 1)
            right = (row & hs) != 0
            e = jnp.exp2(jnp.where(right, p, tb - p))
            zl = (jnp.where(right, q, k) * e).astype(BF16)
            a_lev.append(zl)
            b_lev.append(zl)
            dn = pltpu.roll(tb, hs, 0)
            up = pltpu.roll(tb, c - hs, 0)
            p = p + jnp.where(right, dn, 0.0)
            tb = tb + jnp.where(right, dn, up)
        ops = []
        for g in range(n_groups):
            lo, hi = g * gsz, min((g + 1) * gsz, n_lev)
            n_in = hi - lo
            a_g = jnp.concatenate(a_lev[lo:hi], axis=0) if n_in > 1 else a_lev[lo]
            b_g = a_g if lo > 0 else (
                jnp.concatenate(b_lev[lo:hi], axis=0) if n_in > 1 else b_lev[lo])
            ops.append((a_g, b_g, n_in))
        q_in = (q * jnp.exp2(p)).astype(BF16)
        k_out = (k * jnp.exp2(tb - p)).astype(BF16)
        decay = jnp.exp2(jnp.broadcast_to(tb[0:1, :], (HEAD_DIM, HEAD_DIM)).T)
        return ops, v_bf, q_in, k_out, decay

    def fold_scores(g, sc, n_in):
        if c % HEAD_DIM == 0:
            return jnp.concatenate(
                [sc[i * c:(i + 1) * c, i * c:(i + 1) * c].astype(BF16)
                 * mask_ref[g, i * c:(i + 1) * c, i * c:(i + 1) * c]
                 for i in range(n_in)], axis=1)
        r = n_in * c
        sc = sc * mask_ref[g][:r, :r]
        fold = sc[0:c]
        for i in range(1, n_in):
            fold = fold + sc[i * c:(i + 1) * c]
        return fold.astype(BF16)

    for h0 in range(0, nh, hgrp):
        heads = range(h0, min(h0 + hgrp, nh))
        prep = {h: head_operands(h) for h in heads}
        scores = {h: [lax.dot_general(a_g, b_g, nt, preferred_element_type=F32)
                      for a_g, b_g, _ in prep[h][0]] for h in heads}
        upd = {h: lax.dot_general(prep[h][3], prep[h][1], tn, preferred_element_type=F32)
               for h in heads}
        for h in heads:
            ops, v_bf, q_in, _, decay = prep[h]
            st = st_scr[h]
            lhs = [q_in]
            rhs = [st.astype(BF16)]
            for g, (sc, (_, _, n_in)) in enumerate(zip(scores[h], ops)):
                lhs.append(fold_scores(g, sc, n_in))
                rhs.extend([v_bf] * n_in)
            o = jnp.dot(jnp.concatenate(lhs, axis=1), jnp.concatenate(rhs, axis=0),
                        preferred_element_type=F32)
            st_scr[h] = st * decay + upd[h]
            sl = slice(h * HEAD_DIM, (h + 1) * HEAD_DIM)
            ms = jnp.mean(o * o, axis=-1, keepdims=True)
            o = o * lax.rsqrt(ms + EPS) * gain_ref[:, sl] * jax.nn.silu(g_ref[:, sl])
            o_ref[:, sl] = o.astype(o_ref.dtype)

    @pl.when(ci == nc - 1)
    def _():
        s_ref[0] = st_scr[...]


def _hgrn(main, lb, gain, s0, o_all, s_all, *, layer, n_layers, row_off, n_seq, seq_len, c, nh,
          hgrp):
    n_tok = main.shape[0]
    dh = lb.shape[-1]
    n_heads = dh // HEAD_DIM
    w = nh * HEAD_DIM
    n_hg = n_heads // nh
    n_chunks = seq_len // c
    rb0 = row_off // c
    masks, n_lev, gsz, n_groups = _hgrn_level_masks(c)
    r = masks.shape[-1]

    def col_spec(group):
        return pl.BlockSpec(
            (c, w), lambda b, hg, ci: (rb0 + b * n_chunks + ci, group * n_hg + hg))

    vec_spec = pl.BlockSpec((1, w), lambda b, hg, ci: (0, hg))
    s0_layer = min(layer, s0.shape[0] - 1)
    st_spec = pl.BlockSpec((None, 1, nh, HEAD_DIM, HEAD_DIM),
                           lambda b, hg, ci: (s0_layer, b, hg, 0, 0))
    operands = [main, main, main, main, lb.reshape(1, dh), gain.reshape(1, dh), s0,
                jnp.asarray(masks, BF16 if c % HEAD_DIM == 0 else F32)]
    in_specs = [col_spec(0), col_spec(1), col_spec(2), col_spec(3), vec_spec, vec_spec,
                st_spec, _resident((n_groups, r, r), lambda b, hg, ci: (0, 0, 0))]
    aliases = {}
    for out_idx, buf in enumerate((o_all, s_all)):
        if buf is not None:
            aliases[len(operands)] = out_idx
            operands.append(buf)
            in_specs.append(pl.BlockSpec(memory_space=pl.ANY))
    o, s_new = pl.pallas_call(
        functools.partial(_hgrn_kernel, c=c, nh=nh, hgrp=hgrp, n_lev=n_lev, gsz=gsz,
                          n_groups=n_groups),
        grid=(n_seq, n_hg, n_chunks),
        in_specs=in_specs,
        out_specs=[
            pl.BlockSpec((c, w), lambda b, hg, ci: (rb0 + b * n_chunks + ci, hg)),
            pl.BlockSpec((None, 1, nh, HEAD_DIM, HEAD_DIM),
                         lambda b, hg, ci: (layer, b, hg, 0, 0)),
        ],
        out_shape=[
            jax.ShapeDtypeStruct((n_tok, dh), BF16),
            jax.ShapeDtypeStruct((n_layers, n_seq, n_heads, HEAD_DIM, HEAD_DIM), F32),
        ],
        input_output_aliases=aliases,
        scratch_shapes=[pltpu.VMEM((nh, HEAD_DIM, HEAD_DIM), F32)],
        compiler_params=_cparams(("parallel", "parallel", "arbitrary")),
        name="hgrn_c%d" % c,
    )(*operands)
    return o, s_new


def _cmul(xr, xi, lr, li):
    return xr * lr - xi * li, xr * li + xi * lr


def _s5_kernel(u_ref, kt_ref, pb_ref, ct_ref, lam_ref, x0_ref, y_ref, sp_ref, ss_ref,
               m_scr, b_scr, ct_scr, z_scr, *, n_p, n_b, n_cs, rb):
    n_rows = n_p + n_b * n_cs
    sw = 2 * S5_STATE
    cw = S5_SLAB * S5_GROUP

    gi = lax.broadcasted_iota(jnp.int32, (cw, cw), 0) // S5_GROUP
    gj = lax.broadcasted_iota(jnp.int32, (cw, cw), 1) // S5_GROUP
    same_group = gi == gj
    hw = cw // 2
    hk = S5_CHUNK * hw
    hs = (S5_SLAB // 2) * sw
    low_lanes = lax.broadcasted_iota(jnp.int32, (1, cw), 1) < hw

    def split_halves(a, b):
        return (jnp.where(low_lanes, a, pltpu.roll(b, hw, 1)),
                jnp.where(low_lanes, pltpu.roll(a, hw, 1), b))

    lag_tiles = [
        jnp.where(same_group, jnp.concatenate([kt_ref[0, j]] * S5_SLAB, axis=0), 0.0)
        for j in range(S5_CHUNK)]
    zero_tile = jnp.zeros((cw, cw), F32)
    tpb = MXU_TILE // hw
    for t in range(0, S5_CHUNK, 2):
        for s in range((t // tpb + 1) * tpb):
            pair = split_halves(lag_tiles[t - s] if t >= s else zero_tile,
                                lag_tiles[t + 1 - s] if t + 1 >= s else zero_tile)
            for h in range(2):
                m_scr[h, s * hw:(s + 1) * hw, t * hw:(t + 2) * hw] = (
                    pair[h][h * hw:(h + 1) * hw, :].astype(BF16))
    rg = lax.broadcasted_iota(jnp.int32, (cw, S5_SLAB * sw), 0) // S5_GROUP
    lane = lax.broadcasted_iota(jnp.int32, (cw, S5_SLAB * sw), 1)
    plane, half = lane // sw, (lane % sw) // S5_STATE
    own_state = (plane // 2 == rg // 2) & (half == rg % 2)
    straight = plane % 2 == rg % 2

    def expand(tab):
        tiled = jnp.concatenate([tab] * S5_SLAB, axis=1)
        swapped = jnp.concatenate([pltpu.roll(tab, S5_STATE, 1)] * S5_SLAB, axis=1)
        return jnp.where(own_state, jnp.where(straight, tiled, swapped), 0.0).astype(BF16)

    for s in range(S5_CHUNK):
        eb, ec = expand(pb_ref[0, s]), expand(ct_ref[0, s])
        for h in range(2):
            b_scr[h, s * hw:(s + 1) * hw, :] = eb[h * hw:(h + 1) * hw, h * hs:(h + 1) * hs]
            ct_scr[h, s * hw:(s + 1) * hw, :] = ec[h * hw:(h + 1) * hw, h * hs:(h + 1) * hs]

    def chunk_operands(r0):
        tiles = [u_ref[pl.ds(S5_CHUNK * r0 + s, rb, stride=S5_CHUNK), :]
                 for s in range(S5_CHUNK)]
        pairs = [split_halves(tiles[s], tiles[s + 1]) for s in range(0, S5_CHUNK, 2)]
        return [jnp.concatenate([p[h] for p in pairs], axis=1).astype(BF16) for h in range(2)]

    planes_per_half = S5_SLAB // 2
    for r0 in range(0, n_rows, rb):
        for h, a in enumerate(chunk_operands(r0)):
            z = jnp.dot(a, b_scr[h], preferred_element_type=F32)
            for g in range(planes_per_half):
                z_scr[h * planes_per_half + g, r0:r0 + rb, :] = z[:, g * sw:(g + 1) * sw]

    rowi = lax.broadcasted_iota(jnp.int32, (n_p, sw), 0)

    def shift_rows(x, sh):
        return jnp.where(rowi >= sh, pltpu.roll(x, sh, 0), 0.0)

    def scan_pair(k, carry):
        pr, pi = 2 * k, 2 * k + 1
        lam_k = lam_ref[0, k]
        xr = z_scr[pr, 0:n_p, :]
        xi = z_scr[pi, 0:n_p, :]
        for lev in range(int(math.log2(n_p))):
            sh = 1 << lev
            lr, li = lam_k[2 * lev:2 * lev + 1], lam_k[2 * lev + 1:2 * lev + 2]
            if sh % 8 == 0 and sh < n_p:
                dr, di = _cmul(xr[:n_p - sh], xi[:n_p - sh], lr, li)
                xr = jnp.concatenate([xr[:sh], xr[sh:] + dr], axis=0)
                xi = jnp.concatenate([xi[:sh], xi[sh:] + di], axis=0)
            else:
                dr, di = _cmul(shift_rows(xr, sh), shift_rows(xi, sh), lr, li)
                xr, xi = xr + dr, xi + di
        z_scr[pr, 0:n_p, :] = shift_rows(xr, 1)
        z_scr[pi, 0:n_p, :] = shift_rows(xi, 1)
        sp_ref[0, pr] = xr[n_p - 1:n_p]
        sp_ref[0, pi] = xi[n_p - 1:n_p]

        sr, si = x0_ref[0, pr], x0_ref[0, pi]
        for ci in range(n_cs):
            rows = pl.ds(n_p + ci, n_b, stride=n_cs)
            zr, zi = z_scr[pr, rows, :], z_scr[pi, rows, :]
            z_scr[pr, rows, :] = sr
            z_scr[pi, rows, :] = si
            dr, di = _cmul(sr, si, lam_k[0:1], lam_k[1:2])
            sr, si = dr + zr, di + zi
        ss_ref[0, pr] = sr
        ss_ref[0, pi] = si
        return carry

    for k in range(S5_SLAB // 2):
        scan_pair(k, 0)

    for r0 in range(0, n_rows, rb):
        ops = chunk_operands(r0)
        x_prev = [jnp.concatenate(
            [z_scr[h * planes_per_half + g, r0:r0 + rb, :].astype(BF16)
             for g in range(planes_per_half)], axis=1) for h in range(2)]
        for j in range(S5_CHUNK // tpb):
            cols = slice(j * MXU_TILE, (j + 1) * MXU_TILE)
            y = [jnp.dot(ops[h][:, :(j + 1) * MXU_TILE], m_scr[h, 0:(j + 1) * MXU_TILE, cols],
                         preferred_element_type=F32)
                 + lax.dot_general(x_prev[h], ct_scr[h, cols, :], (((1,), (1,)), ((), ())),
                                   preferred_element_type=F32)
                 for h in range(2)]
            for ti in range(0, tpb, 2):
                lo = slice(ti * hw, (ti + 2) * hw)
                out_even, out_odd = split_halves(y[0][:, lo], y[1][:, lo])
                for dt, out in ((0, out_even), (1, out_odd)):
                    t = j * tpb + ti + dt
                    y_ref[pl.ds(S5_CHUNK * r0 + t, rb, stride=S5_CHUNK), :] = out


def _s5(main, u_col0, kt, pb, ct, lam, l, x0, *, n_p, n_b, n_cs):
    n_slabs = kt.shape[1]
    n_tok = main.shape[0]
    n_rows = n_p + n_b * n_cs
    assert n_rows * S5_CHUNK == n_tok
    n_lam = lam.shape[3]
    cw = S5_SLAB * S5_GROUP
    kw = S5_CHUNK * cw
    gw = 2 * S5_STATE
    sw = S5_SLAB * gw
    rb = max(r for r in range(16, S5_ROW_BLOCK + 1, 16) if n_rows % r == 0)
    cb0 = u_col0 // cw

    def per_slab(shape):
        nd = len(shape)
        return pl.BlockSpec((1,) + shape, lambda v: (v,) + (0,) * nd)

    def table(shape):
        nd = len(shape)
        return pl.BlockSpec((None, 1) + shape, lambda v: (l, v) + (0,) * nd)

    return pl.pallas_call(
        functools.partial(_s5_kernel, n_p=n_p, n_b=n_b, n_cs=n_cs, rb=rb),
        grid=(n_slabs,),
        in_specs=[pl.BlockSpec((n_tok, cw), lambda v: (0, cb0 + v)),
                  table((S5_CHUNK, S5_GROUP, cw)),
                  table((S5_CHUNK, cw, 2 * S5_STATE)),
                  table((S5_CHUNK, cw, 2 * S5_STATE)),
                  table((S5_SLAB // 2, n_lam, gw)), per_slab((S5_SLAB, n_b, gw))],
        out_specs=[_resident((n_tok, cw), lambda v: (0, v)),
                   per_slab((S5_SLAB, 1, gw)), per_slab((S5_SLAB, n_b, gw))],
        out_shape=[
            jax.ShapeDtypeStruct((n_tok, n_slabs * cw), F32),
            jax.ShapeDtypeStruct((n_slabs, S5_SLAB, 1, gw), F32),
            jax.ShapeDtypeStruct((n_slabs, S5_SLAB, n_b, gw), F32),
        ],
        scratch_shapes=[pltpu.VMEM((2, kw // 2, kw // 2), BF16),
                        pltpu.VMEM((2, kw // 2, sw // 2), BF16),
                        pltpu.VMEM((2, kw // 2, sw // 2), BF16),
                        pltpu.VMEM((S5_SLAB, n_rows, 2 * S5_STATE), F32)],
        compiler_params=pltpu.CompilerParams(dimension_semantics=("parallel",),
                                             vmem_limit_bytes=BIG_VMEM_LIMIT),
        name="s5",
    )(main, kt, pb, ct, lam, x0)


def _s5_tables(a_log_neg_re, a_im, log_dt, b_re, b_im, c_re, c_im, n_scan_lev):
    hp = lax.Precision.HIGHEST
    lam_re = -jnp.exp(a_log_neg_re.astype(F32))
    lam_im = a_im.astype(F32)
    dt = jnp.exp(log_dt.astype(F32))[..., None]

    def powers(jj):
        e = jj[None, None, :, None]
        mag = jnp.exp((lam_re * dt)[:, :, None, :] * e)
        ang = (lam_im * dt)[:, :, None, :] * e
        return mag * jnp.cos(ang), mag * jnp.sin(ang)

    pw_re, pw_im = powers(jnp.arange(S5_CHUNK + 1, dtype=F32))
    num_re, num_im = pw_re[:, :, 1] - 1.0, pw_im[:, :, 1]
    den = lam_re * lam_re + lam_im * lam_im
    zoh_re = ((num_re * lam_re + num_im * lam_im) / den)[..., None]
    zoh_im = ((num_im * lam_re - num_re * lam_im) / den)[..., None]
    b_re, b_im = b_re.astype(F32), b_im.astype(F32)
    bb_re = zoh_re * b_re - zoh_im * b_im
    bb_im = zoh_re * b_im + zoh_im * b_re
    c_re, c_im = c_re.astype(F32)[:, :, None], c_im.astype(F32)[:, :, None]
    pr, pi = pw_re[:, :, :, None, :], pw_im[:, :, :, None, :]
    cp_re = c_re * pr - c_im * pi
    cp_im = c_re * pi + c_im * pr
    kj = (jnp.einsum('lgjpn,lgnq->lgjpq', cp_re[:, :, :S5_CHUNK], bb_re, precision=hp)
          - jnp.einsum('lgjpn,lgnq->lgjpq', cp_im[:, :, :S5_CHUNK], bb_im, precision=hp))
    qr, qi = powers(jnp.asarray(np.arange(S5_CHUNK - 1, -1, -1), F32))
    qr, qi = qr[:, :, :, None, :], qi[:, :, :, None, :]
    bt_re = bb_re.transpose(0, 1, 3, 2)[:, :, None]
    bt_im = bb_im.transpose(0, 1, 3, 2)[:, :, None]
    bst = jnp.concatenate([qr * bt_re - qi * bt_im, qr * bt_im + qi * bt_re],
                          axis=-1)
    cst = jnp.concatenate([cp_re[:, :, 1:], -cp_im[:, :, 1:]], axis=-1)
    lp_re, lp_im = powers(S5_CHUNK * (2.0 ** jnp.arange(n_scan_lev, dtype=F32)))

    l, g = kj.shape[:2]
    v, sg = g // S5_SLAB, S5_SLAB
    cw = sg * S5_GROUP
    kt = kj.reshape(l, v, sg, S5_CHUNK, S5_GROUP, S5_GROUP)
    kt = kt.transpose(0, 1, 3, 5, 2, 4).reshape(l, v, S5_CHUNK, S5_GROUP, cw)
    pb = bst.reshape(l, v, sg, S5_CHUNK, S5_GROUP, 2 * S5_STATE)
    pb = pb.transpose(0, 1, 3, 2, 4, 5).reshape(l, v, S5_CHUNK, cw, 2 * S5_STATE)
    ct = cst.reshape(l, v, sg, S5_CHUNK, S5_GROUP, 2 * S5_STATE)
    ct = ct.transpose(0, 1, 3, 2, 4, 5).reshape(l, v, S5_CHUNK, cw, 2 * S5_STATE)

    def pair_lanes(a):
        a = a.reshape(l, v, sg // 2, 2, n_scan_lev, S5_STATE)
        return a.transpose(0, 1, 2, 4, 3, 5).reshape(l, v, sg // 2, n_scan_lev, 2 * S5_STATE)

    lam_big = jnp.stack([pair_lanes(lp_re), pair_lanes(lp_im)], axis=4)
    lam_big = lam_big.reshape(l, v, sg // 2, 2 * n_scan_lev, 2 * S5_STATE)
    return kt, pb, ct, lam_big


def _mix_kernel(*refs, n_src, tiles_a):
    x_refs = refs[:n_src]
    (gh_ref, gs_ref, oh_ref, y_ref, u_ref, d_ref, bglu_ref,
     wglu_ref, wbh_ref, wbs_ref, wout_ref, out_ref) = refs[n_src:]
    ys = jax.nn.gelu(y_ref[...] + d_ref[...] * u_ref[...])
    gate = jnp.dot(ys.astype(BF16), wglu_ref[...], preferred_element_type=F32) + bglu_ref[...]
    glu = (ys * jax.nn.sigmoid(gate)).astype(BF16)
    t_h = jnp.dot(oh_ref[...], wbh_ref[...], preferred_element_type=F32)
    t_s = jnp.dot(glu, wbs_ref[...], preferred_element_type=F32)
    mix = gh_ref[...].astype(F32) * t_h + gs_ref[...].astype(F32) * t_s
    x = x_refs[0][...]
    if n_src == 2:
        x = jnp.where(pl.program_id(0) < tiles_a, x, x_refs[1][...])
    out_ref[...] = x + jnp.dot(mix.astype(BF16), wout_ref[...], preferred_element_type=F32)


def _mix(xs, gates, o_h, y_s, main, u_col_block, d_skip, b_glu, w_glu, w_bh, w_bs, w_out, l, tm):
    t, ds = y_s.shape
    d = w_bh.shape[2]
    layer = lambda i: (l, 0, 0)
    const = lambda i: (0, 0)
    x_specs, tiles_a = _row_tile_specs(xs, tm)
    return pl.pallas_call(
        functools.partial(_mix_kernel, n_src=len(xs), tiles_a=tiles_a),
        grid=(t // tm,),
        in_specs=x_specs + [
            pl.BlockSpec((tm, d), lambda i: (i, 0)),
            pl.BlockSpec((tm, d), lambda i: (i, 1)),
            pl.BlockSpec((tm, ds), lambda i: (i, 0)),
            pl.BlockSpec((tm, ds), lambda i: (i, 0)),
            pl.BlockSpec((tm, ds), lambda i: (i, u_col_block)),
            pl.BlockSpec((1, ds), const),
            pl.BlockSpec((1, ds), const),
            _resident((None, ds, ds), layer),
            _resident((None, ds, d), layer),
            _resident((None, ds, d), layer),
            _resident((None, d, d), layer),
        ],
        out_specs=pl.BlockSpec((tm, d), lambda i: (i, 0)),
        out_shape=jax.ShapeDtypeStruct((t, d), F32),
        compiler_params=_cparams(("parallel",)),
        name="mix",
    )(*xs, gates, gates, o_h, y_s, main, d_skip.reshape(1, ds), b_glu.reshape(1, ds),
      w_glu, w_bh, w_bs, w_out)


def _rms(x, w):
    ms = jnp.mean(x * x, axis=-1, keepdims=True)
    return x * lax.rsqrt(ms + EPS) * w


def _ffn_kernel(x_ref, n2_ref, wg_ref, wu_ref, wd_ref, fn_ref, out_ref, h_scr, *, final_norm):
    j = pl.program_id(1)
    nj = pl.num_programs(1)

    def step(first):
        if first:
            h = _rms(x_ref[...], n2_ref[...]).astype(BF16)
            h_scr[...] = h
        else:
            h = h_scr[...]
        ga = jnp.dot(h, wg_ref[...], preferred_element_type=F32)
        up = jnp.dot(h, wu_ref[...], preferred_element_type=F32)
        act = (jax.nn.silu(ga) * up).astype(BF16)
        contrib = jnp.dot(act, wd_ref[...], preferred_element_type=F32)
        out_ref[...] = (x_ref[...] if first else out_ref[...]) + contrib

    pl.when(j == 0)(functools.partial(step, True))
    pl.when(j > 0)(functools.partial(step, False))

    if final_norm:
        @pl.when(j == nj - 1)
        def _():
            out_ref[...] = _rms(out_ref[...], fn_ref[...])


def _ffn(x, norm2, w_gate_up, w_down, l, fnorm, final_norm, tm, tf, row0=0, n_rows=None):
    d = x.shape[1]
    n_rows = x.shape[0] - row0 if n_rows is None else n_rows
    assert row0 % tm == 0 and n_rows % tm == 0
    tile0 = row0 // tm
    dff = w_down.shape[1]
    nj = dff // tf
    return pl.pallas_call(
        functools.partial(_ffn_kernel, final_norm=final_norm),
        grid=(n_rows // tm, nj),
        in_specs=[
            pl.BlockSpec((tm, d), lambda i, j: (tile0 + i, 0)),
            pl.BlockSpec((1, d), lambda i, j: (0, 0)),
            pl.BlockSpec((None, d, tf), lambda i, j: (l, 0, j)),
            pl.BlockSpec((None, d, tf), lambda i, j: (l, 0, nj + j)),
            pl.BlockSpec((None, tf, d), lambda i, j: (l, j, 0)),
            pl.BlockSpec((1, d), lambda i, j: (0, 0)),
        ],
        out_specs=pl.BlockSpec((tm, d), lambda i, j: (i, 0)),
        out_shape=jax.ShapeDtypeStruct((n_rows, d), F32),
        scratch_shapes=[pltpu.VMEM((tm, d), BF16)],
        compiler_params=_cparams(("parallel", "arbitrary")),
        name="ffn",
    )(x, norm2.reshape(1, d), w_gate_up, w_gate_up, w_down, fnorm.reshape(1, d))


def _pick(n, pref):
    t = pref
    while n % t:
        t //= 2
    return t


def _lower_bounds(lb_logits):
    p = jax.nn.softmax(lb_logits.astype(F32), axis=0)
    cs = jnp.cumsum(p, axis=0)
    return cs - cs[:1]


def kernel(x_prompt, x_sample, state_hgrn, state_s5_re, state_s5_im, lb_logits, norm1, w_in, hgrn_norm, w_bh, s5_a_log_neg_re, s5_a_im, s5_log_dt, s5_b_re, s5_b_im, s5_c_re, s5_c_im, s5_d, w_glu, b_glu, w_bs, w_out, norm2, w_gate_up, w_down, final_norm):
    depth = w_in.shape[0]
    bp, tp, d = x_prompt.shape
    bs, ts, _ = x_sample.shape
    assert bp == 1
    dh = lb_logits.shape[1]
    ds = s5_d.shape[1]
    n_groups = ds // S5_GROUP
    n_heads = dh // HEAD_DIM
    n_tok_p = bp * tp
    n_tok_s = bs * ts
    n_tok = n_tok_p + n_tok_s

    w_in_b = w_in.astype(BF16)
    w_bh_b = w_bh.astype(BF16)
    w_glu_b = w_glu.astype(BF16)
    w_bs_b = w_bs.astype(BF16)
    w_out_b = w_out.astype(BF16)
    w_gu_b = w_gate_up.astype(BF16)
    w_dn_b = w_down.astype(BF16)
    lbs = _lower_bounds(lb_logits)

    n_p = tp // S5_CHUNK
    n_cs = ts // S5_CHUNK
    s5_kt, s5_pb, s5_ct, s5_lam = _s5_tables(s5_a_log_neg_re, s5_a_im, s5_log_dt, s5_b_re,
                                             s5_b_im, s5_c_re, s5_c_im, int(math.log2(n_p)))
    n_slabs = n_groups // S5_SLAB
    n_pairs = S5_SLAB // 2

    def pack_state(re, im):
        nb = re.shape[0]
        x0 = jnp.stack([re.reshape(nb, n_slabs, n_pairs, 2 * S5_STATE),
                        im.reshape(nb, n_slabs, n_pairs, 2 * S5_STATE)], axis=3)
        return x0.transpose(1, 2, 3, 0, 4).reshape(n_slabs, S5_SLAB, nb, 2 * S5_STATE)

    def unpack_state(st):
        nb = st.shape[2]
        st = st.reshape(n_slabs, n_pairs, 2, nb, 2 * S5_STATE).transpose(2, 3, 0, 1, 4)
        st = st.reshape(2, nb, n_groups, S5_STATE)
        return st[0], st[1]

    tm_in = _pick(n_tok, 1024)
    tn_in = 1024
    tm_mix = _pick(n_tok, 256)
    tm_ffn = _pick(n_tok, 1024)
    tf = _pick(w_down.shape[1], 512)
    c_p = _pick(tp, 128)
    c_s = _pick(ts, 128)

    xs = (x_prompt.reshape(n_tok_p, d), x_sample.reshape(n_tok_s, d))
    if n_tok_p % tm_in or n_tok_s % tm_in or n_tok_p % tm_mix or n_tok_s % tm_mix:
        xs = (jnp.concatenate(xs, axis=0),)
    zero_h = jnp.zeros((1, bp, n_heads, HEAD_DIM, HEAD_DIM), F32)

    new_re_p, new_im_p, new_re_s, new_im_s = [], [], [], []
    hgrn_p = hgrn_s = None
    for l in range(depth):
        gates, main = _inproj(xs, norm1[l], w_in_b, l, 2 * d, tm_in, tn_in)

        o_h, hgrn_p = _hgrn(main, lbs[l], hgrn_norm[l], zero_h, None, hgrn_p,
                            layer=l, n_layers=depth, row_off=0, n_seq=bp, seq_len=tp, c=c_p,
                            nh=n_heads, hgrp=n_heads)
        o_h, hgrn_s = _hgrn(main, lbs[l], hgrn_norm[l], state_hgrn, o_h, hgrn_s,
                            layer=l, n_layers=depth, row_off=n_tok_p, n_seq=bs, seq_len=ts, c=c_s,
                            nh=n_heads, hgrp=n_heads)

        y, st_p, st_s = _s5(main, 4 * dh, s5_kt, s5_pb, s5_ct, s5_lam, l,
                            pack_state(state_s5_re[l], state_s5_im[l]),
                            n_p=n_p, n_b=bs, n_cs=n_cs)

        x = _mix(xs, gates, o_h, y, main, (4 * dh) // ds, s5_d[l], b_glu[l],
                 w_glu_b, w_bh_b, w_bs_b, w_out_b, l, tm_mix)
        last = l == depth - 1
        if last and n_tok_p % tm_ffn == 0 and n_tok_s % tm_ffn == 0:
            xs = (_ffn(x, norm2[l], w_gu_b, w_dn_b, l, final_norm, True, tm_ffn, tf, 0, n_tok_p),
                  _ffn(x, norm2[l], w_gu_b, w_dn_b, l, final_norm, True, tm_ffn, tf, n_tok_p,
                       n_tok_s))
        else:
            xs = (_ffn(x, norm2[l], w_gu_b, w_dn_b, l, final_norm, last, tm_ffn, tf),)

        re_p, im_p = unpack_state(st_p)
        re_s, im_s = unpack_state(st_s)
        new_re_p.append(re_p)
        new_im_p.append(im_p)
        new_re_s.append(re_s)
        new_im_s.append(im_s)

    if len(xs) == 1:
        xs = (xs[0][:n_tok_p], xs[0][n_tok_p:])
    y_prompt = xs[0].reshape(bp, tp, d)
    y_sample = xs[1].reshape(bs, ts, d)
    return (y_prompt, y_sample, hgrn_p, jnp.stack(new_re_p), jnp.stack(new_im_p),
            hgrn_s, jnp.stack(new_re_s), jnp.stack(new_im_s))
```

```python
import functools
import math

import jax
import jax.numpy as jnp
import numpy as np
from jax import lax
from jax.experimental import pallas as pl
from jax.experimental.pallas import tpu as pltpu

F32 = jnp.float32
BF16 = jnp.bfloat16

EPS = 1e-6
LOG2_E = 1.4426950408889634
HEAD_DIM = 128
S5_GROUP = 16
S5_STATE = 64
S5_CHUNK = 16
S5_SLAB = 8
S5_ROW_BLOCK = 272
MXU_TILE = 256
HGRN_STACK_ROWS = MXU_TILE
VMEM_LIMIT = 56 * 1024 * 1024
BIG_VMEM_LIMIT = 60 * 1024 * 1024


def _cparams(sem):
    return pltpu.CompilerParams(dimension_semantics=sem, vmem_limit_bytes=VMEM_LIMIT)


def _resident(shape, index_map):
    return pl.BlockSpec(shape, index_map, pipeline_mode=pl.Buffered(1))


def _row_tile_specs(xs, tm):
    d = xs[0].shape[1]
    tiles_a = xs[0].shape[0] // tm
    if len(xs) == 1:
        return [pl.BlockSpec((tm, d), lambda i, *_: (i, 0))], tiles_a
    assert len(xs) == 2 and xs[0].shape[0] % tm == 0 and xs[1].shape[0] % tm == 0
    return [
        pl.BlockSpec((tm, d), lambda i, *_: (jnp.minimum(i, tiles_a - 1), 0)),
        pl.BlockSpec((tm, d), lambda i, *_: (jnp.maximum(i - tiles_a, 0), 0),
                     pipeline_mode=pl.Buffered(1)),
    ], tiles_a


def _for_row_source(x_refs, tiles_a, pred, body):
    i = pl.program_id(0)
    if len(x_refs) == 1:
        conds = [pred]
    else:
        conds = [pred & (i < tiles_a), pred & (i >= tiles_a)]
    for cond, x_ref in zip(conds, x_refs):
        pl.when(cond)(functools.partial(body, x_ref))


def _inproj_kernel(*refs, n_src, tiles_a, n_main_tiles):
    x_refs = refs[:n_src]
    nw_ref, w_ref, gate_ref, main_ref, h_scr = refs[n_src:]
    j = pl.program_id(1)

    def first_tile(x_ref):
        x = x_ref[...]
        ms = jnp.mean(x * x, axis=-1, keepdims=True)
        h = (x * lax.rsqrt(ms + EPS) * nw_ref[...]).astype(BF16)
        h_scr[...] = h
        main_ref[...] = jnp.dot(h, w_ref[...], preferred_element_type=F32)

    _for_row_source(x_refs, tiles_a, j == 0, first_tile)

    @pl.when((j > 0) & (j < n_main_tiles))
    def _():
        main_ref[...] = jnp.dot(h_scr[...], w_ref[...], preferred_element_type=F32)

    @pl.when(j >= n_main_tiles)
    def _():
        gate_ref[...] = jnp.dot(h_scr[...], w_ref[...],
                                preferred_element_type=F32).astype(BF16)


def _inproj(xs, nw, w, l, n_gate_cols, tm, tn):
    t = sum(x.shape[0] for x in xs)
    d = xs[0].shape[1]
    n = w.shape[2]
    n_tiles = n // tn
    n_main_tiles = (n - n_gate_cols) // tn
    assert n_main_tiles >= 1
    x_specs, tiles_a = _row_tile_specs(xs, tm)
    return pl.pallas_call(
        functools.partial(_inproj_kernel, n_src=len(xs), tiles_a=tiles_a,
                          n_main_tiles=n_main_tiles),
        grid=(t // tm, n_tiles),
        in_specs=x_specs + [
            pl.BlockSpec((1, d), lambda i, j: (0, 0)),
            pl.BlockSpec((None, d, tn), lambda i, j: (l, 0, j)),
        ],
        out_specs=[
            pl.BlockSpec((tm, tn), lambda i, j: (i, jnp.maximum(j - n_main_tiles, 0))),
            pl.BlockSpec((tm, tn), lambda i, j: (i, jnp.minimum(j, n_main_tiles - 1))),
        ],
        out_shape=[
            jax.ShapeDtypeStruct((t, n_gate_cols), BF16),
            jax.ShapeDtypeStruct((t, n - n_gate_cols), F32),
        ],
        scratch_shapes=[pltpu.VMEM((tm, d), BF16)],
        compiler_params=_cparams(("parallel", "arbitrary")),
        name="inproj",
    )(*xs, nw.reshape(1, d), w)


def _hgrn_level_masks(c):
    n_lev = int(math.log2(c)) + 1
    gsz = max(1, min(n_lev, HGRN_STACK_ROWS // c))
    n_groups = -(-n_lev // gsz)
    r = gsz * c
    t = np.arange(c)[:, None]
    s = np.arange(c)[None, :]
    masks = np.zeros((n_groups, r, r), np.float32)
    for lev in range(n_lev):
        if lev == 0:
            m = (t == s)
        else:
            h = 1 << (lev - 1)
            m = ((t & h) != 0) & ((s & h) == 0) & ((t // (2 * h)) == (s // (2 * h)))
        g, i = divmod(lev, gsz)
        masks[g, i * c:(i + 1) * c, i * c:(i + 1) * c] = m
    return masks, n_lev, gsz, n_groups


def _hgrn_kernel(q_ref, f_ref, i_ref, g_ref, lb_ref, gain_ref, s0_ref, mask_ref, *rest,
                 c, nh, hgrp, n_lev, gsz, n_groups):
    o_ref, s_ref, st_scr = rest[-3:]
    ci = pl.program_id(2)
    nc = pl.num_programs(2)

    @pl.when(ci == 0)
    def _():
        st_scr[...] = s0_ref[0]

    row = lax.broadcasted_iota(jnp.int32, (c, HEAD_DIM), 0)
    nt = (((1,), (1,)), ((), ()))
    tn = (((0,), (0,)), ((), ()))

    def head_operands(h):
        sl = slice(h * HEAD_DIM, (h + 1) * HEAD_DIM)
        z = f_ref[:, sl]
        lb = lb_ref[:, sl]
        ez = jnp.exp(-jnp.abs(z))
        log_sig = jnp.minimum(z, 0.0) - jnp.log(1.0 + ez)
        a0 = jnp.log(lb)
        a1 = jnp.log1p(-lb) + log_sig
        logf = jnp.maximum(a0, a1) + jnp.log(1.0 + jnp.exp(-jnp.abs(a0 - a1)))
        k = (1.0 - lb) * (jnp.where(z >= 0.0, ez, 1.0) / (1.0 + ez))
        q = q_ref[:, sl]
        v = jax.nn.silu(i_ref[:, sl])
        v_bf = v.astype(BF16)

        p = logf * LOG2_E
        tb = p
        a_lev = [q.astype(BF16)]
        b_lev = [k.astype(BF16)]
        for lev in range(1, n_lev):
            hs = 1 << (lev - 1)
            right = (row & hs) != 0
            e = jnp.exp2(jnp.where(right, p, tb - p))
            zl = (jnp.where(right, q, k) * e).astype(BF16)
            a_lev.append(zl)
            b_lev.append(zl)
            dn = pltpu.roll(tb, hs, 0)
            up = pltpu.roll(tb, c - hs, 0)
            p = p + jnp.where(right, dn, 0.0)
            tb = tb + jnp.where(right, dn, up)
        ops = []
        for g in range(n_groups):
            lo, hi = g * gsz, min((g + 1) * gsz, n_lev)
            n_in = hi - lo
            a_g = jnp.concatenate(a_lev[lo:hi], axis=0) if n_in > 1 else a_lev[lo]
            b_g = a_g if lo > 0 else (
                jnp.concatenate(b_lev[lo:hi], axis=0) if n_in > 1 else b_lev[lo])
            ops.append((a_g, b_g, n_in))
        q_in = (q * jnp.exp2(p)).astype(BF16)
        k_out = (k * jnp.exp2(tb - p)).astype(BF16)
        decay = jnp.exp2(jnp.broadcast_to(tb[0:1, :], (HEAD_DIM, HEAD_DIM)).T)
        return ops, v_bf, q_in, k_out, decay

    def fold_scores(g, sc, n_in):
        if c % HEAD_DIM == 0:
            return jnp.concatenate(
                [sc[i * c:(i + 1) * c, i * c:(i + 1) * c].astype(BF16)
                 * mask_ref[g, i * c:(i + 1) * c, i * c:(i + 1) * c]
                 for i in range(n_in)], axis=1)
        r = n_in * c
        sc = sc * mask_ref[g][:r, :r]
        fold = sc[0:c]
        for i in range(1, n_in):
            fold = fold + sc[i * c:(i + 1) * c]
        return fold.astype(BF16)

    for h0 in range(0, nh, hgrp):
        heads = range(h0, min(h0 + hgrp, nh))
        prep = {h: head_operands(h) for h in heads}
        scores = {h: [lax.dot_general(a_g, b_g, nt, preferred_element_type=F32)
                      for a_g, b_g, _ in prep[h][0]] for h in heads}
        upd = {h: lax.dot_general(prep[h][3], prep[h][1], tn, preferred_element_type=F32)
               for h in heads}
        for h in heads:
            ops, v_bf, q_in, _, decay = prep[h]
            st = st_scr[h]
            lhs = [q_in]
            rhs = [st.astype(BF16)]
            for g, (sc, (_, _, n_in)) in enumerate(zip(scores[h], ops)):
                lhs.append(fold_scores(g, sc, n_in))
                rhs.extend([v_bf] * n_in)
            o = jnp.dot(jnp.concatenate(lhs, axis=1), jnp.concatenate(rhs, axis=0),
                        preferred_element_type=F32)
            st_scr[h] = st * decay + upd[h]
            sl = slice(h * HEAD_DIM, (h + 1) * HEAD_DIM)
            ms = jnp.mean(o * o, axis=-1, keepdims=True)
            o = o * lax.rsqrt(ms + EPS) * gain_ref[:, sl] * jax.nn.silu(g_ref[:, sl])
            o_ref[:, sl] = o.astype(o_ref.dtype)

    @pl.when(ci == nc - 1)
    def _():
        s_ref[0] = st_scr[...]


def _hgrn(main, lb, gain, s0, o_all, s_all, *, layer, n_layers, row_off, n_seq, seq_len, c, nh,
          hgrp):
    n_tok = main.shape[0]
    dh = lb.shape[-1]
    n_heads = dh // HEAD_DIM
    w = nh * HEAD_DIM
    n_hg = n_heads // nh
    n_chunks = seq_len // c
    rb0 = row_off // c
    masks, n_lev, gsz, n_groups = _hgrn_level_masks(c)
    r = masks.shape[-1]

    def col_spec(group):
        return pl.BlockSpec(
            (c, w), lambda b, hg, ci: (rb0 + b * n_chunks + ci, group * n_hg + hg))

    vec_spec = pl.BlockSpec((1, w), lambda b, hg, ci: (0, hg))
    s0_layer = min(layer, s0.shape[0] - 1)
    st_spec = pl.BlockSpec((None, 1, nh, HEAD_DIM, HEAD_DIM),
                           lambda b, hg, ci: (s0_layer, b, hg, 0, 0))
    operands = [main, main, main, main, lb.reshape(1, dh), gain.reshape(1, dh), s0,
                jnp.asarray(masks, BF16 if c % HEAD_DIM == 0 else F32)]
    in_specs = [col_spec(0), col_spec(1), col_spec(2), col_spec(3), vec_spec, vec_spec,
                st_spec, _resident((n_groups, r, r), lambda b, hg, ci: (0, 0, 0))]
    aliases = {}
    for out_idx, buf in enumerate((o_all, s_all)):
        if buf is not None:
            aliases[len(operands)] = out_idx
            operands.append(buf)
            in_specs.append(pl.BlockSpec(memory_space=pl.ANY))
    o, s_new = pl.pallas_call(
        functools.partial(_hgrn_kernel, c=c, nh=nh, hgrp=hgrp, n_lev=n_lev, gsz=gsz,
                          n_groups=n_groups),
        grid=(n_seq, n_hg, n_chunks),
        in_specs=in_specs,
        out_specs=[
            pl.BlockSpec((c, w), lambda b, hg, ci: (rb0 + b * n_chunks + ci, hg)),
            pl.BlockSpec((None, 1, nh, HEAD_DIM, HEAD_DIM),
                         lambda b, hg, ci: (layer, b, hg, 0, 0)),
        ],
        out_shape=[
            jax.ShapeDtypeStruct((n_tok, dh), BF16),
            jax.ShapeDtypeStruct((n_layers, n_seq, n_heads, HEAD_DIM, HEAD_DIM), F32),
        ],
        input_output_aliases=aliases,
        scratch_shapes=[pltpu.VMEM((nh, HEAD_DIM, HEAD_DIM), F32)],
        compiler_params=_cparams(("parallel", "parallel", "arbitrary")),
        name="hgrn_c%d" % c,
    )(*operands)
    return o, s_new


def _cmul(xr, xi, lr, li):
    return xr * lr - xi * li, xr * li + xi * lr


def _s5_kernel(u_ref, kt_ref, pb_ref, ct_ref, lam_ref, x0_ref, y_ref, sp_ref, ss_ref,
               m_scr, b_scr, ct_scr, z_scr, *, n_p, n_b, n_cs, rb):
    n_rows = n_p + n_b * n_cs
    sw = 2 * S5_STATE
    cw = S5_SLAB * S5_GROUP

    gi = lax.broadcasted_iota(jnp.int32, (cw, cw), 0) // S5_GROUP
    gj = lax.broadcasted_iota(jnp.int32, (cw, cw), 1) // S5_GROUP
    same_group = gi == gj
    hw = cw // 2
    hk = S5_CHUNK * hw
    hs = (S5_SLAB // 2) * sw
    low_lanes = lax.broadcasted_iota(jnp.int32, (1, cw), 1) < hw

    def split_halves(a, b):
        return (jnp.where(low_lanes, a, pltpu.roll(b, hw, 1)),
                jnp.where(low_lanes, pltpu.roll(a, hw, 1), b))

    lag_tiles = [
        jnp.where(same_group, jnp.concatenate([kt_ref[0, j]] * S5_SLAB, axis=0), 0.0)
        for j in range(S5_CHUNK)]
    zero_tile = jnp.zeros((cw, cw), F32)
    tpb = MXU_TILE // hw
    for t in range(0, S5_CHUNK, 2):
        for s in range((t // tpb + 1) * tpb):
            pair = split_halves(lag_tiles[t - s] if t >= s else zero_tile,
                                lag_tiles[t + 1 - s] if t + 1 >= s else zero_tile)
            for h in range(2):
                m_scr[h, s * hw:(s + 1) * hw, t * hw:(t + 2) * hw] = (
                    pair[h][h * hw:(h + 1) * hw, :].astype(BF16))
    rg = lax.broadcasted_iota(jnp.int32, (cw, S5_SLAB * sw), 0) // S5_GROUP
    lane = lax.broadcasted_iota(jnp.int32, (cw, S5_SLAB * sw), 1)
    plane, half = lane // sw, (lane % sw) // S5_STATE
    own_state = (plane // 2 == rg // 2) & (half == rg % 2)
    straight = plane % 2 == rg % 2

    def expand(tab):
        tiled = jnp.concatenate([tab] * S5_SLAB, axis=1)
        swapped = jnp.concatenate([pltpu.roll(tab, S5_STATE, 1)] * S5_SLAB, axis=1)
        return jnp.where(own_state, jnp.where(straight, tiled, swapped), 0.0).astype(BF16)

    for s in range(S5_CHUNK):
        eb, ec = expand(pb_ref[0, s]), expand(ct_ref[0, s])
        for h in range(2):
            b_scr[h, s * hw:(s + 1) * hw, :] = eb[h * hw:(h + 1) * hw, h * hs:(h + 1) * hs]
            ct_scr[h, s * hw:(s + 1) * hw, :] = ec[h * hw:(h + 1) * hw, h * hs:(h + 1) * hs]

    def chunk_operands(r0):
        tiles = [u_ref[pl.ds(S5_CHUNK * r0 + s, rb, stride=S5_CHUNK), :]
                 for s in range(S5_CHUNK)]
        pairs = [split_halves(tiles[s], tiles[s + 1]) for s in range(0, S5_CHUNK, 2)]
        return [jnp.concatenate([p[h] for p in pairs], axis=1).astype(BF16) for h in range(2)]

    planes_per_half = S5_SLAB // 2
    for r0 in range(0, n_rows, rb):
        for h, a in enumerate(chunk_operands(r0)):
            z = jnp.dot(a, b_scr[h], preferred_element_type=F32)
            for g in range(planes_per_half):
                z_scr[h * planes_per_half + g, r0:r0 + rb, :] = z[:, g * sw:(g + 1) * sw]

    rowi = lax.broadcasted_iota(jnp.int32, (n_p, sw), 0)

    def shift_rows(x, sh):
        return jnp.where(rowi >= sh, pltpu.roll(x, sh, 0), 0.0)

    def scan_pair(k, carry):
        pr, pi = 2 * k, 2 * k + 1
        lam_k = lam_ref[0, k]
        xr = z_scr[pr, 0:n_p, :]
        xi = z_scr[pi, 0:n_p, :]
        for lev in range(int(math.log2(n_p))):
            sh = 1 << lev
            lr, li = lam_k[2 * lev:2 * lev + 1], lam_k[2 * lev + 1:2 * lev + 2]
            if sh % 8 == 0 and sh < n_p:
                dr, di = _cmul(xr[:n_p - sh], xi[:n_p - sh], lr, li)
                xr = jnp.concatenate([xr[:sh], xr[sh:] + dr], axis=0)
                xi = jnp.concatenate([xi[:sh], xi[sh:] + di], axis=0)
            else:
                dr, di = _cmul(shift_rows(xr, sh), shift_rows(xi, sh), lr, li)
                xr, xi = xr + dr, xi + di
        z_scr[pr, 0:n_p, :] = shift_rows(xr, 1)
        z_scr[pi, 0:n_p, :] = shift_rows(xi, 1)
        sp_ref[0, pr] = xr[n_p - 1:n_p]
        sp_ref[0, pi] = xi[n_p - 1:n_p]

        sr, si = x0_ref[0, pr], x0_ref[0, pi]
        for ci in range(n_cs):
            rows = pl.ds(n_p + ci, n_b, stride=n_cs)
            zr, zi = z_scr[pr, rows, :], z_scr[pi, rows, :]
            z_scr[pr, rows, :] = sr
            z_scr[pi, rows, :] = si
            dr, di = _cmul(sr, si, lam_k[0:1], lam_k[1:2])
            sr, si = dr + zr, di + zi
        ss_ref[0, pr] = sr
        ss_ref[0, pi] = si
        return carry

    for k in range(S5_SLAB // 2):
        scan_pair(k, 0)

    for r0 in range(0, n_rows, rb):
        ops = chunk_operands(r0)
        x_prev = [jnp.concatenate(
            [z_scr[h * planes_per_half + g, r0:r0 + rb, :].astype(BF16)
             for g in range(planes_per_half)], axis=1) for h in range(2)]
        for j in range(S5_CHUNK // tpb):
            cols = slice(j * MXU_TILE, (j + 1) * MXU_TILE)
            y = [jnp.dot(ops[h][:, :(j + 1) * MXU_TILE], m_scr[h, 0:(j + 1) * MXU_TILE, cols],
                         preferred_element_type=F32)
                 + lax.dot_general(x_prev[h], ct_scr[h, cols, :], (((1,), (1,)), ((), ())),
                                   preferred_element_type=F32)
                 for h in range(2)]
            for ti in range(0, tpb, 2):
                lo = slice(ti * hw, (ti + 2) * hw)
                out_even, out_odd = split_halves(y[0][:, lo], y[1][:, lo])
                for dt, out in ((0, out_even), (1, out_odd)):
                    t = j * tpb + ti + dt
                    y_ref[pl.ds(S5_CHUNK * r0 + t, rb, stride=S5_CHUNK), :] = out


def _s5(main, u_col0, kt, pb, ct, lam, l, x0, *, n_p, n_b, n_cs):
    n_slabs = kt.shape[1]
    n_tok = main.shape[0]
    n_rows = n_p + n_b * n_cs
    assert n_rows * S5_CHUNK == n_tok
    n_lam = lam.shape[3]
    cw = S5_SLAB * S5_GROUP
    kw = S5_CHUNK * cw
    gw = 2 * S5_STATE
    sw = S5_SLAB * gw
    rb = max(r for r in range(16, S5_ROW_BLOCK + 1, 16) if n_rows % r == 0)
    cb0 = u_col0 // cw

    def per_slab(shape):
        nd = len(shape)
        return pl.BlockSpec((1,) + shape, lambda v: (v,) + (0,) * nd)

    def table(shape):
        nd = len(shape)
        return pl.BlockSpec((None, 1) + shape, lambda v: (l, v) + (0,) * nd)

    return pl.pallas_call(
        functools.partial(_s5_kernel, n_p=n_p, n_b=n_b, n_cs=n_cs, rb=rb),
        grid=(n_slabs,),
        in_specs=[pl.BlockSpec((n_tok, cw), lambda v: (0, cb0 + v)),
                  table((S5_CHUNK, S5_GROUP, cw)),
                  table((S5_CHUNK, cw, 2 * S5_STATE)),
                  table((S5_CHUNK, cw, 2 * S5_STATE)),
                  table((S5_SLAB // 2, n_lam, gw)), per_slab((S5_SLAB, n_b, gw))],
        out_specs=[_resident((n_tok, cw), lambda v: (0, v)),
                   per_slab((S5_SLAB, 1, gw)), per_slab((S5_SLAB, n_b, gw))],
        out_shape=[
            jax.ShapeDtypeStruct((n_tok, n_slabs * cw), F32),
            jax.ShapeDtypeStruct((n_slabs, S5_SLAB, 1, gw), F32),
            jax.ShapeDtypeStruct((n_slabs, S5_SLAB, n_b, gw), F32),
        ],
        scratch_shapes=[pltpu.VMEM((2, kw // 2, kw // 2), BF16),
                        pltpu.VMEM((2, kw // 2, sw // 2), BF16),
                        pltpu.VMEM((2, kw // 2, sw // 2), BF16),
                        pltpu.VMEM((S5_SLAB, n_rows, 2 * S5_STATE), F32)],
        compiler_params=pltpu.CompilerParams(dimension_semantics=("parallel",),
                                             vmem_limit_bytes=BIG_VMEM_LIMIT),
        name="s5",
    )(main, kt, pb, ct, lam, x0)


def _s5_tables(a_log_neg_re, a_im, log_dt, b_re, b_im, c_re, c_im, n_scan_lev):
    hp = lax.Precision.HIGHEST
    lam_re = -jnp.exp(a_log_neg_re.astype(F32))
    lam_im = a_im.astype(F32)
    dt = jnp.exp(log_dt.astype(F32))[..., None]

    def powers(jj):
        e = jj[None, None, :, None]
        mag = jnp.exp((lam_re * dt)[:, :, None, :] * e)
        ang = (lam_im * dt)[:, :, None, :] * e
        return mag * jnp.cos(ang), mag * jnp.sin(ang)

    pw_re, pw_im = powers(jnp.arange(S5_CHUNK + 1, dtype=F32))
    num_re, num_im = pw_re[:, :, 1] - 1.0, pw_im[:, :, 1]
    den = lam_re * lam_re + lam_im * lam_im
    zoh_re = ((num_re * lam_re + num_im * lam_im) / den)[..., None]
    zoh_im = ((num_im * lam_re - num_re * lam_im) / den)[..., None]
    b_re, b_im = b_re.astype(F32), b_im.astype(F32)
    bb_re = zoh_re * b_re - zoh_im * b_im
    bb_im = zoh_re * b_im + zoh_im * b_re
    c_re, c_im = c_re.astype(F32)[:, :, None], c_im.astype(F32)[:, :, None]
    pr, pi = pw_re[:, :, :, None, :], pw_im[:, :, :, None, :]
    cp_re = c_re * pr - c_im * pi
    cp_im = c_re * pi + c_im * pr
    kj = (jnp.einsum('lgjpn,lgnq->lgjpq', cp_re[:, :, :S5_CHUNK], bb_re, precision=hp)
          - jnp.einsum('lgjpn,lgnq->lgjpq', cp_im[:, :, :S5_CHUNK], bb_im, precision=hp))
    qr, qi = powers(jnp.asarray(np.arange(S5_CHUNK - 1, -1, -1), F32))
    qr, qi = qr[:, :, :, None, :], qi[:, :, :, None, :]
    bt_re = bb_re.transpose(0, 1, 3, 2)[:, :, None]
    bt_im = bb_im.transpose(0, 1, 3, 2)[:, :, None]
    bst = jnp.concatenate([qr * bt_re - qi * bt_im, qr * bt_im + qi * bt_re],
                          axis=-1)
    cst = jnp.concatenate([cp_re[:, :, 1:], -cp_im[:, :, 1:]], axis=-1)
    lp_re, lp_im = powers(S5_CHUNK * (2.0 ** jnp.arange(n_scan_lev, dtype=F32)))

    l, g = kj.shape[:2]
    v, sg = g // S5_SLAB, S5_SLAB
    cw = sg * S5_GROUP
    kt = kj.reshape(l, v, sg, S5_CHUNK, S5_GROUP, S5_GROUP)
    kt = kt.transpose(0, 1, 3, 5, 2, 4).reshape(l, v, S5_CHUNK, S5_GROUP, cw)
    pb = bst.reshape(l, v, sg, S5_CHUNK, S5_GROUP, 2 * S5_STATE)
    pb = pb.transpose(0, 1, 3, 2, 4, 5).reshape(l, v, S5_CHUNK, cw, 2 * S5_STATE)
    ct = cst.reshape(l, v, sg, S5_CHUNK, S5_GROUP, 2 * S5_STATE)
    ct = ct.transpose(0, 1, 3, 2, 4, 5).reshape(l, v, S5_CHUNK, cw, 2 * S5_STATE)

    def pair_lanes(a):
        a = a.reshape(l, v, sg // 2, 2, n_scan_lev, S5_STATE)
        return a.transpose(0, 1, 2, 4, 3, 5).reshape(l, v, sg // 2, n_scan_lev, 2 * S5_STATE)

    lam_big = jnp.stack([pair_lanes(lp_re), pair_lanes(lp_im)], axis=4)
    lam_big = lam_big.reshape(l, v, sg // 2, 2 * n_scan_lev, 2 * S5_STATE)
    return kt, pb, ct, lam_big


def _mix_kernel(*refs, n_src, tiles_a):
    x_refs = refs[:n_src]
    (gh_ref, gs_ref, oh_ref, y_ref, u_ref, d_ref, bglu_ref,
     wglu_ref, wbh_ref, wbs_ref, wout_ref, out_ref) = refs[n_src:]
    ys = jax.nn.gelu(y_ref[...] + d_ref[...] * u_ref[...])
    gate = jnp.dot(ys.astype(BF16), wglu_ref[...], preferred_element_type=F32) + bglu_ref[...]
    glu = (ys * jax.nn.sigmoid(gate)).astype(BF16)
    t_h = jnp.dot(oh_ref[...], wbh_ref[...], preferred_element_type=F32)
    t_s = jnp.dot(glu, wbs_ref[...], preferred_element_type=F32)
    mix = (jax.nn.sigmoid(gh_ref[...].astype(F32)) * t_h
           + jax.nn.sigmoid(gs_ref[...].astype(F32)) * t_s)
    x = x_refs[0][...]
    if n_src == 2:
        x = jnp.where(pl.program_id(0) < tiles_a, x, x_refs[1][...])
    out_ref[...] = x + jnp.dot(mix.astype(BF16), wout_ref[...], preferred_element_type=F32)


def _mix(xs, gates, o_h, y_s, main, u_col_block, d_skip, b_glu, w_glu, w_bh, w_bs, w_out, l, tm):
    t, ds = y_s.shape
    d = w_bh.shape[2]
    layer = lambda i: (l, 0, 0)
    const = lambda i: (0, 0)
    x_specs, tiles_a = _row_tile_specs(xs, tm)
    return pl.pallas_call(
        functools.partial(_mix_kernel, n_src=len(xs), tiles_a=tiles_a),
        grid=(t // tm,),
        in_specs=x_specs + [
            pl.BlockSpec((tm, d), lambda i: (i, 0)),
            pl.BlockSpec((tm, d), lambda i: (i, 1)),
            pl.BlockSpec((tm, ds), lambda i: (i, 0)),
            pl.BlockSpec((tm, ds), lambda i: (i, 0)),
            pl.BlockSpec((tm, ds), lambda i: (i, u_col_block)),
            pl.BlockSpec((1, ds), const),
            pl.BlockSpec((1, ds), const),
            _resident((None, ds, ds), layer),
            _resident((None, ds, d), layer),
            _resident((None, ds, d), layer),
            _resident((None, d, d), layer),
        ],
        out_specs=pl.BlockSpec((tm, d), lambda i: (i, 0)),
        out_shape=jax.ShapeDtypeStruct((t, d), F32),
        compiler_params=_cparams(("parallel",)),
        name="mix",
    )(*xs, gates, gates, o_h, y_s, main, d_skip.reshape(1, ds), b_glu.reshape(1, ds),
      w_glu, w_bh, w_bs, w_out)


def _rms(x, w):
    ms = jnp.mean(x * x, axis=-1, keepdims=True)
    return x * lax.rsqrt(ms + EPS) * w


def _ffn_kernel(x_ref, n2_ref, wg_ref, wu_ref, wd_ref, fn_ref, out_ref, h_scr, *, final_norm):
    j = pl.program_id(1)
    nj = pl.num_programs(1)

    def step(first):
        if first:
            h = _rms(x_ref[...], n2_ref[...]).astype(BF16)
            h_scr[...] = h
        else:
            h = h_scr[...]
        ga = jnp.dot(h, wg_ref[...], preferred_element_type=F32)
        up = jnp.dot(h, wu_ref[...], preferred_element_type=F32)
        act = (jax.nn.silu(ga) * up).astype(BF16)
        contrib = jnp.dot(act, wd_ref[...], preferred_element_type=F32)
        out_ref[...] = (x_ref[...] if first else out_ref[...]) + contrib

    pl.when(j == 0)(functools.partial(step, True))
    pl.when(j > 0)(functools.partial(step, False))

    if final_norm:
        @pl.when(j == nj - 1)
        def _():
            out_ref[...] = _rms(out_ref[...], fn_ref[...])


def _ffn(x, norm2, w_gate_up, w_down, l, fnorm, final_norm, tm, tf, row0=0, n_rows=None):
    d = x.shape[1]
    n_rows = x.shape[0] - row0 if n_rows is None else n_rows
    assert row0 % tm == 0 and n_rows % tm == 0
    tile0 = row0 // tm
    dff = w_down.shape[1]
    nj = dff // tf
    return pl.pallas_call(
        functools.partial(_ffn_kernel, final_norm=final_norm),
        grid=(n_rows // tm, nj),
        in_specs=[
            pl.BlockSpec((tm, d), lambda i, j: (tile0 + i, 0)),
            pl.BlockSpec((1, d), lambda i, j: (0, 0)),
            pl.BlockSpec((None, d, tf), lambda i, j: (l, 0, j)),
            pl.BlockSpec((None, d, tf), lambda i, j: (l, 0, nj + j)),
            pl.BlockSpec((None, tf, d), lambda i, j: (l, j, 0)),
            pl.BlockSpec((1, d), lambda i, j: (0, 0)),
        ],
        out_specs=pl.BlockSpec((tm, d), lambda i, j: (i, 0)),
        out_shape=jax.ShapeDtypeStruct((n_rows, d), F32),
        scratch_shapes=[pltpu.VMEM((tm, d), BF16)],
        compiler_params=_cparams(("parallel", "arbitrary")),
        name="ffn",
    )(x, norm2.reshape(1, d), w_gate_up, w_gate_up, w_down, fnorm.reshape(1, d))


def _pick(n, pref):
    t = pref
    while n % t:
        t //= 2
    return t


def _lower_bounds(lb_logits):
    p = jax.nn.softmax(lb_logits.astype(F32), axis=0)
    cs = jnp.cumsum(p, axis=0)
    return cs - cs[:1]


def kernel(x_prompt, x_sample, state_hgrn, state_s5_re, state_s5_im, lb_logits, norm1, w_in, hgrn_norm, w_bh, s5_a_log_neg_re, s5_a_im, s5_log_dt, s5_b_re, s5_b_im, s5_c_re, s5_c_im, s5_d, w_glu, b_glu, w_bs, w_out, norm2, w_gate_up, w_down, final_norm):
    depth = w_in.shape[0]
    bp, tp, d = x_prompt.shape
    bs, ts, _ = x_sample.shape
    assert bp == 1
    dh = lb_logits.shape[1]
    ds = s5_d.shape[1]
    n_groups = ds // S5_GROUP
    n_heads = dh // HEAD_DIM
    n_tok_p = bp * tp
    n_tok_s = bs * ts
    n_tok = n_tok_p + n_tok_s

    w_in_b = w_in.astype(BF16)
    w_bh_b = w_bh.astype(BF16)
    w_glu_b = w_glu.astype(BF16)
    w_bs_b = w_bs.astype(BF16)
    w_out_b = w_out.astype(BF16)
    w_gu_b = w_gate_up.astype(BF16)
    w_dn_b = w_down.astype(BF16)
    lbs = _lower_bounds(lb_logits)

    n_p = tp // S5_CHUNK
    n_cs = ts // S5_CHUNK
    s5_kt, s5_pb, s5_ct, s5_lam = _s5_tables(s5_a_log_neg_re, s5_a_im, s5_log_dt, s5_b_re,
                                             s5_b_im, s5_c_re, s5_c_im, int(math.log2(n_p)))
    n_slabs = n_groups // S5_SLAB
    n_pairs = S5_SLAB // 2

    def pack_state(re, im):
        nb = re.shape[0]
        x0 = jnp.stack([re.reshape(nb, n_slabs, n_pairs, 2 * S5_STATE),
                        im.reshape(nb, n_slabs, n_pairs, 2 * S5_STATE)], axis=3)
        return x0.transpose(1, 2, 3, 0, 4).reshape(n_slabs, S5_SLAB, nb, 2 * S5_STATE)

    def unpack_state(st):
        nb = st.shape[2]
        st = st.reshape(n_slabs, n_pairs, 2, nb, 2 * S5_STATE).transpose(2, 3, 0, 1, 4)
        st = st.reshape(2, nb, n_groups, S5_STATE)
        return st[0], st[1]

    tm_in = _pick(n_tok, 1024)
    tn_in = 1024
    tm_mix = _pick(n_tok, 256)
    tm_ffn = _pick(n_tok, 1024)
    tf = _pick(w_down.shape[1], 512)
    c_p = _pick(tp, 128)
    c_s = _pick(ts, 128)

    xs = (x_prompt.reshape(n_tok_p, d), x_sample.reshape(n_tok_s, d))
    if n_tok_p % tm_in or n_tok_s % tm_in or n_tok_p % tm_mix or n_tok_s % tm_mix:
        xs = (jnp.concatenate(xs, axis=0),)
    zero_h = jnp.zeros((1, bp, n_heads, HEAD_DIM, HEAD_DIM), F32)

    new_re_p, new_im_p, new_re_s, new_im_s = [], [], [], []
    hgrn_p = hgrn_s = None
    for l in range(depth):
        gates, main = _inproj(xs, norm1[l], w_in_b, l, 2 * d, tm_in, tn_in)

        o_h, hgrn_p = _hgrn(main, lbs[l], hgrn_norm[l], zero_h, None, hgrn_p,
                            layer=l, n_layers=depth, row_off=0, n_seq=bp, seq_len=tp, c=c_p,
                            nh=n_heads, hgrp=n_heads)
        o_h, hgrn_s = _hgrn(main, lbs[l], hgrn_norm[l], state_hgrn, o_h, hgrn_s,
                            layer=l, n_layers=depth, row_off=n_tok_p, n_seq=bs, seq_len=ts, c=c_s,
                            nh=n_heads, hgrp=n_heads)

        y, st_p, st_s = _s5(main, 4 * dh, s5_kt, s5_pb, s5_ct, s5_lam, l,
                            pack_state(state_s5_re[l], state_s5_im[l]),
                            n_p=n_p, n_b=bs, n_cs=n_cs)

        x = _mix(xs, gates, o_h, y, main, (4 * dh) // ds, s5_d[l], b_glu[l],
                 w_glu_b, w_bh_b, w_bs_b, w_out_b, l, tm_mix)
        last = l == depth - 1
        if last and n_tok_p % tm_ffn == 0 and n_tok_s % tm_ffn == 0:
            xs = (_ffn(x, norm2[l], w_gu_b, w_dn_b, l, final_norm, True, tm_ffn, tf, 0, n_tok_p),
                  _ffn(x, norm2[l], w_gu_b, w_dn_b, l, final_norm, True, tm_ffn, tf, n_tok_p,
                       n_tok_s))
        else:
            xs = (_ffn(x, norm2[l], w_gu_b, w_dn_b, l, final_norm, last, tm_ffn, tf),)

        re_p, im_p = unpack_state(st_p)
        re_s, im_s = unpack_state(st_s)
        new_re_p.append(re_p)
        new_im_p.append(im_p)
        new_re_s.append(re_s)
        new_im_s.append(im_s)

    if len(xs) == 1:
        xs = (xs[0][:n_tok_p], xs[0][n_tok_p:])
    y_prompt = xs[0].reshape(bp, tp, d)
    y_sample = xs[1].reshape(bs, ts, d)
    return (y_prompt, y_sample, hgrn_p, jnp.stack(new_re_p), jnp.stack(new_im_p),
            hgrn_s, jnp.stack(new_re_s), jnp.stack(new_im_s))
```

```python
import functools
import math

import jax
import jax.numpy as jnp
import numpy as np
from jax import lax
from jax.experimental import pallas as pl
from jax.experimental.pallas import tpu as pltpu

F32 = jnp.float32
BF16 = jnp.bfloat16

EPS = 1e-6
LOG2_E = 1.4426950408889634
HEAD_DIM = 128
S5_GROUP = 16
S5_STATE = 64
S5_CHUNK = 16
S5_SLAB = 8
S5_ROW_BLOCK = 272
MXU_TILE = 256
HGRN_STACK_ROWS = MXU_TILE
VMEM_LIMIT = 56 * 1024 * 1024
BIG_VMEM_LIMIT = 60 * 1024 * 1024


def _cparams(sem):
    return pltpu.CompilerParams(dimension_semantics=sem, vmem_limit_bytes=VMEM_LIMIT)


def _resident(shape, index_map):
    return pl.BlockSpec(shape, index_map, pipeline_mode=pl.Buffered(1))


def _row_tile_specs(xs, tm):
    d = xs[0].shape[1]
    tiles_a = xs[0].shape[0] // tm
    if len(xs) == 1:
        return [pl.BlockSpec((tm, d), lambda i, *_: (i, 0))], tiles_a
    assert len(xs) == 2 and xs[0].shape[0] % tm == 0 and xs[1].shape[0] % tm == 0
    return [
        pl.BlockSpec((tm, d), lambda i, *_: (jnp.minimum(i, tiles_a - 1), 0)),
        pl.BlockSpec((tm, d), lambda i, *_: (jnp.maximum(i - tiles_a, 0), 0),
                     pipeline_mode=pl.Buffered(1)),
    ], tiles_a


def _for_row_source(x_refs, tiles_a, pred, body):
    i = pl.program_id(0)
    if len(x_refs) == 1:
        conds = [pred]
    else:
        conds = [pred & (i < tiles_a), pred & (i >= tiles_a)]
    for cond, x_ref in zip(conds, x_refs):
        pl.when(cond)(functools.partial(body, x_ref))


def _inproj_kernel(*refs, n_src, tiles_a, n_main_tiles, silu_tiles):
    x_refs = refs[:n_src]
    nw_ref, w_ref, gate_ref, main_ref, h_scr = refs[n_src:]
    j = pl.program_id(1)

    def first_tile(x_ref):
        x = x_ref[...]
        ms = jnp.mean(x * x, axis=-1, keepdims=True)
        h = (x * lax.rsqrt(ms + EPS) * nw_ref[...]).astype(BF16)
        h_scr[...] = h
        main_ref[...] = jnp.dot(h, w_ref[...], preferred_element_type=F32)

    _for_row_source(x_refs, tiles_a, j == 0, first_tile)

    silu_lo, silu_hi = silu_tiles
    is_silu = (j >= silu_lo) & (j < silu_hi)

    @pl.when((j > 0) & (j < n_main_tiles) & jnp.logical_not(is_silu))
    def _():
        main_ref[...] = jnp.dot(h_scr[...], w_ref[...], preferred_element_type=F32)

    @pl.when(is_silu)
    def _():
        main_ref[...] = jax.nn.silu(jnp.dot(h_scr[...], w_ref[...], preferred_element_type=F32))

    @pl.when(j >= n_main_tiles)
    def _():
        gate_ref[...] = jnp.dot(h_scr[...], w_ref[...],
                                preferred_element_type=F32).astype(BF16)


def _inproj(xs, nw, w, l, n_gate_cols, silu_cols, tm, tn):
    t = sum(x.shape[0] for x in xs)
    d = xs[0].shape[1]
    n = w.shape[2]
    n_tiles = n // tn
    n_main_tiles = (n - n_gate_cols) // tn
    assert n_main_tiles >= 1
    assert silu_cols[0] % tn == 0 and silu_cols[1] % tn == 0 and silu_cols[0] >= tn
    silu_tiles = (silu_cols[0] // tn, silu_cols[1] // tn)
    x_specs, tiles_a = _row_tile_specs(xs, tm)
    return pl.pallas_call(
        functools.partial(_inproj_kernel, n_src=len(xs), tiles_a=tiles_a,
                          n_main_tiles=n_main_tiles, silu_tiles=silu_tiles),
        grid=(t // tm, n_tiles),
        in_specs=x_specs + [
            pl.BlockSpec((1, d), lambda i, j: (0, 0)),
            pl.BlockSpec((None, d, tn), lambda i, j: (l, 0, j)),
        ],
        out_specs=[
            pl.BlockSpec((tm, tn), lambda i, j: (i, jnp.maximum(j - n_main_tiles, 0))),
            pl.BlockSpec((tm, tn), lambda i, j: (i, jnp.minimum(j, n_main_tiles - 1))),
        ],
        out_shape=[
            jax.ShapeDtypeStruct((t, n_gate_cols), BF16),
            jax.ShapeDtypeStruct((t, n - n_gate_cols), F32),
        ],
        scratch_shapes=[pltpu.VMEM((tm, d), BF16)],
        compiler_params=_cparams(("parallel", "arbitrary")),
        name="inproj",
    )(*xs, nw.reshape(1, d), w)


def _hgrn_level_masks(c):
    n_lev = int(math.log2(c)) + 1
    gsz = max(1, min(n_lev, HGRN_STACK_ROWS // c))
    n_groups = -(-n_lev // gsz)
    r = gsz * c
    t = np.arange(c)[:, None]
    s = np.arange(c)[None, :]
    masks = np.zeros((n_groups, r, r), np.float32)
    for lev in range(n_lev):
        if lev == 0:
            m = (t == s)
        else:
            h = 1 << (lev - 1)
            m = ((t & h) != 0) & ((s & h) == 0) & ((t // (2 * h)) == (s // (2 * h)))
        g, i = divmod(lev, gsz)
        masks[g, i * c:(i + 1) * c, i * c:(i + 1) * c] = m
    return masks, n_lev, gsz, n_groups


def _hgrn_kernel(q_ref, f_ref, i_ref, g_ref, lb_ref, gain_ref, s0_ref, mask_ref, *rest,
                 c, nh, hgrp, n_lev, gsz, n_groups):
    o_ref, s_ref, st_scr = rest[-3:]
    ci = pl.program_id(2)
    nc = pl.num_programs(2)

    @pl.when(ci == 0)
    def _():
        st_scr[...] = s0_ref[0]

    row = lax.broadcasted_iota(jnp.int32, (c, HEAD_DIM), 0)
    nt = (((1,), (1,)), ((), ()))
    tn = (((0,), (0,)), ((), ()))

    def head_operands(h):
        sl = slice(h * HEAD_DIM, (h + 1) * HEAD_DIM)
        z = f_ref[:, sl]
        lb = lb_ref[:, sl]
        ez = jnp.exp(-jnp.abs(z))
        log_sig = jnp.minimum(z, 0.0) - jnp.log(1.0 + ez)
        a0 = jnp.log(lb)
        a1 = jnp.log1p(-lb) + log_sig
        logf = jnp.maximum(a0, a1) + jnp.log(1.0 + jnp.exp(-jnp.abs(a0 - a1)))
        k = (1.0 - lb) * (jnp.where(z >= 0.0, ez, 1.0) / (1.0 + ez))
        q = q_ref[:, sl]
        v = i_ref[:, sl]
        v_bf = v.astype(BF16)

        p = logf * LOG2_E
        tb = p
        a_lev = [q.astype(BF16)]
        b_lev = [k.astype(BF16)]
        for lev in range(1, n_lev):
            hs = 1 << (lev - 1)
            right = (row & hs) != 0
            e = jnp.exp2(jnp.where(right, p, tb - p))
            zl = (jnp.where(right, q, k) * e).astype(BF16)
            a_lev.append(zl)
            b_lev.append(zl)
            dn = pltpu.roll(tb, hs, 0)
            up = pltpu.roll(tb, c - hs, 0)
            p = p + jnp.where(right, dn, 0.0)
            tb = tb + jnp.where(right, dn, up)
        ops = []
        for g in range(n_groups):
            lo, hi = g * gsz, min((g + 1) * gsz, n_lev)
            n_in = hi - lo
            a_g = jnp.concatenate(a_lev[lo:hi], axis=0) if n_in > 1 else a_lev[lo]
            b_g = a_g if lo > 0 else (
                jnp.concatenate(b_lev[lo:hi], axis=0) if n_in > 1 else b_lev[lo])
            ops.append((a_g, b_g, n_in))
        q_in = (q * jnp.exp2(p)).astype(BF16)
        k_out = (k * jnp.exp2(tb - p)).astype(BF16)
        decay = jnp.exp2(jnp.broadcast_to(tb[0:1, :], (HEAD_DIM, HEAD_DIM)).T)
        return ops, v_bf, q_in, k_out, decay

    def fold_scores(g, sc, n_in):
        if c % HEAD_DIM == 0:
            return jnp.concatenate(
                [sc[i * c:(i + 1) * c, i * c:(i + 1) * c].astype(BF16)
                 * mask_ref[g, i * c:(i + 1) * c, i * c:(i + 1) * c]
                 for i in range(n_in)], axis=1)
        r = n_in * c
        sc = sc * mask_ref[g][:r, :r]
        fold = sc[0:c]
        for i in range(1, n_in):
            fold = fold + sc[i * c:(i + 1) * c]
        return fold.astype(BF16)

    for h0 in range(0, nh, hgrp):
        heads = range(h0, min(h0 + hgrp, nh))
        prep = {h: head_operands(h) for h in heads}
        scores = {h: [lax.dot_general(a_g, b_g, nt, preferred_element_type=F32)
                      for a_g, b_g, _ in prep[h][0]] for h in heads}
        upd = {h: lax.dot_general(prep[h][3], prep[h][1], tn, preferred_element_type=F32)
               for h in heads}
        for h in heads:
            ops, v_bf, q_in, _, decay = prep[h]
            st = st_scr[h]
            lhs = [q_in]
            rhs = [st.astype(BF16)]
            for g, (sc, (_, _, n_in)) in enumerate(zip(scores[h], ops)):
                lhs.append(fold_scores(g, sc, n_in))
                rhs.extend([v_bf] * n_in)
            o = jnp.dot(jnp.concatenate(lhs, axis=1), jnp.concatenate(rhs, axis=0),
                        preferred_element_type=F32)
            st_scr[h] = st * decay + upd[h]
            sl = slice(h * HEAD_DIM, (h + 1) * HEAD_DIM)
            ms = jnp.mean(o * o, axis=-1, keepdims=True)
            o = o * lax.rsqrt(ms + EPS) * gain_ref[:, sl] * g_ref[:, sl]
            o_ref[:, sl] = o.astype(o_ref.dtype)

    @pl.when(ci == nc - 1)
    def _():
        s_ref[0] = st_scr[...]


def _hgrn(main, lb, gain, s0, o_all, s_all, *, layer, n_layers, row_off, n_seq, seq_len, c, nh,
          hgrp):
    n_tok = main.shape[0]
    dh = lb.shape[-1]
    n_heads = dh // HEAD_DIM
    w = nh * HEAD_DIM
    n_hg = n_heads // nh
    n_chunks = seq_len // c
    rb0 = row_off // c
    masks, n_lev, gsz, n_groups = _hgrn_level_masks(c)
    r = masks.shape[-1]

    def col_spec(group):
        return pl.BlockSpec(
            (c, w), lambda b, hg, ci: (rb0 + b * n_chunks + ci, group * n_hg + hg))

    vec_spec = pl.BlockSpec((1, w), lambda b, hg, ci: (0, hg))
    s0_layer = min(layer, s0.shape[0] - 1)
    st_spec = pl.BlockSpec((None, 1, nh, HEAD_DIM, HEAD_DIM),
                           lambda b, hg, ci: (s0_layer, b, hg, 0, 0))
    operands = [main, main, main, main, lb.reshape(1, dh), gain.reshape(1, dh), s0,
                jnp.asarray(masks, BF16 if c % HEAD_DIM == 0 else F32)]
    in_specs = [col_spec(0), col_spec(1), col_spec(2), col_spec(3), vec_spec, vec_spec,
                st_spec, _resident((n_groups, r, r), lambda b, hg, ci: (0, 0, 0))]
    aliases = {}
    for out_idx, buf in enumerate((o_all, s_all)):
        if buf is not None:
            aliases[len(operands)] = out_idx
            operands.append(buf)
            in_specs.append(pl.BlockSpec(memory_space=pl.ANY))
    o, s_new = pl.pallas_call(
        functools.partial(_hgrn_kernel, c=c, nh=nh, hgrp=hgrp, n_lev=n_lev, gsz=gsz,
                          n_groups=n_groups),
        grid=(n_seq, n_hg, n_chunks),
        in_specs=in_specs,
        out_specs=[
            pl.BlockSpec((c, w), lambda b, hg, ci: (rb0 + b * n_chunks + ci, hg)),
            pl.BlockSpec((None, 1, nh, HEAD_DIM, HEAD_DIM),
                         lambda b, hg, ci: (layer, b, hg, 0, 0)),
        ],
        out_shape=[
            jax.ShapeDtypeStruct((n_tok, dh), BF16),
            jax.ShapeDtypeStruct((n_layers, n_seq, n_heads, HEAD_DIM, HEAD_DIM), F32),
        ],
        input_output_aliases=aliases,
        scratch_shapes=[pltpu.VMEM((nh, HEAD_DIM, HEAD_DIM), F32)],
        compiler_params=_cparams(("parallel", "parallel", "arbitrary")),
        name="hgrn_c%d" % c,
    )(*operands)
    return o, s_new


def _cmul(xr, xi, lr, li):
    return xr * lr - xi * li, xr * li + xi * lr


def _s5_kernel(u_ref, kt_ref, pb_ref, ct_ref, lam_ref, x0_ref, y_ref, sp_ref, ss_ref,
               m_scr, b_scr, ct_scr, z_scr, *, n_p, n_b, n_cs, rb):
    n_rows = n_p + n_b * n_cs
    sw = 2 * S5_STATE
    cw = S5_SLAB * S5_GROUP

    gi = lax.broadcasted_iota(jnp.int32, (cw, cw), 0) // S5_GROUP
    gj = lax.broadcasted_iota(jnp.int32, (cw, cw), 1) // S5_GROUP
    same_group = gi == gj
    hw = cw // 2
    hk = S5_CHUNK * hw
    hs = (S5_SLAB // 2) * sw
    low_lanes = lax.broadcasted_iota(jnp.int32, (1, cw), 1) < hw

    def split_halves(a, b):
        return (jnp.where(low_lanes, a, pltpu.roll(b, hw, 1)),
                jnp.where(low_lanes, pltpu.roll(a, hw, 1), b))

    lag_tiles = [
        jnp.where(same_group, jnp.concatenate([kt_ref[0, j]] * S5_SLAB, axis=0), 0.0)
        for j in range(S5_CHUNK)]
    zero_tile = jnp.zeros((cw, cw), F32)
    tpb = MXU_TILE // hw
    for t in range(0, S5_CHUNK, 2):
        for s in range((t // tpb + 1) * tpb):
            pair = split_halves(lag_tiles[t - s] if t >= s else zero_tile,
                                lag_tiles[t + 1 - s] if t + 1 >= s else zero_tile)
            for h in range(2):
                m_scr[h, s * hw:(s + 1) * hw, t * hw:(t + 2) * hw] = (
                    pair[h][h * hw:(h + 1) * hw, :].astype(BF16))
    rg = lax.broadcasted_iota(jnp.int32, (cw, S5_SLAB * sw), 0) // S5_GROUP
    lane = lax.broadcasted_iota(jnp.int32, (cw, S5_SLAB * sw), 1)
    plane, half = lane // sw, (lane % sw) // S5_STATE
    own_state = (plane // 2 == rg // 2) & (half == rg % 2)
    straight = plane % 2 == rg % 2

    def expand(tab):
        tiled = jnp.concatenate([tab] * S5_SLAB, axis=1)
        swapped = jnp.concatenate([pltpu.roll(tab, S5_STATE, 1)] * S5_SLAB, axis=1)
        return jnp.where(own_state, jnp.where(straight, tiled, swapped), 0.0).astype(BF16)

    for s in range(S5_CHUNK):
        eb, ec = expand(pb_ref[0, s]), expand(ct_ref[0, s])
        for h in range(2):
            b_scr[h, s * hw:(s + 1) * hw, :] = eb[h * hw:(h + 1) * hw, h * hs:(h + 1) * hs]
            ct_scr[h, s * hw:(s + 1) * hw, :] = ec[h * hw:(h + 1) * hw, h * hs:(h + 1) * hs]

    def chunk_operands(r0):
        tiles = [u_ref[pl.ds(S5_CHUNK * r0 + s, rb, stride=S5_CHUNK), :]
                 for s in range(S5_CHUNK)]
        pairs = [split_halves(tiles[s], tiles[s + 1]) for s in range(0, S5_CHUNK, 2)]
        return [jnp.concatenate([p[h] for p in pairs], axis=1).astype(BF16) for h in range(2)]

    planes_per_half = S5_SLAB // 2
    for r0 in range(0, n_rows, rb):
        for h, a in enumerate(chunk_operands(r0)):
            z = jnp.dot(a, b_scr[h], preferred_element_type=F32)
            for g in range(planes_per_half):
                z_scr[h * planes_per_half + g, r0:r0 + rb, :] = z[:, g * sw:(g + 1) * sw]

    rowi = lax.broadcasted_iota(jnp.int32, (n_p, sw), 0)

    def shift_rows(x, sh):
        return jnp.where(rowi >= sh, pltpu.roll(x, sh, 0), 0.0)

    def scan_pair(k, carry):
        pr, pi = 2 * k, 2 * k + 1
        lam_k = lam_ref[0, k]
        xr = z_scr[pr, 0:n_p, :]
        xi = z_scr[pi, 0:n_p, :]
        for lev in range(int(math.log2(n_p))):
            sh = 1 << lev
            lr, li = lam_k[2 * lev:2 * lev + 1], lam_k[2 * lev + 1:2 * lev + 2]
            if sh % 8 == 0 and sh < n_p:
                dr, di = _cmul(xr[:n_p - sh], xi[:n_p - sh], lr, li)
                xr = jnp.concatenate([xr[:sh], xr[sh:] + dr], axis=0)
                xi = jnp.concatenate([xi[:sh], xi[sh:] + di], axis=0)
            else:
                dr, di = _cmul(shift_rows(xr, sh), shift_rows(xi, sh), lr, li)
                xr, xi = xr + dr, xi + di
        z_scr[pr, 0:n_p, :] = shift_rows(xr, 1)
        z_scr[pi, 0:n_p, :] = shift_rows(xi, 1)
        sp_ref[0, pr] = xr[n_p - 1:n_p]
        sp_ref[0, pi] = xi[n_p - 1:n_p]

        sr, si = x0_ref[0, pr], x0_ref[0, pi]
        for ci in range(n_cs):
            rows = pl.ds(n_p + ci, n_b, stride=n_cs)
            zr, zi = z_scr[pr, rows, :], z_scr[pi, rows, :]
            z_scr[pr, rows, :] = sr
            z_scr[pi, rows, :] = si
            dr, di = _cmul(sr, si, lam_k[0:1], lam_k[1:2])
            sr, si = dr + zr, di + zi
        ss_ref[0, pr] = sr
        ss_ref[0, pi] = si
        return carry

    for k in range(S5_SLAB // 2):
        scan_pair(k, 0)

    for r0 in range(0, n_rows, rb):
        ops = chunk_operands(r0)
        x_prev = [jnp.concatenate(
            [z_scr[h * planes_per_half + g, r0:r0 + rb, :].astype(BF16)
             for g in range(planes_per_half)], axis=1) for h in range(2)]
        for j in range(S5_CHUNK // tpb):
            cols = slice(j * MXU_TILE, (j + 1) * MXU_TILE)
            y = [jnp.dot(ops[h][:, :(j + 1) * MXU_TILE], m_scr[h, 0:(j + 1) * MXU_TILE, cols],
                         preferred_element_type=F32)
                 + lax.dot_general(x_prev[h], ct_scr[h, cols, :], (((1,), (1,)), ((), ())),
                                   preferred_element_type=F32)
                 for h in range(2)]
            for ti in range(0, tpb, 2):
                lo = slice(ti * hw, (ti + 2) * hw)
                out_even, out_odd = split_halves(y[0][:, lo], y[1][:, lo])
                for dt, out in ((0, out_even), (1, out_odd)):
                    t = j * tpb + ti + dt
                    y_ref[pl.ds(S5_CHUNK * r0 + t, rb, stride=S5_CHUNK), :] = out


def _s5(main, u_col0, kt, pb, ct, lam, l, x0, *, n_p, n_b, n_cs):
    n_slabs = kt.shape[1]
    n_tok = main.shape[0]
    n_rows = n_p + n_b * n_cs
    assert n_rows * S5_CHUNK == n_tok
    n_lam = lam.shape[3]
    cw = S5_SLAB * S5_GROUP
    kw = S5_CHUNK * cw
    gw = 2 * S5_STATE
    sw = S5_SLAB * gw
    rb = max(r for r in range(16, S5_ROW_BLOCK + 1, 16) if n_rows % r == 0)
    cb0 = u_col0 // cw

    def per_slab(shape):
        nd = len(shape)
        return pl.BlockSpec((1,) + shape, lambda v: (v,) + (0,) * nd)

    def table(shape):
        nd = len(shape)
        return pl.BlockSpec((None, 1) + shape, lambda v: (l, v) + (0,) * nd)

    return pl.pallas_call(
        functools.partial(_s5_kernel, n_p=n_p, n_b=n_b, n_cs=n_cs, rb=rb),
        grid=(n_slabs,),
        in_specs=[pl.BlockSpec((n_tok, cw), lambda v: (0, cb0 + v)),
                  table((S5_CHUNK, S5_GROUP, cw)),
                  table((S5_CHUNK, cw, 2 * S5_STATE)),
                  table((S5_CHUNK, cw, 2 * S5_STATE)),
                  table((S5_SLAB // 2, n_lam, gw)), per_slab((S5_SLAB, n_b, gw))],
        out_specs=[_resident((n_tok, cw), lambda v: (0, v)),
                   per_slab((S5_SLAB, 1, gw)), per_slab((S5_SLAB, n_b, gw))],
        out_shape=[
            jax.ShapeDtypeStruct((n_tok, n_slabs * cw), F32),
            jax.ShapeDtypeStruct((n_slabs, S5_SLAB, 1, gw), F32),
            jax.ShapeDtypeStruct((n_slabs, S5_SLAB, n_b, gw), F32),
        ],
        scratch_shapes=[pltpu.VMEM((2, kw // 2, kw // 2), BF16),
                        pltpu.VMEM((2, kw // 2, sw // 2), BF16),
                        pltpu.VMEM((2, kw // 2, sw // 2), BF16),
                        pltpu.VMEM((S5_SLAB, n_rows, 2 * S5_STATE), F32)],
        compiler_params=pltpu.CompilerParams(dimension_semantics=("parallel",),
                                             vmem_limit_bytes=BIG_VMEM_LIMIT),
        name="s5",
    )(main, kt, pb, ct, lam, x0)


def _s5_tables(a_log_neg_re, a_im, log_dt, b_re, b_im, c_re, c_im, n_scan_lev):
    hp = lax.Precision.HIGHEST
    lam_re = -jnp.exp(a_log_neg_re.astype(F32))
    lam_im = a_im.astype(F32)
    dt = jnp.exp(log_dt.astype(F32))[..., None]

    def powers(jj):
        e = jj[None, None, :, None]
        mag = jnp.exp((lam_re * dt)[:, :, None, :] * e)
        ang = (lam_im * dt)[:, :, None, :] * e
        return mag * jnp.cos(ang), mag * jnp.sin(ang)

    pw_re, pw_im = powers(jnp.arange(S5_CHUNK + 1, dtype=F32))
    num_re, num_im = pw_re[:, :, 1] - 1.0, pw_im[:, :, 1]
    den = lam_re * lam_re + lam_im * lam_im
    zoh_re = ((num_re * lam_re + num_im * lam_im) / den)[..., None]
    zoh_im = ((num_im * lam_re - num_re * lam_im) / den)[..., None]
    b_re, b_im = b_re.astype(F32), b_im.astype(F32)
    bb_re = zoh_re * b_re - zoh_im * b_im
    bb_im = zoh_re * b_im + zoh_im * b_re
    c_re, c_im = c_re.astype(F32)[:, :, None], c_im.astype(F32)[:, :, None]
    pr, pi = pw_re[:, :, :, None, :], pw_im[:, :, :, None, :]
    cp_re = c_re * pr - c_im * pi
    cp_im = c_re * pi + c_im * pr
    kj = (jnp.einsum('lgjpn,lgnq->lgjpq', cp_re[:, :, :S5_CHUNK], bb_re, precision=hp)
          - jnp.einsum('lgjpn,lgnq->lgjpq', cp_im[:, :, :S5_CHUNK], bb_im, precision=hp))
    qr, qi = powers(jnp.asarray(np.arange(S5_CHUNK - 1, -1, -1), F32))
    qr, qi = qr[:, :, :, None, :], qi[:, :, :, None, :]
    bt_re = bb_re.transpose(0, 1, 3, 2)[:, :, None]
    bt_im = bb_im.transpose(0, 1, 3, 2)[:, :, None]
    bst = jnp.concatenate([qr * bt_re - qi * bt_im, qr * bt_im + qi * bt_re],
                          axis=-1)
    cst = jnp.concatenate([cp_re[:, :, 1:], -cp_im[:, :, 1:]], axis=-1)
    lp_re, lp_im = powers(S5_CHUNK * (2.0 ** jnp.arange(n_scan_lev, dtype=F32)))

    l, g = kj.shape[:2]
    v, sg = g // S5_SLAB, S5_SLAB
    cw = sg * S5_GROUP
    kt = kj.reshape(l, v, sg, S5_CHUNK, S5_GROUP, S5_GROUP)
    kt = kt.transpose(0, 1, 3, 5, 2, 4).reshape(l, v, S5_CHUNK, S5_GROUP, cw)
    pb = bst.reshape(l, v, sg, S5_CHUNK, S5_GROUP, 2 * S5_STATE)
    pb = pb.transpose(0, 1, 3, 2, 4, 5).reshape(l, v, S5_CHUNK, cw, 2 * S5_STATE)
    ct = cst.reshape(l, v, sg, S5_CHUNK, S5_GROUP, 2 * S5_STATE)
    ct = ct.transpose(0, 1, 3, 2, 4, 5).reshape(l, v, S5_CHUNK, cw, 2 * S5_STATE)

    def pair_lanes(a):
        a = a.reshape(l, v, sg // 2, 2, n_scan_lev, S5_STATE)
        return a.transpose(0, 1, 2, 4, 3, 5).reshape(l, v, sg // 2, n_scan_lev, 2 * S5_STATE)

    lam_big = jnp.stack([pair_lanes(lp_re), pair_lanes(lp_im)], axis=4)
    lam_big = lam_big.reshape(l, v, sg // 2, 2 * n_scan_lev, 2 * S5_STATE)
    return kt, pb, ct, lam_big


def _mix_kernel(*refs, n_src, tiles_a):
    x_refs = refs[:n_src]
    (gh_ref, gs_ref, oh_ref, y_ref, u_ref, d_ref, bglu_ref,
     wglu_ref, wbh_ref, wbs_ref, wout_ref, out_ref) = refs[n_src:]
    ys = jax.nn.gelu(y_ref[...] + d_ref[...] * u_ref[...])
    gate = jnp.dot(ys.astype(BF16), wglu_ref[...], preferred_element_type=F32) + bglu_ref[...]
    glu = (ys * jax.nn.sigmoid(gate)).astype(BF16)
    t_h = jnp.dot(oh_ref[...], wbh_ref[...], preferred_element_type=F32)
    t_s = jnp.dot(glu, wbs_ref[...], preferred_element_type=F32)
    mix = (jax.nn.sigmoid(gh_ref[...].astype(F32)) * t_h
           + jax.nn.sigmoid(gs_ref[...].astype(F32)) * t_s)
    x = x_refs[0][...]
    if n_src == 2:
        x = jnp.where(pl.program_id(0) < tiles_a, x, x_refs[1][...])
    out_ref[...] = x + jnp.dot(mix.astype(BF16), wout_ref[...], preferred_element_type=F32)


def _mix(xs, gates, o_h, y_s, main, u_col_block, d_skip, b_glu, w_glu, w_bh, w_bs, w_out, l, tm):
    t, ds = y_s.shape
    d = w_bh.shape[2]
    layer = lambda i: (l, 0, 0)
    const = lambda i: (0, 0)
    x_specs, tiles_a = _row_tile_specs(xs, tm)
    return pl.pallas_call(
        functools.partial(_mix_kernel, n_src=len(xs), tiles_a=tiles_a),
        grid=(t // tm,),
        in_specs=x_specs + [
            pl.BlockSpec((tm, d), lambda i: (i, 0)),
            pl.BlockSpec((tm, d), lambda i: (i, 1)),
            pl.BlockSpec((tm, ds), lambda i: (i, 0)),
            pl.BlockSpec((tm, ds), lambda i: (i, 0)),
            pl.BlockSpec((tm, ds), lambda i: (i, u_col_block)),
            pl.BlockSpec((1, ds), const),
            pl.BlockSpec((1, ds), const),
            _resident((None, ds, ds), layer),
            _resident((None, ds, d), layer),
            _resident((None, ds, d), layer),
            _resident((None, d, d), layer),
        ],
        out_specs=pl.BlockSpec((tm, d), lambda i: (i, 0)),
        out_shape=jax.ShapeDtypeStruct((t, d), F32),
        compiler_params=_cparams(("parallel",)),
        name="mix",
    )(*xs, gates, gates, o_h, y_s, main, d_skip.reshape(1, ds), b_glu.reshape(1, ds),
      w_glu, w_bh, w_bs, w_out)


def _rms(x, w):
    ms = jnp.mean(x * x, axis=-1, keepdims=True)
    return x * lax.rsqrt(ms + EPS) * w


def _ffn_kernel(x_ref, n2_ref, wg_ref, wu_ref, wd_ref, fn_ref, out_ref, h_scr, *, final_norm):
    j = pl.program_id(1)
    nj = pl.num_programs(1)

    def step(first):
        if first:
            h = _rms(x_ref[...], n2_ref[...]).astype(BF16)
            h_scr[...] = h
        else:
            h = h_scr[...]
        ga = jnp.dot(h, wg_ref[...], preferred_element_type=F32)
        up = jnp.dot(h, wu_ref[...], preferred_element_type=F32)
        act = (jax.nn.silu(ga) * up).astype(BF16)
        contrib = jnp.dot(act, wd_ref[...], preferred_element_type=F32)
        out_ref[...] = (x_ref[...] if first else out_ref[...]) + contrib

    pl.when(j == 0)(functools.partial(step, True))
    pl.when(j > 0)(functools.partial(step, False))

    if final_norm:
        @pl.when(j == nj - 1)
        def _():
            out_ref[...] = _rms(out_ref[...], fn_ref[...])


def _ffn(x, norm2, w_gate_up, w_down, l, fnorm, final_norm, tm, tf, row0=0, n_rows=None):
    d = x.shape[1]
    n_rows = x.shape[0] - row0 if n_rows is None else n_rows
    assert row0 % tm == 0 and n_rows % tm == 0
    tile0 = row0 // tm
    dff = w_down.shape[1]
    nj = dff // tf
    return pl.pallas_call(
        functools.partial(_ffn_kernel, final_norm=final_norm),
        grid=(n_rows // tm, nj),
        in_specs=[
            pl.BlockSpec((tm, d), lambda i, j: (tile0 + i, 0)),
            pl.BlockSpec((1, d), lambda i, j: (0, 0)),
            pl.BlockSpec((None, d, tf), lambda i, j: (l, 0, j)),
            pl.BlockSpec((None, d, tf), lambda i, j: (l, 0, nj + j)),
            pl.BlockSpec((None, tf, d), lambda i, j: (l, j, 0)),
            pl.BlockSpec((1, d), lambda i, j: (0, 0)),
        ],
        out_specs=pl.BlockSpec((tm, d), lambda i, j: (i, 0)),
        out_shape=jax.ShapeDtypeStruct((n_rows, d), F32),
        scratch_shapes=[pltpu.VMEM((tm, d), BF16)],
        compiler_params=_cparams(("parallel", "arbitrary")),
        name="ffn",
    )(x, norm2.reshape(1, d), w_gate_up, w_gate_up, w_down, fnorm.reshape(1, d))


def _pick(n, pref):
    t = pref
    while n % t:
        t //= 2
    return t


def _lower_bounds(lb_logits):
    p = jax.nn.softmax(lb_logits.astype(F32), axis=0)
    cs = jnp.cumsum(p, axis=0)
    return cs - cs[:1]


def kernel(x_prompt, x_sample, state_hgrn, state_s5_re, state_s5_im, lb_logits, norm1, w_in, hgrn_norm, w_bh, s5_a_log_neg_re, s5_a_im, s5_log_dt, s5_b_re, s5_b_im, s5_c_re, s5_c_im, s5_d, w_glu, b_glu, w_bs, w_out, norm2, w_gate_up, w_down, final_norm):
    depth = w_in.shape[0]
    bp, tp, d = x_prompt.shape
    bs, ts, _ = x_sample.shape
    assert bp == 1
    dh = lb_logits.shape[1]
    ds = s5_d.shape[1]
    n_groups = ds // S5_GROUP
    n_heads = dh // HEAD_DIM
    n_tok_p = bp * tp
    n_tok_s = bs * ts
    n_tok = n_tok_p + n_tok_s

    w_in_b = w_in.astype(BF16)
    w_bh_b = w_bh.astype(BF16)
    w_glu_b = w_glu.astype(BF16)
    w_bs_b = w_bs.astype(BF16)
    w_out_b = w_out.astype(BF16)
    w_gu_b = w_gate_up.astype(BF16)
    w_dn_b = w_down.astype(BF16)
    lbs = _lower_bounds(lb_logits)

    n_p = tp // S5_CHUNK
    n_cs = ts // S5_CHUNK
    s5_kt, s5_pb, s5_ct, s5_lam = _s5_tables(s5_a_log_neg_re, s5_a_im, s5_log_dt, s5_b_re,
                                             s5_b_im, s5_c_re, s5_c_im, int(math.log2(n_p)))
    n_slabs = n_groups // S5_SLAB
    n_pairs = S5_SLAB // 2

    def pack_state(re, im):
        nb = re.shape[0]
        x0 = jnp.stack([re.reshape(nb, n_slabs, n_pairs, 2 * S5_STATE),
                        im.reshape(nb, n_slabs, n_pairs, 2 * S5_STATE)], axis=3)
        return x0.transpose(1, 2, 3, 0, 4).reshape(n_slabs, S5_SLAB, nb, 2 * S5_STATE)

    def unpack_state(st):
        nb = st.shape[2]
        st = st.reshape(n_slabs, n_pairs, 2, nb, 2 * S5_STATE).transpose(2, 3, 0, 1, 4)
        st = st.reshape(2, nb, n_groups, S5_STATE)
        return st[0], st[1]

    tm_in = _pick(n_tok, 1024)
    tn_in = 1024
    tm_mix = _pick(n_tok, 256)
    tm_ffn = _pick(n_tok, 1024)
    tf = _pick(w_down.shape[1], 512)
    c_p = _pick(tp, 128)
    c_s = _pick(ts, 128)

    xs = (x_prompt.reshape(n_tok_p, d), x_sample.reshape(n_tok_s, d))
    if n_tok_p % tm_in or n_tok_s % tm_in or n_tok_p % tm_mix or n_tok_s % tm_mix:
        xs = (jnp.concatenate(xs, axis=0),)
    zero_h = jnp.zeros((1, bp, n_heads, HEAD_DIM, HEAD_DIM), F32)

    new_re_p, new_im_p, new_re_s, new_im_s = [], [], [], []
    hgrn_p = hgrn_s = None
    for l in range(depth):
        gates, main = _inproj(xs, norm1[l], w_in_b, l, 2 * d, (2 * dh, 4 * dh), tm_in, tn_in)

        o_h, hgrn_p = _hgrn(main, lbs[l], hgrn_norm[l], zero_h, None, hgrn_p,
                            layer=l, n_layers=depth, row_off=0, n_seq=bp, seq_len=tp, c=c_p,
                            nh=n_heads, hgrp=n_heads)
        o_h, hgrn_s = _hgrn(main, lbs[l], hgrn_norm[l], state_hgrn, o_h, hgrn_s,
                            layer=l, n_layers=depth, row_off=n_tok_p, n_seq=bs, seq_len=ts, c=c_s,
                            nh=n_heads, hgrp=n_heads)

        y, st_p, st_s = _s5(main, 4 * dh, s5_kt, s5_pb, s5_ct, s5_lam, l,
                            pack_state(state_s5_re[l], state_s5_im[l]),
                            n_p=n_p, n_b=bs, n_cs=n_cs)

        x = _mix(xs, gates, o_h, y, main, (4 * dh) // ds, s5_d[l], b_glu[l],
                 w_glu_b, w_bh_b, w_bs_b, w_out_b, l, tm_mix)
        last = l == depth - 1
        if last and n_tok_p % tm_ffn == 0 and n_tok_s % tm_ffn == 0:
            xs = (_ffn(x, norm2[l], w_gu_b, w_dn_b, l, final_norm, True, tm_ffn, tf, 0, n_tok_p),
                  _ffn(x, norm2[l], w_gu_b, w_dn_b, l, final_norm, True, tm_ffn, tf, n_tok_p,
                       n_tok_s))
        else:
            xs = (_ffn(x, norm2[l], w_gu_b, w_dn_b, l, final_norm, last, tm_ffn, tf),)

        re_p, im_p = unpack_state(st_p)
        re_s, im_s = unpack_state(st_s)
        new_re_p.append(re_p)
        new_im_p.append(im_p)
        new_re_s.append(re_s)
        new_im_s.append(im_s)

    if len(xs) == 1:
        xs = (xs[0][:n_tok_p], xs[0][n_tok_p:])
    y_prompt = xs[0].reshape(bp, tp, d)
    y_sample = xs[1].reshape(bs, ts, d)
    return (y_prompt, y_sample, hgrn_p, jnp.stack(new_re_p), jnp.stack(new_im_p),
            hgrn_s, jnp.stack(new_re_s), jnp.stack(new_im_s))
```

```python
import functools
import math

import jax
import jax.numpy as jnp
import numpy as np
from jax import lax
from jax.experimental import pallas as pl
from jax.experimental.pallas import tpu as pltpu

F32 = jnp.float32
BF16 = jnp.bfloat16

EPS = 1e-6
LOG2_E = 1.4426950408889634
HEAD_DIM = 128
S5_GROUP = 16
S5_STATE = 64
S5_CHUNK = 16
S5_SLAB = 8
S5_ROW_BLOCK = 272
MXU_TILE = 256
HGRN_STACK_ROWS = MXU_TILE
VMEM_LIMIT = 56 * 1024 * 1024
BIG_VMEM_LIMIT = 60 * 1024 * 1024


def _cparams(sem):
    return pltpu.CompilerParams(dimension_semantics=sem, vmem_limit_bytes=VMEM_LIMIT)


def _resident(shape, index_map):
    return pl.BlockSpec(shape, index_map, pipeline_mode=pl.Buffered(1))


def _row_tile_specs(xs, tm):
    d = xs[0].shape[1]
    tiles_a = xs[0].shape[0] // tm
    if len(xs) == 1:
        return [pl.BlockSpec((tm, d), lambda i, *_: (i, 0))], tiles_a
    assert len(xs) == 2 and xs[0].shape[0] % tm == 0 and xs[1].shape[0] % tm == 0
    return [
        pl.BlockSpec((tm, d), lambda i, *_: (jnp.minimum(i, tiles_a - 1), 0)),
        pl.BlockSpec((tm, d), lambda i, *_: (jnp.maximum(i - tiles_a, 0), 0),
                     pipeline_mode=pl.Buffered(1)),
    ], tiles_a


def _for_row_source(x_refs, tiles_a, pred, body):
    i = pl.program_id(0)
    if len(x_refs) == 1:
        conds = [pred]
    else:
        conds = [pred & (i < tiles_a), pred & (i >= tiles_a)]
    for cond, x_ref in zip(conds, x_refs):
        pl.when(cond)(functools.partial(body, x_ref))


def _inproj_kernel(*refs, n_src, tiles_a, n_main_tiles, silu_tiles, decay_tile):
    x_refs = refs[:n_src]
    nw_ref, lb_ref, w_ref, gate_ref, main_ref, h_scr = refs[n_src:]
    j = pl.program_id(1)

    def first_tile(x_ref):
        x = x_ref[...]
        ms = jnp.mean(x * x, axis=-1, keepdims=True)
        h = (x * lax.rsqrt(ms + EPS) * nw_ref[...]).astype(BF16)
        h_scr[...] = h
        main_ref[...] = jnp.dot(h, w_ref[...], preferred_element_type=F32)

    _for_row_source(x_refs, tiles_a, j == 0, first_tile)

    silu_lo, silu_hi = silu_tiles
    is_silu = (j >= silu_lo) & (j < silu_hi)
    is_decay = j == decay_tile

    @pl.when((j > 0) & (j < n_main_tiles) & jnp.logical_not(is_silu | is_decay))
    def _():
        main_ref[...] = jnp.dot(h_scr[...], w_ref[...], preferred_element_type=F32)

    @pl.when(is_decay)
    def _():
        z = jnp.dot(h_scr[...], w_ref[...], preferred_element_type=F32)
        lb = lb_ref[...]
        log_sig = jnp.minimum(z, 0.0) - jnp.log(1.0 + jnp.exp(-jnp.abs(z)))
        a0 = jnp.log(lb)
        a1 = jnp.log1p(-lb) + log_sig
        logf = jnp.maximum(a0, a1) + jnp.log(1.0 + jnp.exp(-jnp.abs(a0 - a1)))
        main_ref[...] = logf * LOG2_E

    @pl.when(is_silu)
    def _():
        main_ref[...] = jax.nn.silu(jnp.dot(h_scr[...], w_ref[...], preferred_element_type=F32))

    @pl.when(j >= n_main_tiles)
    def _():
        gate_ref[...] = jnp.dot(h_scr[...], w_ref[...],
                                preferred_element_type=F32).astype(BF16)


def _inproj(xs, nw, w, l, n_gate_cols, silu_cols, decay_col, lb, tm, tn):
    t = sum(x.shape[0] for x in xs)
    d = xs[0].shape[1]
    n = w.shape[2]
    n_tiles = n // tn
    n_main_tiles = (n - n_gate_cols) // tn
    assert n_main_tiles >= 1
    assert silu_cols[0] % tn == 0 and silu_cols[1] % tn == 0 and silu_cols[0] >= tn
    silu_tiles = (silu_cols[0] // tn, silu_cols[1] // tn)
    assert decay_col % tn == 0 and decay_col >= tn and lb.shape[-1] == tn
    x_specs, tiles_a = _row_tile_specs(xs, tm)
    return pl.pallas_call(
        functools.partial(_inproj_kernel, n_src=len(xs), tiles_a=tiles_a,
                          n_main_tiles=n_main_tiles, silu_tiles=silu_tiles,
                          decay_tile=decay_col // tn),
        grid=(t // tm, n_tiles),
        in_specs=x_specs + [
            pl.BlockSpec((1, d), lambda i, j: (0, 0)),
            pl.BlockSpec((1, tn), lambda i, j: (0, 0)),
            pl.BlockSpec((None, d, tn), lambda i, j: (l, 0, j)),
        ],
        out_specs=[
            pl.BlockSpec((tm, tn), lambda i, j: (i, jnp.maximum(j - n_main_tiles, 0))),
            pl.BlockSpec((tm, tn), lambda i, j: (i, jnp.minimum(j, n_main_tiles - 1))),
        ],
        out_shape=[
            jax.ShapeDtypeStruct((t, n_gate_cols), BF16),
            jax.ShapeDtypeStruct((t, n - n_gate_cols), F32),
        ],
        scratch_shapes=[pltpu.VMEM((tm, d), BF16)],
        compiler_params=_cparams(("parallel", "arbitrary")),
        name="inproj",
    )(*xs, nw.reshape(1, d), lb.reshape(1, tn), w)


def _hgrn_level_masks(c):
    n_lev = int(math.log2(c)) + 1
    gsz = max(1, min(n_lev, HGRN_STACK_ROWS // c))
    n_groups = -(-n_lev // gsz)
    r = gsz * c
    t = np.arange(c)[:, None]
    s = np.arange(c)[None, :]
    masks = np.zeros((n_groups, r, r), np.float32)
    for lev in range(n_lev):
        if lev == 0:
            m = (t == s)
        else:
            h = 1 << (lev - 1)
            m = ((t & h) != 0) & ((s & h) == 0) & ((t // (2 * h)) == (s // (2 * h)))
        g, i = divmod(lev, gsz)
        masks[g, i * c:(i + 1) * c, i * c:(i + 1) * c] = m
    return masks, n_lev, gsz, n_groups


def _hgrn_kernel(q_ref, f_ref, i_ref, g_ref, lb_ref, gain_ref, s0_ref, mask_ref, *rest,
                 c, nh, hgrp, n_lev, gsz, n_groups):
    o_ref, s_ref, st_scr = rest[-3:]
    ci = pl.program_id(2)
    nc = pl.num_programs(2)

    @pl.when(ci == 0)
    def _():
        st_scr[...] = s0_ref[0]

    row = lax.broadcasted_iota(jnp.int32, (c, HEAD_DIM), 0)
    nt = (((1,), (1,)), ((), ()))
    tn = (((0,), (0,)), ((), ()))

    def head_operands(h):
        sl = slice(h * HEAD_DIM, (h + 1) * HEAD_DIM)
        log2f = f_ref[:, sl]
        k = 1.0 - jnp.exp2(log2f)
        q = q_ref[:, sl]
        v = i_ref[:, sl]
        v_bf = v.astype(BF16)

        p = log2f
        tb = p
        a_lev = [q.astype(BF16)]
        b_lev = [k.astype(BF16)]
        for lev in range(1, n_lev):
            hs = 1 << (lev - 1)
            right = (row & hs) != 0
            e = jnp.exp2(jnp.where(right, p, tb - p))
            zl = (jnp.where(right, q, k) * e).astype(BF16)
            a_lev.append(zl)
            b_lev.append(zl)
            dn = pltpu.roll(tb, hs, 0)
            up = pltpu.roll(tb, c - hs, 0)
            p = p + jnp.where(right, dn, 0.0)
            tb = tb + jnp.where(right, dn, up)
        ops = []
        for g in range(n_groups):
            lo, hi = g * gsz, min((g + 1) * gsz, n_lev)
            n_in = hi - lo
            a_g = jnp.concatenate(a_lev[lo:hi], axis=0) if n_in > 1 else a_lev[lo]
            b_g = a_g if lo > 0 else (
                jnp.concatenate(b_lev[lo:hi], axis=0) if n_in > 1 else b_lev[lo])
            ops.append((a_g, b_g, n_in))
        q_in = (q * jnp.exp2(p)).astype(BF16)
        k_out = (k * jnp.exp2(tb - p)).astype(BF16)
        decay = jnp.exp2(jnp.broadcast_to(tb[0:1, :], (HEAD_DIM, HEAD_DIM)).T)
        return ops, v_bf, q_in, k_out, decay

    def fold_scores(g, sc, n_in):
        if c % HEAD_DIM == 0:
            return jnp.concatenate(
                [sc[i * c:(i + 1) * c, i * c:(i + 1) * c].astype(BF16)
                 * mask_ref[g, i * c:(i + 1) * c, i * c:(i + 1) * c]
                 for i in range(n_in)], axis=1)
        r = n_in * c
        sc = sc * mask_ref[g][:r, :r]
        fold = sc[0:c]
        for i in range(1, n_in):
            fold = fold + sc[i * c:(i + 1) * c]
        return fold.astype(BF16)

    for h0 in range(0, nh, hgrp):
        heads = range(h0, min(h0 + hgrp, nh))
        prep = {h: head_operands(h) for h in heads}
        scores = {h: [lax.dot_general(a_g, b_g, nt, preferred_element_type=F32)
                      for a_g, b_g, _ in prep[h][0]] for h in heads}
        upd = {h: lax.dot_general(prep[h][3], prep[h][1], tn, preferred_element_type=F32)
               for h in heads}
        for h in heads:
            ops, v_bf, q_in, _, decay = prep[h]
            st = st_scr[h]
            lhs = [q_in]
            rhs = [st.astype(BF16)]
            for g, (sc, (_, _, n_in)) in enumerate(zip(scores[h], ops)):
                lhs.append(fold_scores(g, sc, n_in))
                rhs.extend([v_bf] * n_in)
            o = jnp.dot(jnp.concatenate(lhs, axis=1), jnp.concatenate(rhs, axis=0),
                        preferred_element_type=F32)
            st_scr[h] = st * decay + upd[h]
            sl = slice(h * HEAD_DIM, (h + 1) * HEAD_DIM)
            ms = jnp.mean(o * o, axis=-1, keepdims=True)
            o = o * lax.rsqrt(ms + EPS) * gain_ref[:, sl] * g_ref[:, sl]
            o_ref[:, sl] = o.astype(o_ref.dtype)

    @pl.when(ci == nc - 1)
    def _():
        s_ref[0] = st_scr[...]


def _hgrn(main, lb, gain, s0, o_all, s_all, *, layer, n_layers, row_off, n_seq, seq_len, c, nh,
          hgrp):
    n_tok = main.shape[0]
    dh = lb.shape[-1]
    n_heads = dh // HEAD_DIM
    w = nh * HEAD_DIM
    n_hg = n_heads // nh
    n_chunks = seq_len // c
    rb0 = row_off // c
    masks, n_lev, gsz, n_groups = _hgrn_level_masks(c)
    r = masks.shape[-1]

    def col_spec(group):
        return pl.BlockSpec(
            (c, w), lambda b, hg, ci: (rb0 + b * n_chunks + ci, group * n_hg + hg))

    vec_spec = pl.BlockSpec((1, w), lambda b, hg, ci: (0, hg))
    s0_layer = min(layer, s0.shape[0] - 1)
    st_spec = pl.BlockSpec((None, 1, nh, HEAD_DIM, HEAD_DIM),
                           lambda b, hg, ci: (s0_layer, b, hg, 0, 0))
    operands = [main, main, main, main, lb.reshape(1, dh), gain.reshape(1, dh), s0,
                jnp.asarray(masks, BF16 if c % HEAD_DIM == 0 else F32)]
    in_specs = [col_spec(0), col_spec(1), col_spec(2), col_spec(3), vec_spec, vec_spec,
                st_spec, _resident((n_groups, r, r), lambda b, hg, ci: (0, 0, 0))]
    aliases = {}
    for out_idx, buf in enumerate((o_all, s_all)):
        if buf is not None:
            aliases[len(operands)] = out_idx
            operands.append(buf)
            in_specs.append(pl.BlockSpec(memory_space=pl.ANY))
    o, s_new = pl.pallas_call(
        functools.partial(_hgrn_kernel, c=c, nh=nh, hgrp=hgrp, n_lev=n_lev, gsz=gsz,
                          n_groups=n_groups),
        grid=(n_seq, n_hg, n_chunks),
        in_specs=in_specs,
        out_specs=[
            pl.BlockSpec((c, w), lambda b, hg, ci: (rb0 + b * n_chunks + ci, hg)),
            pl.BlockSpec((None, 1, nh, HEAD_DIM, HEAD_DIM),
                         lambda b, hg, ci: (layer, b, hg, 0, 0)),
        ],
        out_shape=[
            jax.ShapeDtypeStruct((n_tok, dh), BF16),
            jax.ShapeDtypeStruct((n_layers, n_seq, n_heads, HEAD_DIM, HEAD_DIM), F32),
        ],
        input_output_aliases=aliases,
        scratch_shapes=[pltpu.VMEM((nh, HEAD_DIM, HEAD_DIM), F32)],
        compiler_params=_cparams(("parallel", "parallel", "arbitrary")),
        name="hgrn_c%d" % c,
    )(*operands)
    return o, s_new


def _cmul(xr, xi, lr, li):
    return xr * lr - xi * li, xr * li + xi * lr


def _s5_kernel(u_ref, kt_ref, pb_ref, ct_ref, lam_ref, x0_ref, y_ref, sp_ref, ss_ref,
               m_scr, b_scr, ct_scr, z_scr, *, n_p, n_b, n_cs, rb):
    n_rows = n_p + n_b * n_cs
    sw = 2 * S5_STATE
    cw = S5_SLAB * S5_GROUP

    gi = lax.broadcasted_iota(jnp.int32, (cw, cw), 0) // S5_GROUP
    gj = lax.broadcasted_iota(jnp.int32, (cw, cw), 1) // S5_GROUP
    same_group = gi == gj
    hw = cw // 2
    hk = S5_CHUNK * hw
    hs = (S5_SLAB // 2) * sw
    low_lanes = lax.broadcasted_iota(jnp.int32, (1, cw), 1) < hw

    def split_halves(a, b):
        return (jnp.where(low_lanes, a, pltpu.roll(b, hw, 1)),
                jnp.where(low_lanes, pltpu.roll(a, hw, 1), b))

    lag_tiles = [
        jnp.where(same_group, jnp.concatenate([kt_ref[0, j]] * S5_SLAB, axis=0), 0.0)
        for j in range(S5_CHUNK)]
    zero_tile = jnp.zeros((cw, cw), F32)
    tpb = MXU_TILE // hw
    for t in range(0, S5_CHUNK, 2):
        for s in range((t // tpb + 1) * tpb):
            pair = split_halves(lag_tiles[t - s] if t >= s else zero_tile,
                                lag_tiles[t + 1 - s] if t + 1 >= s else zero_tile)
            for h in range(2):
                m_scr[h, s * hw:(s + 1) * hw, t * hw:(t + 2) * hw] = (
                    pair[h][h * hw:(h + 1) * hw, :].astype(BF16))
    rg = lax.broadcasted_iota(jnp.int32, (cw, S5_SLAB * sw), 0) // S5_GROUP
    lane = lax.broadcasted_iota(jnp.int32, (cw, S5_SLAB * sw), 1)
    plane, half = lane // sw, (lane % sw) // S5_STATE
    own_state = (plane // 2 == rg // 2) & (half == rg % 2)
    straight = plane % 2 == rg % 2

    def expand(tab):
        tiled = jnp.concatenate([tab] * S5_SLAB, axis=1)
        swapped = jnp.concatenate([pltpu.roll(tab, S5_STATE, 1)] * S5_SLAB, axis=1)
        return jnp.where(own_state, jnp.where(straight, tiled, swapped), 0.0).astype(BF16)

    for s in range(S5_CHUNK):
        eb, ec = expand(pb_ref[0, s]), expand(ct_ref[0, s])
        for h in range(2):
            b_scr[h, s * hw:(s + 1) * hw, :] = eb[h * hw:(h + 1) * hw, h * hs:(h + 1) * hs]
            ct_scr[h, s * hw:(s + 1) * hw, :] = ec[h * hw:(h + 1) * hw, h * hs:(h + 1) * hs]

    def chunk_operands(r0):
        tiles = [u_ref[pl.ds(S5_CHUNK * r0 + s, rb, stride=S5_CHUNK), :]
                 for s in range(S5_CHUNK)]
        pairs = [split_halves(tiles[s], tiles[s + 1]) for s in range(0, S5_CHUNK, 2)]
        return [jnp.concatenate([p[h] for p in pairs], axis=1).astype(BF16) for h in range(2)]

    planes_per_half = S5_SLAB // 2
    for r0 in range(0, n_rows, rb):
        for h, a in enumerate(chunk_operands(r0)):
            z = jnp.dot(a, b_scr[h], preferred_element_type=F32)
            for g in range(planes_per_half):
                z_scr[h * planes_per_half + g, r0:r0 + rb, :] = z[:, g * sw:(g + 1) * sw]

    rowi = lax.broadcasted_iota(jnp.int32, (n_p, sw), 0)

    def shift_rows(x, sh):
        return jnp.where(rowi >= sh, pltpu.roll(x, sh, 0), 0.0)

    def scan_pair(k, carry):
        pr, pi = 2 * k, 2 * k + 1
        lam_k = lam_ref[0, k]
        xr = z_scr[pr, 0:n_p, :]
        xi = z_scr[pi, 0:n_p, :]
        for lev in range(int(math.log2(n_p))):
            sh = 1 << lev
            lr, li = lam_k[2 * lev:2 * lev + 1], lam_k[2 * lev + 1:2 * lev + 2]
            if sh % 8 == 0 and sh < n_p:
                dr, di = _cmul(xr[:n_p - sh], xi[:n_p - sh], lr, li)
                xr = jnp.concatenate([xr[:sh], xr[sh:] + dr], axis=0)
                xi = jnp.concatenate([xi[:sh], xi[sh:] + di], axis=0)
            else:
                dr, di = _cmul(shift_rows(xr, sh), shift_rows(xi, sh), lr, li)
                xr, xi = xr + dr, xi + di
        z_scr[pr, 0:n_p, :] = shift_rows(xr, 1)
        z_scr[pi, 0:n_p, :] = shift_rows(xi, 1)
        sp_ref[0, pr] = xr[n_p - 1:n_p]
        sp_ref[0, pi] = xi[n_p - 1:n_p]

        sr, si = x0_ref[0, pr], x0_ref[0, pi]
        for ci in range(n_cs):
            rows = pl.ds(n_p + ci, n_b, stride=n_cs)
            zr, zi = z_scr[pr, rows, :], z_scr[pi, rows, :]
            z_scr[pr, rows, :] = sr
            z_scr[pi, rows, :] = si
            dr, di = _cmul(sr, si, lam_k[0:1], lam_k[1:2])
            sr, si = dr + zr, di + zi
        ss_ref[0, pr] = sr
        ss_ref[0, pi] = si
        return carry

    for k in range(S5_SLAB // 2):
        scan_pair(k, 0)

    for r0 in range(0, n_rows, rb):
        ops = chunk_operands(r0)
        x_prev = [jnp.concatenate(
            [z_scr[h * planes_per_half + g, r0:r0 + rb, :].astype(BF16)
             for g in range(planes_per_half)], axis=1) for h in range(2)]
        for j in range(S5_CHUNK // tpb):
            cols = slice(j * MXU_TILE, (j + 1) * MXU_TILE)
            y = [jnp.dot(ops[h][:, :(j + 1) * MXU_TILE], m_scr[h, 0:(j + 1) * MXU_TILE, cols],
                         preferred_element_type=F32)
                 + lax.dot_general(x_prev[h], ct_scr[h, cols, :], (((1,), (1,)), ((), ())),
                                   preferred_element_type=F32)
                 for h in range(2)]
            for ti in range(0, tpb, 2):
                lo = slice(ti * hw, (ti + 2) * hw)
                out_even, out_odd = split_halves(y[0][:, lo], y[1][:, lo])
                for dt, out in ((0, out_even), (1, out_odd)):
                    t = j * tpb + ti + dt
                    y_ref[pl.ds(S5_CHUNK * r0 + t, rb, stride=S5_CHUNK), :] = out


def _s5(main, u_col0, kt, pb, ct, lam, l, x0, *, n_p, n_b, n_cs):
    n_slabs = kt.shape[1]
    n_tok = main.shape[0]
    n_rows = n_p + n_b * n_cs
    assert n_rows * S5_CHUNK == n_tok
    n_lam = lam.shape[3]
    cw = S5_SLAB * S5_GROUP
    kw = S5_CHUNK * cw
    gw = 2 * S5_STATE
    sw = S5_SLAB * gw
    rb = max(r for r in range(16, S5_ROW_BLOCK + 1, 16) if n_rows % r == 0)
    cb0 = u_col0 // cw

    def per_slab(shape):
        nd = len(shape)
        return pl.BlockSpec((1,) + shape, lambda v: (v,) + (0,) * nd)

    def table(shape):
        nd = len(shape)
        return pl.BlockSpec((None, 1) + shape, lambda v: (l, v) + (0,) * nd)

    return pl.pallas_call(
        functools.partial(_s5_kernel, n_p=n_p, n_b=n_b, n_cs=n_cs, rb=rb),
        grid=(n_slabs,),
        in_specs=[pl.BlockSpec((n_tok, cw), lambda v: (0, cb0 + v)),
                  table((S5_CHUNK, S5_GROUP, cw)),
                  table((S5_CHUNK, cw, 2 * S5_STATE)),
                  table((S5_CHUNK, cw, 2 * S5_STATE)),
                  table((S5_SLAB // 2, n_lam, gw)), per_slab((S5_SLAB, n_b, gw))],
        out_specs=[_resident((n_tok, cw), lambda v: (0, v)),
                   per_slab((S5_SLAB, 1, gw)), per_slab((S5_SLAB, n_b, gw))],
        out_shape=[
            jax.ShapeDtypeStruct((n_tok, n_slabs * cw), F32),
            jax.ShapeDtypeStruct((n_slabs, S5_SLAB, 1, gw), F32),
            jax.ShapeDtypeStruct((n_slabs, S5_SLAB, n_b, gw), F32),
        ],
        scratch_shapes=[pltpu.VMEM((2, kw // 2, kw // 2), BF16),
                        pltpu.VMEM((2, kw // 2, sw // 2), BF16),
                        pltpu.VMEM((2, kw // 2, sw // 2), BF16),
                        pltpu.VMEM((S5_SLAB, n_rows, 2 * S5_STATE), F32)],
        compiler_params=pltpu.CompilerParams(dimension_semantics=("parallel",),
                                             vmem_limit_bytes=BIG_VMEM_LIMIT),
        name="s5",
    )(main, kt, pb, ct, lam, x0)


def _s5_tables(a_log_neg_re, a_im, log_dt, b_re, b_im, c_re, c_im, n_scan_lev):
    hp = lax.Precision.HIGHEST
    lam_re = -jnp.exp(a_log_neg_re.astype(F32))
    lam_im = a_im.astype(F32)
    dt = jnp.exp(log_dt.astype(F32))[..., None]

    def powers(jj):
        e = jj[None, None, :, None]
        mag = jnp.exp((lam_re * dt)[:, :, None, :] * e)
        ang = (lam_im * dt)[:, :, None, :] * e
        return mag * jnp.cos(ang), mag * jnp.sin(ang)

    pw_re, pw_im = powers(jnp.arange(S5_CHUNK + 1, dtype=F32))
    num_re, num_im = pw_re[:, :, 1] - 1.0, pw_im[:, :, 1]
    den = lam_re * lam_re + lam_im * lam_im
    zoh_re = ((num_re * lam_re + num_im * lam_im) / den)[..., None]
    zoh_im = ((num_im * lam_re - num_re * lam_im) / den)[..., None]
    b_re, b_im = b_re.astype(F32), b_im.astype(F32)
    bb_re = zoh_re * b_re - zoh_im * b_im
    bb_im = zoh_re * b_im + zoh_im * b_re
    c_re, c_im = c_re.astype(F32)[:, :, None], c_im.astype(F32)[:, :, None]
    pr, pi = pw_re[:, :, :, None, :], pw_im[:, :, :, None, :]
    cp_re = c_re * pr - c_im * pi
    cp_im = c_re * pi + c_im * pr
    kj = (jnp.einsum('lgjpn,lgnq->lgjpq', cp_re[:, :, :S5_CHUNK], bb_re, precision=hp)
          - jnp.einsum('lgjpn,lgnq->lgjpq', cp_im[:, :, :S5_CHUNK], bb_im, precision=hp))
    qr, qi = powers(jnp.asarray(np.arange(S5_CHUNK - 1, -1, -1), F32))
    qr, qi = qr[:, :, :, None, :], qi[:, :, :, None, :]
    bt_re = bb_re.transpose(0, 1, 3, 2)[:, :, None]
    bt_im = bb_im.transpose(0, 1, 3, 2)[:, :, None]
    bst = jnp.concatenate([qr * bt_re - qi * bt_im, qr * bt_im + qi * bt_re],
                          axis=-1)
    cst = jnp.concatenate([cp_re[:, :, 1:], -cp_im[:, :, 1:]], axis=-1)
    lp_re, lp_im = powers(S5_CHUNK * (2.0 ** jnp.arange(n_scan_lev, dtype=F32)))

    l, g = kj.shape[:2]
    v, sg = g // S5_SLAB, S5_SLAB
    cw = sg * S5_GROUP
    kt = kj.reshape(l, v, sg, S5_CHUNK, S5_GROUP, S5_GROUP)
    kt = kt.transpose(0, 1, 3, 5, 2, 4).reshape(l, v, S5_CHUNK, S5_GROUP, cw)
    pb = bst.reshape(l, v, sg, S5_CHUNK, S5_GROUP, 2 * S5_STATE)
    pb = pb.transpose(0, 1, 3, 2, 4, 5).reshape(l, v, S5_CHUNK, cw, 2 * S5_STATE)
    ct = cst.reshape(l, v, sg, S5_CHUNK, S5_GROUP, 2 * S5_STATE)
    ct = ct.transpose(0, 1, 3, 2, 4, 5).reshape(l, v, S5_CHUNK, cw, 2 * S5_STATE)

    def pair_lanes(a):
        a = a.reshape(l, v, sg // 2, 2, n_scan_lev, S5_STATE)
        return a.transpose(0, 1, 2, 4, 3, 5).reshape(l, v, sg // 2, n_scan_lev, 2 * S5_STATE)

    lam_big = jnp.stack([pair_lanes(lp_re), pair_lanes(lp_im)], axis=4)
    lam_big = lam_big.reshape(l, v, sg // 2, 2 * n_scan_lev, 2 * S5_STATE)
    return kt, pb, ct, lam_big


def _mix_kernel(*refs, n_src, tiles_a):
    x_refs = refs[:n_src]
    (gh_ref, gs_ref, oh_ref, y_ref, u_ref, d_ref, bglu_ref,
     wglu_ref, wbh_ref, wbs_ref, wout_ref, out_ref) = refs[n_src:]
    ys = jax.nn.gelu(y_ref[...] + d_ref[...] * u_ref[...])
    gate = jnp.dot(ys.astype(BF16), wglu_ref[...], preferred_element_type=F32) + bglu_ref[...]
    glu = (ys * jax.nn.sigmoid(gate)).astype(BF16)
    t_h = jnp.dot(oh_ref[...], wbh_ref[...], preferred_element_type=F32)
    t_s = jnp.dot(glu, wbs_ref[...], preferred_element_type=F32)
    mix = (jax.nn.sigmoid(gh_ref[...].astype(F32)) * t_h
           + jax.nn.sigmoid(gs_ref[...].astype(F32)) * t_s)
    x = x_refs[0][...]
    if n_src == 2:
        x = jnp.where(pl.program_id(0) < tiles_a, x, x_refs[1][...])
    out_ref[...] = x + jnp.dot(mix.astype(BF16), wout_ref[...], preferred_element_type=F32)


def _mix(xs, gates, o_h, y_s, main, u_col_block, d_skip, b_glu, w_glu, w_bh, w_bs, w_out, l, tm):
    t, ds = y_s.shape
    d = w_bh.shape[2]
    layer = lambda i: (l, 0, 0)
    const = lambda i: (0, 0)
    x_specs, tiles_a = _row_tile_specs(xs, tm)
    return pl.pallas_call(
        functools.partial(_mix_kernel, n_src=len(xs), tiles_a=tiles_a),
        grid=(t // tm,),
        in_specs=x_specs + [
            pl.BlockSpec((tm, d), lambda i: (i, 0)),
            pl.BlockSpec((tm, d), lambda i: (i, 1)),
            pl.BlockSpec((tm, ds), lambda i: (i, 0)),
            pl.BlockSpec((tm, ds), lambda i: (i, 0)),
            pl.BlockSpec((tm, ds), lambda i: (i, u_col_block)),
            pl.BlockSpec((1, ds), const),
            pl.BlockSpec((1, ds), const),
            _resident((None, ds, ds), layer),
            _resident((None, ds, d), layer),
            _resident((None, ds, d), layer),
            _resident((None, d, d), layer),
        ],
        out_specs=pl.BlockSpec((tm, d), lambda i: (i, 0)),
        out_shape=jax.ShapeDtypeStruct((t, d), F32),
        compiler_params=_cparams(("parallel",)),
        name="mix",
    )(*xs, gates, gates, o_h, y_s, main, d_skip.reshape(1, ds), b_glu.reshape(1, ds),
      w_glu, w_bh, w_bs, w_out)


def _rms(x, w):
    ms = jnp.mean(x * x, axis=-1, keepdims=True)
    return x * lax.rsqrt(ms + EPS) * w


def _ffn_kernel(x_ref, n2_ref, wg_ref, wu_ref, wd_ref, fn_ref, out_ref, h_scr, *, final_norm):
    j = pl.program_id(1)
    nj = pl.num_programs(1)

    def step(first):
        if first:
            h = _rms(x_ref[...], n2_ref[...]).astype(BF16)
            h_scr[...] = h
        else:
            h = h_scr[...]
        ga = jnp.dot(h, wg_ref[...], preferred_element_type=F32)
        up = jnp.dot(h, wu_ref[...], preferred_element_type=F32)
        act = (jax.nn.silu(ga) * up).astype(BF16)
        contrib = jnp.dot(act, wd_ref[...], preferred_element_type=F32)
        out_ref[...] = (x_ref[...] if first else out_ref[...]) + contrib

    pl.when(j == 0)(functools.partial(step, True))
    pl.when(j > 0)(functools.partial(step, False))

    if final_norm:
        @pl.when(j == nj - 1)
        def _():
            out_ref[...] = _rms(out_ref[...], fn_ref[...])


def _ffn(x, norm2, w_gate_up, w_down, l, fnorm, final_norm, tm, tf, row0=0, n_rows=None):
    d = x.shape[1]
    n_rows = x.shape[0] - row0 if n_rows is None else n_rows
    assert row0 % tm == 0 and n_rows % tm == 0
    tile0 = row0 // tm
    dff = w_down.shape[1]
    nj = dff // tf
    return pl.pallas_call(
        functools.partial(_ffn_kernel, final_norm=final_norm),
        grid=(n_rows // tm, nj),
        in_specs=[
            pl.BlockSpec((tm, d), lambda i, j: (tile0 + i, 0)),
            pl.BlockSpec((1, d), lambda i, j: (0, 0)),
            pl.BlockSpec((None, d, tf), lambda i, j: (l, 0, j)),
            pl.BlockSpec((None, d, tf), lambda i, j: (l, 0, nj + j)),
            pl.BlockSpec((None, tf, d), lambda i, j: (l, j, 0)),
            pl.BlockSpec((1, d), lambda i, j: (0, 0)),
        ],
        out_specs=pl.BlockSpec((tm, d), lambda i, j: (i, 0)),
        out_shape=jax.ShapeDtypeStruct((n_rows, d), F32),
        scratch_shapes=[pltpu.VMEM((tm, d), BF16)],
        compiler_params=_cparams(("parallel", "arbitrary")),
        name="ffn",
    )(x, norm2.reshape(1, d), w_gate_up, w_gate_up, w_down, fnorm.reshape(1, d))


def _pick(n, pref):
    t = pref
    while n % t:
        t //= 2
    return t


def _lower_bounds(lb_logits):
    p = jax.nn.softmax(lb_logits.astype(F32), axis=0)
    cs = jnp.cumsum(p, axis=0)
    return cs - cs[:1]


def kernel(x_prompt, x_sample, state_hgrn, state_s5_re, state_s5_im, lb_logits, norm1, w_in, hgrn_norm, w_bh, s5_a_log_neg_re, s5_a_im, s5_log_dt, s5_b_re, s5_b_im, s5_c_re, s5_c_im, s5_d, w_glu, b_glu, w_bs, w_out, norm2, w_gate_up, w_down, final_norm):
    depth = w_in.shape[0]
    bp, tp, d = x_prompt.shape
    bs, ts, _ = x_sample.shape
    assert bp == 1
    dh = lb_logits.shape[1]
    ds = s5_d.shape[1]
    n_groups = ds // S5_GROUP
    n_heads = dh // HEAD_DIM
    n_tok_p = bp * tp
    n_tok_s = bs * ts
    n_tok = n_tok_p + n_tok_s

    w_in_b = w_in.astype(BF16)
    w_bh_b = w_bh.astype(BF16)
    w_glu_b = w_glu.astype(BF16)
    w_bs_b = w_bs.astype(BF16)
    w_out_b = w_out.astype(BF16)
    w_gu_b = w_gate_up.astype(BF16)
    w_dn_b = w_down.astype(BF16)
    lbs = _lower_bounds(lb_logits)

    n_p = tp // S5_CHUNK
    n_cs = ts // S5_CHUNK
    s5_kt, s5_pb, s5_ct, s5_lam = _s5_tables(s5_a_log_neg_re, s5_a_im, s5_log_dt, s5_b_re,
                                             s5_b_im, s5_c_re, s5_c_im, int(math.log2(n_p)))
    n_slabs = n_groups // S5_SLAB
    n_pairs = S5_SLAB // 2

    def pack_state(re, im):
        nb = re.shape[0]
        x0 = jnp.stack([re.reshape(nb, n_slabs, n_pairs, 2 * S5_STATE),
                        im.reshape(nb, n_slabs, n_pairs, 2 * S5_STATE)], axis=3)
        return x0.transpose(1, 2, 3, 0, 4).reshape(n_slabs, S5_SLAB, nb, 2 * S5_STATE)

    def unpack_state(st):
        nb = st.shape[2]
        st = st.reshape(n_slabs, n_pairs, 2, nb, 2 * S5_STATE).transpose(2, 3, 0, 1, 4)
        st = st.reshape(2, nb, n_groups, S5_STATE)
        return st[0], st[1]

    tm_in = _pick(n_tok, 1024)
    tn_in = 1024
    tm_mix = _pick(n_tok, 256)
    tm_ffn = _pick(n_tok, 1024)
    tf = _pick(w_down.shape[1], 512)
    c_p = _pick(tp, 128)
    c_s = _pick(ts, 128)

    xs = (x_prompt.reshape(n_tok_p, d), x_sample.reshape(n_tok_s, d))
    if n_tok_p % tm_in or n_tok_s % tm_in or n_tok_p % tm_mix or n_tok_s % tm_mix:
        xs = (jnp.concatenate(xs, axis=0),)
    zero_h = jnp.zeros((1, bp, n_heads, HEAD_DIM, HEAD_DIM), F32)

    new_re_p, new_im_p, new_re_s, new_im_s = [], [], [], []
    hgrn_p = hgrn_s = None
    for l in range(depth):
        gates, main = _inproj(xs, norm1[l], w_in_b, l, 2 * d, (2 * dh, 4 * dh), dh, lbs[l],
                              tm_in, tn_in)

        o_h, hgrn_p = _hgrn(main, lbs[l], hgrn_norm[l], zero_h, None, hgrn_p,
                            layer=l, n_layers=depth, row_off=0, n_seq=bp, seq_len=tp, c=c_p,
                            nh=n_heads, hgrp=n_heads)
        o_h, hgrn_s = _hgrn(main, lbs[l], hgrn_norm[l], state_hgrn, o_h, hgrn_s,
                            layer=l, n_layers=depth, row_off=n_tok_p, n_seq=bs, seq_len=ts, c=c_s,
                            nh=n_heads, hgrp=n_heads)

        y, st_p, st_s = _s5(main, 4 * dh, s5_kt, s5_pb, s5_ct, s5_lam, l,
                            pack_state(state_s5_re[l], state_s5_im[l]),
                            n_p=n_p, n_b=bs, n_cs=n_cs)

        x = _mix(xs, gates, o_h, y, main, (4 * dh) // ds, s5_d[l], b_glu[l],
                 w_glu_b, w_bh_b, w_bs_b, w_out_b, l, tm_mix)
        last = l == depth - 1
        if last and n_tok_p % tm_ffn == 0 and n_tok_s % tm_ffn == 0:
            xs = (_ffn(x, norm2[l], w_gu_b, w_dn_b, l, final_norm, True, tm_ffn, tf, 0, n_tok_p),
                  _ffn(x, norm2[l], w_gu_b, w_dn_b, l, final_norm, True, tm_ffn, tf, n_tok_p,
                       n_tok_s))
        else:
            xs = (_ffn(x, norm2[l], w_gu_b, w_dn_b, l, final_norm, last, tm_ffn, tf),)

        re_p, im_p = unpack_state(st_p)
        re_s, im_s = unpack_state(st_s)
        new_re_p.append(re_p)
        new_im_p.append(im_p)
        new_re_s.append(re_s)
        new_im_s.append(im_s)

    if len(xs) == 1:
        xs = (xs[0][:n_tok_p], xs[0][n_tok_p:])
    y_prompt = xs[0].reshape(bp, tp, d)
    y_sample = xs[1].reshape(bs, ts, d)
    return (y_prompt, y_sample, hgrn_p, jnp.stack(new_re_p), jnp.stack(new_im_p),
            hgrn_s, jnp.stack(new_re_s), jnp.stack(new_im_s))
```

```python
import functools
import math

import jax
import jax.numpy as jnp
import numpy as np
from jax import lax
from jax.experimental import pallas as pl
from jax.experimental.pallas import tpu as pltpu

F32 = jnp.float32
BF16 = jnp.bfloat16

EPS = 1e-6
LOG2_E = 1.4426950408889634
HEAD_DIM = 128
S5_GROUP = 16
S5_STATE = 64
S5_CHUNK = 16
S5_SLAB = 8
S5_ROW_BLOCK = 272
MXU_TILE = 256
HGRN_STACK_ROWS = MXU_TILE
VMEM_LIMIT = 56 * 1024 * 1024
BIG_VMEM_LIMIT = 60 * 1024 * 1024


def _cparams(sem):
    return pltpu.CompilerParams(dimension_semantics=sem, vmem_limit_bytes=VMEM_LIMIT)


def _resident(shape, index_map):
    return pl.BlockSpec(shape, index_map, pipeline_mode=pl.Buffered(1))


def _row_tile_specs(xs, tm):
    d = xs[0].shape[1]
    tiles_a = xs[0].shape[0] // tm
    if len(xs) == 1:
        return [pl.BlockSpec((tm, d), lambda i, *_: (i, 0))], tiles_a
    assert len(xs) == 2 and xs[0].shape[0] % tm == 0 and xs[1].shape[0] % tm == 0
    return [
        pl.BlockSpec((tm, d), lambda i, *_: (jnp.minimum(i, tiles_a - 1), 0)),
        pl.BlockSpec((tm, d), lambda i, *_: (jnp.maximum(i - tiles_a, 0), 0),
                     pipeline_mode=pl.Buffered(1)),
    ], tiles_a


def _for_row_source(x_refs, tiles_a, pred, body):
    i = pl.program_id(0)
    if len(x_refs) == 1:
        conds = [pred]
    else:
        conds = [pred & (i < tiles_a), pred & (i >= tiles_a)]
    for cond, x_ref in zip(conds, x_refs):
        pl.when(cond)(functools.partial(body, x_ref))


def _inproj_kernel(*refs, n_src, tiles_a, n_main_tiles, silu_tiles, decay_tile):
    x_refs = refs[:n_src]
    nw_ref, lb_ref, w_ref, gate_ref, main_ref, h_scr = refs[n_src:]
    j = pl.program_id(1)

    def first_tile(x_ref):
        x = x_ref[...]
        ms = jnp.mean(x * x, axis=-1, keepdims=True)
        h = (x * lax.rsqrt(ms + EPS) * nw_ref[...]).astype(BF16)
        h_scr[...] = h
        main_ref[...] = jnp.dot(h, w_ref[...], preferred_element_type=F32)

    _for_row_source(x_refs, tiles_a, j == 0, first_tile)

    silu_lo, silu_hi = silu_tiles
    is_silu = (j >= silu_lo) & (j < silu_hi)
    is_decay = j == decay_tile

    @pl.when((j > 0) & (j < n_main_tiles) & jnp.logical_not(is_silu | is_decay))
    def _():
        main_ref[...] = jnp.dot(h_scr[...], w_ref[...], preferred_element_type=F32)

    @pl.when(is_decay)
    def _():
        z = jnp.dot(h_scr[...], w_ref[...], preferred_element_type=F32)
        lb = lb_ref[...]
        log_sig = jnp.minimum(z, 0.0) - jnp.log(1.0 + jnp.exp(-jnp.abs(z)))
        a0 = jnp.log(lb)
        a1 = jnp.log1p(-lb) + log_sig
        logf = jnp.maximum(a0, a1) + jnp.log(1.0 + jnp.exp(-jnp.abs(a0 - a1)))
        main_ref[...] = logf * LOG2_E

    @pl.when(is_silu)
    def _():
        main_ref[...] = jax.nn.silu(jnp.dot(h_scr[...], w_ref[...], preferred_element_type=F32))

    @pl.when(j >= n_main_tiles)
    def _():
        gate_ref[...] = jnp.dot(h_scr[...], w_ref[...],
                                preferred_element_type=F32).astype(BF16)


def _inproj(xs, nw, w, l, n_gate_cols, silu_cols, decay_col, lb, tm, tn):
    t = sum(x.shape[0] for x in xs)
    d = xs[0].shape[1]
    n = w.shape[2]
    n_tiles = n // tn
    n_main_tiles = (n - n_gate_cols) // tn
    assert n_main_tiles >= 1
    assert silu_cols[0] % tn == 0 and silu_cols[1] % tn == 0 and silu_cols[0] >= tn
    silu_tiles = (silu_cols[0] // tn, silu_cols[1] // tn)
    assert decay_col % tn == 0 and decay_col >= tn and lb.shape[-1] == tn
    x_specs, tiles_a = _row_tile_specs(xs, tm)
    return pl.pallas_call(
        functools.partial(_inproj_kernel, n_src=len(xs), tiles_a=tiles_a,
                          n_main_tiles=n_main_tiles, silu_tiles=silu_tiles,
                          decay_tile=decay_col // tn),
        grid=(t // tm, n_tiles),
        in_specs=x_specs + [
            pl.BlockSpec((1, d), lambda i, j: (0, 0)),
            pl.BlockSpec((1, tn), lambda i, j: (0, 0)),
            pl.BlockSpec((None, d, tn), lambda i, j: (l, 0, j)),
        ],
        out_specs=[
            pl.BlockSpec((tm, tn), lambda i, j: (i, jnp.maximum(j - n_main_tiles, 0))),
            pl.BlockSpec((tm, tn), lambda i, j: (i, jnp.minimum(j, n_main_tiles - 1))),
        ],
        out_shape=[
            jax.ShapeDtypeStruct((t, n_gate_cols), BF16),
            jax.ShapeDtypeStruct((t, n - n_gate_cols), F32),
        ],
        scratch_shapes=[pltpu.VMEM((tm, d), BF16)],
        compiler_params=_cparams(("parallel", "arbitrary")),
        name="inproj",
    )(*xs, nw.reshape(1, d), lb.reshape(1, tn), w)


def _hgrn_level_masks(c):
    n_lev = int(math.log2(c)) + 1
    gsz = max(1, min(n_lev, HGRN_STACK_ROWS // c))
    n_groups = -(-n_lev // gsz)
    r = gsz * c
    t = np.arange(c)[:, None]
    s = np.arange(c)[None, :]
    masks = np.zeros((n_groups, r, r), np.float32)
    for lev in range(n_lev):
        if lev == 0:
            m = (t == s)
        else:
            h = 1 << (lev - 1)
            m = ((t & h) != 0) & ((s & h) == 0) & ((t // (2 * h)) == (s // (2 * h)))
        g, i = divmod(lev, gsz)
        masks[g, i * c:(i + 1) * c, i * c:(i + 1) * c] = m
    return masks, n_lev, gsz, n_groups


def _hgrn_kernel(q_ref, f_ref, i_ref, g_ref, gain_ref, s0_ref, mask_ref, *rest,
                 c, nh, hgrp, n_lev, gsz, n_groups):
    o_ref, s_ref, st_scr = rest[-3:]
    ci = pl.program_id(2)
    nc = pl.num_programs(2)

    @pl.when(ci == 0)
    def _():
        st_scr[...] = s0_ref[0]

    row = lax.broadcasted_iota(jnp.int32, (c, HEAD_DIM), 0)
    nt = (((1,), (1,)), ((), ()))
    tn = (((0,), (0,)), ((), ()))

    def head_operands(h):
        sl = slice(h * HEAD_DIM, (h + 1) * HEAD_DIM)
        log2f = f_ref[:, sl]
        k = 1.0 - jnp.exp2(log2f)
        q = q_ref[:, sl]
        v = i_ref[:, sl]
        v_bf = v.astype(BF16)

        p = log2f
        tb = p
        a_lev = [q.astype(BF16)]
        b_lev = [k.astype(BF16)]
        for lev in range(1, n_lev):
            hs = 1 << (lev - 1)
            right = (row & hs) != 0
            e = jnp.exp2(jnp.where(right, p, tb - p))
            zl = (jnp.where(right, q, k) * e).astype(BF16)
            a_lev.append(zl)
            b_lev.append(zl)
            dn = pltpu.roll(tb, hs, 0)
            up = pltpu.roll(tb, c - hs, 0)
            p = p + jnp.where(right, dn, 0.0)
            tb = tb + jnp.where(right, dn, up)
        ops = []
        for g in range(n_groups):
            lo, hi = g * gsz, min((g + 1) * gsz, n_lev)
            n_in = hi - lo
            a_g = jnp.concatenate(a_lev[lo:hi], axis=0) if n_in > 1 else a_lev[lo]
            b_g = a_g if lo > 0 else (
                jnp.concatenate(b_lev[lo:hi], axis=0) if n_in > 1 else b_lev[lo])
            ops.append((a_g, b_g, n_in))
        q_in = (q * jnp.exp2(p)).astype(BF16)
        k_out = (k * jnp.exp2(tb - p)).astype(BF16)
        decay = jnp.exp2(jnp.broadcast_to(tb[0:1, :], (HEAD_DIM, HEAD_DIM)).T)
        return ops, v_bf, q_in, k_out, decay

    def fold_scores(g, sc, n_in):
        if c % HEAD_DIM == 0:
            return jnp.concatenate(
                [sc[i * c:(i + 1) * c, i * c:(i + 1) * c].astype(BF16)
                 * mask_ref[g, i * c:(i + 1) * c, i * c:(i + 1) * c]
                 for i in range(n_in)], axis=1)
        r = n_in * c
        sc = sc * mask_ref[g][:r, :r]
        fold = sc[0:c]
        for i in range(1, n_in):
            fold = fold + sc[i * c:(i + 1) * c]
        return fold.astype(BF16)

    for h0 in range(0, nh, hgrp):
        heads = range(h0, min(h0 + hgrp, nh))
        prep = {h: head_operands(h) for h in heads}
        scores = {h: [lax.dot_general(a_g, b_g, nt, preferred_element_type=F32)
                      for a_g, b_g, _ in prep[h][0]] for h in heads}
        upd = {h: lax.dot_general(prep[h][3], prep[h][1], tn, preferred_element_type=F32)
               for h in heads}
        for h in heads:
            ops, v_bf, q_in, _, decay = prep[h]
            st = st_scr[h]
            lhs = [q_in]
            rhs = [st.astype(BF16)]
            for g, (sc, (_, _, n_in)) in enumerate(zip(scores[h], ops)):
                lhs.append(fold_scores(g, sc, n_in))
                rhs.extend([v_bf] * n_in)
            o = jnp.dot(jnp.concatenate(lhs, axis=1), jnp.concatenate(rhs, axis=0),
                        preferred_element_type=F32)
            st_scr[h] = st * decay + upd[h]
            sl = slice(h * HEAD_DIM, (h + 1) * HEAD_DIM)
            ms = jnp.mean(o * o, axis=-1, keepdims=True)
            o = o * lax.rsqrt(ms + EPS) * gain_ref[:, sl] * g_ref[:, sl]
            o_ref[:, sl] = o.astype(o_ref.dtype)

    @pl.when(ci == nc - 1)
    def _():
        s_ref[0] = st_scr[...]


def _hgrn(main, gain, s0, o_all, s_all, *, layer, n_layers, row_off, n_seq, seq_len, c, nh,
          hgrp):
    n_tok = main.shape[0]
    dh = gain.shape[-1]
    n_heads = dh // HEAD_DIM
    w = nh * HEAD_DIM
    n_hg = n_heads // nh
    n_chunks = seq_len // c
    rb0 = row_off // c
    masks, n_lev, gsz, n_groups = _hgrn_level_masks(c)
    r = masks.shape[-1]

    def col_spec(group):
        return pl.BlockSpec(
            (c, w), lambda b, hg, ci: (rb0 + b * n_chunks + ci, group * n_hg + hg))

    vec_spec = pl.BlockSpec((1, w), lambda b, hg, ci: (0, hg))
    s0_layer = min(layer, s0.shape[0] - 1)
    st_spec = pl.BlockSpec((None, 1, nh, HEAD_DIM, HEAD_DIM),
                           lambda b, hg, ci: (s0_layer, b, hg, 0, 0))
    operands = [main, main, main, main, gain.reshape(1, dh), s0,
                jnp.asarray(masks, BF16 if c % HEAD_DIM == 0 else F32)]
    in_specs = [col_spec(0), col_spec(1), col_spec(2), col_spec(3), vec_spec,
                st_spec, _resident((n_groups, r, r), lambda b, hg, ci: (0, 0, 0))]
    aliases = {}
    for out_idx, buf in enumerate((o_all, s_all)):
        if buf is not None:
            aliases[len(operands)] = out_idx
            operands.append(buf)
            in_specs.append(pl.BlockSpec(memory_space=pl.ANY))
    o, s_new = pl.pallas_call(
        functools.partial(_hgrn_kernel, c=c, nh=nh, hgrp=hgrp, n_lev=n_lev, gsz=gsz,
                          n_groups=n_groups),
        grid=(n_seq, n_hg, n_chunks),
        in_specs=in_specs,
        out_specs=[
            pl.BlockSpec((c, w), lambda b, hg, ci: (rb0 + b * n_chunks + ci, hg)),
            pl.BlockSpec((None, 1, nh, HEAD_DIM, HEAD_DIM),
                         lambda b, hg, ci: (layer, b, hg, 0, 0)),
        ],
        out_shape=[
            jax.ShapeDtypeStruct((n_tok, dh), BF16),
            jax.ShapeDtypeStruct((n_layers, n_seq, n_heads, HEAD_DIM, HEAD_DIM), F32),
        ],
        input_output_aliases=aliases,
        scratch_shapes=[pltpu.VMEM((nh, HEAD_DIM, HEAD_DIM), F32)],
        compiler_params=_cparams(("parallel", "parallel", "arbitrary")),
        name="hgrn_c%d" % c,
    )(*operands)
    return o, s_new


def _cmul(xr, xi, lr, li):
    return xr * lr - xi * li, xr * li + xi * lr


def _s5_kernel(u_ref, kt_ref, pb_ref, ct_ref, lam_ref, x0_ref, y_ref, sp_ref, ss_ref,
               m_scr, b_scr, ct_scr, z_scr, *, n_p, n_b, n_cs, rb):
    n_rows = n_p + n_b * n_cs
    sw = 2 * S5_STATE
    cw = S5_SLAB * S5_GROUP

    gi = lax.broadcasted_iota(jnp.int32, (cw, cw), 0) // S5_GROUP
    gj = lax.broadcasted_iota(jnp.int32, (cw, cw), 1) // S5_GROUP
    same_group = gi == gj
    hw = cw // 2
    hs = (S5_SLAB // 2) * sw
    low_lanes = lax.broadcasted_iota(jnp.int32, (1, cw), 1) < hw

    def split_halves(a, b):
        return (jnp.where(low_lanes, a, pltpu.roll(b, hw, 1)),
                jnp.where(low_lanes, pltpu.roll(a, hw, 1), b))

    lag_tiles = [
        jnp.where(same_group, jnp.concatenate([kt_ref[0, j]] * S5_SLAB, axis=0), 0.0)
        for j in range(S5_CHUNK)]
    zero_tile = jnp.zeros((cw, cw), F32)
    tpb = MXU_TILE // hw
    for t in range(0, S5_CHUNK, 2):
        for s in range((t // tpb + 1) * tpb):
            pair = split_halves(lag_tiles[t - s] if t >= s else zero_tile,
                                lag_tiles[t + 1 - s] if t + 1 >= s else zero_tile)
            for h in range(2):
                m_scr[h, s * hw:(s + 1) * hw, t * hw:(t + 2) * hw] = (
                    pair[h][h * hw:(h + 1) * hw, :].astype(BF16))
    rg = lax.broadcasted_iota(jnp.int32, (cw, S5_SLAB * sw), 0) // S5_GROUP
    lane = lax.broadcasted_iota(jnp.int32, (cw, S5_SLAB * sw), 1)
    plane, half = lane // sw, (lane % sw) // S5_STATE
    own_state = (plane // 2 == rg // 2) & (half == rg % 2)
    straight = plane % 2 == rg % 2

    def expand(tab):
        tiled = jnp.concatenate([tab] * S5_SLAB, axis=1)
        swapped = jnp.concatenate([pltpu.roll(tab, S5_STATE, 1)] * S5_SLAB, axis=1)
        return jnp.where(own_state, jnp.where(straight, tiled, swapped), 0.0).astype(BF16)

    for s in range(S5_CHUNK):
        eb, ec = expand(pb_ref[0, s]), expand(ct_ref[0, s])
        for h in range(2):
            b_scr[h, s * hw:(s + 1) * hw, :] = eb[h * hw:(h + 1) * hw, h * hs:(h + 1) * hs]
            ct_scr[h, s * hw:(s + 1) * hw, :] = ec[h * hw:(h + 1) * hw, h * hs:(h + 1) * hs]

    def chunk_operands(r0):
        tiles = [u_ref[pl.ds(S5_CHUNK * r0 + s, rb, stride=S5_CHUNK), :]
                 for s in range(S5_CHUNK)]
        pairs = [split_halves(tiles[s], tiles[s + 1]) for s in range(0, S5_CHUNK, 2)]
        return [jnp.concatenate([p[h] for p in pairs], axis=1).astype(BF16) for h in range(2)]

    planes_per_half = S5_SLAB // 2
    for r0 in range(0, n_rows, rb):
        for h, a in enumerate(chunk_operands(r0)):
            z = jnp.dot(a, b_scr[h], preferred_element_type=F32)
            for g in range(planes_per_half):
                z_scr[h * planes_per_half + g, r0:r0 + rb, :] = z[:, g * sw:(g + 1) * sw]

    rowi = lax.broadcasted_iota(jnp.int32, (n_p, sw), 0)

    def shift_rows(x, sh):
        return jnp.where(rowi >= sh, pltpu.roll(x, sh, 0), 0.0)

    def scan_pair(k, carry):
        pr, pi = 2 * k, 2 * k + 1
        lam_k = lam_ref[0, k]
        xr = z_scr[pr, 0:n_p, :]
        xi = z_scr[pi, 0:n_p, :]
        for lev in range(int(math.log2(n_p))):
            sh = 1 << lev
            lr, li = lam_k[2 * lev:2 * lev + 1], lam_k[2 * lev + 1:2 * lev + 2]
            if sh % 8 == 0 and sh < n_p:
                dr, di = _cmul(xr[:n_p - sh], xi[:n_p - sh], lr, li)
                xr = jnp.concatenate([xr[:sh], xr[sh:] + dr], axis=0)
                xi = jnp.concatenate([xi[:sh], xi[sh:] + di], axis=0)
            else:
                dr, di = _cmul(shift_rows(xr, sh), shift_rows(xi, sh), lr, li)
                xr, xi = xr + dr, xi + di
        z_scr[pr, 0:n_p, :] = shift_rows(xr, 1)
        z_scr[pi, 0:n_p, :] = shift_rows(xi, 1)
        sp_ref[0, pr] = xr[n_p - 1:n_p]
        sp_ref[0, pi] = xi[n_p - 1:n_p]

        sr, si = x0_ref[0, pr], x0_ref[0, pi]
        for ci in range(n_cs):
            rows = pl.ds(n_p + ci, n_b, stride=n_cs)
            zr, zi = z_scr[pr, rows, :], z_scr[pi, rows, :]
            z_scr[pr, rows, :] = sr
            z_scr[pi, rows, :] = si
            dr, di = _cmul(sr, si, lam_k[0:1], lam_k[1:2])
            sr, si = dr + zr, di + zi
        ss_ref[0, pr] = sr
        ss_ref[0, pi] = si
        return carry

    for k in range(S5_SLAB // 2):
        scan_pair(k, 0)

    for r0 in range(0, n_rows, rb):
        ops = chunk_operands(r0)
        x_prev = [jnp.concatenate(
            [z_scr[h * planes_per_half + g, r0:r0 + rb, :].astype(BF16)
             for g in range(planes_per_half)], axis=1) for h in range(2)]
        for j in range(S5_CHUNK // tpb):
            cols = slice(j * MXU_TILE, (j + 1) * MXU_TILE)
            y = [jnp.dot(ops[h][:, :(j + 1) * MXU_TILE], m_scr[h, 0:(j + 1) * MXU_TILE, cols],
                         preferred_element_type=F32)
                 + lax.dot_general(x_prev[h], ct_scr[h, cols, :], (((1,), (1,)), ((), ())),
                                   preferred_element_type=F32)
                 for h in range(2)]
            for ti in range(0, tpb, 2):
                lo = slice(ti * hw, (ti + 2) * hw)
                out_even, out_odd = split_halves(y[0][:, lo], y[1][:, lo])
                for dt, out in ((0, out_even), (1, out_odd)):
                    t = j * tpb + ti + dt
                    y_ref[pl.ds(S5_CHUNK * r0 + t, rb, stride=S5_CHUNK), :] = out


def _s5(main, u_col0, kt, pb, ct, lam, l, x0, *, n_p, n_b, n_cs):
    n_slabs = kt.shape[1]
    n_tok = main.shape[0]
    n_rows = n_p + n_b * n_cs
    assert n_rows * S5_CHUNK == n_tok
    n_lam = lam.shape[3]
    cw = S5_SLAB * S5_GROUP
    kw = S5_CHUNK * cw
    gw = 2 * S5_STATE
    sw = S5_SLAB * gw
    rb = max(r for r in range(16, S5_ROW_BLOCK + 1, 16) if n_rows % r == 0)
    cb0 = u_col0 // cw

    def per_slab(shape):
        nd = len(shape)
        return pl.BlockSpec((1,) + shape, lambda v: (v,) + (0,) * nd)

    def table(shape):
        nd = len(shape)
        return pl.BlockSpec((None, 1) + shape, lambda v: (l, v) + (0,) * nd)

    return pl.pallas_call(
        functools.partial(_s5_kernel, n_p=n_p, n_b=n_b, n_cs=n_cs, rb=rb),
        grid=(n_slabs,),
        in_specs=[pl.BlockSpec((n_tok, cw), lambda v: (0, cb0 + v)),
                  table((S5_CHUNK, S5_GROUP, cw)),
                  table((S5_CHUNK, cw, 2 * S5_STATE)),
                  table((S5_CHUNK, cw, 2 * S5_STATE)),
                  table((S5_SLAB // 2, n_lam, gw)), per_slab((S5_SLAB, n_b, gw))],
        out_specs=[_resident((n_tok, cw), lambda v: (0, v)),
                   per_slab((S5_SLAB, 1, gw)), per_slab((S5_SLAB, n_b, gw))],
        out_shape=[
            jax.ShapeDtypeStruct((n_tok, n_slabs * cw), F32),
            jax.ShapeDtypeStruct((n_slabs, S5_SLAB, 1, gw), F32),
            jax.ShapeDtypeStruct((n_slabs, S5_SLAB, n_b, gw), F32),
        ],
        scratch_shapes=[pltpu.VMEM((2, kw // 2, kw // 2), BF16),
                        pltpu.VMEM((2, kw // 2, sw // 2), BF16),
                        pltpu.VMEM((2, kw // 2, sw // 2), BF16),
                        pltpu.VMEM((S5_SLAB, n_rows, 2 * S5_STATE), F32)],
        compiler_params=pltpu.CompilerParams(dimension_semantics=("parallel",),
                                             vmem_limit_bytes=BIG_VMEM_LIMIT),
        name="s5",
    )(main, kt, pb, ct, lam, x0)


def _s5_tables(a_log_neg_re, a_im, log_dt, b_re, b_im, c_re, c_im, n_scan_lev):
    hp = lax.Precision.HIGHEST
    lam_re = -jnp.exp(a_log_neg_re.astype(F32))
    lam_im = a_im.astype(F32)
    dt = jnp.exp(log_dt.astype(F32))[..., None]

    def powers(jj):
        e = jj[None, None, :, None]
        mag = jnp.exp((lam_re * dt)[:, :, None, :] * e)
        ang = (lam_im * dt)[:, :, None, :] * e
        return mag * jnp.cos(ang), mag * jnp.sin(ang)

    pw_re, pw_im = powers(jnp.arange(S5_CHUNK + 1, dtype=F32))
    num_re, num_im = pw_re[:, :, 1] - 1.0, pw_im[:, :, 1]
    den = lam_re * lam_re + lam_im * lam_im
    zoh_re = ((num_re * lam_re + num_im * lam_im) / den)[..., None]
    zoh_im = ((num_im * lam_re - num_re * lam_im) / den)[..., None]
    b_re, b_im = b_re.astype(F32), b_im.astype(F32)
    bb_re = zoh_re * b_re - zoh_im * b_im
    bb_im = zoh_re * b_im + zoh_im * b_re
    c_re, c_im = c_re.astype(F32)[:, :, None], c_im.astype(F32)[:, :, None]
    pr, pi = pw_re[:, :, :, None, :], pw_im[:, :, :, None, :]
    cp_re = c_re * pr - c_im * pi
    cp_im = c_re * pi + c_im * pr
    kj = (jnp.einsum('lgjpn,lgnq->lgjpq', cp_re[:, :, :S5_CHUNK], bb_re, precision=hp)
          - jnp.einsum('lgjpn,lgnq->lgjpq', cp_im[:, :, :S5_CHUNK], bb_im, precision=hp))
    qr, qi = powers(jnp.asarray(np.arange(S5_CHUNK - 1, -1, -1), F32))
    qr, qi = qr[:, :, :, None, :], qi[:, :, :, None, :]
    bt_re = bb_re.transpose(0, 1, 3, 2)[:, :, None]
    bt_im = bb_im.transpose(0, 1, 3, 2)[:, :, None]
    bst = jnp.concatenate([qr * bt_re - qi * bt_im, qr * bt_im + qi * bt_re],
                          axis=-1)
    cst = jnp.concatenate([cp_re[:, :, 1:], -cp_im[:, :, 1:]], axis=-1)
    lp_re, lp_im = powers(S5_CHUNK * (2.0 ** jnp.arange(n_scan_lev, dtype=F32)))

    l, g = kj.shape[:2]
    v, sg = g // S5_SLAB, S5_SLAB
    cw = sg * S5_GROUP
    kt = kj.reshape(l, v, sg, S5_CHUNK, S5_GROUP, S5_GROUP)
    kt = kt.transpose(0, 1, 3, 5, 2, 4).reshape(l, v, S5_CHUNK, S5_GROUP, cw)
    pb = bst.reshape(l, v, sg, S5_CHUNK, S5_GROUP, 2 * S5_STATE)
    pb = pb.transpose(0, 1, 3, 2, 4, 5).reshape(l, v, S5_CHUNK, cw, 2 * S5_STATE)
    ct = cst.reshape(l, v, sg, S5_CHUNK, S5_GROUP, 2 * S5_STATE)
    ct = ct.transpose(0, 1, 3, 2, 4, 5).reshape(l, v, S5_CHUNK, cw, 2 * S5_STATE)

    def pair_lanes(a):
        a = a.reshape(l, v, sg // 2, 2, n_scan_lev, S5_STATE)
        return a.transpose(0, 1, 2, 4, 3, 5).reshape(l, v, sg // 2, n_scan_lev, 2 * S5_STATE)

    lam_big = jnp.stack([pair_lanes(lp_re), pair_lanes(lp_im)], axis=4)
    lam_big = lam_big.reshape(l, v, sg // 2, 2 * n_scan_lev, 2 * S5_STATE)
    return kt, pb, ct, lam_big


def _mix_kernel(*refs, n_src, tiles_a):
    x_refs = refs[:n_src]
    (gh_ref, gs_ref, oh_ref, y_ref, u_ref, d_ref, bglu_ref,
     wglu_ref, wbh_ref, wbs_ref, wout_ref, out_ref) = refs[n_src:]
    ys = jax.nn.gelu(y_ref[...] + d_ref[...] * u_ref[...])
    gate = jnp.dot(ys.astype(BF16), wglu_ref[...], preferred_element_type=F32) + bglu_ref[...]
    glu = (ys * jax.nn.sigmoid(gate)).astype(BF16)
    t_h = jnp.dot(oh_ref[...], wbh_ref[...], preferred_element_type=F32)
    t_s = jnp.dot(glu, wbs_ref[...], preferred_element_type=F32)
    mix = (jax.nn.sigmoid(gh_ref[...].astype(F32)) * t_h
           + jax.nn.sigmoid(gs_ref[...].astype(F32)) * t_s)
    x = x_refs[0][...]
    if n_src == 2:
        x = jnp.where(pl.program_id(0) < tiles_a, x, x_refs[1][...])
    out_ref[...] = x + jnp.dot(mix.astype(BF16), wout_ref[...], preferred_element_type=F32)


def _mix(xs, gates, o_h, y_s, main, u_col_block, d_skip, b_glu, w_glu, w_bh, w_bs, w_out, l, tm):
    t, ds = y_s.shape
    d = w_bh.shape[2]
    layer = lambda i: (l, 0, 0)
    const = lambda i: (0, 0)
    x_specs, tiles_a = _row_tile_specs(xs, tm)
    return pl.pallas_call(
        functools.partial(_mix_kernel, n_src=len(xs), tiles_a=tiles_a),
        grid=(t // tm,),
        in_specs=x_specs + [
            pl.BlockSpec((tm, d), lambda i: (i, 0)),
            pl.BlockSpec((tm, d), lambda i: (i, 1)),
            pl.BlockSpec((tm, ds), lambda i: (i, 0)),
            pl.BlockSpec((tm, ds), lambda i: (i, 0)),
            pl.BlockSpec((tm, ds), lambda i: (i, u_col_block)),
            pl.BlockSpec((1, ds), const),
            pl.BlockSpec((1, ds), const),
            _resident((None, ds, ds), layer),
            _resident((None, ds, d), layer),
            _resident((None, ds, d), layer),
            _resident((None, d, d), layer),
        ],
        out_specs=pl.BlockSpec((tm, d), lambda i: (i, 0)),
        out_shape=jax.ShapeDtypeStruct((t, d), F32),
        compiler_params=_cparams(("parallel",)),
        name="mix",
    )(*xs, gates, gates, o_h, y_s, main, d_skip.reshape(1, ds), b_glu.reshape(1, ds),
      w_glu, w_bh, w_bs, w_out)


def _rms(x, w):
    ms = jnp.mean(x * x, axis=-1, keepdims=True)
    return x * lax.rsqrt(ms + EPS) * w


def _ffn_kernel(x_ref, n2_ref, wg_ref, wu_ref, wd_ref, fn_ref, out_ref, h_scr, *, final_norm):
    j = pl.program_id(1)
    nj = pl.num_programs(1)

    def step(first):
        if first:
            h = _rms(x_ref[...], n2_ref[...]).astype(BF16)
            h_scr[...] = h
        else:
            h = h_scr[...]
        ga = jnp.dot(h, wg_ref[...], preferred_element_type=F32)
        up = jnp.dot(h, wu_ref[...], preferred_element_type=F32)
        act = (jax.nn.silu(ga) * up).astype(BF16)
        contrib = jnp.dot(act, wd_ref[...], preferred_element_type=F32)
        out_ref[...] = (x_ref[...] if first else out_ref[...]) + contrib

    pl.when(j == 0)(functools.partial(step, True))
    pl.when(j > 0)(functools.partial(step, False))

    if final_norm:
        @pl.when(j == nj - 1)
        def _():
            out_ref[...] = _rms(out_ref[...], fn_ref[...])


def _ffn(x, norm2, w_gate_up, w_down, l, fnorm, final_norm, tm, tf, row0=0, n_rows=None):
    d = x.shape[1]
    n_rows = x.shape[0] - row0 if n_rows is None else n_rows
    assert row0 % tm == 0 and n_rows % tm == 0
    tile0 = row0 // tm
    dff = w_down.shape[1]
    nj = dff // tf
    return pl.pallas_call(
        functools.partial(_ffn_kernel, final_norm=final_norm),
        grid=(n_rows // tm, nj),
        in_specs=[
            pl.BlockSpec((tm, d), lambda i, j: (tile0 + i, 0)),
            pl.BlockSpec((1, d), lambda i, j: (0, 0)),
            pl.BlockSpec((None, d, tf), lambda i, j: (l, 0, j)),
            pl.BlockSpec((None, d, tf), lambda i, j: (l, 0, nj + j)),
            pl.BlockSpec((None, tf, d), lambda i, j: (l, j, 0)),
            pl.BlockSpec((1, d), lambda i, j: (0, 0)),
        ],
        out_specs=pl.BlockSpec((tm, d), lambda i, j: (i, 0)),
        out_shape=jax.ShapeDtypeStruct((n_rows, d), F32),
        scratch_shapes=[pltpu.VMEM((tm, d), BF16)],
        compiler_params=_cparams(("parallel", "arbitrary")),
        name="ffn",
    )(x, norm2.reshape(1, d), w_gate_up, w_gate_up, w_down, fnorm.reshape(1, d))


def _pick(n, pref):
    t = pref
    while n % t:
        t //= 2
    return t


def _lower_bounds(lb_logits):
    p = jax.nn.softmax(lb_logits.astype(F32), axis=0)
    cs = jnp.cumsum(p, axis=0)
    return cs - cs[:1]


def kernel(x_prompt, x_sample, state_hgrn, state_s5_re, state_s5_im, lb_logits, norm1, w_in, hgrn_norm, w_bh, s5_a_log_neg_re, s5_a_im, s5_log_dt, s5_b_re, s5_b_im, s5_c_re, s5_c_im, s5_d, w_glu, b_glu, w_bs, w_out, norm2, w_gate_up, w_down, final_norm):
    depth = w_in.shape[0]
    bp, tp, d = x_prompt.shape
    bs, ts, _ = x_sample.shape
    assert bp == 1
    dh = lb_logits.shape[1]
    ds = s5_d.shape[1]
    n_groups = ds // S5_GROUP
    n_heads = dh // HEAD_DIM
    n_tok_p = bp * tp
    n_tok_s = bs * ts
    n_tok = n_tok_p + n_tok_s

    w_in_b = w_in.astype(BF16)
    w_bh_b = w_bh.astype(BF16)
    w_glu_b = w_glu.astype(BF16)
    w_bs_b = w_bs.astype(BF16)
    w_out_b = w_out.astype(BF16)
    w_gu_b = w_gate_up.astype(BF16)
    w_dn_b = w_down.astype(BF16)
    lbs = _lower_bounds(lb_logits)

    n_p = tp // S5_CHUNK
    n_cs = ts // S5_CHUNK
    s5_kt, s5_pb, s5_ct, s5_lam = _s5_tables(s5_a_log_neg_re, s5_a_im, s5_log_dt, s5_b_re,
                                             s5_b_im, s5_c_re, s5_c_im, int(math.log2(n_p)))
    n_slabs = n_groups // S5_SLAB
    n_pairs = S5_SLAB // 2

    def pack_state(re, im):
        nb = re.shape[0]
        x0 = jnp.stack([re.reshape(nb, n_slabs, n_pairs, 2 * S5_STATE),
                        im.reshape(nb, n_slabs, n_pairs, 2 * S5_STATE)], axis=3)
        return x0.transpose(1, 2, 3, 0, 4).reshape(n_slabs, S5_SLAB, nb, 2 * S5_STATE)

    def unpack_state(st):
        nb = st.shape[2]
        st = st.reshape(n_slabs, n_pairs, 2, nb, 2 * S5_STATE).transpose(2, 3, 0, 1, 4)
        st = st.reshape(2, nb, n_groups, S5_STATE)
        return st[0], st[1]

    tm_in = _pick(n_tok, 1024)
    tn_in = 1024
    tm_mix = _pick(n_tok, 256)
    tm_ffn = _pick(n_tok, 1024)
    tf = _pick(w_down.shape[1], 512)
    c_p = _pick(tp, 128)
    c_s = _pick(ts, 128)

    xs = (x_prompt.reshape(n_tok_p, d), x_sample.reshape(n_tok_s, d))
    if n_tok_p % tm_in or n_tok_s % tm_in or n_tok_p % tm_mix or n_tok_s % tm_mix:
        xs = (jnp.concatenate(xs, axis=0),)
    zero_h = jnp.zeros((1, bp, n_heads, HEAD_DIM, HEAD_DIM), F32)

    new_re_p, new_im_p, new_re_s, new_im_s = [], [], [], []
    hgrn_p = hgrn_s = None
    for l in range(depth):
        gates, main = _inproj(xs, norm1[l], w_in_b, l, 2 * d, (2 * dh, 4 * dh), dh, lbs[l],
                              tm_in, tn_in)

        o_h, hgrn_p = _hgrn(main, hgrn_norm[l], zero_h, None, hgrn_p,
                            layer=l, n_layers=depth, row_off=0, n_seq=bp, seq_len=tp, c=c_p,
                            nh=n_heads, hgrp=n_heads)
        o_h, hgrn_s = _hgrn(main, hgrn_norm[l], state_hgrn, o_h, hgrn_s,
                            layer=l, n_layers=depth, row_off=n_tok_p, n_seq=bs, seq_len=ts, c=c_s,
                            nh=n_heads, hgrp=n_heads)

        y, st_p, st_s = _s5(main, 4 * dh, s5_kt, s5_pb, s5_ct, s5_lam, l,
                            pack_state(state_s5_re[l], state_s5_im[l]),
                            n_p=n_p, n_b=bs, n_cs=n_cs)

        x = _mix(xs, gates, o_h, y, main, (4 * dh) // ds, s5_d[l], b_glu[l],
                 w_glu_b, w_bh_b, w_bs_b, w_out_b, l, tm_mix)
        last = l == depth - 1
        if last and n_tok_p % tm_ffn == 0 and n_tok_s % tm_ffn == 0:
            xs = (_ffn(x, norm2[l], w_gu_b, w_dn_b, l, final_norm, True, tm_ffn, tf, 0, n_tok_p),
                  _ffn(x, norm2[l], w_gu_b, w_dn_b, l, final_norm, True, tm_ffn, tf, n_tok_p,
                       n_tok_s))
        else:
            xs = (_ffn(x, norm2[l], w_gu_b, w_dn_b, l, final_norm, last, tm_ffn, tf),)

        re_p, im_p = unpack_state(st_p)
        re_s, im_s = unpack_state(st_s)
        new_re_p.append(re_p)
        new_im_p.append(im_p)
        new_re_s.append(re_s)
        new_im_s.append(im_s)

    if len(xs) == 1:
        xs = (xs[0][:n_tok_p], xs[0][n_tok_p:])
    y_prompt = xs[0].reshape(bp, tp, d)
    y_sample = xs[1].reshape(bs, ts, d)
    return (y_prompt, y_sample, hgrn_p, jnp.stack(new_re_p), jnp.stack(new_im_p),
            hgrn_s, jnp.stack(new_re_s), jnp.stack(new_im_s))
```
